```python
import math
import jax, jax.numpy as jnp
from jax import lax
import numpy as np

D_MODEL = 1024
BATCH = 16
SEQ = 256
DEPTH = 4
DEC_BATCH = 4
DEC_SEQ = 1024
PAST_LEN = 512

GRID_W = 64
N_MIXERS = 3
D_FF = 4 * D_MODEL
EPS = 1e-6
S5_GROUP = 16
S5_GROUPS = D_MODEL // S5_GROUP
S5_STATE = 64
S5_DT_MIN = 1e-3
S5_DT_MAX = 1e-1
NA_HEAD_DIM = 64
NA_HEADS = D_MODEL // NA_HEAD_DIM
NA_KH = 8
NA_KW = 16
NA_QCB = 16
NA_KCB = 2 * NA_KW
GQA_HEAD_DIM = 128
GQA_Q_HEADS = D_MODEL // GQA_HEAD_DIM
GQA_KV_HEADS = 2
ROPE_THETA = 10000.0
Q_BLOCK = 128
N_S5 = (DEPTH + 2) // 3
N_NA = (DEPTH + 1) // 3
N_GQA = DEPTH // 3

kernel_name = "hybrid_s5_natten_gqa_diffusion_step"

F32 = jnp.float32


def rms_norm(x, g):
    xf = x.astype(F32)
    y = xf * lax.rsqrt(jnp.mean(xf * xf, axis=-1, keepdims=True) + EPS)
    return (y * g.astype(F32)).astype(x.dtype)


def modulation(cvec, w, b):
    m = jax.nn.silu(cvec) @ w + b
    return [t[:, None, :] for t in jnp.split(m, 6, axis=-1)]


def sqrelu_mlp(h, w1, w2):
    a = jax.nn.relu(h @ w1)
    return (a * a) @ w2


def axial_rope_tables(L, d):
    t = jnp.arange(L)
    row = (t // GRID_W).astype(F32)
    col = (t % GRID_W).astype(F32)
    half = d // 2
    inv = ROPE_THETA ** (-jnp.arange(0, half, 2, dtype=F32) / half)
    ang = jnp.concatenate([row[:, None] * inv, col[:, None] * inv], axis=-1)
    return jnp.cos(ang), jnp.sin(ang)


def apply_rope(x, cos, sin):
    x1, x2 = jnp.split(x.astype(F32), 2, axis=-1)
    c = cos[None, :, None, :]
    s = sin[None, :, None, :]
    return jnp.concatenate([x1 * c - x2 * s, x1 * s + x2 * c], axis=-1).astype(x.dtype)


def blocked_attention(q, k, v):
    B_, Lq, Hq, d = q.shape
    Hkv = k.shape[2]
    rep = Hq // Hkv
    nb = Lq // Q_BLOCK
    qb = q.reshape(B_, nb, Q_BLOCK, Hkv, rep, d).transpose(1, 0, 2, 3, 4, 5)
    scale = d ** -0.5

    def one(qblk):
        s = jnp.einsum('bqgrd,bkgd->bgrqk', qblk, k).astype(F32) * scale
        p = jax.nn.softmax(s, axis=-1).astype(v.dtype)
        return jnp.einsum('bgrqk,bkgd->bqgrd', p, v)

    o = lax.map(one, qb)
    return o.transpose(1, 0, 2, 3, 4, 5).reshape(B_, Lq, Hq, d)


def s5_discretize(lam_re, lam_im, log_dt, b_re, b_im):
    dt = jnp.exp(log_dt)[:, None]
    ar = lam_re * dt
    ai = lam_im * dt
    mag = jnp.exp(ar)
    abr = mag * jnp.cos(ai)
    abi = mag * jnp.sin(ai)
    nr = abr - 1.0
    ni = abi
    den = lam_re * lam_re + lam_im * lam_im
    f_re = (nr * lam_re + ni * lam_im) / den
    f_im = (ni * lam_re - nr * lam_im) / den
    bbr = f_re[..., None] * b_re - f_im[..., None] * b_im
    bbi = f_re[..., None] * b_im + f_im[..., None] * b_re
    return ar, ai, abr, abi, bbr, bbi


def s5_combine(e1, e2):
    a1r, a1i, b1r, b1i = e1
    a2r, a2i, b2r, b2i = e2
    return (a2r * a1r - a2i * a1i,
            a2r * a1i + a2i * a1r,
            a2r * b1r - a2i * b1i + b2r,
            a2r * b1i + a2i * b1r + b2i)


def s5_scan(ug, ar, ai, abr, abi, bbr, bbi, h0r, h0i, reverse):
    L = ug.shape[1]
    bur = jnp.einsum('gpk,blgk->blgp', bbr, ug)
    bui = jnp.einsum('gpk,blgk->blgp', bbi, ug)
    a_re = jnp.broadcast_to(abr, bur.shape)
    a_im = jnp.broadcast_to(abi, bur.shape)
    _, _, hr, hi = lax.associative_scan(s5_combine, (a_re, a_im, bur, bui), reverse=reverse, axis=1)
    if h0r is not None:
        t = jnp.arange(L, dtype=F32)
        n = ((L - t) if reverse else (t + 1.0))[:, None, None]
        mag = jnp.exp(n * ar)
        pr = mag * jnp.cos(n * ai)
        pim = mag * jnp.sin(n * ai)
        hr = hr + pr * h0r[:, None] - pim * h0i[:, None]
        hi = hi + pr * h0i[:, None] + pim * h0r[:, None]
    return hr, hi


def s5_mixer(u, lam_re, lam_im, log_dt, b_re, b_im, c_re, c_im, d_skip, glu_w, glu_b, h0):
    B_, L, _ = u.shape
    uf = u.astype(F32)
    ug = uf.reshape(B_, L, S5_GROUPS, S5_GROUP)
    y = uf * d_skip.astype(F32)
    finals = []
    for dr in range(2):
        ar, ai, abr, abi, bbr, bbi = s5_discretize(lam_re[dr].astype(F32), lam_im[dr].astype(F32),
                                                   log_dt[dr].astype(F32), b_re[dr].astype(F32),
                                                   b_im[dr].astype(F32))
        h0r = None if h0 is None else h0[:, dr, 0].astype(F32)
        h0i = None if h0 is None else h0[:, dr, 1].astype(F32)
        hr, hi = s5_scan(ug, ar, ai, abr, abi, bbr, bbi, h0r, h0i, reverse=(dr == 1))
        yd = (jnp.einsum('gkp,blgp->blgk', c_re[dr].astype(F32), hr)
              - jnp.einsum('gkp,blgp->blgk', c_im[dr].astype(F32), hi))
        y = y + yd.reshape(B_, L, D_MODEL)
        idx = L - 1 if dr == 0 else 0
        finals.append(jnp.stack([hr[:, idx], hi[:, idx]], axis=1))
    z = jax.nn.gelu(y).astype(u.dtype) @ glu_w + glu_b
    za, zg = jnp.split(z, 2, axis=-1)
    return za * jax.nn.sigmoid(zg), jnp.stack(finals, axis=1)


def na_qkv(h, w_qkv, qn, kn):
    B_, L, _ = h.shape
    qkv = (h @ w_qkv).reshape(B_, L, 3, NA_HEADS, NA_HEAD_DIM)
    return rms_norm(qkv[:, :, 0], qn), rms_norm(qkv[:, :, 1], kn), qkv[:, :, 2]


def na_context(h, w_qkv, qn, kn, w_o):
    B_, L, _ = h.shape
    q, k, v = na_qkv(h, w_qkv, qn, kn)
    o = blocked_attention(q, k, v)
    return o.reshape(B_, L, D_MODEL) @ w_o, k, v


def na_latent(h, w_qkv, qn, kn, rpb, w_o, k_ctx, v_ctx):
    B_, L, _ = h.shape
    rows = L // GRID_W
    kh = min(NA_KH, rows)
    ncb = GRID_W // NA_QCB
    H, d = NA_HEADS, NA_HEAD_DIM
    q, k, v = na_qkv(h, w_qkv, qn, kn)
    r = jnp.arange(rows)
    rs = jnp.clip(r - kh // 2, 0, rows - kh)
    ri = rs[:, None] + jnp.arange(kh)
    j = jnp.arange(ncb)
    cb = jnp.clip(j * NA_QCB - NA_KW // 2, 0, GRID_W - NA_KCB)
    ci = cb[:, None] + jnp.arange(NA_KCB)
    rsel = ri[:, None, :, None]
    csel = ci[None, :, None, :]
    kg = k.reshape(B_, rows, GRID_W, H, d)[:, rsel, csel].reshape(B_, rows, ncb, kh * NA_KCB, H, d)
    vg = v.reshape(B_, rows, GRID_W, H, d)[:, rsel, csel].reshape(B_, rows, ncb, kh * NA_KCB, H, d)
    qg = q.reshape(B_, rows, ncb, NA_QCB, H, d)
    scale = d ** -0.5
    s_loc = jnp.einsum('brjqhd,brjkhd->bhrjqk', qg, kg).astype(F32) * scale
    qc = j[:, None] * NA_QCB + jnp.arange(NA_QCB)
    cs = jnp.clip(qc - NA_KW // 2, 0, GRID_W - NA_KW)
    kc = ci[:, None, :]
    col_ok = (kc >= cs[:, :, None]) & (kc < cs[:, :, None] + NA_KW)
    dc = jnp.clip(kc - qc[:, :, None] + NA_KW - 1, 0, 2 * NA_KW - 2)
    drow = ri - r[:, None] + NA_KH - 1
    bias = rpb.astype(F32)[:, drow[:, None, None, :, None], dc[None, :, :, None, :]]
    bias = bias.reshape(H, rows, ncb, NA_QCB, kh * NA_KCB)
    mask = jnp.broadcast_to(col_ok[:, :, None, :], (ncb, NA_QCB, kh, NA_KCB)).reshape(ncb, NA_QCB, kh * NA_KCB)
    s_loc = jnp.where(mask, s_loc + bias, -jnp.inf)
    s_ctx = jnp.einsum('brjqhd,bchd->bhrjqc', qg, k_ctx).astype(F32) * scale
    nloc = kh * NA_KCB
    p = jax.nn.softmax(jnp.concatenate([s_loc, s_ctx], axis=-1), axis=-1).astype(v.dtype)
    o = (jnp.einsum('bhrjqk,brjkhd->brjqhd', p[..., :nloc], vg)
         + jnp.einsum('bhrjqc,bchd->brjqhd', p[..., nloc:], v_ctx))
    return o.reshape(B_, L, D_MODEL) @ w_o


def gqa_qkv(h, w_qkv, qn, kn):
    B_, L, _ = h.shape
    z = h @ w_qkv
    nq = GQA_Q_HEADS * GQA_HEAD_DIM
    nk = GQA_KV_HEADS * GQA_HEAD_DIM
    q = z[..., :nq].reshape(B_, L, GQA_Q_HEADS, GQA_HEAD_DIM)
    k = z[..., nq:nq + nk].reshape(B_, L, GQA_KV_HEADS, GQA_HEAD_DIM)
    v = z[..., nq + nk:].reshape(B_, L, GQA_KV_HEADS, GQA_HEAD_DIM)
    return rms_norm(q, qn), rms_norm(k, kn), v


def gqa_context(h, w_qkv, qn, kn, w_o):
    B_, L, _ = h.shape
    q, k, v = gqa_qkv(h, w_qkv, qn, kn)
    o = blocked_attention(q, k, v)
    return o.reshape(B_, L, D_MODEL) @ w_o, k, v


def gqa_latent(h, w_qkv, qn, kn, w_o, k_ctx, v_ctx):
    B_, L, _ = h.shape
    q, k, v = gqa_qkv(h, w_qkv, qn, kn)
    cos, sin = axial_rope_tables(L, GQA_HEAD_DIM)
    q = apply_rope(q, cos, sin)
    k = apply_rope(k, cos, sin)
    k_all = jnp.concatenate([k, k_ctx.astype(k.dtype)], axis=1)
    v_all = jnp.concatenate([v, v_ctx.astype(v.dtype)], axis=1)
    o = blocked_attention(q, k_all, v_all)
    return o.reshape(B_, L, D_MODEL) @ w_o


def setup_inputs(seed: int = 0) -> dict:
    key = jax.random.key(seed)
    ks = iter(jax.random.split(key, 48))

    def nrm(shape, s):
        return jax.random.normal(next(ks), shape, F32) * s

    G, P, K = S5_GROUPS, S5_STATE, S5_GROUP
    D = D_MODEL
    inp = {}
    inp['x_prompt'] = nrm((BATCH, SEQ, D), 1.0)
    inp['x_sample'] = nrm((DEC_BATCH, DEC_SEQ, D), 1.0)
    inp['state_s5'] = nrm((DEC_BATCH, N_S5, 2, 2, G, P), 0.1)
    inp['cache_na_k'] = nrm((DEC_BATCH, N_NA, PAST_LEN, NA_HEADS, NA_HEAD_DIM), 1.0)
    inp['cache_na_v'] = nrm((DEC_BATCH, N_NA, PAST_LEN, NA_HEADS, NA_HEAD_DIM), 1.0)
    inp['cache_gqa_k'] = nrm((DEC_BATCH, N_GQA, PAST_LEN, GQA_KV_HEADS, GQA_HEAD_DIM), 1.0)
    inp['cache_gqa_v'] = nrm((DEC_BATCH, N_GQA, PAST_LEN, GQA_KV_HEADS, GQA_HEAD_DIM), 1.0)
    inp['c'] = nrm((DEC_BATCH, D), 1.0)
    inp['c_ctx'] = nrm((D,), 1.0)
    inp['norm_g'] = 1.0 + nrm((DEPTH, 2, D), 0.01)
    inp['ada_w'] = nrm((DEPTH, D, 6 * D), D ** -0.5)
    inp['ada_b'] = nrm((DEPTH, 6 * D), 0.02)
    inp['mlp_w1'] = nrm((DEPTH, D, D_FF), D ** -0.5)
    inp['mlp_w2'] = nrm((DEPTH, D_FF, D), D_FF ** -0.5)
    inp['s5_lam_re'] = -0.5 + nrm((N_S5, 2, G, P), 0.01)
    inp['s5_lam_im'] = jnp.broadcast_to(jnp.pi * jnp.arange(P, dtype=F32), (N_S5, 2, G, P))
    inp['s5_log_dt'] = jax.random.uniform(next(ks), (N_S5, 2, G), F32,
                                          minval=math.log(S5_DT_MIN), maxval=math.log(S5_DT_MAX))
    inp['s5_b_re'] = nrm((N_S5, 2, G, P, K), (2.0 * K) ** -0.5)
    inp['s5_b_im'] = nrm((N_S5, 2, G, P, K), (2.0 * K) ** -0.5)
    inp['s5_c_re'] = nrm((N_S5, 2, G, K, P), (2.0 * P) ** -0.5)
    inp['s5_c_im'] = nrm((N_S5, 2, G, K, P), (2.0 * P) ** -0.5)
    inp['s5_d'] = nrm((N_S5, D), 0.5)
    inp['s5_glu_w'] = nrm((N_S5, D, 2 * D), D ** -0.5)
    inp['s5_glu_b'] = nrm((N_S5, 2 * D), 0.02)
    inp['na_w_qkv'] = nrm((N_NA, D, 3 * D), D ** -0.5)
    inp['na_q_norm'] = 1.0 + nrm((N_NA, NA_HEAD_DIM), 0.01)
    inp['na_k_norm'] = 1.0 + nrm((N_NA, NA_HEAD_DIM), 0.01)
    inp['na_rpb'] = nrm((N_NA, NA_HEADS, 2 * NA_KH - 1, 2 * NA_KW - 1), 0.1)
    inp['na_w_o'] = nrm((N_NA, D, D), D ** -0.5)
    inp['gqa_w_qkv'] = nrm((N_GQA, D, (GQA_Q_HEADS + 2 * GQA_KV_HEADS) * GQA_HEAD_DIM), D ** -0.5)
    inp['gqa_q_norm'] = 1.0 + nrm((N_GQA, GQA_HEAD_DIM), 0.01)
    inp['gqa_k_norm'] = 1.0 + nrm((N_GQA, GQA_HEAD_DIM), 0.01)
    inp['gqa_w_o'] = nrm((N_GQA, D, D), D ** -0.5)
    return inp


def reference(x_prompt, x_sample, state_s5, cache_na_k, cache_na_v, cache_gqa_k, cache_gqa_v,
              c, c_ctx, norm_g, ada_w, ada_b, mlp_w1, mlp_w2,
              s5_lam_re, s5_lam_im, s5_log_dt, s5_b_re, s5_b_im, s5_c_re, s5_c_im, s5_d,
              s5_glu_w, s5_glu_b, na_w_qkv, na_q_norm, na_k_norm, na_rpb, na_w_o,
              gqa_w_qkv, gqa_q_norm, gqa_k_norm, gqa_w_o):
    xp = x_prompt
    xs = x_sample
    new_s5, new_na_k, new_na_v, new_gqa_k, new_gqa_v = [], [], [], [], []
    for i in range(DEPTH):
        kind = i % N_MIXERS
        slot = i // N_MIXERS
        p_sh1, p_sc1, p_g1, p_sh2, p_sc2, p_g2 = modulation(c_ctx[None, :], ada_w[i], ada_b[i])
        s_sh1, s_sc1, s_g1, s_sh2, s_sc2, s_g2 = modulation(c, ada_w[i], ada_b[i])
        hp = rms_norm(xp, norm_g[i, 0]) * (1.0 + p_sc1) + p_sh1
        hs = rms_norm(xs, norm_g[i, 0]) * (1.0 + s_sc1) + s_sh1
        if kind == 0:
            s5_args = (s5_lam_re[slot], s5_lam_im[slot], s5_log_dt[slot], s5_b_re[slot], s5_b_im[slot],
                       s5_c_re[slot], s5_c_im[slot], s5_d[slot], s5_glu_w[slot], s5_glu_b[slot])
            yp, st = s5_mixer(hp, *s5_args, None)
            ys, _ = s5_mixer(hs, *s5_args, state_s5[:, slot])
            new_s5.append(st)
        elif kind == 1:
            yp, kc, vc = na_context(hp, na_w_qkv[slot], na_q_norm[slot], na_k_norm[slot], na_w_o[slot])
            ys = na_latent(hs, na_w_qkv[slot], na_q_norm[slot], na_k_norm[slot], na_rpb[slot], na_w_o[slot],
                           cache_na_k[:, slot], cache_na_v[:, slot])
            new_na_k.append(kc)
            new_na_v.append(vc)
        else:
            yp, kc, vc = gqa_context(hp, gqa_w_qkv[slot], gqa_q_norm[slot], gqa_k_norm[slot], gqa_w_o[slot])
            ys = gqa_latent(hs, gqa_w_qkv[slot], gqa_q_norm[slot], gqa_k_norm[slot], gqa_w_o[slot],
                            cache_gqa_k[:, slot], cache_gqa_v[:, slot])
            new_gqa_k.append(kc)
            new_gqa_v.append(vc)
        xp = xp + p_g1 * yp.astype(xp.dtype)
        xs = xs + s_g1 * ys.astype(xs.dtype)
        hp = rms_norm(xp, norm_g[i, 1]) * (1.0 + p_sc2) + p_sh2
        hs = rms_norm(xs, norm_g[i, 1]) * (1.0 + s_sc2) + s_sh2
        xp = xp + p_g2 * sqrelu_mlp(hp, mlp_w1[i], mlp_w2[i])
        xs = xs + s_g2 * sqrelu_mlp(hs, mlp_w1[i], mlp_w2[i])
    return (xp, xs, jnp.stack(new_s5, axis=1), jnp.stack(new_na_k, axis=1), jnp.stack(new_na_v, axis=1),
            jnp.stack(new_gqa_k, axis=1), jnp.stack(new_gqa_v, axis=1))
```

```python
import functools
import math

import jax
import jax.numpy as jnp
from jax import lax
from jax.experimental import pallas as pl
from jax.experimental.pallas import tpu as pltpu

F32 = jnp.float32
BF16 = jnp.bfloat16

EPS = 1e-6
GRID_W = 64
S5_GROUP = 16
NA_KH = 8
NA_KW = 16
ROPE_THETA = 10000.0
S5_CHUNK = 16
LANES = 128
SUBLANES = 8
VMEM_LIMIT = 56 * 1024 * 1024


def _cparams(*sem):
    return pltpu.CompilerParams(dimension_semantics=sem, vmem_limit_bytes=VMEM_LIMIT)


def _pow2_tile(pref, *ns):
    t = pref
    while any(n % t for n in ns):
        t //= 2
    return t


def _normmod(x, g, sc, sh):
    ms = jnp.mean(x * x, axis=-1, keepdims=True)
    y = x * lax.rsqrt(ms + EPS) * g
    return y * (1.0 + sc) + sh


class _Tok:
    def __init__(self, n_prompt, n_sample, dec_seq, pref):
        self.tm = _pow2_tile(pref, n_prompt, dec_seq)
        self.n = n_prompt + n_sample
        self.tiles = self.n // self.tm
        self.np_tiles = n_prompt // self.tm
        self.tps = dec_seq // self.tm

    def mod_row(self, i):
        return jnp.where(i < self.np_tiles, 0, 1 + (i - self.np_tiles) // self.tps)

    def mod_spec(self, layer, which, d, nargs):
        if nargs == 1:
            return pl.BlockSpec((None, None, None, 1, d), lambda i: (layer, self.mod_row(i), which, 0, 0))
        return pl.BlockSpec((None, None, None, 1, d), lambda i, j: (layer, self.mod_row(i), which, 0, 0))


def _mod_kernel(c_ref, w_ref, b_ref, o_ref):
    c = c_ref[...]
    s = c * jax.nn.sigmoid(c)
    o_ref[...] = jnp.dot(s.astype(BF16), w_ref[...].astype(BF16), preferred_element_type=F32) + b_ref[...]


def _modulation(cvec, ada_w, ada_b):
    depth, d, d6 = ada_w.shape
    rows = cvec.shape[0]
    tn = _pow2_tile(2048, d6) if d6 % 2048 == 0 else d
    return pl.pallas_call(
        _mod_kernel,
        grid=(depth, d6 // tn),
        in_specs=[pl.BlockSpec((rows, d), lambda l, j: (0, 0)),
                  pl.BlockSpec((None, d, tn), lambda l, j: (l, 0, j)),
                  pl.BlockSpec((None, 1, tn), lambda l, j: (l, 0, j))],
        out_specs=pl.BlockSpec((None, rows, tn), lambda l, j: (l, 0, j)),
        out_shape=jax.ShapeDtypeStruct((depth, rows, d6), F32),
        compiler_params=_cparams("arbitrary", "arbitrary"),
        name="adaln_modulation",
    )(cvec, ada_w, ada_b.reshape(depth, 1, d6))


def _s5_pre_kernel(x_ref, g_ref, sc_ref, sh_ref, h_ref):
    h_ref[...] = _normmod(x_ref[...], g_ref[...], sc_ref[...], sh_ref[...]).astype(BF16)


def _s5_pre(x, g, mod, layer, tok):
    n, d = x.shape
    return pl.pallas_call(
        _s5_pre_kernel,
        grid=(tok.tiles,),
        in_specs=[pl.BlockSpec((tok.tm, d), lambda i: (i, 0)),
                  pl.BlockSpec((1, d), lambda i: (0, 0)),
                  tok.mod_spec(layer, 1, d, 1),
                  tok.mod_spec(layer, 0, d, 1)],
        out_specs=pl.BlockSpec((tok.tm, d), lambda i: (i, 0)),
        out_shape=jax.ShapeDtypeStruct((n, d), BF16),
        compiler_params=_cparams("arbitrary"),
        name="s5_pre",
    )(x, g, mod, mod)


def _s5_stream(x_ref, gi, w1, w2, ar, ai, hr, hi, n_chunks, bpad):
    p2 = ar.shape[-1]
    z = jnp.dot(x_ref[gi], w1, preferred_element_type=F32)
    tk = z.shape[1] - 2 * p2
    s_re = z[:, tk:tk + p2]
    s_im = z[:, tk + p2:]
    fwd = lax.broadcasted_iota(jnp.int32, (1, p2), 1) < (p2 // 2)
    ent_re, ent_im = [], []
    for t in range(n_chunks):
        ent_re.append(hr)
        ent_im.append(hi)
        a, b = t * bpad, (n_chunks - 1 - t) * bpad
        in_re = jnp.where(fwd, s_re[a:a + bpad], s_re[b:b + bpad])
        in_im = jnp.where(fwd, s_im[a:a + bpad], s_im[b:b + bpad])
        hr, hi = ar * hr - ai * hi + in_re, ar * hi + ai * hr + in_im
    e_re = jnp.concatenate([jnp.where(fwd, ent_re[c], ent_re[n_chunks - 1 - c]) for c in range(n_chunks)], axis=0)
    e_im = jnp.concatenate([jnp.where(fwd, ent_im[c], ent_im[n_chunks - 1 - c]) for c in range(n_chunks)], axis=0)
    e = jnp.concatenate([e_re, e_im], axis=1).astype(BF16)
    y = z[:, :tk] + jnp.dot(e, w2, preferred_element_type=F32)
    return y, hr, hi


def _s5_scan_kernel(xp_ref, xs_ref, w1_ref, w2_ref, at_ref, h0_ref, yp_ref, ys_ref, fin_ref, *,
                    gb, cp, bp, cs, bs):
    for gi in range(gb):
        w1 = w1_ref[gi]
        w2 = w2_ref[gi]
        ar = at_ref[gi, 0:1, :]
        ai = at_ref[gi, 1:2, :]
        zero = jnp.zeros((bp, ar.shape[-1]), F32)
        y, hr, hi = _s5_stream(xp_ref, gi, w1, w2, ar, ai, zero, zero, cp, bp)
        yp_ref[gi] = y
        fin_ref[gi, 0] = hr
        fin_ref[gi, 1] = hi
        y, _, _ = _s5_stream(xs_ref, gi, w1, w2, ar, ai, h0_ref[gi, 0], h0_ref[gi, 1], cs, bs)
        ys_ref[gi] = y


def _s5_scan(xgp, xgs, w1, w2, at, h0, cp, bp, cs, bs):
    g, rp, tk = xgp.shape
    rs = xgs.shape[1]
    p2 = at.shape[-1]
    gb = _pow2_tile(4, g)
    kern = functools.partial(_s5_scan_kernel, gb=gb, cp=cp, bp=bp, cs=cs, bs=bs)
    return pl.pallas_call(
        kern,
        grid=(g // gb,),
        in_specs=[pl.BlockSpec((gb, rp, tk), lambda i: (i, 0, 0)),
                  pl.BlockSpec((gb, rs, tk), lambda i: (i, 0, 0)),
                  pl.BlockSpec((gb, tk, tk + 2 * p2), lambda i: (i, 0, 0)),
                  pl.BlockSpec((gb, 2 * p2, tk), lambda i: (i, 0, 0)),
                  pl.BlockSpec((gb, 2, p2), lambda i: (i, 0, 0)),
                  pl.BlockSpec((gb, 2, bs, p2), lambda i: (i, 0, 0, 0))],
        out_specs=[pl.BlockSpec((gb, rp, tk), lambda i: (i, 0, 0)),
                   pl.BlockSpec((gb, rs, tk), lambda i: (i, 0, 0)),
                   pl.BlockSpec((gb, 2, bp, p2), lambda i: (i, 0, 0, 0))],
        out_shape=[jax.ShapeDtypeStruct((g, rp, tk), F32),
                   jax.ShapeDtypeStruct((g, rs, tk), F32),
                   jax.ShapeDtypeStruct((g, 2, bp, p2), F32)],
        compiler_params=_cparams("arbitrary"),
        name="s5_scan",
    )(xgp, xgs, w1, w2, at, h0)


def _s5_post_kernel(x_ref, y_ref, g_ref, sc_ref, sh_ref, gate_ref, dsk_ref, wa_ref, wg_ref, ba_ref, bg_ref, o_ref):
    x = x_ref[...]
    h = _normmod(x, g_ref[...], sc_ref[...], sh_ref[...])
    y = h * dsk_ref[...] + y_ref[...]
    a = jax.nn.gelu(y).astype(BF16)
    za = jnp.dot(a, wa_ref[...].astype(BF16), preferred_element_type=F32) + ba_ref[...]
    zg = jnp.dot(a, wg_ref[...].astype(BF16), preferred_element_type=F32) + bg_ref[...]
    o_ref[...] = x + gate_ref[...] * (za * jax.nn.sigmoid(zg))


def _s5_post(x, y, g, mod, layer, dsk, glu_w, glu_b, tok):
    n, d = x.shape
    row = lambda i: (i, 0)
    fixed = lambda i: (0, 0)
    gb2 = glu_b.reshape(1, 2 * d)
    return pl.pallas_call(
        _s5_post_kernel,
        grid=(tok.tiles,),
        in_specs=[pl.BlockSpec((tok.tm, d), row),
                  pl.BlockSpec((tok.tm, d), row),
                  pl.BlockSpec((1, d), fixed),
                  tok.mod_spec(layer, 1, d, 1),
                  tok.mod_spec(layer, 0, d, 1),
                  tok.mod_spec(layer, 2, d, 1),
                  pl.BlockSpec((1, d), fixed),
                  pl.BlockSpec((d, d), lambda i: (0, 0)),
                  pl.BlockSpec((d, d), lambda i: (0, 1)),
                  pl.BlockSpec((1, d), lambda i: (0, 0)),
                  pl.BlockSpec((1, d), lambda i: (0, 1))],
        out_specs=pl.BlockSpec((tok.tm, d), row),
        out_shape=jax.ShapeDtypeStruct((n, d), F32),
        compiler_params=_cparams("arbitrary"),
        name="s5_post",
    )(x, y, g, mod, mod, mod, dsk, glu_w, glu_w, gb2, gb2)


def _na_qkv_kernel(x_ref, g_ref, sc_ref, sh_ref, w_ref, qn_ref, kn_ref, q_ref, k_ref, v_ref, h_scr, *, hd):
    j = pl.program_id(1)

    @pl.when(j == 0)
    def _():
        h_scr[...] = _normmod(x_ref[...], g_ref[...], sc_ref[...], sh_ref[...]).astype(BF16)

    z = jnp.dot(h_scr[...], w_ref[...].astype(BF16), preferred_element_type=F32)
    d = z.shape[1]

    def head_norm(gain_ref, out_ref):
        lo = lax.broadcasted_iota(jnp.int32, (1, LANES), 1) < hd
        for s in range(d // LANES):
            seg = z[:, s * LANES:(s + 1) * LANES]
            sq = seg * seg
            s_lo = jnp.sum(jnp.where(lo, sq, 0.0), axis=-1, keepdims=True)
            s_hi = jnp.sum(jnp.where(lo, 0.0, sq), axis=-1, keepdims=True)
            ms = jnp.where(lo, s_lo, s_hi) / hd
            out_ref[:, s * LANES:(s + 1) * LANES] = (seg * lax.rsqrt(ms + EPS) * gain_ref[...]).astype(out_ref.dtype)

    @pl.when(j == 0)
    def _():
        head_norm(qn_ref, q_ref)

    @pl.when(j == 1)
    def _():
        head_norm(kn_ref, k_ref)

    @pl.when(j == 2)
    def _():
        v_ref[...] = z


def _na_qkv(x, g, mod, layer, w_qkv, qn, kn, tok):
    n, d = x.shape
    hd = qn.shape[-1]
    assert 2 * hd == LANES
    row = lambda i, j: (i, 0)
    fixed = lambda i, j: (0, 0)
    qn2 = jnp.tile(qn, 2).reshape(1, LANES)
    kn2 = jnp.tile(kn, 2).reshape(1, LANES)
    return pl.pallas_call(
        functools.partial(_na_qkv_kernel, hd=hd),
        grid=(tok.tiles, 3),
        in_specs=[pl.BlockSpec((tok.tm, d), row),
                  pl.BlockSpec((1, d), fixed),
                  tok.mod_spec(layer, 1, d, 2),
                  tok.mod_spec(layer, 0, d, 2),
                  pl.BlockSpec((d, d), lambda i, j: (0, j)),
                  pl.BlockSpec((1, LANES), fixed),
                  pl.BlockSpec((1, LANES), fixed)],
        out_specs=[pl.BlockSpec((tok.tm, d), row)] * 3,
        out_shape=[jax.ShapeDtypeStruct((n, d), BF16),
                   jax.ShapeDtypeStruct((n, d), F32),
                   jax.ShapeDtypeStruct((n, d), F32)],
        scratch_shapes=[pltpu.VMEM((tok.tm, d), BF16)],
        compiler_params=_cparams("arbitrary", "arbitrary"),
        name="na_qkv",
    )(x, g, mod, mod, w_qkv, qn2, kn2)


def _softmax_pv(s_parts, v_parts):
    m = s_parts[0].max(axis=-1, keepdims=True)
    for s in s_parts[1:]:
        m = jnp.maximum(m, s.max(axis=-1, keepdims=True))
    den = 0.0
    acc = 0.0
    for s, v in zip(s_parts, v_parts):
        e = jnp.exp(s - m)
        den = den + e.sum(axis=-1, keepdims=True)
        acc = acc + jnp.dot(e.astype(BF16), v, preferred_element_type=F32)
    return acc / den


def _qk(q, k):
    return lax.dot_general(q, k, (((1,), (1,)), ((), ())), preferred_element_type=F32)


def _na_ctx_attn_kernel(q_ref, k_ref, v_ref, o_ref, *, hd):
    lo = lax.broadcasted_iota(jnp.int32, (1, LANES), 1) < hd
    scale = hd ** -0.5
    for s in range(q_ref.shape[1] // LANES):
        sl = slice(s * LANES, (s + 1) * LANES)
        q = q_ref[:, sl]
        k = k_ref[:, sl].astype(BF16)
        v = v_ref[:, sl].astype(BF16)
        zero = jnp.zeros_like(q)
        o_lo = _softmax_pv([_qk(jnp.where(lo, q, zero), k) * scale], [v])
        o_hi = _softmax_pv([_qk(jnp.where(lo, zero, q), k) * scale], [v])
        o_ref[:, sl] = jnp.where(lo, o_lo, o_hi).astype(o_ref.dtype)


def _na_ctx_attn(q, k, v, batch, seq, hd):
    d = q.shape[1]
    spec = pl.BlockSpec((seq, d), lambda b: (b, 0))
    return pl.pallas_call(
        functools.partial(_na_ctx_attn_kernel, hd=hd),
        grid=(batch,),
        in_specs=[spec, spec, spec],
        out_specs=spec,
        out_shape=jax.ShapeDtypeStruct((batch * seq, d), BF16),
        compiler_params=_cparams("arbitrary"),
        name="na_ctx_attn",
    )(q, k, v)


def _na_lat_attn_kernel(q_ref, kl_ref, vl_ref, kc_ref, vc_ref, bias_ref, o_ref, *, hd):
    lo = lax.broadcasted_iota(jnp.int32, (1, LANES), 1) < hd
    scale = hd ** -0.5
    q = q_ref[...]
    kl = kl_ref[...].astype(BF16)
    vl = vl_ref[...].astype(BF16)
    kc = kc_ref[...].astype(BF16)
    vc = vc_ref[...].astype(BF16)
    zero = jnp.zeros_like(q)
    outs = []
    for hh in range(2):
        qm = jnp.where(lo, q, zero) if hh == 0 else jnp.where(lo, zero, q)
        s_loc = _qk(qm, kl) * scale + bias_ref[hh]
        s_ctx = _qk(qm, kc) * scale
        outs.append(_softmax_pv([s_loc, s_ctx], [vl, vc]))
    o_ref[...] = jnp.where(lo, outs[0], outs[1]).astype(o_ref.dtype)


def _na_lat_attn(q, k, v, cache_k, cache_v, slot, bias, n_prompt, dec_batch, dec_seq, hd):
    d = q.shape[1]
    past = cache_k.shape[2]
    tq = _pow2_tile(256, dec_seq, n_prompt)
    nqb = dec_seq // tq
    q0 = n_prompt // tq
    l0 = n_prompt // dec_seq
    return pl.pallas_call(
        functools.partial(_na_lat_attn_kernel, hd=hd),
        grid=(d // LANES, nqb, dec_batch),
        in_specs=[pl.BlockSpec((tq, LANES), lambda hp, qb, b: (q0 + b * nqb + qb, hp)),
                  pl.BlockSpec((dec_seq, LANES), lambda hp, qb, b: (l0 + b, hp)),
                  pl.BlockSpec((dec_seq, LANES), lambda hp, qb, b: (l0 + b, hp)),
                  pl.BlockSpec((None, None, past, LANES), lambda hp, qb, b: (b, slot, 0, hp)),
                  pl.BlockSpec((None, None, past, LANES), lambda hp, qb, b: (b, slot, 0, hp)),
                  pl.BlockSpec((2, tq, dec_seq), lambda hp, qb, b: (hp, qb, 0))],
        out_specs=pl.BlockSpec((tq, LANES), lambda hp, qb, b: (b * nqb + qb, hp)),
        out_shape=jax.ShapeDtypeStruct((dec_batch * dec_seq, d), BF16),
        compiler_params=_cparams("arbitrary", "arbitrary", "arbitrary"),
        name="na_lat_attn",
    )(q, k, v, cache_k, cache_v, bias)


def _na_bias_table(rpb, dec_seq):
    rows = dec_seq // GRID_W
    kh = min(NA_KH, rows)
    t = jnp.arange(dec_seq)
    r, c = t // GRID_W, t % GRID_W
    rs = jnp.clip(r - kh // 2, 0, rows - kh)
    cs = jnp.clip(c - NA_KW // 2, 0, GRID_W - NA_KW)
    row_ok = (r[None, :] >= rs[:, None]) & (r[None, :] < rs[:, None] + kh)
    col_ok = (c[None, :] >= cs[:, None]) & (c[None, :] < cs[:, None] + NA_KW)
    drow = jnp.clip(r[None, :] - r[:, None] + NA_KH - 1, 0, 2 * NA_KH - 2)
    dcol = jnp.clip(c[None, :] - c[:, None] + NA_KW - 1, 0, 2 * NA_KW - 2)
    b = rpb.astype(F32)[:, drow, dcol]
    return jnp.where((row_ok & col_ok)[None], b, -jnp.inf)


def _gqa_qkv_kernel(x_ref, g_ref, sc_ref, sh_ref, w_ref, qn_ref, kn_ref, cos_ref, sin_ref,
                    q_ref, k_ref, v_ref, *, nq, nk, np_tiles):
    i = pl.program_id(0)
    h = _normmod(x_ref[...], g_ref[...], sc_ref[...], sh_ref[...]).astype(BF16)
    z = jnp.dot(h, w_ref[...].astype(BF16), preferred_element_type=F32)
    is_sample = i >= np_tiles
    cos = cos_ref[...]
    sin = sin_ref[...]

    def norm_rope(seg, gain):
        ms = jnp.mean(seg * seg, axis=-1, keepdims=True)
        y = seg * lax.rsqrt(ms + EPS) * gain
        roped = y * cos + pltpu.roll(y, LANES // 2, 1) * sin
        return jnp.where(is_sample, roped, y)

    for hh in range(nq):
        sl = slice(hh * LANES, (hh + 1) * LANES)
        q_ref[:, sl] = norm_rope(z[:, sl], qn_ref[...]).astype(q_ref.dtype)
    for hh in range(nk):
        k_ref[:, hh * LANES:(hh + 1) * LANES] = norm_rope(z[:, (nq + hh) * LANES:(nq + hh + 1) * LANES], kn_ref[...])
    v_ref[...] = z[:, (nq + nk) * LANES:]


def _gqa_qkv(x, g, mod, layer, w_qkv, qn, kn, cos_t, sin_t, nk, tok):
    n, d = x.shape
    hd = qn.shape[-1]
    assert hd == LANES
    nq = d // hd
    dk = nk * hd
    row = lambda i: (i, 0)
    fixed = lambda i: (0, 0)
    pos = lambda i: (jnp.maximum(i - tok.np_tiles, 0) % tok.tps, 0)
    return pl.pallas_call(
        functools.partial(_gqa_qkv_kernel, nq=nq, nk=nk, np_tiles=tok.np_tiles),
        grid=(tok.tiles,),
        in_specs=[pl.BlockSpec((tok.tm, d), row),
                  pl.BlockSpec((1, d), fixed),
                  tok.mod_spec(layer, 1, d, 1),
                  tok.mod_spec(layer, 0, d, 1),
                  pl.BlockSpec((d, d + 2 * dk), fixed),
                  pl.BlockSpec((1, hd), fixed),
                  pl.BlockSpec((1, hd), fixed),
                  pl.BlockSpec((tok.tm, hd), pos),
                  pl.BlockSpec((tok.tm, hd), pos)],
        out_specs=[pl.BlockSpec((tok.tm, d), row),
                   pl.BlockSpec((tok.tm, dk), row),
                   pl.BlockSpec((tok.tm, dk), row)],
        out_shape=[jax.ShapeDtypeStruct((n, d), BF16),
                   jax.ShapeDtypeStruct((n, dk), F32),
                   jax.ShapeDtypeStruct((n, dk), F32)],
        compiler_params=_cparams("arbitrary"),
        name="gqa_qkv",
    )(x, g, mod, mod, w_qkv, qn.reshape(1, hd), kn.reshape(1, hd), cos_t, sin_t)


def _rope_tables(dec_seq, hd):
    t = jnp.arange(dec_seq)
    row = (t // GRID_W).astype(F32)
    col = (t % GRID_W).astype(F32)
    half = hd // 2
    inv = ROPE_THETA ** (-jnp.arange(0, half, 2, dtype=F32) / half)
    ang = jnp.concatenate([row[:, None] * inv, col[:, None] * inv], axis=-1)
    cos, sin = jnp.cos(ang), jnp.sin(ang)
    return jnp.concatenate([cos, cos], axis=-1), jnp.concatenate([-sin, sin], axis=-1)


def _gqa_attn_kernel(q_ref, *refs, rep, nk, n_kv):
    k_refs, v_refs, o_ref = refs[:n_kv], refs[n_kv:2 * n_kv], refs[2 * n_kv]
    tq = q_ref.shape[0]
    scale = LANES ** -0.5
    for kv in range(nk):
        sl = slice(kv * LANES, (kv + 1) * LANES)
        ks = [r[:, sl].astype(BF16) for r in k_refs]
        vs = [r[:, sl].astype(BF16) for r in v_refs]
        qs = jnp.concatenate([q_ref[:, (kv * rep + r) * LANES:(kv * rep + r + 1) * LANES] for r in range(rep)], axis=0)
        o = _softmax_pv([_qk(qs, k) * scale for k in ks], vs)
        for r in range(rep):
            o_ref[:, (kv * rep + r) * LANES:(kv * rep + r + 1) * LANES] = o[r * tq:(r + 1) * tq].astype(o_ref.dtype)


def _gqa_ctx_attn(q, k, v, batch, seq, nk):
    d = q.shape[1]
    dk = k.shape[1]
    rep = d // dk
    qspec = pl.BlockSpec((seq, d), lambda b: (b, 0))
    kspec = pl.BlockSpec((seq, dk), lambda b: (b, 0))
    return pl.pallas_call(
        functools.partial(_gqa_attn_kernel, rep=rep, nk=nk, n_kv=1),
        grid=(batch,),
        in_specs=[qspec, kspec, kspec],
        out_specs=qspec,
        out_shape=jax.ShapeDtypeStruct((batch * seq, d), BF16),
        compiler_params=_cparams("arbitrary"),
        name="gqa_ctx_attn",
    )(q, k, v)


def _gqa_lat_attn(q, k, v, cache_k, cache_v, slot, n_prompt, dec_batch, dec_seq, nk):
    d = q.shape[1]
    dk = k.shape[1]
    rep = d // dk
    past = cache_k.shape[2]
    tq = _pow2_tile(256, dec_seq, n_prompt)
    nqb = dec_seq // tq
    q0 = n_prompt // tq
    l0 = n_prompt // dec_seq
    lspec = pl.BlockSpec((dec_seq, dk), lambda b, qb: (l0 + b, 0))
    cspec = pl.BlockSpec((None, None, past, dk), lambda b, qb: (b, slot, 0, 0))
    return pl.pallas_call(
        functools.partial(_gqa_attn_kernel, rep=rep, nk=nk, n_kv=2),
        grid=(dec_batch, nqb),
        in_specs=[pl.BlockSpec((tq, d), lambda b, qb: (q0 + b * nqb + qb, 0)),
                  lspec, cspec, lspec, cspec],
        out_specs=pl.BlockSpec((tq, d), lambda b, qb: (b * nqb + qb, 0)),
        out_shape=jax.ShapeDtypeStruct((dec_batch * dec_seq, d), BF16),
        compiler_params=_cparams("arbitrary", "arbitrary"),
        name="gqa_lat_attn",
    )(q, k, cache_k, v, cache_v)


def _proj_res_kernel(x_ref, o_ref, w_ref, gate_ref, out_ref):
    y = jnp.dot(o_ref[...], w_ref[...].astype(BF16), preferred_element_type=F32)
    out_ref[...] = x_ref[...] + gate_ref[...] * y


def _proj_res(x, o, w_o, mod, layer, tok):
    n, d = x.shape
    row = lambda i: (i, 0)
    return pl.pallas_call(
        _proj_res_kernel,
        grid=(tok.tiles,),
        in_specs=[pl.BlockSpec((tok.tm, d), row),
                  pl.BlockSpec((tok.tm, d), row),
                  pl.BlockSpec((d, d), lambda i: (0, 0)),
                  tok.mod_spec(layer, 2, d, 1)],
        out_specs=pl.BlockSpec((tok.tm, d), row),
        out_shape=jax.ShapeDtypeStruct((n, d), F32),
        compiler_params=_cparams("arbitrary"),
        name="attn_proj_res",
    )(x, o, w_o, mod)


def _mlp_kernel(x_ref, g_ref, sc_ref, sh_ref, gate_ref, w1_ref, w2_ref, o_ref, h_scr, acc_scr):
    j = pl.program_id(1)

    @pl.when(j == 0)
    def _():
        h_scr[...] = _normmod(x_ref[...], g_ref[...], sc_ref[...], sh_ref[...]).astype(BF16)
        acc_scr[...] = jnp.zeros_like(acc_scr)

    a = jnp.maximum(jnp.dot(h_scr[...], w1_ref[...].astype(BF16), preferred_element_type=F32), 0.0)
    acc_scr[...] += jnp.dot((a * a).astype(BF16), w2_ref[...].astype(BF16), preferred_element_type=F32)

    @pl.when(j == pl.num_programs(1) - 1)
    def _():
        o_ref[...] = x_ref[...] + gate_ref[...] * acc_scr[...]


def _mlp(x, g, mod, layer, w1, w2, tok):
    n, d = x.shape
    f = w1.shape[1]
    tf = _pow2_tile(512, f)
    row = lambda i, j: (i, 0)
    return pl.pallas_call(
        _mlp_kernel,
        grid=(tok.tiles, f // tf),
        in_specs=[pl.BlockSpec((tok.tm, d), row),
                  pl.BlockSpec((1, d), lambda i, j: (0, 0)),
                  tok.mod_spec(layer, 4, d, 2),
                  tok.mod_spec(layer, 3, d, 2),
                  tok.mod_spec(layer, 5, d, 2),
                  pl.BlockSpec((d, tf), lambda i, j: (0, j)),
                  pl.BlockSpec((tf, d), lambda i, j: (j, 0))],
        out_specs=pl.BlockSpec((tok.tm, d), row),
        out_shape=jax.ShapeDtypeStruct((n, d), F32),
        scratch_shapes=[pltpu.VMEM((tok.tm, d), BF16), pltpu.VMEM((tok.tm, d), F32)],
        compiler_params=_cparams("arbitrary", "arbitrary"),
        name="mlp",
    )(x, g, mod, mod, mod, w1, w2)


def _s5_tables(lam_re, lam_im, log_dt, b_re, b_im, c_re, c_im):
    hi = lax.Precision.HIGHEST
    t = S5_CHUNK
    g, p, k = b_re.shape[1:]
    dt = jnp.exp(log_dt.astype(F32))[:, :, None]
    lr, li = lam_re.astype(F32), lam_im.astype(F32)
    ar, ai = lr * dt, li * dt
    mag = jnp.exp(ar)
    abr, abi = mag * jnp.cos(ai), mag * jnp.sin(ai)
    nr, ni = abr - 1.0, abi
    den = lr * lr + li * li
    f_re = (nr * lr + ni * li) / den
    f_im = (ni * lr - nr * li) / den
    bbr = f_re[..., None] * b_re - f_im[..., None] * b_im
    bbi = f_re[..., None] * b_im + f_im[..., None] * b_re
    n = jnp.arange(t + 1, dtype=F32)[:, None, None, None]
    pm = jnp.exp(n * ar[None])
    pr, pi = pm * jnp.cos(n * ai[None]), pm * jnp.sin(n * ai[None])
    cr, ci = c_re.astype(F32), c_im.astype(F32)
    car = cr[None] * pr[:, :, :, None, :] - ci[None] * pi[:, :, :, None, :]
    cai = cr[None] * pi[:, :, :, None, :] + ci[None] * pr[:, :, :, None, :]
    kern = (jnp.einsum('ndgkp,dgpj->ndgkj', car[:t], bbr, precision=hi)
            - jnp.einsum('ndgkp,dgpj->ndgkj', cai[:t], bbi, precision=hi))
    zpad = jnp.zeros((t - 1,) + kern.shape[2:], F32)
    k_all = jnp.concatenate([zpad, kern[:, 0]], axis=0) + jnp.concatenate([kern[::-1, 1], zpad], axis=0)
    idx = jnp.arange(t)[None, :] - jnp.arange(t)[:, None] + t - 1
    m = k_all[idx]
    m = m.transpose(2, 0, 4, 1, 3).reshape(g, t * k, t * k)
    def summ(pw_r, pw_i, d):
        sr = pw_r[:, :, :, None] * bbr[d][None] - pw_i[:, :, :, None] * bbi[d][None]
        si = pw_r[:, :, :, None] * bbi[d][None] + pw_i[:, :, :, None] * bbr[d][None]
        f = lambda a: a.transpose(1, 0, 3, 2).reshape(g, t * k, p)
        return f(sr), f(si)
    sfr, sfi = summ(pr[:t, 0][::-1], pi[:t, 0][::-1], 0)
    sbr, sbi = summ(pr[:t, 1], pi[:t, 1], 1)
    w1 = jnp.concatenate([m, sfr, sbr, sfi, sbi], axis=-1)
    def rd(a):
        return a.transpose(1, 3, 0, 2).reshape(g, p, t * k)
    w2 = jnp.concatenate([rd(car[1:, 0]), rd(car[1:, 1][::-1]), -rd(cai[1:, 0]), -rd(cai[1:, 1][::-1])], axis=1)
    at = jnp.stack([jnp.concatenate([pr[t, 0], pr[t, 1]], axis=-1),
                    jnp.concatenate([pi[t, 0], pi[t, 1]], axis=-1)], axis=1)
    return w1.astype(BF16), w2.astype(BF16), at


def _to_groups(h, batch, seq, bpad):
    d = h.shape[1]
    g = d // S5_GROUP
    c = seq // S5_CHUNK
    a = h.reshape(batch, c, S5_CHUNK, g, S5_GROUP).transpose(3, 1, 0, 2, 4)
    if bpad > batch:
        a = jnp.pad(a, ((0, 0), (0, 0), (0, bpad - batch), (0, 0), (0, 0)))
    return a.reshape(g, c * bpad, S5_CHUNK * S5_GROUP)


def _from_groups(y, batch, seq, bpad):
    g = y.shape[0]
    c = seq // S5_CHUNK
    a = y.reshape(g, c, bpad, S5_CHUNK, S5_GROUP)[:, :, :batch]
    return a.transpose(2, 1, 3, 0, 4).reshape(batch * seq, g * S5_GROUP)


def _s5_mixer(x, g, mod, layer, tok, params, h0, dims):
    (lam_re, lam_im, log_dt, b_re, b_im, c_re, c_im, d_skip, glu_w, glu_b) = params
    batch, seq, dec_batch, dec_seq = dims
    n_prompt = batch * seq
    d = x.shape[1]
    ngrp = d // S5_GROUP
    p = lam_re.shape[-1]
    w1, w2, at = _s5_tables(lam_re, lam_im, log_dt, b_re, b_im, c_re, c_im)
    h = _s5_pre(x, g, mod, layer, tok)
    bp = -(-batch // SUBLANES) * SUBLANES
    bs = -(-dec_batch // SUBLANES) * SUBLANES
    xgp = _to_groups(h[:n_prompt], batch, seq, bp)
    xgs = _to_groups(h[n_prompt:], dec_batch, dec_seq, bs)
    h0g = h0.astype(F32).transpose(3, 2, 0, 1, 4).reshape(ngrp, 2, dec_batch, 2 * p)
    h0g = jnp.pad(h0g, ((0, 0), (0, 0), (0, bs - dec_batch), (0, 0)))
    yp, ys, fin = _s5_scan(xgp, xgs, w1, w2, at, h0g, seq // S5_CHUNK, bp, dec_seq // S5_CHUNK, bs)
    y = jnp.concatenate([_from_groups(yp, batch, seq, bp), _from_groups(ys, dec_batch, dec_seq, bs)], axis=0)
    x_new = _s5_post(x, y, g, mod, layer, d_skip.reshape(1, d), glu_w, glu_b, tok)
    st = fin[:, :, :batch].reshape(ngrp, 2, batch, 2, p).transpose(2, 3, 1, 0, 4)
    return x_new, st


def kernel(x_prompt, x_sample, state_s5, cache_na_k, cache_na_v, cache_gqa_k, cache_gqa_v, c, c_ctx, norm_g, ada_w, ada_b, mlp_w1, mlp_w2, s5_lam_re, s5_lam_im, s5_log_dt, s5_b_re, s5_b_im, s5_c_re, s5_c_im, s5_d, s5_glu_w, s5_glu_b, na_w_qkv, na_q_norm, na_k_norm, na_rpb, na_w_o, gqa_w_qkv, gqa_q_norm, gqa_k_norm, gqa_w_o):
    batch, seq, d = x_prompt.shape
    dec_batch, dec_seq, _ = x_sample.shape
    depth = ada_w.shape[0]
    n_prompt = batch * seq
    n_sample = dec_batch * dec_seq
    na_heads, na_hd = cache_na_k.shape[3], cache_na_k.shape[4]
    gqa_kv, gqa_hd = cache_gqa_k.shape[3], cache_gqa_k.shape[4]
    assert n_prompt % dec_seq == 0

    tok = _Tok(n_prompt, n_sample, dec_seq, 1024)
    tok_half = _Tok(n_prompt, n_sample, dec_seq, 512)

    mod_rows = -(-(1 + dec_batch) // SUBLANES) * SUBLANES
    cvec = jnp.concatenate([c_ctx[None, :], c, jnp.zeros((mod_rows - 1 - dec_batch, d), F32)], axis=0)
    mod = _modulation(cvec, ada_w, ada_b).reshape(depth, mod_rows, 6, 1, d)

    x = jnp.concatenate([x_prompt.reshape(n_prompt, d), x_sample.reshape(n_sample, d)], axis=0)
    cache_na_k2 = cache_na_k.reshape(cache_na_k.shape[:3] + (na_heads * na_hd,))
    cache_na_v2 = cache_na_v.reshape(cache_na_v.shape[:3] + (na_heads * na_hd,))
    cache_gqa_k2 = cache_gqa_k.reshape(cache_gqa_k.shape[:3] + (gqa_kv * gqa_hd,))
    cache_gqa_v2 = cache_gqa_v.reshape(cache_gqa_v.shape[:3] + (gqa_kv * gqa_hd,))
    cos_t, sin_t = _rope_tables(dec_seq, gqa_hd)

    new_s5, new_na_k, new_na_v, new_gqa_k, new_gqa_v = [], [], [], [], []
    for i in range(depth):
        kind, slot = i % 3, i // 3
        g1 = norm_g[i, 0].reshape(1, d)
        g2 = norm_g[i, 1].reshape(1, d)
        if kind == 0:
            params = (s5_lam_re[slot], s5_lam_im[slot], s5_log_dt[slot], s5_b_re[slot], s5_b_im[slot],
                      s5_c_re[slot], s5_c_im[slot], s5_d[slot], s5_glu_w[slot], s5_glu_b[slot])
            x, st = _s5_mixer(x, g1, mod, i, tok_half, params, state_s5[:, slot], (batch, seq, dec_batch, dec_seq))
            new_s5.append(st)
        elif kind == 1:
            q, k, v = _na_qkv(x, g1, mod, i, na_w_qkv[slot], na_q_norm[slot], na_k_norm[slot], tok)
            o_p = _na_ctx_attn(q, k, v, batch, seq, na_hd)
            bias = _na_bias_table(na_rpb[slot], dec_seq)
            o_s = _na_lat_attn(q, k, v, cache_na_k2, cache_na_v2, slot, bias, n_prompt, dec_batch, dec_seq, na_hd)
            x = _proj_res(x, jnp.concatenate([o_p, o_s], axis=0), na_w_o[slot], mod, i, tok)
            new_na_k.append(k[:n_prompt].reshape(batch, seq, na_heads, na_hd))
            new_na_v.append(v[:n_prompt].reshape(batch, seq, na_heads, na_hd))
        else:
            q, k, v = _gqa_qkv(x, g1, mod, i, gqa_w_qkv[slot], gqa_q_norm[slot], gqa_k_norm[slot],
                               cos_t, sin_t, gqa_kv, tok_half)
            o_p = _gqa_ctx_attn(q, k, v, batch, seq, gqa_kv)
            o_s = _gqa_lat_attn(q, k, v, cache_gqa_k2, cache_gqa_v2, slot, n_prompt, dec_batch, dec_seq, gqa_kv)
            x = _proj_res(x, jnp.concatenate([o_p, o_s], axis=0), gqa_w_o[slot], mod, i, tok)
            new_gqa_k.append(k[:n_prompt].reshape(batch, seq, gqa_kv, gqa_hd))
            new_gqa_v.append(v[:n_prompt].reshape(batch, seq, gqa_kv, gqa_hd))
        x = _mlp(x, g2, mod, i, mlp_w1[i], mlp_w2[i], tok)
    return (x[:n_prompt].reshape(batch, seq, d), x[n_prompt:].reshape(dec_batch, dec_seq, d),
            jnp.stack(new_s5, axis=1), jnp.stack(new_na_k, axis=1), jnp.stack(new_na_v, axis=1),
            jnp.stack(new_gqa_k, axis=1), jnp.stack(new_gqa_v, axis=1))
```

```python
import functools
import math

import jax
import jax.numpy as jnp
from jax import lax
from jax.experimental import pallas as pl
from jax.experimental.pallas import tpu as pltpu

F32 = jnp.float32
BF16 = jnp.bfloat16

EPS = 1e-6
GRID_W = 64
S5_GROUP = 16
NA_KH = 8
NA_KW = 16
ROPE_THETA = 10000.0
S5_CHUNK = 16
LANES = 128
SUBLANES = 8
VMEM_LIMIT = 56 * 1024 * 1024


def _cparams(*sem):
    return pltpu.CompilerParams(dimension_semantics=sem, vmem_limit_bytes=VMEM_LIMIT)


def _pow2_tile(pref, *ns):
    t = pref
    while any(n % t for n in ns):
        t //= 2
    return t


def _normmod(x, g, sc, sh):
    ms = jnp.mean(x * x, axis=-1, keepdims=True)
    y = x * lax.rsqrt(ms + EPS) * g
    return y * (1.0 + sc) + sh


class _Tok:
    def __init__(self, n_prompt, n_sample, dec_seq, pref):
        self.tm = _pow2_tile(pref, n_prompt, dec_seq)
        self.n = n_prompt + n_sample
        self.tiles = self.n // self.tm
        self.np_tiles = n_prompt // self.tm
        self.tps = dec_seq // self.tm

    def mod_row(self, i):
        return jnp.where(i < self.np_tiles, 0, 1 + (i - self.np_tiles) // self.tps)

    def mod_spec(self, layer, which, d, nargs):
        if nargs == 1:
            return pl.BlockSpec((None, None, None, 1, d), lambda i: (layer, self.mod_row(i), which, 0, 0))
        return pl.BlockSpec((None, None, None, 1, d), lambda i, j: (layer, self.mod_row(i), which, 0, 0))


def _mod_kernel(c_ref, w_ref, b_ref, o_ref):
    c = c_ref[...]
    s = c * jax.nn.sigmoid(c)
    o_ref[...] = jnp.dot(s.astype(BF16), w_ref[...].astype(BF16), preferred_element_type=F32) + b_ref[...]


def _modulation(cvec, ada_w, ada_b):
    depth, d, d6 = ada_w.shape
    rows = cvec.shape[0]
    tn = _pow2_tile(2048, d6) if d6 % 2048 == 0 else d
    return pl.pallas_call(
        _mod_kernel,
        grid=(depth, d6 // tn),
        in_specs=[pl.BlockSpec((rows, d), lambda l, j: (0, 0)),
                  pl.BlockSpec((None, d, tn), lambda l, j: (l, 0, j)),
                  pl.BlockSpec((None, 1, tn), lambda l, j: (l, 0, j))],
        out_specs=pl.BlockSpec((None, rows, tn), lambda l, j: (l, 0, j)),
        out_shape=jax.ShapeDtypeStruct((depth, rows, d6), F32),
        compiler_params=_cparams("arbitrary", "arbitrary"),
        name="adaln_modulation",
    )(cvec, ada_w, ada_b.reshape(depth, 1, d6))


def _s5_pre_kernel(x_ref, g_ref, sc_ref, sh_ref, h_ref):
    h_ref[...] = _normmod(x_ref[...], g_ref[...], sc_ref[...], sh_ref[...]).astype(BF16)


def _s5_pre(x, g, mod, layer, tok):
    n, d = x.shape
    return pl.pallas_call(
        _s5_pre_kernel,
        grid=(tok.tiles,),
        in_specs=[pl.BlockSpec((tok.tm, d), lambda i: (i, 0)),
                  pl.BlockSpec((1, d), lambda i: (0, 0)),
                  tok.mod_spec(layer, 1, d, 1),
                  tok.mod_spec(layer, 0, d, 1)],
        out_specs=pl.BlockSpec((tok.tm, d), lambda i: (i, 0)),
        out_shape=jax.ShapeDtypeStruct((n, d), BF16),
        compiler_params=_cparams("arbitrary"),
        name="s5_pre",
    )(x, g, mod, mod)


def _s5_stream(x_ref, gi, w1, w2, ar, ai, hr, hi, n_chunks, bpad):
    p2 = ar.shape[-1]
    z = jnp.dot(x_ref[gi], w1, preferred_element_type=F32)
    tk = z.shape[1] - 2 * p2
    s_re = z[:, tk:tk + p2]
    s_im = z[:, tk + p2:]
    fwd = lax.broadcasted_iota(jnp.int32, (1, p2), 1) < (p2 // 2)
    ent_re, ent_im = [], []
    for t in range(n_chunks):
        ent_re.append(hr)
        ent_im.append(hi)
        a, b = t * bpad, (n_chunks - 1 - t) * bpad
        in_re = jnp.where(fwd, s_re[a:a + bpad], s_re[b:b + bpad])
        in_im = jnp.where(fwd, s_im[a:a + bpad], s_im[b:b + bpad])
        hr, hi = ar * hr - ai * hi + in_re, ar * hi + ai * hr + in_im
    e_re = jnp.concatenate([jnp.where(fwd, ent_re[c], ent_re[n_chunks - 1 - c]) for c in range(n_chunks)], axis=0)
    e_im = jnp.concatenate([jnp.where(fwd, ent_im[c], ent_im[n_chunks - 1 - c]) for c in range(n_chunks)], axis=0)
    e = jnp.concatenate([e_re, e_im], axis=1).astype(BF16)
    y = z[:, :tk] + jnp.dot(e, w2, preferred_element_type=F32)
    return y, hr, hi


def _s5_scan_kernel(xp_ref, xs_ref, w1_ref, w2_ref, at_ref, h0_ref, yp_ref, ys_ref, fin_ref, *,
                    gb, cp, bp, cs, bs):
    for gi in range(gb):
        w1 = w1_ref[gi]
        w2 = w2_ref[gi]
        ar = at_ref[gi, 0:1, :]
        ai = at_ref[gi, 1:2, :]
        zero = jnp.zeros((bp, ar.shape[-1]), F32)
        y, hr, hi = _s5_stream(xp_ref, gi, w1, w2, ar, ai, zero, zero, cp, bp)
        yp_ref[gi] = y
        fin_ref[gi, 0] = hr
        fin_ref[gi, 1] = hi
        y, _, _ = _s5_stream(xs_ref, gi, w1, w2, ar, ai, h0_ref[gi, 0], h0_ref[gi, 1], cs, bs)
        ys_ref[gi] = y


def _s5_scan(xgp, xgs, w1, w2, at, h0, cp, bp, cs, bs):
    g, rp, tk = xgp.shape
    rs = xgs.shape[1]
    p2 = at.shape[-1]
    gb = _pow2_tile(4, g)
    kern = functools.partial(_s5_scan_kernel, gb=gb, cp=cp, bp=bp, cs=cs, bs=bs)
    return pl.pallas_call(
        kern,
        grid=(g // gb,),
        in_specs=[pl.BlockSpec((gb, rp, tk), lambda i: (i, 0, 0)),
                  pl.BlockSpec((gb, rs, tk), lambda i: (i, 0, 0)),
                  pl.BlockSpec((gb, tk, tk + 2 * p2), lambda i: (i, 0, 0)),
                  pl.BlockSpec((gb, 2 * p2, tk), lambda i: (i, 0, 0)),
                  pl.BlockSpec((gb, 2, p2), lambda i: (i, 0, 0)),
                  pl.BlockSpec((gb, 2, bs, p2), lambda i: (i, 0, 0, 0))],
        out_specs=[pl.BlockSpec((gb, rp, tk), lambda i: (i, 0, 0)),
                   pl.BlockSpec((gb, rs, tk), lambda i: (i, 0, 0)),
                   pl.BlockSpec((gb, 2, bp, p2), lambda i: (i, 0, 0, 0))],
        out_shape=[jax.ShapeDtypeStruct((g, rp, tk), F32),
                   jax.ShapeDtypeStruct((g, rs, tk), F32),
                   jax.ShapeDtypeStruct((g, 2, bp, p2), F32)],
        compiler_params=_cparams("arbitrary"),
        name="s5_scan",
    )(xgp, xgs, w1, w2, at, h0)


def _s5_post_kernel(x_ref, y_ref, g_ref, sc_ref, sh_ref, gate_ref, dsk_ref, wa_ref, wg_ref, ba_ref, bg_ref, o_ref):
    x = x_ref[...]
    h = _normmod(x, g_ref[...], sc_ref[...], sh_ref[...])
    y = h * dsk_ref[...] + y_ref[...]
    a = jax.nn.gelu(y).astype(BF16)
    za = jnp.dot(a, wa_ref[...].astype(BF16), preferred_element_type=F32) + ba_ref[...]
    zg = jnp.dot(a, wg_ref[...].astype(BF16), preferred_element_type=F32) + bg_ref[...]
    o_ref[...] = x + gate_ref[...] * (za * jax.nn.sigmoid(zg))


def _s5_post(x, y, g, mod, layer, dsk, glu_w, glu_b, tok):
    n, d = x.shape
    row = lambda i: (i, 0)
    fixed = lambda i: (0, 0)
    gb2 = glu_b.reshape(1, 2 * d)
    return pl.pallas_call(
        _s5_post_kernel,
        grid=(tok.tiles,),
        in_specs=[pl.BlockSpec((tok.tm, d), row),
                  pl.BlockSpec((tok.tm, d), row),
                  pl.BlockSpec((1, d), fixed),
                  tok.mod_spec(layer, 1, d, 1),
                  tok.mod_spec(layer, 0, d, 1),
                  tok.mod_spec(layer, 2, d, 1),
                  pl.BlockSpec((1, d), fixed),
                  pl.BlockSpec((d, d), lambda i: (0, 0)),
                  pl.BlockSpec((d, d), lambda i: (0, 1)),
                  pl.BlockSpec((1, d), lambda i: (0, 0)),
                  pl.BlockSpec((1, d), lambda i: (0, 1))],
        out_specs=pl.BlockSpec((tok.tm, d), row),
        out_shape=jax.ShapeDtypeStruct((n, d), F32),
        compiler_params=_cparams("arbitrary"),
        name="s5_post",
    )(x, y, g, mod, mod, mod, dsk, glu_w, glu_w, gb2, gb2)


def _na_qkv_kernel(x_ref, g_ref, sc_ref, sh_ref, w_ref, qn_ref, kn_ref, q_ref, k_ref, v_ref, h_scr, *, hd):
    j = pl.program_id(1)

    @pl.when(j == 0)
    def _():
        h_scr[...] = _normmod(x_ref[...], g_ref[...], sc_ref[...], sh_ref[...]).astype(BF16)

    z = jnp.dot(h_scr[...], w_ref[...].astype(BF16), preferred_element_type=F32)
    d = z.shape[1]

    def head_norm(gain_ref, out_ref):
        lo = lax.broadcasted_iota(jnp.int32, (1, LANES), 1) < hd
        for s in range(d // LANES):
            seg = z[:, s * LANES:(s + 1) * LANES]
            sq = seg * seg
            s_lo = jnp.sum(jnp.where(lo, sq, 0.0), axis=-1, keepdims=True)
            s_hi = jnp.sum(jnp.where(lo, 0.0, sq), axis=-1, keepdims=True)
            ms = jnp.where(lo, s_lo, s_hi) / hd
            out_ref[:, s * LANES:(s + 1) * LANES] = (seg * lax.rsqrt(ms + EPS) * gain_ref[...]).astype(out_ref.dtype)

    @pl.when(j == 0)
    def _():
        head_norm(qn_ref, q_ref)

    @pl.when(j == 1)
    def _():
        head_norm(kn_ref, k_ref)

    @pl.when(j == 2)
    def _():
        v_ref[...] = z


def _na_qkv(x, g, mod, layer, w_qkv, qn, kn, tok):
    n, d = x.shape
    hd = qn.shape[-1]
    assert 2 * hd == LANES
    row = lambda i, j: (i, 0)
    fixed = lambda i, j: (0, 0)
    qn2 = jnp.tile(qn, 2).reshape(1, LANES)
    kn2 = jnp.tile(kn, 2).reshape(1, LANES)
    return pl.pallas_call(
        functools.partial(_na_qkv_kernel, hd=hd),
        grid=(tok.tiles, 3),
        in_specs=[pl.BlockSpec((tok.tm, d), row),
                  pl.BlockSpec((1, d), fixed),
                  tok.mod_spec(layer, 1, d, 2),
                  tok.mod_spec(layer, 0, d, 2),
                  pl.BlockSpec((d, d), lambda i, j: (0, j)),
                  pl.BlockSpec((1, LANES), fixed),
                  pl.BlockSpec((1, LANES), fixed)],
        out_specs=[pl.BlockSpec((tok.tm, d), row)] * 3,
        out_shape=[jax.ShapeDtypeStruct((n, d), BF16),
                   jax.ShapeDtypeStruct((n, d), F32),
                   jax.ShapeDtypeStruct((n, d), F32)],
        scratch_shapes=[pltpu.VMEM((tok.tm, d), BF16)],
        compiler_params=_cparams("arbitrary", "arbitrary"),
        name="na_qkv",
    )(x, g, mod, mod, w_qkv, qn2, kn2)


def _softmax_pv(s_parts, v_parts):
    m = s_parts[0].max(axis=-1, keepdims=True)
    for s in s_parts[1:]:
        m = jnp.maximum(m, s.max(axis=-1, keepdims=True))
    den = 0.0
    acc = 0.0
    for s, v in zip(s_parts, v_parts):
        e = jnp.exp(s - m)
        den = den + e.sum(axis=-1, keepdims=True)
        acc = acc + jnp.dot(e.astype(BF16), v, preferred_element_type=F32)
    return acc / den


def _qk(q, k):
    return lax.dot_general(q, k, (((1,), (1,)), ((), ())), preferred_element_type=F32)


def _na_ctx_attn_kernel(q_ref, k_ref, v_ref, o_ref, *, hd):
    lo = lax.broadcasted_iota(jnp.int32, (1, LANES), 1) < hd
    scale = hd ** -0.5
    for s in range(q_ref.shape[1] // LANES):
        sl = slice(s * LANES, (s + 1) * LANES)
        q = q_ref[:, sl]
        k = k_ref[:, sl].astype(BF16)
        v = v_ref[:, sl].astype(BF16)
        zero = jnp.zeros_like(q)
        o_lo = _softmax_pv([_qk(jnp.where(lo, q, zero), k) * scale], [v])
        o_hi = _softmax_pv([_qk(jnp.where(lo, zero, q), k) * scale], [v])
        o_ref[:, sl] = jnp.where(lo, o_lo, o_hi).astype(o_ref.dtype)


def _na_ctx_attn(q, k, v, batch, seq, hd):
    d = q.shape[1]
    spec = pl.BlockSpec((seq, d), lambda b: (b, 0))
    return pl.pallas_call(
        functools.partial(_na_ctx_attn_kernel, hd=hd),
        grid=(batch,),
        in_specs=[spec, spec, spec],
        out_specs=spec,
        out_shape=jax.ShapeDtypeStruct((batch * seq, d), BF16),
        compiler_params=_cparams("arbitrary"),
        name="na_ctx_attn",
    )(q, k, v)


def _na_lat_attn_kernel(q_ref, kl_ref, vl_ref, kc_ref, vc_ref, tab_ref, o_ref, *, hd, rows, kh):
    qb = pl.program_id(1)
    lane = lax.broadcasted_iota(jnp.int32, (1, LANES), 1)
    lo = lane < hd
    upper = (lane >= GRID_W).astype(jnp.int32)
    scale = hd ** -0.5
    q = q_ref[...]
    kl = kl_ref[...].astype(BF16)
    vl = vl_ref[...].astype(BF16)
    kc = kc_ref[...].astype(BF16)
    vc = vc_ref[...].astype(BF16)
    zero = jnp.zeros_like(q)
    q_rows = q.shape[0] // GRID_W

    def bias(hh):
        blocks = []
        for r in range(q_rows):
            rq = qb * q_rows + r
            rs = jnp.clip(rq - kh // 2, 0, rows - kh)
            tiles = []
            for kp in range(rows // 2):
                rk = 2 * kp + upper
                ok = (rk >= rs) & (rk < rs + kh)
                tiles.append(jnp.where(ok, tab_ref[hh, 2 * kp - rq + rows - 1], -jnp.inf))
            blocks.append(jnp.concatenate(tiles, axis=1))
        return jnp.concatenate(blocks, axis=0)

    outs = []
    for hh in range(2):
        qm = jnp.where(lo, q, zero) if hh == 0 else jnp.where(lo, zero, q)
        s_loc = _qk(qm, kl) * scale + bias(hh)
        s_ctx = _qk(qm, kc) * scale
        outs.append(_softmax_pv([s_loc, s_ctx], [vl, vc]))
    o_ref[...] = jnp.where(lo, outs[0], outs[1]).astype(o_ref.dtype)


def _na_lat_attn(q, k, v, cache_k, cache_v, slot, tab, n_prompt, dec_batch, dec_seq, hd):
    d = q.shape[1]
    past = cache_k.shape[2]
    rows = dec_seq // GRID_W
    kh = min(NA_KH, rows)
    tq = _pow2_tile(256, dec_seq, n_prompt)
    assert tq % GRID_W == 0 and rows % 2 == 0 and 2 * GRID_W == LANES
    nqb = dec_seq // tq
    q0 = n_prompt // tq
    l0 = n_prompt // dec_seq
    return pl.pallas_call(
        functools.partial(_na_lat_attn_kernel, hd=hd, rows=rows, kh=kh),
        grid=(d // LANES, nqb, dec_batch),
        in_specs=[pl.BlockSpec((tq, LANES), lambda hp, qb, b: (q0 + b * nqb + qb, hp)),
                  pl.BlockSpec((dec_seq, LANES), lambda hp, qb, b: (l0 + b, hp)),
                  pl.BlockSpec((dec_seq, LANES), lambda hp, qb, b: (l0 + b, hp)),
                  pl.BlockSpec((None, None, past, LANES), lambda hp, qb, b: (b, slot, 0, hp)),
                  pl.BlockSpec((None, None, past, LANES), lambda hp, qb, b: (b, slot, 0, hp)),
                  pl.BlockSpec((2,) + tab.shape[1:], lambda hp, qb, b: (hp, 0, 0, 0))],
        out_specs=pl.BlockSpec((tq, LANES), lambda hp, qb, b: (b * nqb + qb, hp)),
        out_shape=jax.ShapeDtypeStruct((dec_batch * dec_seq, d), BF16),
        compiler_params=_cparams("arbitrary", "arbitrary", "arbitrary"),
        name="na_lat_attn",
    )(q, k, v, cache_k, cache_v, tab)


def _na_bias_table(rpb, dec_seq):
    rows = dec_seq // GRID_W
    kh = min(NA_KH, rows)
    nh, nd, nc = rpb.shape
    c = jnp.arange(GRID_W)
    cs = jnp.clip(c - NA_KW // 2, 0, GRID_W - NA_KW)
    col_ok = (c[None, :] >= cs[:, None]) & (c[None, :] < cs[:, None] + NA_KW)
    dcol = c[None, :] - c[:, None] + NA_KW - 1
    onehot = (dcol[None] == jnp.arange(nc)[:, None, None]).astype(F32)
    tmp = jnp.einsum('hdc,cxy->hdxy', rpb.astype(F32), onehot, precision=lax.Precision.HIGHEST)
    tmp = jnp.where(col_ok[None, None], tmp, -jnp.inf)
    n_e = 2 * rows - 2
    front = rows - kh
    back = n_e + 1 - nd - front
    pad = lambda n: jnp.full((nh, n, GRID_W, GRID_W), -jnp.inf, F32)
    ext = jnp.concatenate([pad(front), tmp, pad(back)], axis=1)
    return jnp.concatenate([ext[:, :-1], ext[:, 1:]], axis=-1)


def _gqa_qkv_kernel(x_ref, g_ref, sc_ref, sh_ref, w_ref, qn_ref, kn_ref, cos_ref, sin_ref,
                    q_ref, k_ref, v_ref, *, nq, nk, np_tiles):
    i = pl.program_id(0)
    h = _normmod(x_ref[...], g_ref[...], sc_ref[...], sh_ref[...]).astype(BF16)
    z = jnp.dot(h, w_ref[...].astype(BF16), preferred_element_type=F32)
    is_sample = i >= np_tiles
    cos = cos_ref[...]
    sin = sin_ref[...]

    def norm_rope(seg, gain):
        ms = jnp.mean(seg * seg, axis=-1, keepdims=True)
        y = seg * lax.rsqrt(ms + EPS) * gain
        roped = y * cos + pltpu.roll(y, LANES // 2, 1) * sin
        return jnp.where(is_sample, roped, y)

    for hh in range(nq):
        sl = slice(hh * LANES, (hh + 1) * LANES)
        q_ref[:, sl] = norm_rope(z[:, sl], qn_ref[...]).astype(q_ref.dtype)
    for hh in range(nk):
        k_ref[:, hh * LANES:(hh + 1) * LANES] = norm_rope(z[:, (nq + hh) * LANES:(nq + hh + 1) * LANES], kn_ref[...])
    v_ref[...] = z[:, (nq + nk) * LANES:]


def _gqa_qkv(x, g, mod, layer, w_qkv, qn, kn, cos_t, sin_t, nk, tok):
    n, d = x.shape
    hd = qn.shape[-1]
    assert hd == LANES
    nq = d // hd
    dk = nk * hd
    row = lambda i: (i, 0)
    fixed = lambda i: (0, 0)
    pos = lambda i: (jnp.maximum(i - tok.np_tiles, 0) % tok.tps, 0)
    return pl.pallas_call(
        functools.partial(_gqa_qkv_kernel, nq=nq, nk=nk, np_tiles=tok.np_tiles),
        grid=(tok.tiles,),
        in_specs=[pl.BlockSpec((tok.tm, d), row),
                  pl.BlockSpec((1, d), fixed),
                  tok.mod_spec(layer, 1, d, 1),
                  tok.mod_spec(layer, 0, d, 1),
                  pl.BlockSpec((d, d + 2 * dk), fixed),
                  pl.BlockSpec((1, hd), fixed),
                  pl.BlockSpec((1, hd), fixed),
                  pl.BlockSpec((tok.tm, hd), pos),
                  pl.BlockSpec((tok.tm, hd), pos)],
        out_specs=[pl.BlockSpec((tok.tm, d), row),
                   pl.BlockSpec((tok.tm, dk), row),
                   pl.BlockSpec((tok.tm, dk), row)],
        out_shape=[jax.ShapeDtypeStruct((n, d), BF16),
                   jax.ShapeDtypeStruct((n, dk), F32),
                   jax.ShapeDtypeStruct((n, dk), F32)],
        compiler_params=_cparams("arbitrary"),
        name="gqa_qkv",
    )(x, g, mod, mod, w_qkv, qn.reshape(1, hd), kn.reshape(1, hd), cos_t, sin_t)


def _rope_tables(dec_seq, hd):
    t = jnp.arange(dec_seq)
    row = (t // GRID_W).astype(F32)
    col = (t % GRID_W).astype(F32)
    half = hd // 2
    inv = ROPE_THETA ** (-jnp.arange(0, half, 2, dtype=F32) / half)
    ang = jnp.concatenate([row[:, None] * inv, col[:, None] * inv], axis=-1)
    cos, sin = jnp.cos(ang), jnp.sin(ang)
    return jnp.concatenate([cos, cos], axis=-1), jnp.concatenate([-sin, sin], axis=-1)


def _gqa_attn_kernel(q_ref, *refs, rep, nk, n_kv):
    k_refs, v_refs, o_ref = refs[:n_kv], refs[n_kv:2 * n_kv], refs[2 * n_kv]
    tq = q_ref.shape[0]
    scale = LANES ** -0.5
    for kv in range(nk):
        sl = slice(kv * LANES, (kv + 1) * LANES)
        ks = [r[:, sl].astype(BF16) for r in k_refs]
        vs = [r[:, sl].astype(BF16) for r in v_refs]
        qs = jnp.concatenate([q_ref[:, (kv * rep + r) * LANES:(kv * rep + r + 1) * LANES] for r in range(rep)], axis=0)
        o = _softmax_pv([_qk(qs, k) * scale for k in ks], vs)
        for r in range(rep):
            o_ref[:, (kv * rep + r) * LANES:(kv * rep + r + 1) * LANES] = o[r * tq:(r + 1) * tq].astype(o_ref.dtype)


def _gqa_ctx_attn(q, k, v, batch, seq, nk):
    d = q.shape[1]
    dk = k.shape[1]
    rep = d // dk
    qspec = pl.BlockSpec((seq, d), lambda b: (b, 0))
    kspec = pl.BlockSpec((seq, dk), lambda b: (b, 0))
    return pl.pallas_call(
        functools.partial(_gqa_attn_kernel, rep=rep, nk=nk, n_kv=1),
        grid=(batch,),
        in_specs=[qspec, kspec, kspec],
        out_specs=qspec,
        out_shape=jax.ShapeDtypeStruct((batch * seq, d), BF16),
        compiler_params=_cparams("arbitrary"),
        name="gqa_ctx_attn",
    )(q, k, v)


def _gqa_lat_attn(q, k, v, cache_k, cache_v, slot, n_prompt, dec_batch, dec_seq, nk):
    d = q.shape[1]
    dk = k.shape[1]
    rep = d // dk
    past = cache_k.shape[2]
    tq = _pow2_tile(256, dec_seq, n_prompt)
    nqb = dec_seq // tq
    q0 = n_prompt // tq
    l0 = n_prompt // dec_seq
    lspec = pl.BlockSpec((dec_seq, dk), lambda b, qb: (l0 + b, 0))
    cspec = pl.BlockSpec((None, None, past, dk), lambda b, qb: (b, slot, 0, 0))
    return pl.pallas_call(
        functools.partial(_gqa_attn_kernel, rep=rep, nk=nk, n_kv=2),
        grid=(dec_batch, nqb),
        in_specs=[pl.BlockSpec((tq, d), lambda b, qb: (q0 + b * nqb + qb, 0)),
                  lspec, cspec, lspec, cspec],
        out_specs=pl.BlockSpec((tq, d), lambda b, qb: (b * nqb + qb, 0)),
        out_shape=jax.ShapeDtypeStruct((dec_batch * dec_seq, d), BF16),
        compiler_params=_cparams("arbitrary", "arbitrary"),
        name="gqa_lat_attn",
    )(q, k, cache_k, v, cache_v)


def _proj_res_kernel(x_ref, o_ref, w_ref, gate_ref, out_ref):
    y = jnp.dot(o_ref[...], w_ref[...].astype(BF16), preferred_element_type=F32)
    out_ref[...] = x_ref[...] + gate_ref[...] * y


def _proj_res(x, o, w_o, mod, layer, tok):
    n, d = x.shape
    row = lambda i: (i, 0)
    return pl.pallas_call(
        _proj_res_kernel,
        grid=(tok.tiles,),
        in_specs=[pl.BlockSpec((tok.tm, d), row),
                  pl.BlockSpec((tok.tm, d), row),
                  pl.BlockSpec((d, d), lambda i: (0, 0)),
                  tok.mod_spec(layer, 2, d, 1)],
        out_specs=pl.BlockSpec((tok.tm, d), row),
        out_shape=jax.ShapeDtypeStruct((n, d), F32),
        compiler_params=_cparams("arbitrary"),
        name="attn_proj_res",
    )(x, o, w_o, mod)


def _mlp_kernel(x_ref, g_ref, sc_ref, sh_ref, gate_ref, w1_ref, w2_ref, o_ref, h_scr, acc_scr):
    j = pl.program_id(1)

    @pl.when(j == 0)
    def _():
        h_scr[...] = _normmod(x_ref[...], g_ref[...], sc_ref[...], sh_ref[...]).astype(BF16)
        acc_scr[...] = jnp.zeros_like(acc_scr)

    a = jnp.maximum(jnp.dot(h_scr[...], w1_ref[...].astype(BF16), preferred_element_type=F32), 0.0)
    acc_scr[...] += jnp.dot((a * a).astype(BF16), w2_ref[...].astype(BF16), preferred_element_type=F32)

    @pl.when(j == pl.num_programs(1) - 1)
    def _():
        o_ref[...] = x_ref[...] + gate_ref[...] * acc_scr[...]


def _mlp(x, g, mod, layer, w1, w2, tok):
    n, d = x.shape
    f = w1.shape[1]
    tf = _pow2_tile(512, f)
    row = lambda i, j: (i, 0)
    return pl.pallas_call(
        _mlp_kernel,
        grid=(tok.tiles, f // tf),
        in_specs=[pl.BlockSpec((tok.tm, d), row),
                  pl.BlockSpec((1, d), lambda i, j: (0, 0)),
                  tok.mod_spec(layer, 4, d, 2),
                  tok.mod_spec(layer, 3, d, 2),
                  tok.mod_spec(layer, 5, d, 2),
                  pl.BlockSpec((d, tf), lambda i, j: (0, j)),
                  pl.BlockSpec((tf, d), lambda i, j: (j, 0))],
        out_specs=pl.BlockSpec((tok.tm, d), row),
        out_shape=jax.ShapeDtypeStruct((n, d), F32),
        scratch_shapes=[pltpu.VMEM((tok.tm, d), BF16), pltpu.VMEM((tok.tm, d), F32)],
        compiler_params=_cparams("arbitrary", "arbitrary"),
        name="mlp",
    )(x, g, mod, mod, mod, w1, w2)


def _s5_tables(lam_re, lam_im, log_dt, b_re, b_im, c_re, c_im):
    hi = lax.Precision.HIGHEST
    t = S5_CHUNK
    g, p, k = b_re.shape[1:]
    dt = jnp.exp(log_dt.astype(F32))[:, :, None]
    lr, li = lam_re.astype(F32), lam_im.astype(F32)
    ar, ai = lr * dt, li * dt
    mag = jnp.exp(ar)
    abr, abi = mag * jnp.cos(ai), mag * jnp.sin(ai)
    nr, ni = abr - 1.0, abi
    den = lr * lr + li * li
    f_re = (nr * lr + ni * li) / den
    f_im = (ni * lr - nr * li) / den
    bbr = f_re[..., None] * b_re - f_im[..., None] * b_im
    bbi = f_re[..., None] * b_im + f_im[..., None] * b_re
    n = jnp.arange(t + 1, dtype=F32)[:, None, None, None]
    pm = jnp.exp(n * ar[None])
    pr, pi = pm * jnp.cos(n * ai[None]), pm * jnp.sin(n * ai[None])
    cr, ci = c_re.astype(F32), c_im.astype(F32)
    car = cr[None] * pr[:, :, :, None, :] - ci[None] * pi[:, :, :, None, :]
    cai = cr[None] * pi[:, :, :, None, :] + ci[None] * pr[:, :, :, None, :]
    kern = (jnp.einsum('ndgkp,dgpj->ndgkj', car[:t], bbr, precision=hi)
            - jnp.einsum('ndgkp,dgpj->ndgkj', cai[:t], bbi, precision=hi))
    zpad = jnp.zeros((t - 1,) + kern.shape[2:], F32)
    k_all = jnp.concatenate([zpad, kern[:, 0]], axis=0) + jnp.concatenate([kern[::-1, 1], zpad], axis=0)
    idx = jnp.arange(t)[None, :] - jnp.arange(t)[:, None] + t - 1
    m = k_all[idx]
    m = m.transpose(2, 0, 4, 1, 3).reshape(g, t * k, t * k)
    def summ(pw_r, pw_i, d):
        sr = pw_r[:, :, :, None] * bbr[d][None] - pw_i[:, :, :, None] * bbi[d][None]
        si = pw_r[:, :, :, None] * bbi[d][None] + pw_i[:, :, :, None] * bbr[d][None]
        f = lambda a: a.transpose(1, 0, 3, 2).reshape(g, t * k, p)
        return f(sr), f(si)
    sfr, sfi = summ(pr[:t, 0][::-1], pi[:t, 0][::-1], 0)
    sbr, sbi = summ(pr[:t, 1], pi[:t, 1], 1)
    w1 = jnp.concatenate([m, sfr, sbr, sfi, sbi], axis=-1)
    def rd(a):
        return a.transpose(1, 3, 0, 2).reshape(g, p, t * k)
    w2 = jnp.concatenate([rd(car[1:, 0]), rd(car[1:, 1][::-1]), -rd(cai[1:, 0]), -rd(cai[1:, 1][::-1])], axis=1)
    at = jnp.stack([jnp.concatenate([pr[t, 0], pr[t, 1]], axis=-1),
                    jnp.concatenate([pi[t, 0], pi[t, 1]], axis=-1)], axis=1)
    return w1.astype(BF16), w2.astype(BF16), at


def _to_groups(h, batch, seq, bpad):
    d = h.shape[1]
    g = d // S5_GROUP
    c = seq // S5_CHUNK
    a = h.reshape(batch, c, S5_CHUNK, g, S5_GROUP).transpose(3, 1, 0, 2, 4)
    if bpad > batch:
        a = jnp.pad(a, ((0, 0), (0, 0), (0, bpad - batch), (0, 0), (0, 0)))
    return a.reshape(g, c * bpad, S5_CHUNK * S5_GROUP)


def _from_groups(y, batch, seq, bpad):
    g = y.shape[0]
    c = seq // S5_CHUNK
    a = y.reshape(g, c, bpad, S5_CHUNK, S5_GROUP)[:, :, :batch]
    return a.transpose(2, 1, 3, 0, 4).reshape(batch * seq, g * S5_GROUP)


def _s5_mixer(x, g, mod, layer, tok, params, h0, dims):
    (lam_re, lam_im, log_dt, b_re, b_im, c_re, c_im, d_skip, glu_w, glu_b) = params
    batch, seq, dec_batch, dec_seq = dims
    n_prompt = batch * seq
    d = x.shape[1]
    ngrp = d // S5_GROUP
    p = lam_re.shape[-1]
    w1, w2, at = _s5_tables(lam_re, lam_im, log_dt, b_re, b_im, c_re, c_im)
    h = _s5_pre(x, g, mod, layer, tok)
    bp = -(-batch // SUBLANES) * SUBLANES
    bs = -(-dec_batch // SUBLANES) * SUBLANES
    xgp = _to_groups(h[:n_prompt], batch, seq, bp)
    xgs = _to_groups(h[n_prompt:], dec_batch, dec_seq, bs)
    h0g = h0.astype(F32).transpose(3, 2, 0, 1, 4).reshape(ngrp, 2, dec_batch, 2 * p)
    h0g = jnp.pad(h0g, ((0, 0), (0, 0), (0, bs - dec_batch), (0, 0)))
    yp, ys, fin = _s5_scan(xgp, xgs, w1, w2, at, h0g, seq // S5_CHUNK, bp, dec_seq // S5_CHUNK, bs)
    y = jnp.concatenate([_from_groups(yp, batch, seq, bp), _from_groups(ys, dec_batch, dec_seq, bs)], axis=0)
    x_new = _s5_post(x, y, g, mod, layer, d_skip.reshape(1, d), glu_w, glu_b, tok)
    st = fin[:, :, :batch].reshape(ngrp, 2, batch, 2, p).transpose(2, 3, 1, 0, 4)
    return x_new, st


def kernel(x_prompt, x_sample, state_s5, cache_na_k, cache_na_v, cache_gqa_k, cache_gqa_v, c, c_ctx, norm_g, ada_w, ada_b, mlp_w1, mlp_w2, s5_lam_re, s5_lam_im, s5_log_dt, s5_b_re, s5_b_im, s5_c_re, s5_c_im, s5_d, s5_glu_w, s5_glu_b, na_w_qkv, na_q_norm, na_k_norm, na_rpb, na_w_o, gqa_w_qkv, gqa_q_norm, gqa_k_norm, gqa_w_o):
    batch, seq, d = x_prompt.shape
    dec_batch, dec_seq, _ = x_sample.shape
    depth = ada_w.shape[0]
    n_prompt = batch * seq
    n_sample = dec_batch * dec_seq
    na_heads, na_hd = cache_na_k.shape[3], cache_na_k.shape[4]
    gqa_kv, gqa_hd = cache_gqa_k.shape[3], cache_gqa_k.shape[4]
    assert n_prompt % dec_seq == 0

    tok = _Tok(n_prompt, n_sample, dec_seq, 1024)
    tok_half = _Tok(n_prompt, n_sample, dec_seq, 512)

    mod_rows = -(-(1 + dec_batch) // SUBLANES) * SUBLANES
    cvec = jnp.concatenate([c_ctx[None, :], c, jnp.zeros((mod_rows - 1 - dec_batch, d), F32)], axis=0)
    mod = _modulation(cvec, ada_w, ada_b).reshape(depth, mod_rows, 6, 1, d)

    x = jnp.concatenate([x_prompt.reshape(n_prompt, d), x_sample.reshape(n_sample, d)], axis=0)
    cache_na_k2 = cache_na_k.reshape(cache_na_k.shape[:3] + (na_heads * na_hd,))
    cache_na_v2 = cache_na_v.reshape(cache_na_v.shape[:3] + (na_heads * na_hd,))
    cache_gqa_k2 = cache_gqa_k.reshape(cache_gqa_k.shape[:3] + (gqa_kv * gqa_hd,))
    cache_gqa_v2 = cache_gqa_v.reshape(cache_gqa_v.shape[:3] + (gqa_kv * gqa_hd,))
    cos_t, sin_t = _rope_tables(dec_seq, gqa_hd)

    new_s5, new_na_k, new_na_v, new_gqa_k, new_gqa_v = [], [], [], [], []
    for i in range(depth):
        kind, slot = i % 3, i // 3
        g1 = norm_g[i, 0].reshape(1, d)
        g2 = norm_g[i, 1].reshape(1, d)
        if kind == 0:
            params = (s5_lam_re[slot], s5_lam_im[slot], s5_log_dt[slot], s5_b_re[slot], s5_b_im[slot],
                      s5_c_re[slot], s5_c_im[slot], s5_d[slot], s5_glu_w[slot], s5_glu_b[slot])
            x, st = _s5_mixer(x, g1, mod, i, tok_half, params, state_s5[:, slot], (batch, seq, dec_batch, dec_seq))
            new_s5.append(st)
        elif kind == 1:
            q, k, v = _na_qkv(x, g1, mod, i, na_w_qkv[slot], na_q_norm[slot], na_k_norm[slot], tok)
            o_p = _na_ctx_attn(q, k, v, batch, seq, na_hd)
            bias = _na_bias_table(na_rpb[slot], dec_seq)
            o_s = _na_lat_attn(q, k, v, cache_na_k2, cache_na_v2, slot, bias, n_prompt, dec_batch, dec_seq, na_hd)
            x = _proj_res(x, jnp.concatenate([o_p, o_s], axis=0), na_w_o[slot], mod, i, tok)
            new_na_k.append(k[:n_prompt].reshape(batch, seq, na_heads, na_hd))
            new_na_v.append(v[:n_prompt].reshape(batch, seq, na_heads, na_hd))
        else:
            q, k, v = _gqa_qkv(x, g1, mod, i, gqa_w_qkv[slot], gqa_q_norm[slot], gqa_k_norm[slot],
                               cos_t, sin_t, gqa_kv, tok_half)
            o_p = _gqa_ctx_attn(q, k, v, batch, seq, gqa_kv)
            o_s = _gqa_lat_attn(q, k, v, cache_gqa_k2, cache_gqa_v2, slot, n_prompt, dec_batch, dec_seq, gqa_kv)
            x = _proj_res(x, jnp.concatenate([o_p, o_s], axis=0), gqa_w_o[slot], mod, i, tok)
            new_gqa_k.append(k[:n_prompt].reshape(batch, seq, gqa_kv, gqa_hd))
            new_gqa_v.append(v[:n_prompt].reshape(batch, seq, gqa_kv, gqa_hd))
        x = _mlp(x, g2, mod, i, mlp_w1[i], mlp_w2[i], tok)
    return (x[:n_prompt].reshape(batch, seq, d), x[n_prompt:].reshape(dec_batch, dec_seq, d),
            jnp.stack(new_s5, axis=1), jnp.stack(new_na_k, axis=1), jnp.stack(new_na_v, axis=1),
            jnp.stack(new_gqa_k, axis=1), jnp.stack(new_gqa_v, axis=1))
```

```python
import functools
import math

import jax
import jax.numpy as jnp
from jax import lax
from jax.experimental import pallas as pl
from jax.experimental.pallas import tpu as pltpu

F32 = jnp.float32
BF16 = jnp.bfloat16

EPS = 1e-6
GRID_W = 64
S5_GROUP = 16
NA_KH = 8
NA_KW = 16
ROPE_THETA = 10000.0
S5_CHUNK = 16
LANES = 128
SUBLANES = 8
VMEM_LIMIT = 56 * 1024 * 1024


def _cparams(*sem):
    return pltpu.CompilerParams(dimension_semantics=sem, vmem_limit_bytes=VMEM_LIMIT)


def _pow2_tile(pref, *ns):
    t = pref
    while any(n % t for n in ns):
        t //= 2
    return t


def _normmod(x, g, sc, sh):
    ms = jnp.mean(x * x, axis=-1, keepdims=True)
    y = x * lax.rsqrt(ms + EPS) * g
    return y * (1.0 + sc) + sh


class _Tok:
    def __init__(self, n_prompt, n_sample, dec_seq, pref):
        self.tm = _pow2_tile(pref, n_prompt, dec_seq)
        self.n = n_prompt + n_sample
        self.tiles = self.n // self.tm
        self.np_tiles = n_prompt // self.tm
        self.tps = dec_seq // self.tm

    def mod_row(self, i):
        return jnp.where(i < self.np_tiles, 0, 1 + (i - self.np_tiles) // self.tps)

    def mod_spec(self, layer, which, d, nargs):
        if nargs == 1:
            return pl.BlockSpec((None, None, None, 1, d), lambda i: (layer, self.mod_row(i), which, 0, 0))
        return pl.BlockSpec((None, None, None, 1, d), lambda i, j: (layer, self.mod_row(i), which, 0, 0))


def _mod_kernel(c_ref, w_ref, b_ref, o_ref):
    c = c_ref[...]
    s = c * jax.nn.sigmoid(c)
    o_ref[...] = jnp.dot(s.astype(BF16), w_ref[...].astype(BF16), preferred_element_type=F32) + b_ref[...]


def _modulation(cvec, ada_w, ada_b):
    depth, d, d6 = ada_w.shape
    rows = cvec.shape[0]
    tn = _pow2_tile(2048, d6) if d6 % 2048 == 0 else d
    return pl.pallas_call(
        _mod_kernel,
        grid=(depth, d6 // tn),
        in_specs=[pl.BlockSpec((rows, d), lambda l, j: (0, 0)),
                  pl.BlockSpec((None, d, tn), lambda l, j: (l, 0, j)),
                  pl.BlockSpec((None, 1, tn), lambda l, j: (l, 0, j))],
        out_specs=pl.BlockSpec((None, rows, tn), lambda l, j: (l, 0, j)),
        out_shape=jax.ShapeDtypeStruct((depth, rows, d6), F32),
        compiler_params=_cparams("arbitrary", "arbitrary"),
        name="adaln_modulation",
    )(cvec, ada_w, ada_b.reshape(depth, 1, d6))


def _s5_pre_kernel(x_ref, g_ref, sc_ref, sh_ref, h_ref, h_scr):
    h = _normmod(x_ref[...], g_ref[...], sc_ref[...], sh_ref[...])
    nc = h_ref.shape[1]
    for a in range(h_scr.shape[0]):
        sl = slice(a * LANES, (a + 1) * LANES)
        h_scr[a] = h[:, sl]
        for j in range(S5_CHUNK):
            h_ref[j, :, sl] = h_scr[a, pl.ds(j, nc, stride=S5_CHUNK), :]


def _s5_pre(x, g, mod, layer, tok):
    n, d = x.shape
    nc = tok.tm // S5_CHUNK
    return pl.pallas_call(
        _s5_pre_kernel,
        grid=(tok.tiles,),
        in_specs=[pl.BlockSpec((tok.tm, d), lambda i: (i, 0)),
                  pl.BlockSpec((1, d), lambda i: (0, 0)),
                  tok.mod_spec(layer, 1, d, 1),
                  tok.mod_spec(layer, 0, d, 1)],
        out_specs=pl.BlockSpec((S5_CHUNK, nc, d), lambda i: (0, i, 0)),
        out_shape=jax.ShapeDtypeStruct((S5_CHUNK, n // S5_CHUNK, d), F32),
        scratch_shapes=[pltpu.VMEM((d // LANES, tok.tm, LANES), F32)],
        compiler_params=_cparams("arbitrary"),
        name="s5_pre",
    )(x, g, mod, mod)


def _s5_scan_kernel(h_ref, w1t_ref, w2t_ref, at_ref, h0_ref, y_ref, fin_ref, xt_scr, yt_scr, s_scr, e_scr, *, streams):
    ngrp = w1t_ref.shape[0]
    tk = w2t_ref.shape[1]
    p2 = at_ref.shape[-1]
    fwd = lax.broadcasted_iota(jnp.int32, (1, p2), 1) < (p2 // 2)
    for row0, n_chunks, nb, use_h0, write_fin in streams:
        rows = n_chunks * nb
        for j in range(S5_CHUNK):
            xt_scr[j, :, 0:rows] = h_ref[j, row0:row0 + rows, :].T.astype(BF16)

        def group(r, carry):
            c0 = pl.multiple_of(r * S5_GROUP, S5_GROUP)
            rmat = jnp.concatenate([xt_scr[j, pl.ds(c0, S5_GROUP), 0:rows] for j in range(S5_CHUNK)], axis=0)
            zt = jnp.dot(w1t_ref[r], rmat, preferred_element_type=F32)
            s_scr[0, 0:rows, :] = zt[tk:tk + p2].T
            s_scr[1, 0:rows, :] = zt[tk + p2:].T
            ar = at_ref[r, 0:1, :]
            ai = at_ref[r, 1:2, :]
            if use_h0:
                hr, hi = h0_ref[r, 0], h0_ref[r, 1]
            else:
                hr = hi = jnp.zeros((nb, p2), F32)
            for t in range(n_chunks):
                ft = pl.ds(t, nb, stride=n_chunks)
                bt = pl.ds(n_chunks - 1 - t, nb, stride=n_chunks)
                e_scr[0, ft, :] = hr
                e_scr[1, bt, :] = hr
                e_scr[2, ft, :] = hi
                e_scr[3, bt, :] = hi
                in_re = jnp.where(fwd, s_scr[0, ft, :], s_scr[0, bt, :])
                in_im = jnp.where(fwd, s_scr[1, ft, :], s_scr[1, bt, :])
                hr, hi = ar * hr - ai * hi + in_re, ar * hi + ai * hr + in_im
            if write_fin:
                fin_ref[r, 0] = hr
                fin_ref[r, 1] = hi
            e = jnp.concatenate([jnp.where(fwd, e_scr[0, 0:rows, :], e_scr[1, 0:rows, :]),
                                 jnp.where(fwd, e_scr[2, 0:rows, :], e_scr[3, 0:rows, :])], axis=1).astype(BF16)
            yt = zt[0:tk] + lax.dot_general(w2t_ref[r], e, (((1,), (1,)), ((), ())), preferred_element_type=F32)
            for i in range(S5_CHUNK):
                yt_scr[i, pl.ds(c0, S5_GROUP), 0:rows] = yt[i * S5_GROUP:(i + 1) * S5_GROUP]
            return carry

        lax.fori_loop(0, ngrp, group, 0)
        for i in range(S5_CHUNK):
            y_ref[i, row0:row0 + rows, :] = yt_scr[i, :, 0:rows].T


def _s5_scan(hperm, w1t, w2t, at, h0, nb_fin, streams):
    t, nrows, d = hperm.shape
    g, tk, p4 = w2t.shape
    p2 = at.shape[-1]
    gl = LANES // S5_GROUP
    nb0 = h0.shape[2]
    max_rows = max(s[1] * s[2] for s in streams)
    return pl.pallas_call(
        functools.partial(_s5_scan_kernel, streams=streams),
        grid=(d // LANES,),
        in_specs=[pl.BlockSpec((t, nrows, LANES), lambda a: (0, 0, a)),
                  pl.BlockSpec((gl, tk + p4, tk), lambda a: (a, 0, 0)),
                  pl.BlockSpec((gl, tk, p4), lambda a: (a, 0, 0)),
                  pl.BlockSpec((gl, 2, p2), lambda a: (a, 0, 0)),
                  pl.BlockSpec((gl, 2, nb0, p2), lambda a: (a, 0, 0, 0))],
        out_specs=[pl.BlockSpec((t, nrows, LANES), lambda a: (0, 0, a)),
                   pl.BlockSpec((gl, 2, nb_fin, p2), lambda a: (a, 0, 0, 0))],
        out_shape=[jax.ShapeDtypeStruct((t, nrows, d), F32),
                   jax.ShapeDtypeStruct((g, 2, nb_fin, p2), F32)],
        scratch_shapes=[pltpu.VMEM((t, LANES, max_rows), BF16),
                        pltpu.VMEM((t, LANES, max_rows), F32),
                        pltpu.VMEM((2, max_rows, p2), F32),
                        pltpu.VMEM((4, max_rows, p2), F32)],
        compiler_params=_cparams("arbitrary"),
        name="s5_scan",
    )(hperm, w1t, w2t, at, h0)


def _s5_post_kernel(x_ref, y_ref, g_ref, sc_ref, sh_ref, gate_ref, dsk_ref, wa_ref, wg_ref, ba_ref, bg_ref, o_ref,
                    y_scr):
    nc = y_ref.shape[1]
    for a in range(y_scr.shape[0]):
        for j in range(S5_CHUNK):
            y_scr[a, pl.ds(j, nc, stride=S5_CHUNK), :] = y_ref[j, :, a * LANES:(a + 1) * LANES]
    x = x_ref[...]
    h = _normmod(x, g_ref[...], sc_ref[...], sh_ref[...])
    y = h * dsk_ref[...] + jnp.concatenate([y_scr[a] for a in range(y_scr.shape[0])], axis=1)
    a = jax.nn.gelu(y).astype(BF16)
    za = jnp.dot(a, wa_ref[...].astype(BF16), preferred_element_type=F32) + ba_ref[...]
    zg = jnp.dot(a, wg_ref[...].astype(BF16), preferred_element_type=F32) + bg_ref[...]
    o_ref[...] = x + gate_ref[...] * (za * jax.nn.sigmoid(zg))


def _s5_post(x, y, g, mod, layer, dsk, glu_w, glu_b, tok):
    n, d = x.shape
    row = lambda i: (i, 0)
    fixed = lambda i: (0, 0)
    gb2 = glu_b.reshape(1, 2 * d)
    return pl.pallas_call(
        _s5_post_kernel,
        grid=(tok.tiles,),
        in_specs=[pl.BlockSpec((tok.tm, d), row),
                  pl.BlockSpec((S5_CHUNK, tok.tm // S5_CHUNK, d), lambda i: (0, i, 0)),
                  pl.BlockSpec((1, d), fixed),
                  tok.mod_spec(layer, 1, d, 1),
                  tok.mod_spec(layer, 0, d, 1),
                  tok.mod_spec(layer, 2, d, 1),
                  pl.BlockSpec((1, d), fixed),
                  pl.BlockSpec((d, d), lambda i: (0, 0)),
                  pl.BlockSpec((d, d), lambda i: (0, 1)),
                  pl.BlockSpec((1, d), lambda i: (0, 0)),
                  pl.BlockSpec((1, d), lambda i: (0, 1))],
        out_specs=pl.BlockSpec((tok.tm, d), row),
        out_shape=jax.ShapeDtypeStruct((n, d), F32),
        scratch_shapes=[pltpu.VMEM((d // LANES, tok.tm, LANES), F32)],
        compiler_params=_cparams("arbitrary"),
        name="s5_post",
    )(x, y, g, mod, mod, mod, dsk, glu_w, glu_w, gb2, gb2)


def _na_qkv_kernel(x_ref, g_ref, sc_ref, sh_ref, w_ref, qn_ref, kn_ref, q_ref, k_ref, v_ref, h_scr, *, hd):
    j = pl.program_id(1)

    @pl.when(j == 0)
    def _():
        h_scr[...] = _normmod(x_ref[...], g_ref[...], sc_ref[...], sh_ref[...]).astype(BF16)

    z = jnp.dot(h_scr[...], w_ref[...].astype(BF16), preferred_element_type=F32)
    d = z.shape[1]

    def head_norm(gain_ref, out_ref):
        lo = lax.broadcasted_iota(jnp.int32, (1, LANES), 1) < hd
        for s in range(d // LANES):
            seg = z[:, s * LANES:(s + 1) * LANES]
            sq = seg * seg
            s_lo = jnp.sum(jnp.where(lo, sq, 0.0), axis=-1, keepdims=True)
            s_hi = jnp.sum(jnp.where(lo, 0.0, sq), axis=-1, keepdims=True)
            ms = jnp.where(lo, s_lo, s_hi) / hd
            out_ref[:, s * LANES:(s + 1) * LANES] = (seg * lax.rsqrt(ms + EPS) * gain_ref[...]).astype(out_ref.dtype)

    @pl.when(j == 0)
    def _():
        head_norm(qn_ref, q_ref)

    @pl.when(j == 1)
    def _():
        head_norm(kn_ref, k_ref)

    @pl.when(j == 2)
    def _():
        v_ref[...] = z


def _na_qkv(x, g, mod, layer, w_qkv, qn, kn, tok):
    n, d = x.shape
    hd = qn.shape[-1]
    assert 2 * hd == LANES
    row = lambda i, j: (i, 0)
    fixed = lambda i, j: (0, 0)
    qn2 = jnp.tile(qn, 2).reshape(1, LANES)
    kn2 = jnp.tile(kn, 2).reshape(1, LANES)
    return pl.pallas_call(
        functools.partial(_na_qkv_kernel, hd=hd),
        grid=(tok.tiles, 3),
        in_specs=[pl.BlockSpec((tok.tm, d), row),
                  pl.BlockSpec((1, d), fixed),
                  tok.mod_spec(layer, 1, d, 2),
                  tok.mod_spec(layer, 0, d, 2),
                  pl.BlockSpec((d, d), lambda i, j: (0, j)),
                  pl.BlockSpec((1, LANES), fixed),
                  pl.BlockSpec((1, LANES), fixed)],
        out_specs=[pl.BlockSpec((tok.tm, d), row)] * 3,
        out_shape=[jax.ShapeDtypeStruct((n, d), BF16),
                   jax.ShapeDtypeStruct((n, d), F32),
                   jax.ShapeDtypeStruct((n, d), F32)],
        scratch_shapes=[pltpu.VMEM((tok.tm, d), BF16)],
        compiler_params=_cparams("arbitrary", "arbitrary"),
        name="na_qkv",
    )(x, g, mod, mod, w_qkv, qn2, kn2)


def _softmax_pv(s_parts, v_parts):
    m = s_parts[0].max(axis=-1, keepdims=True)
    for s in s_parts[1:]:
        m = jnp.maximum(m, s.max(axis=-1, keepdims=True))
    den = 0.0
    acc = 0.0
    for s, v in zip(s_parts, v_parts):
        e = jnp.exp(s - m)
        den = den + e.sum(axis=-1, keepdims=True)
        acc = acc + jnp.dot(e.astype(BF16), v, preferred_element_type=F32)
    return acc / den


def _qk(q, k):
    return lax.dot_general(q, k, (((1,), (1,)), ((), ())), preferred_element_type=F32)


def _na_ctx_attn_kernel(q_ref, k_ref, v_ref, o_ref, *, hd):
    lo = lax.broadcasted_iota(jnp.int32, (1, LANES), 1) < hd
    scale = hd ** -0.5
    for s in range(q_ref.shape[1] // LANES):
        sl = slice(s * LANES, (s + 1) * LANES)
        q = q_ref[:, sl]
        k = k_ref[:, sl].astype(BF16)
        v = v_ref[:, sl].astype(BF16)
        zero = jnp.zeros_like(q)
        o_lo = _softmax_pv([_qk(jnp.where(lo, q, zero), k) * scale], [v])
        o_hi = _softmax_pv([_qk(jnp.where(lo, zero, q), k) * scale], [v])
        o_ref[:, sl] = jnp.where(lo, o_lo, o_hi).astype(o_ref.dtype)


def _na_ctx_attn(q, k, v, batch, seq, hd):
    d = q.shape[1]
    spec = pl.BlockSpec((seq, d), lambda b: (b, 0))
    return pl.pallas_call(
        functools.partial(_na_ctx_attn_kernel, hd=hd),
        grid=(batch,),
        in_specs=[spec, spec, spec],
        out_specs=spec,
        out_shape=jax.ShapeDtypeStruct((batch * seq, d), BF16),
        compiler_params=_cparams("arbitrary"),
        name="na_ctx_attn",
    )(q, k, v)


def _na_lat_attn_kernel(q_ref, kl_ref, vl_ref, kc_ref, vc_ref, tab_ref, o_ref, *, hd, rows, kh):
    qb = pl.program_id(1)
    lane = lax.broadcasted_iota(jnp.int32, (1, LANES), 1)
    lo = lane < hd
    upper = (lane >= GRID_W).astype(jnp.int32)
    scale = hd ** -0.5
    q = q_ref[...]
    kl = kl_ref[...].astype(BF16)
    vl = vl_ref[...].astype(BF16)
    kc = kc_ref[...].astype(BF16)
    vc = vc_ref[...].astype(BF16)
    zero = jnp.zeros_like(q)
    q_rows = q.shape[0] // GRID_W

    def bias(hh):
        blocks = []
        for r in range(q_rows):
            rq = qb * q_rows + r
            rs = jnp.clip(rq - kh // 2, 0, rows - kh)
            tiles = []
            for kp in range(rows // 2):
                rk = 2 * kp + upper
                ok = (rk >= rs) & (rk < rs + kh)
                tiles.append(jnp.where(ok, tab_ref[hh, 2 * kp - rq + rows - 1], -jnp.inf))
            blocks.append(jnp.concatenate(tiles, axis=1))
        return jnp.concatenate(blocks, axis=0)

    outs = []
    for hh in range(2):
        qm = jnp.where(lo, q, zero) if hh == 0 else jnp.where(lo, zero, q)
        s_loc = _qk(qm, kl) * scale + bias(hh)
        s_ctx = _qk(qm, kc) * scale
        outs.append(_softmax_pv([s_loc, s_ctx], [vl, vc]))
    o_ref[...] = jnp.where(lo, outs[0], outs[1]).astype(o_ref.dtype)


def _na_lat_attn(q, k, v, cache_k, cache_v, slot, tab, n_prompt, dec_batch, dec_seq, hd):
    d = q.shape[1]
    past = cache_k.shape[2]
    rows = dec_seq // GRID_W
    kh = min(NA_KH, rows)
    tq = _pow2_tile(256, dec_seq, n_prompt)
    assert tq % GRID_W == 0 and rows % 2 == 0 and 2 * GRID_W == LANES
    nqb = dec_seq // tq
    q0 = n_prompt // tq
    l0 = n_prompt // dec_seq
    return pl.pallas_call(
        functools.partial(_na_lat_attn_kernel, hd=hd, rows=rows, kh=kh),
        grid=(d // LANES, nqb, dec_batch),
        in_specs=[pl.BlockSpec((tq, LANES), lambda hp, qb, b: (q0 + b * nqb + qb, hp)),
                  pl.BlockSpec((dec_seq, LANES), lambda hp, qb, b: (l0 + b, hp)),
                  pl.BlockSpec((dec_seq, LANES), lambda hp, qb, b: (l0 + b, hp)),
                  pl.BlockSpec((None, None, past, LANES), lambda hp, qb, b: (b, slot, 0, hp)),
                  pl.BlockSpec((None, None, past, LANES), lambda hp, qb, b: (b, slot, 0, hp)),
                  pl.BlockSpec((2,) + tab.shape[1:], lambda hp, qb, b: (hp, 0, 0, 0))],
        out_specs=pl.BlockSpec((tq, LANES), lambda hp, qb, b: (b * nqb + qb, hp)),
        out_shape=jax.ShapeDtypeStruct((dec_batch * dec_seq, d), BF16),
        compiler_params=_cparams("arbitrary", "arbitrary", "arbitrary"),
        name="na_lat_attn",
    )(q, k, v, cache_k, cache_v, tab)


def _na_bias_table(rpb, dec_seq):
    rows = dec_seq // GRID_W
    kh = min(NA_KH, rows)
    nh, nd, nc = rpb.shape
    c = jnp.arange(GRID_W)
    cs = jnp.clip(c - NA_KW // 2, 0, GRID_W - NA_KW)
    col_ok = (c[None, :] >= cs[:, None]) & (c[None, :] < cs[:, None] + NA_KW)
    dcol = c[None, :] - c[:, None] + NA_KW - 1
    onehot = (dcol[None] == jnp.arange(nc)[:, None, None]).astype(F32)
    tmp = jnp.einsum('hdc,cxy->hdxy', rpb.astype(F32), onehot, precision=lax.Precision.HIGHEST)
    tmp = jnp.where(col_ok[None, None], tmp, -jnp.inf)
    n_e = 2 * rows - 2
    front = rows - kh
    back = n_e + 1 - nd - front
    pad = lambda n: jnp.full((nh, n, GRID_W, GRID_W), -jnp.inf, F32)
    ext = jnp.concatenate([pad(front), tmp, pad(back)], axis=1)
    return jnp.concatenate([ext[:, :-1], ext[:, 1:]], axis=-1)


def _gqa_qkv_kernel(x_ref, g_ref, sc_ref, sh_ref, w_ref, qn_ref, kn_ref, cos_ref, sin_ref,
                    q_ref, k_ref, v_ref, *, nq, nk, np_tiles):
    i = pl.program_id(0)
    h = _normmod(x_ref[...], g_ref[...], sc_ref[...], sh_ref[...]).astype(BF16)
    z = jnp.dot(h, w_ref[...].astype(BF16), preferred_element_type=F32)
    is_sample = i >= np_tiles
    cos = cos_ref[...]
    sin = sin_ref[...]

    def norm_rope(seg, gain):
        ms = jnp.mean(seg * seg, axis=-1, keepdims=True)
        y = seg * lax.rsqrt(ms + EPS) * gain
        roped = y * cos + pltpu.roll(y, LANES // 2, 1) * sin
        return jnp.where(is_sample, roped, y)

    for hh in range(nq):
        sl = slice(hh * LANES, (hh + 1) * LANES)
        q_ref[:, sl] = norm_rope(z[:, sl], qn_ref[...]).astype(q_ref.dtype)
    for hh in range(nk):
        k_ref[:, hh * LANES:(hh + 1) * LANES] = norm_rope(z[:, (nq + hh) * LANES:(nq + hh + 1) * LANES], kn_ref[...])
    v_ref[...] = z[:, (nq + nk) * LANES:]


def _gqa_qkv(x, g, mod, layer, w_qkv, qn, kn, cos_t, sin_t, nk, tok):
    n, d = x.shape
    hd = qn.shape[-1]
    assert hd == LANES
    nq = d // hd
    dk = nk * hd
    row = lambda i: (i, 0)
    fixed = lambda i: (0, 0)
    pos = lambda i: (jnp.maximum(i - tok.np_tiles, 0) % tok.tps, 0)
    return pl.pallas_call(
        functools.partial(_gqa_qkv_kernel, nq=nq, nk=nk, np_tiles=tok.np_tiles),
        grid=(tok.tiles,),
        in_specs=[pl.BlockSpec((tok.tm, d), row),
                  pl.BlockSpec((1, d), fixed),
                  tok.mod_spec(layer, 1, d, 1),
                  tok.mod_spec(layer, 0, d, 1),
                  pl.BlockSpec((d, d + 2 * dk), fixed),
                  pl.BlockSpec((1, hd), fixed),
                  pl.BlockSpec((1, hd), fixed),
                  pl.BlockSpec((tok.tm, hd), pos),
                  pl.BlockSpec((tok.tm, hd), pos)],
        out_specs=[pl.BlockSpec((tok.tm, d), row),
                   pl.BlockSpec((tok.tm, dk), row),
                   pl.BlockSpec((tok.tm, dk), row)],
        out_shape=[jax.ShapeDtypeStruct((n, d), BF16),
                   jax.ShapeDtypeStruct((n, dk), F32),
                   jax.ShapeDtypeStruct((n, dk), F32)],
        compiler_params=_cparams("arbitrary"),
        name="gqa_qkv",
    )(x, g, mod, mod, w_qkv, qn.reshape(1, hd), kn.reshape(1, hd), cos_t, sin_t)


def _rope_tables(dec_seq, hd):
    t = jnp.arange(dec_seq)
    row = (t // GRID_W).astype(F32)
    col = (t % GRID_W).astype(F32)
    half = hd // 2
    inv = ROPE_THETA ** (-jnp.arange(0, half, 2, dtype=F32) / half)
    ang = jnp.concatenate([row[:, None] * inv, col[:, None] * inv], axis=-1)
    cos, sin = jnp.cos(ang), jnp.sin(ang)
    return jnp.concatenate([cos, cos], axis=-1), jnp.concatenate([-sin, sin], axis=-1)


def _gqa_attn_kernel(q_ref, *refs, rep, nk, n_kv):
    k_refs, v_refs, o_ref = refs[:n_kv], refs[n_kv:2 * n_kv], refs[2 * n_kv]
    tq = q_ref.shape[0]
    scale = LANES ** -0.5
    for kv in range(nk):
        sl = slice(kv * LANES, (kv + 1) * LANES)
        ks = [r[:, sl].astype(BF16) for r in k_refs]
        vs = [r[:, sl].astype(BF16) for r in v_refs]
        qs = jnp.concatenate([q_ref[:, (kv * rep + r) * LANES:(kv * rep + r + 1) * LANES] for r in range(rep)], axis=0)
        o = _softmax_pv([_qk(qs, k) * scale for k in ks], vs)
        for r in range(rep):
            o_ref[:, (kv * rep + r) * LANES:(kv * rep + r + 1) * LANES] = o[r * tq:(r + 1) * tq].astype(o_ref.dtype)


def _gqa_ctx_attn(q, k, v, batch, seq, nk):
    d = q.shape[1]
    dk = k.shape[1]
    rep = d // dk
    qspec = pl.BlockSpec((seq, d), lambda b: (b, 0))
    kspec = pl.BlockSpec((seq, dk), lambda b: (b, 0))
    return pl.pallas_call(
        functools.partial(_gqa_attn_kernel, rep=rep, nk=nk, n_kv=1),
        grid=(batch,),
        in_specs=[qspec, kspec, kspec],
        out_specs=qspec,
        out_shape=jax.ShapeDtypeStruct((batch * seq, d), BF16),
        compiler_params=_cparams("arbitrary"),
        name="gqa_ctx_attn",
    )(q, k, v)


def _gqa_lat_attn(q, k, v, cache_k, cache_v, slot, n_prompt, dec_batch, dec_seq, nk):
    d = q.shape[1]
    dk = k.shape[1]
    rep = d // dk
    past = cache_k.shape[2]
    tq = _pow2_tile(256, dec_seq, n_prompt)
    nqb = dec_seq // tq
    q0 = n_prompt // tq
    l0 = n_prompt // dec_seq
    lspec = pl.BlockSpec((dec_seq, dk), lambda b, qb: (l0 + b, 0))
    cspec = pl.BlockSpec((None, None, past, dk), lambda b, qb: (b, slot, 0, 0))
    return pl.pallas_call(
        functools.partial(_gqa_attn_kernel, rep=rep, nk=nk, n_kv=2),
        grid=(dec_batch, nqb),
        in_specs=[pl.BlockSpec((tq, d), lambda b, qb: (q0 + b * nqb + qb, 0)),
                  lspec, cspec, lspec, cspec],
        out_specs=pl.BlockSpec((tq, d), lambda b, qb: (b * nqb + qb, 0)),
        out_shape=jax.ShapeDtypeStruct((dec_batch * dec_seq, d), BF16),
        compiler_params=_cparams("arbitrary", "arbitrary"),
        name="gqa_lat_attn",
    )(q, k, cache_k, v, cache_v)


def _proj_res_kernel(x_ref, o_ref, w_ref, gate_ref, out_ref):
    y = jnp.dot(o_ref[...], w_ref[...].astype(BF16), preferred_element_type=F32)
    out_ref[...] = x_ref[...] + gate_ref[...] * y


def _proj_res(x, o, w_o, mod, layer, tok):
    n, d = x.shape
    row = lambda i: (i, 0)
    return pl.pallas_call(
        _proj_res_kernel,
        grid=(tok.tiles,),
        in_specs=[pl.BlockSpec((tok.tm, d), row),
                  pl.BlockSpec((tok.tm, d), row),
                  pl.BlockSpec((d, d), lambda i: (0, 0)),
                  tok.mod_spec(layer, 2, d, 1)],
        out_specs=pl.BlockSpec((tok.tm, d), row),
        out_shape=jax.ShapeDtypeStruct((n, d), F32),
        compiler_params=_cparams("arbitrary"),
        name="attn_proj_res",
    )(x, o, w_o, mod)


def _mlp_kernel(x_ref, g_ref, sc_ref, sh_ref, gate_ref, w1_ref, w2_ref, o_ref, h_scr, acc_scr):
    j = pl.program_id(1)

    @pl.when(j == 0)
    def _():
        h_scr[...] = _normmod(x_ref[...], g_ref[...], sc_ref[...], sh_ref[...]).astype(BF16)
        acc_scr[...] = jnp.zeros_like(acc_scr)

    a = jnp.maximum(jnp.dot(h_scr[...], w1_ref[...].astype(BF16), preferred_element_type=F32), 0.0)
    acc_scr[...] += jnp.dot((a * a).astype(BF16), w2_ref[...].astype(BF16), preferred_element_type=F32)

    @pl.when(j == pl.num_programs(1) - 1)
    def _():
        o_ref[...] = x_ref[...] + gate_ref[...] * acc_scr[...]


def _mlp(x, g, mod, layer, w1, w2, tok):
    n, d = x.shape
    f = w1.shape[1]
    tf = _pow2_tile(512, f)
    row = lambda i, j: (i, 0)
    return pl.pallas_call(
        _mlp_kernel,
        grid=(tok.tiles, f // tf),
        in_specs=[pl.BlockSpec((tok.tm, d), row),
                  pl.BlockSpec((1, d), lambda i, j: (0, 0)),
                  tok.mod_spec(layer, 4, d, 2),
                  tok.mod_spec(layer, 3, d, 2),
                  tok.mod_spec(layer, 5, d, 2),
                  pl.BlockSpec((d, tf), lambda i, j: (0, j)),
                  pl.BlockSpec((tf, d), lambda i, j: (j, 0))],
        out_specs=pl.BlockSpec((tok.tm, d), row),
        out_shape=jax.ShapeDtypeStruct((n, d), F32),
        scratch_shapes=[pltpu.VMEM((tok.tm, d), BF16), pltpu.VMEM((tok.tm, d), F32)],
        compiler_params=_cparams("arbitrary", "arbitrary"),
        name="mlp",
    )(x, g, mod, mod, mod, w1, w2)


def _s5_tables(lam_re, lam_im, log_dt, b_re, b_im, c_re, c_im):
    hi = lax.Precision.HIGHEST
    t = S5_CHUNK
    g, p, k = b_re.shape[1:]
    dt = jnp.exp(log_dt.astype(F32))[:, :, None]
    lr, li = lam_re.astype(F32), lam_im.astype(F32)
    ar, ai = lr * dt, li * dt
    mag = jnp.exp(ar)
    abr, abi = mag * jnp.cos(ai), mag * jnp.sin(ai)
    nr, ni = abr - 1.0, abi
    den = lr * lr + li * li
    f_re = (nr * lr + ni * li) / den
    f_im = (ni * lr - nr * li) / den
    bbr = f_re[..., None] * b_re - f_im[..., None] * b_im
    bbi = f_re[..., None] * b_im + f_im[..., None] * b_re
    n = jnp.arange(t + 1, dtype=F32)[None, None, :, None]
    pm = jnp.exp(n * ar[:, :, None, :])
    pr, pi = pm * jnp.cos(n * ai[:, :, None, :]), pm * jnp.sin(n * ai[:, :, None, :])
    cr, ci = c_re.astype(F32), c_im.astype(F32)
    car = cr[:, :, None] * pr[:, :, :, None, :] - ci[:, :, None] * pi[:, :, :, None, :]
    cai = cr[:, :, None] * pi[:, :, :, None, :] + ci[:, :, None] * pr[:, :, :, None, :]
    kern = (jnp.einsum('dgnkp,dgpj->dgknj', car[:, :, :t], bbr, precision=hi)
            - jnp.einsum('dgnkp,dgpj->dgknj', cai[:, :, :t], bbi, precision=hi))
    zpad = jnp.zeros((g, k, t - 1, k), F32)
    krev = (jnp.concatenate([kern[0][:, :, ::-1], zpad], axis=2) + jnp.concatenate([zpad, kern[1]], axis=2))
    intra = jnp.stack([krev[:, :, t - 1 - i:2 * t - 1 - i] for i in range(t)], axis=1)
    intra = intra.reshape(g, t * k, t * k)

    def summ(pw_r, pw_i, d):
        pw_r = pw_r.transpose(0, 2, 1)[:, :, :, None]
        pw_i = pw_i.transpose(0, 2, 1)[:, :, :, None]
        sr = pw_r * bbr[d][:, :, None, :] - pw_i * bbi[d][:, :, None, :]
        si = pw_r * bbi[d][:, :, None, :] + pw_i * bbr[d][:, :, None, :]
        return sr.reshape(g, p, t * k), si.reshape(g, p, t * k)

    sfr, sfi = summ(pr[0, :, :t][:, ::-1], pi[0, :, :t][:, ::-1], 0)
    sbr, sbi = summ(pr[1, :, :t], pi[1, :, :t], 1)
    w1t = jnp.concatenate([intra, sfr, sbr, sfi, sbi], axis=1)
    rd = lambda a: a.reshape(g, t * k, p)
    w2t = jnp.concatenate([rd(car[0, :, 1:]), rd(car[1, :, 1:][:, ::-1]),
                           -rd(cai[0, :, 1:]), -rd(cai[1, :, 1:][:, ::-1])], axis=2)
    at = jnp.stack([jnp.concatenate([pr[0, :, t], pr[1, :, t]], axis=-1),
                    jnp.concatenate([pi[0, :, t], pi[1, :, t]], axis=-1)], axis=1)
    return w1t.astype(BF16), w2t.astype(BF16), at


def _s5_mixer(x, g, mod, layer, tok, params, h0, dims):
    (lam_re, lam_im, log_dt, b_re, b_im, c_re, c_im, d_skip, glu_w, glu_b) = params
    batch, seq, dec_batch, dec_seq = dims
    n_prompt = batch * seq
    d = x.shape[1]
    ngrp = d // S5_GROUP
    p = lam_re.shape[-1]
    w1t, w2t, at = _s5_tables(lam_re, lam_im, log_dt, b_re, b_im, c_re, c_im)
    hperm = _s5_pre(x, g, mod, layer, tok)
    h0g = h0.astype(F32).transpose(3, 2, 0, 1, 4).reshape(ngrp, 2, dec_batch, 2 * p)
    streams = ((0, seq // S5_CHUNK, batch, False, True),
               (n_prompt // S5_CHUNK, dec_seq // S5_CHUNK, dec_batch, True, False))
    yperm, fin = _s5_scan(hperm, w1t, w2t, at, h0g, batch, streams)
    x_new = _s5_post(x, yperm, g, mod, layer, d_skip.reshape(1, d), glu_w, glu_b, tok)
    st = fin.reshape(ngrp, 2, batch, 2, p).transpose(2, 3, 1, 0, 4)
    return x_new, st


def kernel(x_prompt, x_sample, state_s5, cache_na_k, cache_na_v, cache_gqa_k, cache_gqa_v, c, c_ctx, norm_g, ada_w, ada_b, mlp_w1, mlp_w2, s5_lam_re, s5_lam_im, s5_log_dt, s5_b_re, s5_b_im, s5_c_re, s5_c_im, s5_d, s5_glu_w, s5_glu_b, na_w_qkv, na_q_norm, na_k_norm, na_rpb, na_w_o, gqa_w_qkv, gqa_q_norm, gqa_k_norm, gqa_w_o):
    batch, seq, d = x_prompt.shape
    dec_batch, dec_seq, _ = x_sample.shape
    depth = ada_w.shape[0]
    n_prompt = batch * seq
    n_sample = dec_batch * dec_seq
    na_heads, na_hd = cache_na_k.shape[3], cache_na_k.shape[4]
    gqa_kv, gqa_hd = cache_gqa_k.shape[3], cache_gqa_k.shape[4]
    assert n_prompt % dec_seq == 0

    tok = _Tok(n_prompt, n_sample, dec_seq, 1024)
    tok_half = _Tok(n_prompt, n_sample, dec_seq, 512)

    mod_rows = -(-(1 + dec_batch) // SUBLANES) * SUBLANES
    cvec = jnp.concatenate([c_ctx[None, :], c, jnp.zeros((mod_rows - 1 - dec_batch, d), F32)], axis=0)
    mod = _modulation(cvec, ada_w, ada_b).reshape(depth, mod_rows, 6, 1, d)

    x = jnp.concatenate([x_prompt.reshape(n_prompt, d), x_sample.reshape(n_sample, d)], axis=0)
    cache_na_k2 = cache_na_k.reshape(cache_na_k.shape[:3] + (na_heads * na_hd,))
    cache_na_v2 = cache_na_v.reshape(cache_na_v.shape[:3] + (na_heads * na_hd,))
    cache_gqa_k2 = cache_gqa_k.reshape(cache_gqa_k.shape[:3] + (gqa_kv * gqa_hd,))
    cache_gqa_v2 = cache_gqa_v.reshape(cache_gqa_v.shape[:3] + (gqa_kv * gqa_hd,))
    cos_t, sin_t = _rope_tables(dec_seq, gqa_hd)

    new_s5, new_na_k, new_na_v, new_gqa_k, new_gqa_v = [], [], [], [], []
    for i in range(depth):
        kind, slot = i % 3, i // 3
        g1 = norm_g[i, 0].reshape(1, d)
        g2 = norm_g[i, 1].reshape(1, d)
        if kind == 0:
            params = (s5_lam_re[slot], s5_lam_im[slot], s5_log_dt[slot], s5_b_re[slot], s5_b_im[slot],
                      s5_c_re[slot], s5_c_im[slot], s5_d[slot], s5_glu_w[slot], s5_glu_b[slot])
            x, st = _s5_mixer(x, g1, mod, i, tok_half, params, state_s5[:, slot], (batch, seq, dec_batch, dec_seq))
            new_s5.append(st)
        elif kind == 1:
            q, k, v = _na_qkv(x, g1, mod, i, na_w_qkv[slot], na_q_norm[slot], na_k_norm[slot], tok)
            o_p = _na_ctx_attn(q, k, v, batch, seq, na_hd)
            bias = _na_bias_table(na_rpb[slot], dec_seq)
            o_s = _na_lat_attn(q, k, v, cache_na_k2, cache_na_v2, slot, bias, n_prompt, dec_batch, dec_seq, na_hd)
            x = _proj_res(x, jnp.concatenate([o_p, o_s], axis=0), na_w_o[slot], mod, i, tok)
            new_na_k.append(k[:n_prompt].reshape(batch, seq, na_heads, na_hd))
            new_na_v.append(v[:n_prompt].reshape(batch, seq, na_heads, na_hd))
        else:
            q, k, v = _gqa_qkv(x, g1, mod, i, gqa_w_qkv[slot], gqa_q_norm[slot], gqa_k_norm[slot],
                               cos_t, sin_t, gqa_kv, tok_half)
            o_p = _gqa_ctx_attn(q, k, v, batch, seq, gqa_kv)
            o_s = _gqa_lat_attn(q, k, v, cache_gqa_k2, cache_gqa_v2, slot, n_prompt, dec_batch, dec_seq, gqa_kv)
            x = _proj_res(x, jnp.concatenate([o_p, o_s], axis=0), gqa_w_o[slot], mod, i, tok)
            new_gqa_k.append(k[:n_prompt].reshape(batch, seq, gqa_kv, gqa_hd))
            new_gqa_v.append(v[:n_prompt].reshape(batch, seq, gqa_kv, gqa_hd))
        x = _mlp(x, g2, mod, i, mlp_w1[i], mlp_w2[i], tok)
    return (x[:n_prompt].reshape(batch, seq, d), x[n_prompt:].reshape(dec_batch, dec_seq, d),
            jnp.stack(new_s5, axis=1), jnp.stack(new_na_k, axis=1), jnp.stack(new_na_v, axis=1),
            jnp.stack(new_gqa_k, axis=1), jnp.stack(new_gqa_v, axis=1))
```

```python
import functools
import math

import jax
import jax.numpy as jnp
from jax import lax
from jax.experimental import pallas as pl
from jax.experimental.pallas import tpu as pltpu

F32 = jnp.float32
BF16 = jnp.bfloat16

EPS = 1e-6
GRID_W = 64
S5_GROUP = 16
NA_KH = 8
NA_KW = 16
ROPE_THETA = 10000.0
S5_CHUNK = 16
LANES = 128
SUBLANES = 8
VMEM_LIMIT = 56 * 1024 * 1024


def _cparams(*sem):
    return pltpu.CompilerParams(dimension_semantics=sem, vmem_limit_bytes=VMEM_LIMIT)


def _pow2_tile(pref, *ns):
    t = pref
    while any(n % t for n in ns):
        t //= 2
    return t


def _normmod(x, g, sc, sh):
    ms = jnp.mean(x * x, axis=-1, keepdims=True)
    y = x * lax.rsqrt(ms + EPS) * g
    return y * (1.0 + sc) + sh


class _Tok:
    def __init__(self, n_prompt, n_sample, dec_seq, pref):
        self.tm = _pow2_tile(pref, n_prompt, dec_seq)
        self.n = n_prompt + n_sample
        self.tiles = self.n // self.tm
        self.np_tiles = n_prompt // self.tm
        self.tps = dec_seq // self.tm

    def mod_row(self, i):
        return jnp.where(i < self.np_tiles, 0, 1 + (i - self.np_tiles) // self.tps)

    def prompt_spec(self, d, nargs):
        last = self.np_tiles - 1
        if nargs == 1:
            return pl.BlockSpec((self.tm, d), lambda i: (jnp.minimum(i, last), 0))
        return pl.BlockSpec((self.tm, d), lambda i, j: (jnp.minimum(i, last), 0))

    def sample_spec(self, d, nargs):
        first = self.np_tiles
        if nargs == 1:
            return pl.BlockSpec((self.tm, d), lambda i: (jnp.maximum(i - first, 0), 0))
        return pl.BlockSpec((self.tm, d), lambda i, j: (jnp.maximum(i - first, 0), 0))

    def mod_spec(self, layer, which, d, nargs):
        if nargs == 1:
            return pl.BlockSpec((None, None, None, 1, d), lambda i: (layer, self.mod_row(i), which, 0, 0))
        return pl.BlockSpec((None, None, None, 1, d), lambda i, j: (layer, self.mod_row(i), which, 0, 0))


def _mod_kernel(c_ref, w_ref, b_ref, o_ref):
    c = c_ref[...]
    s = c * jax.nn.sigmoid(c)
    o_ref[...] = jnp.dot(s.astype(BF16), w_ref[...].astype(BF16), preferred_element_type=F32) + b_ref[...]


def _modulation(cvec, ada_w, ada_b):
    depth, d, d6 = ada_w.shape
    rows = cvec.shape[0]
    tn = _pow2_tile(2048, d6) if d6 % 2048 == 0 else d
    return pl.pallas_call(
        _mod_kernel,
        grid=(depth, d6 // tn),
        in_specs=[pl.BlockSpec((rows, d), lambda l, j: (0, 0)),
                  pl.BlockSpec((None, d, tn), lambda l, j: (l, 0, j)),
                  pl.BlockSpec((None, 1, tn), lambda l, j: (l, 0, j))],
        out_specs=pl.BlockSpec((None, rows, tn), lambda l, j: (l, 0, j)),
        out_shape=jax.ShapeDtypeStruct((depth, rows, d6), F32),
        compiler_params=_cparams("arbitrary", "arbitrary"),
        name="adaln_modulation",
    )(cvec, ada_w, ada_b.reshape(depth, 1, d6))


def _read_x(x_refs, np_tiles):
    if len(x_refs) == 1:
        return x_refs[0][...]
    return jnp.where(pl.program_id(0) < np_tiles, x_refs[0][...], x_refs[1][...])


def _x_specs(x, tok, nargs):
    if isinstance(x, tuple):
        d = x[0].shape[1]
        return [tok.prompt_spec(d, nargs), tok.sample_spec(d, nargs)], x
    d = x.shape[1]
    return [pl.BlockSpec((tok.tm, d), (lambda i: (i, 0)) if nargs == 1 else (lambda i, j: (i, 0)))], (x,)


def _s5_pre_kernel(*refs, n_x, np_tiles):
    x_refs, (g_ref, sc_ref, sh_ref, h_ref, h_scr) = refs[:n_x], refs[n_x:]
    h = _normmod(_read_x(x_refs, np_tiles), g_ref[...], sc_ref[...], sh_ref[...])
    nc = h_ref.shape[1]
    for a in range(h_scr.shape[0]):
        sl = slice(a * LANES, (a + 1) * LANES)
        h_scr[a] = h[:, sl]
        for j in range(S5_CHUNK):
            h_ref[j, :, sl] = h_scr[a, pl.ds(j, nc, stride=S5_CHUNK), :]


def _s5_pre(x, g, mod, layer, tok):
    x_specs, xs = _x_specs(x, tok, 1)
    d = xs[0].shape[1]
    nc = tok.tm // S5_CHUNK
    return pl.pallas_call(
        functools.partial(_s5_pre_kernel, n_x=len(xs), np_tiles=tok.np_tiles),
        grid=(tok.tiles,),
        in_specs=x_specs + [pl.BlockSpec((1, d), lambda i: (0, 0)),
                            tok.mod_spec(layer, 1, d, 1),
                            tok.mod_spec(layer, 0, d, 1)],
        out_specs=pl.BlockSpec((S5_CHUNK, nc, d), lambda i: (0, i, 0)),
        out_shape=jax.ShapeDtypeStruct((S5_CHUNK, tok.n // S5_CHUNK, d), F32),
        scratch_shapes=[pltpu.VMEM((d // LANES, tok.tm, LANES), F32)],
        compiler_params=_cparams("arbitrary"),
        name="s5_pre",
    )(*xs, g, mod, mod)


def _s5_scan_kernel(h_ref, wi_ref, ws_ref, w2t_ref, at_ref, h0_ref, y_ref, fin_ref, xt_scr, yt_scr, s_scr, e_scr, *,
                    streams):
    ngrp = wi_ref.shape[0]
    tk = w2t_ref.shape[1]
    p2 = at_ref.shape[-1]
    fwd = lax.broadcasted_iota(jnp.int32, (1, p2), 1) < (p2 // 2)
    for row0, n_chunks, nb, use_h0, write_fin in streams:
        rows = n_chunks * nb
        for j in range(S5_CHUNK):
            xt_scr[j, :, 0:rows] = h_ref[j, row0:row0 + rows, :].T.astype(BF16)

        def group(r, carry):
            c0 = pl.multiple_of(r * S5_GROUP, S5_GROUP)
            rmat = jnp.concatenate([xt_scr[j, pl.ds(c0, S5_GROUP), 0:rows] for j in range(S5_CHUNK)], axis=0)
            zs = jnp.dot(ws_ref[r], rmat, preferred_element_type=F32)
            s_scr[0, 0:rows, :] = zs[0:p2].T
            s_scr[1, 0:rows, :] = zs[p2:].T
            ar = at_ref[r, 0:1, :]
            ai = at_ref[r, 1:2, :]
            if use_h0:
                hr, hi = h0_ref[r, 0], h0_ref[r, 1]
            else:
                hr = hi = jnp.zeros((nb, p2), F32)
            for t in range(n_chunks):
                ft = pl.ds(t, nb, stride=n_chunks)
                bt = pl.ds(n_chunks - 1 - t, nb, stride=n_chunks)
                e_scr[0, ft, :] = hr
                e_scr[1, bt, :] = hr
                e_scr[2, ft, :] = hi
                e_scr[3, bt, :] = hi
                in_re = jnp.where(fwd, s_scr[0, ft, :], s_scr[0, bt, :])
                in_im = jnp.where(fwd, s_scr[1, ft, :], s_scr[1, bt, :])
                hr, hi = ar * hr - ai * hi + in_re, ar * hi + ai * hr + in_im
            if write_fin:
                fin_ref[r, 0] = hr
                fin_ref[r, 1] = hi
            e = jnp.concatenate([jnp.where(fwd, e_scr[0, 0:rows, :], e_scr[1, 0:rows, :]),
                                 jnp.where(fwd, e_scr[2, 0:rows, :], e_scr[3, 0:rows, :])], axis=1).astype(BF16)
            yt = jnp.dot(wi_ref[r], rmat, preferred_element_type=F32) + lax.dot_general(w2t_ref[r], e, (((1,), (1,)), ((), ())), preferred_element_type=F32)
            for i in range(S5_CHUNK):
                yt_scr[i, pl.ds(c0, S5_GROUP), 0:rows] = yt[i * S5_GROUP:(i + 1) * S5_GROUP]
            return carry

        lax.fori_loop(0, ngrp, group, 0)
        for i in range(S5_CHUNK):
            y_ref[i, row0:row0 + rows, :] = yt_scr[i, :, 0:rows].T


def _s5_scan(hperm, w_intra, w_sum, w2t, at, h0, nb_fin, streams):
    t, nrows, d = hperm.shape
    g, tk, p4 = w2t.shape
    p2 = at.shape[-1]
    gl = LANES // S5_GROUP
    nb0 = h0.shape[2]
    max_rows = max(s[1] * s[2] for s in streams)
    return pl.pallas_call(
        functools.partial(_s5_scan_kernel, streams=streams),
        grid=(d // LANES,),
        in_specs=[pl.BlockSpec((t, nrows, LANES), lambda a: (0, 0, a)),
                  pl.BlockSpec((gl, tk, tk), lambda a: (a, 0, 0)),
                  pl.BlockSpec((gl, p4, tk), lambda a: (a, 0, 0)),
                  pl.BlockSpec((gl, tk, p4), lambda a: (a, 0, 0)),
                  pl.BlockSpec((gl, 2, p2), lambda a: (a, 0, 0)),
                  pl.BlockSpec((gl, 2, nb0, p2), lambda a: (a, 0, 0, 0))],
        out_specs=[pl.BlockSpec((t, nrows, LANES), lambda a: (0, 0, a)),
                   pl.BlockSpec((gl, 2, nb_fin, p2), lambda a: (a, 0, 0, 0))],
        out_shape=[jax.ShapeDtypeStruct((t, nrows, d), F32),
                   jax.ShapeDtypeStruct((g, 2, nb_fin, p2), F32)],
        scratch_shapes=[pltpu.VMEM((t, LANES, max_rows), BF16),
                        pltpu.VMEM((t, LANES, max_rows), F32),
                        pltpu.VMEM((2, max_rows, p2), F32),
                        pltpu.VMEM((4, max_rows, p2), F32)],
        compiler_params=_cparams("arbitrary"),
        name="s5_scan",
    )(hperm, w_intra, w_sum, w2t, at, h0)


def _s5_post_kernel(*refs, n_x, np_tiles):
    x_refs = refs[:n_x]
    y_ref, g_ref, sc_ref, sh_ref, gate_ref, dsk_ref, wa_ref, wg_ref, ba_ref, bg_ref, o_ref, y_scr = refs[n_x:]
    nc = y_ref.shape[1]
    for a in range(y_scr.shape[0]):
        for j in range(S5_CHUNK):
            y_scr[a, pl.ds(j, nc, stride=S5_CHUNK), :] = y_ref[j, :, a * LANES:(a + 1) * LANES]
    x = _read_x(x_refs, np_tiles)
    h = _normmod(x, g_ref[...], sc_ref[...], sh_ref[...])
    y = h * dsk_ref[...] + jnp.concatenate([y_scr[a] for a in range(y_scr.shape[0])], axis=1)
    a = jax.nn.gelu(y).astype(BF16)
    za = jnp.dot(a, wa_ref[...].astype(BF16), preferred_element_type=F32) + ba_ref[...]
    zg = jnp.dot(a, wg_ref[...].astype(BF16), preferred_element_type=F32) + bg_ref[...]
    o_ref[...] = x + gate_ref[...] * (za * jax.nn.sigmoid(zg))


def _s5_post(x, y, g, mod, layer, dsk, glu_w, glu_b, tok):
    x_specs, xs = _x_specs(x, tok, 1)
    n, d = tok.n, xs[0].shape[1]
    row = lambda i: (i, 0)
    fixed = lambda i: (0, 0)
    gb2 = glu_b.reshape(1, 2 * d)
    return pl.pallas_call(
        functools.partial(_s5_post_kernel, n_x=len(xs), np_tiles=tok.np_tiles),
        grid=(tok.tiles,),
        in_specs=x_specs + [
                  pl.BlockSpec((S5_CHUNK, tok.tm // S5_CHUNK, d), lambda i: (0, i, 0)),
                  pl.BlockSpec((1, d), fixed),
                  tok.mod_spec(layer, 1, d, 1),
                  tok.mod_spec(layer, 0, d, 1),
                  tok.mod_spec(layer, 2, d, 1),
                  pl.BlockSpec((1, d), fixed),
                  pl.BlockSpec((d, d), lambda i: (0, 0)),
                  pl.BlockSpec((d, d), lambda i: (0, 1)),
                  pl.BlockSpec((1, d), lambda i: (0, 0)),
                  pl.BlockSpec((1, d), lambda i: (0, 1))],
        out_specs=pl.BlockSpec((tok.tm, d), row),
        out_shape=jax.ShapeDtypeStruct((n, d), F32),
        scratch_shapes=[pltpu.VMEM((d // LANES, tok.tm, LANES), F32)],
        compiler_params=_cparams("arbitrary"),
        name="s5_post",
    )(*xs, y, g, mod, mod, mod, dsk, glu_w, glu_w, gb2, gb2)


def _na_qkv_kernel(x_ref, g_ref, sc_ref, sh_ref, w_ref, qn_ref, kn_ref, q_ref, kp_ref, ks_ref, vp_ref, vs_ref, h_scr, *,
                   hd, np_tiles):
    is_prompt = pl.program_id(0) < np_tiles
    j = pl.program_id(1)

    @pl.when(j == 0)
    def _():
        h_scr[...] = _normmod(x_ref[...], g_ref[...], sc_ref[...], sh_ref[...]).astype(BF16)

    z = jnp.dot(h_scr[...], w_ref[...].astype(BF16), preferred_element_type=F32)
    d = z.shape[1]

    def head_norm(gain_ref, out_ref):
        lo = lax.broadcasted_iota(jnp.int32, (1, LANES), 1) < hd
        for s in range(d // LANES):
            seg = z[:, s * LANES:(s + 1) * LANES]
            sq = seg * seg
            s_lo = jnp.sum(jnp.where(lo, sq, 0.0), axis=-1, keepdims=True)
            s_hi = jnp.sum(jnp.where(lo, 0.0, sq), axis=-1, keepdims=True)
            ms = jnp.where(lo, s_lo, s_hi) / hd
            out_ref[:, s * LANES:(s + 1) * LANES] = (seg * lax.rsqrt(ms + EPS) * gain_ref[...]).astype(out_ref.dtype)

    @pl.when(j == 0)
    def _():
        head_norm(qn_ref, q_ref)

    @pl.when(jnp.logical_and(j == 1, is_prompt))
    def _():
        head_norm(kn_ref, kp_ref)

    @pl.when(jnp.logical_and(j == 1, jnp.logical_not(is_prompt)))
    def _():
        head_norm(kn_ref, ks_ref)

    @pl.when(jnp.logical_and(j == 2, is_prompt))
    def _():
        vp_ref[...] = z

    @pl.when(jnp.logical_and(j == 2, jnp.logical_not(is_prompt)))
    def _():
        vs_ref[...] = z.astype(vs_ref.dtype)


def _na_qkv(x, g, mod, layer, w_qkv, qn, kn, tok):
    n, d = x.shape
    hd = qn.shape[-1]
    assert 2 * hd == LANES
    row = lambda i, j: (i, 0)
    fixed = lambda i, j: (0, 0)
    qn2 = jnp.tile(qn, 2).reshape(1, LANES)
    kn2 = jnp.tile(kn, 2).reshape(1, LANES)
    n_p, n_s = tok.np_tiles * tok.tm, n - tok.np_tiles * tok.tm
    return pl.pallas_call(
        functools.partial(_na_qkv_kernel, hd=hd, np_tiles=tok.np_tiles),
        grid=(tok.tiles, 3),
        in_specs=[pl.BlockSpec((tok.tm, d), row),
                  pl.BlockSpec((1, d), fixed),
                  tok.mod_spec(layer, 1, d, 2),
                  tok.mod_spec(layer, 0, d, 2),
                  pl.BlockSpec((d, d), lambda i, j: (0, j)),
                  pl.BlockSpec((1, LANES), fixed),
                  pl.BlockSpec((1, LANES), fixed)],
        out_specs=[pl.BlockSpec((tok.tm, d), row),
                   tok.prompt_spec(d, 2), tok.sample_spec(d, 2), tok.prompt_spec(d, 2), tok.sample_spec(d, 2)],
        out_shape=[jax.ShapeDtypeStruct((n, d), BF16),
                   jax.ShapeDtypeStruct((n_p, d), F32), jax.ShapeDtypeStruct((n_s, d), BF16),
                   jax.ShapeDtypeStruct((n_p, d), F32), jax.ShapeDtypeStruct((n_s, d), BF16)],
        scratch_shapes=[pltpu.VMEM((tok.tm, d), BF16)],
        compiler_params=_cparams("arbitrary", "arbitrary"),
        name="na_qkv",
    )(x, g, mod, mod, w_qkv, qn2, kn2)


def _softmax_pv(s_parts, v_parts):
    m = s_parts[0].max(axis=-1, keepdims=True)
    for s in s_parts[1:]:
        m = jnp.maximum(m, s.max(axis=-1, keepdims=True))
    den = 0.0
    acc = 0.0
    for s, v in zip(s_parts, v_parts):
        e = jnp.exp(s - m)
        den = den + e.sum(axis=-1, keepdims=True)
        acc = acc + jnp.dot(e.astype(BF16), v, preferred_element_type=F32)
    return acc / den


def _qk(q, k):
    return lax.dot_general(q, k, (((1,), (1,)), ((), ())), preferred_element_type=F32)


def _na_ctx_attn_kernel(q_ref, k_ref, v_ref, o_ref, *, hd):
    lo = lax.broadcasted_iota(jnp.int32, (1, LANES), 1) < hd
    scale = hd ** -0.5
    for s in range(q_ref.shape[1] // LANES):
        sl = slice(s * LANES, (s + 1) * LANES)
        q = q_ref[:, sl]
        k = k_ref[:, sl].astype(BF16)
        v = v_ref[:, sl].astype(BF16)
        zero = jnp.zeros_like(q)
        o_lo = _softmax_pv([_qk(jnp.where(lo, q, zero), k) * scale], [v])
        o_hi = _softmax_pv([_qk(jnp.where(lo, zero, q), k) * scale], [v])
        o_ref[:, sl] = jnp.where(lo, o_lo, o_hi).astype(o_ref.dtype)


def _na_ctx_attn(q, k, v, batch, seq, hd):
    d = q.shape[1]
    spec = pl.BlockSpec((seq, d), lambda b: (b, 0))
    return pl.pallas_call(
        functools.partial(_na_ctx_attn_kernel, hd=hd),
        grid=(batch,),
        in_specs=[spec, spec, spec],
        out_specs=spec,
        out_shape=jax.ShapeDtypeStruct((batch * seq, d), BF16),
        compiler_params=_cparams("arbitrary"),
        name="na_ctx_attn",
    )(q, k, v)


def _na_lat_attn_kernel(q_ref, kl_ref, vl_ref, kc_ref, vc_ref, tab_ref, o_ref, *, hd, rows, kh):
    qb = pl.program_id(1)
    lane = lax.broadcasted_iota(jnp.int32, (1, LANES), 1)
    lo = lane < hd
    upper = (lane >= GRID_W).astype(jnp.int32)
    scale = hd ** -0.5
    q = q_ref[...]
    kl = kl_ref[...].astype(BF16)
    vl = vl_ref[...].astype(BF16)
    kc = kc_ref[...].astype(BF16)
    vc = vc_ref[...].astype(BF16)
    zero = jnp.zeros_like(q)
    q_rows = q.shape[0] // GRID_W

    def bias(hh):
        blocks = []
        for r in range(q_rows):
            rq = qb * q_rows + r
            rs = jnp.clip(rq - kh // 2, 0, rows - kh)
            tiles = []
            for kp in range(rows // 2):
                rk = 2 * kp + upper
                ok = (rk >= rs) & (rk < rs + kh)
                tiles.append(jnp.where(ok, tab_ref[hh, 2 * kp - rq + rows - 1], -jnp.inf))
            blocks.append(jnp.concatenate(tiles, axis=1))
        return jnp.concatenate(blocks, axis=0)

    outs = []
    for hh in range(2):
        qm = jnp.where(lo, q, zero) if hh == 0 else jnp.where(lo, zero, q)
        s_loc = _qk(qm, kl) * scale + bias(hh)
        s_ctx = _qk(qm, kc) * scale
        outs.append(_softmax_pv([s_loc, s_ctx], [vl, vc]))
    o_ref[...] = jnp.where(lo, outs[0], outs[1]).astype(o_ref.dtype)


def _na_lat_attn(q, k, v, cache_k, cache_v, slot, tab, n_prompt, dec_batch, dec_seq, hd):
    d = q.shape[1]
    past = cache_k.shape[2]
    rows = dec_seq // GRID_W
    kh = min(NA_KH, rows)
    tq = _pow2_tile(256, dec_seq, n_prompt)
    assert tq % GRID_W == 0 and rows % 2 == 0 and 2 * GRID_W == LANES
    nqb = dec_seq // tq
    q0 = n_prompt // tq
    return pl.pallas_call(
        functools.partial(_na_lat_attn_kernel, hd=hd, rows=rows, kh=kh),
        grid=(d // LANES, nqb, dec_batch),
        in_specs=[pl.BlockSpec((tq, LANES), lambda hp, qb, b: (q0 + b * nqb + qb, hp)),
                  pl.BlockSpec((dec_seq, LANES), lambda hp, qb, b: (b, hp)),
                  pl.BlockSpec((dec_seq, LANES), lambda hp, qb, b: (b, hp)),
                  pl.BlockSpec((None, None, past, LANES), lambda hp, qb, b: (b, slot, 0, hp)),
                  pl.BlockSpec((None, None, past, LANES), lambda hp, qb, b: (b, slot, 0, hp)),
                  pl.BlockSpec((2,) + tab.shape[1:], lambda hp, qb, b: (hp, 0, 0, 0))],
        out_specs=pl.BlockSpec((tq, LANES), lambda hp, qb, b: (b * nqb + qb, hp)),
        out_shape=jax.ShapeDtypeStruct((dec_batch * dec_seq, d), BF16),
        compiler_params=_cparams("arbitrary", "arbitrary", "arbitrary"),
        name="na_lat_attn",
    )(q, k, v, cache_k, cache_v, tab)


def _na_bias_table(rpb, dec_seq):
    rows = dec_seq // GRID_W
    kh = min(NA_KH, rows)
    nh, nd, nc = rpb.shape
    c = jnp.arange(GRID_W)
    cs = jnp.clip(c - NA_KW // 2, 0, GRID_W - NA_KW)
    col_ok = (c[None, :] >= cs[:, None]) & (c[None, :] < cs[:, None] + NA_KW)
    dcol = c[None, :] - c[:, None] + NA_KW - 1
    onehot = (dcol[None] == jnp.arange(nc)[:, None, None]).astype(F32)
    tmp = jnp.einsum('hdc,cxy->hdxy', rpb.astype(F32), onehot, precision=lax.Precision.HIGHEST)
    tmp = jnp.where(col_ok[None, None], tmp, -jnp.inf)
    n_e = 2 * rows - 2
    front = rows - kh
    back = n_e + 1 - nd - front
    pad = lambda n: jnp.full((nh, n, GRID_W, GRID_W), -jnp.inf, F32)
    ext = jnp.concatenate([pad(front), tmp, pad(back)], axis=1)
    return jnp.concatenate([ext[:, :-1], ext[:, 1:]], axis=-1)


def _gqa_qkv_kernel(x_ref, g_ref, sc_ref, sh_ref, w_ref, qn_ref, kn_ref, cos_ref, sin_ref,
                    q_ref, kp_ref, ks_ref, vp_ref, vs_ref, *, nq, nk, np_tiles):
    i = pl.program_id(0)
    h = _normmod(x_ref[...], g_ref[...], sc_ref[...], sh_ref[...]).astype(BF16)
    z = jnp.dot(h, w_ref[...].astype(BF16), preferred_element_type=F32)
    is_sample = i >= np_tiles
    cos = cos_ref[...]
    sin = sin_ref[...]

    def norm_rope(seg, gain):
        ms = jnp.mean(seg * seg, axis=-1, keepdims=True)
        y = seg * lax.rsqrt(ms + EPS) * gain
        roped = y * cos + pltpu.roll(y, LANES // 2, 1) * sin
        return jnp.where(is_sample, roped, y)

    for hh in range(nq):
        sl = slice(hh * LANES, (hh + 1) * LANES)
        q_ref[:, sl] = norm_rope(z[:, sl], qn_ref[...]).astype(q_ref.dtype)
    k = jnp.concatenate([norm_rope(z[:, (nq + hh) * LANES:(nq + hh + 1) * LANES], kn_ref[...]) for hh in range(nk)],
                        axis=1)
    v = z[:, (nq + nk) * LANES:]

    @pl.when(i < np_tiles)
    def _():
        kp_ref[...] = k
        vp_ref[...] = v

    @pl.when(is_sample)
    def _():
        ks_ref[...] = k.astype(ks_ref.dtype)
        vs_ref[...] = v.astype(vs_ref.dtype)


def _gqa_qkv(x, g, mod, layer, w_qkv, qn, kn, cos_t, sin_t, nk, tok):
    n, d = x.shape
    hd = qn.shape[-1]
    assert hd == LANES
    nq = d // hd
    dk = nk * hd
    row = lambda i: (i, 0)
    fixed = lambda i: (0, 0)
    pos = lambda i: (jnp.maximum(i - tok.np_tiles, 0) % tok.tps, 0)
    n_p, n_s = tok.np_tiles * tok.tm, n - tok.np_tiles * tok.tm
    return pl.pallas_call(
        functools.partial(_gqa_qkv_kernel, nq=nq, nk=nk, np_tiles=tok.np_tiles),
        grid=(tok.tiles,),
        in_specs=[pl.BlockSpec((tok.tm, d), row),
                  pl.BlockSpec((1, d), fixed),
                  tok.mod_spec(layer, 1, d, 1),
                  tok.mod_spec(layer, 0, d, 1),
                  pl.BlockSpec((d, d + 2 * dk), fixed),
                  pl.BlockSpec((1, hd), fixed),
                  pl.BlockSpec((1, hd), fixed),
                  pl.BlockSpec((tok.tm, hd), pos),
                  pl.BlockSpec((tok.tm, hd), pos)],
        out_specs=[pl.BlockSpec((tok.tm, d), row),
                   tok.prompt_spec(dk, 1), tok.sample_spec(dk, 1), tok.prompt_spec(dk, 1), tok.sample_spec(dk, 1)],
        out_shape=[jax.ShapeDtypeStruct((n, d), BF16),
                   jax.ShapeDtypeStruct((n_p, dk), F32), jax.ShapeDtypeStruct((n_s, dk), BF16),
                   jax.ShapeDtypeStruct((n_p, dk), F32), jax.ShapeDtypeStruct((n_s, dk), BF16)],
        compiler_params=_cparams("arbitrary"),
        name="gqa_qkv",
    )(x, g, mod, mod, w_qkv, qn.reshape(1, hd), kn.reshape(1, hd), cos_t, sin_t)


def _rope_tables(dec_seq, hd):
    t = jnp.arange(dec_seq)
    row = (t // GRID_W).astype(F32)
    col = (t % GRID_W).astype(F32)
    half = hd // 2
    inv = ROPE_THETA ** (-jnp.arange(0, half, 2, dtype=F32) / half)
    ang = jnp.concatenate([row[:, None] * inv, col[:, None] * inv], axis=-1)
    cos, sin = jnp.cos(ang), jnp.sin(ang)
    return jnp.concatenate([cos, cos], axis=-1), jnp.concatenate([-sin, sin], axis=-1)


def _gqa_attn_kernel(q_ref, *refs, rep, nk, n_kv):
    k_refs, v_refs, o_ref = refs[:n_kv], refs[n_kv:2 * n_kv], refs[2 * n_kv]
    tq = q_ref.shape[0]
    scale = LANES ** -0.5
    for kv in range(nk):
        sl = slice(kv * LANES, (kv + 1) * LANES)
        ks = [r[:, sl].astype(BF16) for r in k_refs]
        vs = [r[:, sl].astype(BF16) for r in v_refs]
        qs = jnp.concatenate([q_ref[:, (kv * rep + r) * LANES:(kv * rep + r + 1) * LANES] for r in range(rep)], axis=0)
        o = _softmax_pv([_qk(qs, k) * scale for k in ks], vs)
        for r in range(rep):
            o_ref[:, (kv * rep + r) * LANES:(kv * rep + r + 1) * LANES] = o[r * tq:(r + 1) * tq].astype(o_ref.dtype)


def _gqa_ctx_attn(q, k, v, batch, seq, nk):
    d = q.shape[1]
    dk = k.shape[1]
    rep = d // dk
    qspec = pl.BlockSpec((seq, d), lambda b: (b, 0))
    kspec = pl.BlockSpec((seq, dk), lambda b: (b, 0))
    return pl.pallas_call(
        functools.partial(_gqa_attn_kernel, rep=rep, nk=nk, n_kv=1),
        grid=(batch,),
        in_specs=[qspec, kspec, kspec],
        out_specs=qspec,
        out_shape=jax.ShapeDtypeStruct((batch * seq, d), BF16),
        compiler_params=_cparams("arbitrary"),
        name="gqa_ctx_attn",
    )(q, k, v)


def _gqa_lat_attn(q, k, v, cache_k, cache_v, slot, n_prompt, dec_batch, dec_seq, nk):
    d = q.shape[1]
    dk = k.shape[1]
    rep = d // dk
    past = cache_k.shape[2]
    tq = _pow2_tile(256, dec_seq, n_prompt)
    nqb = dec_seq // tq
    q0 = n_prompt // tq
    lspec = pl.BlockSpec((dec_seq, dk), lambda b, qb: (b, 0))
    cspec = pl.BlockSpec((None, None, past, dk), lambda b, qb: (b, slot, 0, 0))
    return pl.pallas_call(
        functools.partial(_gqa_attn_kernel, rep=rep, nk=nk, n_kv=2),
        grid=(dec_batch, nqb),
        in_specs=[pl.BlockSpec((tq, d), lambda b, qb: (q0 + b * nqb + qb, 0)),
                  lspec, cspec, lspec, cspec],
        out_specs=pl.BlockSpec((tq, d), lambda b, qb: (b * nqb + qb, 0)),
        out_shape=jax.ShapeDtypeStruct((dec_batch * dec_seq, d), BF16),
        compiler_params=_cparams("arbitrary", "arbitrary"),
        name="gqa_lat_attn",
    )(q, k, cache_k, v, cache_v)


def _proj_res_kernel(x_ref, op_ref, os_ref, w_ref, gate_ref, out_ref, *, np_tiles):
    o = jnp.where(pl.program_id(0) < np_tiles, op_ref[...], os_ref[...])
    y = jnp.dot(o, w_ref[...].astype(BF16), preferred_element_type=F32)
    out_ref[...] = x_ref[...] + gate_ref[...] * y


def _proj_res(x, o_p, o_s, w_o, mod, layer, tok):
    n, d = x.shape
    row = lambda i: (i, 0)
    return pl.pallas_call(
        functools.partial(_proj_res_kernel, np_tiles=tok.np_tiles),
        grid=(tok.tiles,),
        in_specs=[pl.BlockSpec((tok.tm, d), row),
                  tok.prompt_spec(d, 1),
                  tok.sample_spec(d, 1),
                  pl.BlockSpec((d, d), lambda i: (0, 0)),
                  tok.mod_spec(layer, 2, d, 1)],
        out_specs=pl.BlockSpec((tok.tm, d), row),
        out_shape=jax.ShapeDtypeStruct((n, d), F32),
        compiler_params=_cparams("arbitrary"),
        name="attn_proj_res",
    )(x, o_p, o_s, w_o, mod)


def _mlp_kernel(x_ref, g_ref, sc_ref, sh_ref, gate_ref, w1_ref, w2_ref, *rest, np_tiles):
    out_refs, (h_scr, acc_scr) = rest[:-2], rest[-2:]
    i = pl.program_id(0)
    j = pl.program_id(1)
    last = j == pl.num_programs(1) - 1

    @pl.when(j == 0)
    def _():
        h_scr[...] = _normmod(x_ref[...], g_ref[...], sc_ref[...], sh_ref[...]).astype(BF16)
        acc_scr[...] = jnp.zeros_like(acc_scr)

    a = jnp.maximum(jnp.dot(h_scr[...], w1_ref[...].astype(BF16), preferred_element_type=F32), 0.0)
    acc_scr[...] += jnp.dot((a * a).astype(BF16), w2_ref[...].astype(BF16), preferred_element_type=F32)

    def result():
        return x_ref[...] + gate_ref[...] * acc_scr[...]

    if len(out_refs) == 1:
        @pl.when(last)
        def _():
            out_refs[0][...] = result()
    else:
        @pl.when(jnp.logical_and(last, i < np_tiles))
        def _():
            out_refs[0][...] = result()

        @pl.when(jnp.logical_and(last, i >= np_tiles))
        def _():
            out_refs[1][...] = result()


def _mlp(x, g, mod, layer, w1, w2, tok, split_out):
    n, d = x.shape
    f = w1.shape[1]
    tf = _pow2_tile(512, f)
    row = lambda i, j: (i, 0)
    if split_out:
        n_p = tok.np_tiles * tok.tm
        out_specs = [tok.prompt_spec(d, 2), tok.sample_spec(d, 2)]
        out_shape = [jax.ShapeDtypeStruct((n_p, d), F32), jax.ShapeDtypeStruct((n - n_p, d), F32)]
    else:
        out_specs = pl.BlockSpec((tok.tm, d), row)
        out_shape = jax.ShapeDtypeStruct((n, d), F32)
    return pl.pallas_call(
        functools.partial(_mlp_kernel, np_tiles=tok.np_tiles),
        grid=(tok.tiles, f // tf),
        in_specs=[pl.BlockSpec((tok.tm, d), row),
                  pl.BlockSpec((1, d), lambda i, j: (0, 0)),
                  tok.mod_spec(layer, 4, d, 2),
                  tok.mod_spec(layer, 3, d, 2),
                  tok.mod_spec(layer, 5, d, 2),
                  pl.BlockSpec((d, tf), lambda i, j: (0, j)),
                  pl.BlockSpec((tf, d), lambda i, j: (j, 0))],
        out_specs=out_specs,
        out_shape=out_shape,
        scratch_shapes=[pltpu.VMEM((tok.tm, d), BF16), pltpu.VMEM((tok.tm, d), F32)],
        compiler_params=_cparams("arbitrary", "arbitrary"),
        name="mlp",
    )(x, g, mod, mod, mod, w1, w2)


def _s5_tables(lam_re, lam_im, log_dt, b_re, b_im, c_re, c_im):
    t = S5_CHUNK
    g, p, k = b_re.shape[1:]
    dt = jnp.exp(log_dt.astype(F32))[:, :, None]
    lr, li = lam_re.astype(F32), lam_im.astype(F32)
    ar, ai = lr * dt, li * dt
    mag = jnp.exp(ar)
    abr, abi = mag * jnp.cos(ai), mag * jnp.sin(ai)
    nr, ni = abr - 1.0, abi
    den = lr * lr + li * li
    f_re = (nr * lr + ni * li) / den
    f_im = (ni * lr - nr * li) / den
    bbr = f_re[..., None] * b_re - f_im[..., None] * b_im
    bbi = f_re[..., None] * b_im + f_im[..., None] * b_re
    n = jnp.arange(t + 1, dtype=F32)[None, None, :, None]
    pm = jnp.exp(n * ar[:, :, None, :])
    pr, pi = pm * jnp.cos(n * ai[:, :, None, :]), pm * jnp.sin(n * ai[:, :, None, :])
    cr, ci = c_re.astype(F32), c_im.astype(F32)

    def summ(pw_r, pw_i, d):
        pw_r = pw_r.transpose(0, 2, 1)[:, :, :, None]
        pw_i = pw_i.transpose(0, 2, 1)[:, :, :, None]
        sr = pw_r * bbr[d][:, :, None, :] - pw_i * bbi[d][:, :, None, :]
        si = pw_r * bbi[d][:, :, None, :] + pw_i * bbr[d][:, :, None, :]
        return sr.reshape(g, p, t * k), si.reshape(g, p, t * k)

    sfr, sfi = summ(pr[0, :, :t][:, ::-1], pi[0, :, :t][:, ::-1], 0)
    sbr, sbi = summ(pr[1, :, :t], pi[1, :, :t], 1)
    w_sum = jnp.concatenate([sfr, sbr, sfi, sbi], axis=1)
    w_intra = _s5_toeplitz(jnp.concatenate([cr[0], -ci[0]], axis=-1), jnp.concatenate([cr[1], -ci[1]], axis=-1),
                           jnp.concatenate([sfr, sfi], axis=1), jnp.concatenate([sbr, sbi], axis=1))
    car = cr[:, :, None] * pr[:, :, 1:, None, :] - ci[:, :, None] * pi[:, :, 1:, None, :]
    cai = cr[:, :, None] * pi[:, :, 1:, None, :] + ci[:, :, None] * pr[:, :, 1:, None, :]
    rd = lambda a: a.reshape(g, t * k, p)
    w2t = jnp.concatenate([rd(car[0]), rd(car[1][:, ::-1]), -rd(cai[0]), -rd(cai[1][:, ::-1])], axis=2)
    at = jnp.stack([jnp.concatenate([pr[0, :, t], pr[1, :, t]], axis=-1),
                    jnp.concatenate([pi[0, :, t], pi[1, :, t]], axis=-1)], axis=1)
    return w_intra, w_sum.astype(BF16), w2t.astype(BF16), at


def _s5_toeplitz_kernel(cf_ref, cb_ref, abf_ref, abb_ref, o_ref):
    hi = lax.Precision.HIGHEST
    gl, k, _ = cf_ref.shape
    tk = abf_ref.shape[2]
    for g in range(gl):
        kf = jnp.dot(cf_ref[g], abf_ref[g], precision=hi, preferred_element_type=F32)
        kb = jnp.dot(cb_ref[g], abb_ref[g], precision=hi, preferred_element_type=F32)
        z = jnp.zeros_like(kf)
        krev = jnp.concatenate([kf, z], axis=1) + pltpu.roll(jnp.concatenate([z, kb], axis=1), 2 * tk - k, 1)
        for i in range(S5_CHUNK):
            sh = (S5_CHUNK - 1 - i) * k
            win = krev if sh == 0 else pltpu.roll(krev, 2 * tk - sh, 1)
            o_ref[g, i * k:(i + 1) * k, :] = win[:, 0:tk].astype(o_ref.dtype)


def _s5_toeplitz(cf, cb, abf, abb):
    g, k, p2 = cf.shape
    tk = abf.shape[2]
    gl = _pow2_tile(8, g)
    return pl.pallas_call(
        _s5_toeplitz_kernel,
        grid=(g // gl,),
        in_specs=[pl.BlockSpec((gl, k, p2), lambda a: (a, 0, 0)),
                  pl.BlockSpec((gl, k, p2), lambda a: (a, 0, 0)),
                  pl.BlockSpec((gl, p2, tk), lambda a: (a, 0, 0)),
                  pl.BlockSpec((gl, p2, tk), lambda a: (a, 0, 0))],
        out_specs=pl.BlockSpec((gl, tk, tk), lambda a: (a, 0, 0)),
        out_shape=jax.ShapeDtypeStruct((g, tk, tk), BF16),
        compiler_params=_cparams("arbitrary"),
        name="s5_toeplitz",
    )(cf, cb, abf, abb)


def _s5_mixer(x, g, mod, layer, tok, params, h0, dims):
    (lam_re, lam_im, log_dt, b_re, b_im, c_re, c_im, d_skip, glu_w, glu_b) = params
    batch, seq, dec_batch, dec_seq = dims
    n_prompt = batch * seq
    d = d_skip.shape[0]
    ngrp = d // S5_GROUP
    p = lam_re.shape[-1]
    w_intra, w_sum, w2t, at = _s5_tables(lam_re, lam_im, log_dt, b_re, b_im, c_re, c_im)
    hperm = _s5_pre(x, g, mod, layer, tok)
    h0g = h0.astype(F32).transpose(3, 2, 0, 1, 4).reshape(ngrp, 2, dec_batch, 2 * p)
    streams = ((0, seq // S5_CHUNK, batch, False, True),
               (n_prompt // S5_CHUNK, dec_seq // S5_CHUNK, dec_batch, True, False))
    yperm, fin = _s5_scan(hperm, w_intra, w_sum, w2t, at, h0g, batch, streams)
    x_new = _s5_post(x, yperm, g, mod, layer, d_skip.reshape(1, d), glu_w, glu_b, tok)
    st = fin.reshape(ngrp, 2, batch, 2, p).transpose(2, 3, 1, 0, 4)
    return x_new, st


def kernel(x_prompt, x_sample, state_s5, cache_na_k, cache_na_v, cache_gqa_k, cache_gqa_v, c, c_ctx, norm_g, ada_w, ada_b, mlp_w1, mlp_w2, s5_lam_re, s5_lam_im, s5_log_dt, s5_b_re, s5_b_im, s5_c_re, s5_c_im, s5_d, s5_glu_w, s5_glu_b, na_w_qkv, na_q_norm, na_k_norm, na_rpb, na_w_o, gqa_w_qkv, gqa_q_norm, gqa_k_norm, gqa_w_o):
    batch, seq, d = x_prompt.shape
    dec_batch, dec_seq, _ = x_sample.shape
    depth = ada_w.shape[0]
    n_prompt = batch * seq
    n_sample = dec_batch * dec_seq
    na_heads, na_hd = cache_na_k.shape[3], cache_na_k.shape[4]
    gqa_kv, gqa_hd = cache_gqa_k.shape[3], cache_gqa_k.shape[4]
    assert n_prompt % dec_seq == 0

    tok = _Tok(n_prompt, n_sample, dec_seq, 1024)
    tok_half = _Tok(n_prompt, n_sample, dec_seq, 512)

    mod_rows = -(-(1 + dec_batch) // SUBLANES) * SUBLANES
    cvec = jnp.concatenate([c_ctx[None, :], c, jnp.zeros((mod_rows - 1 - dec_batch, d), F32)], axis=0)
    mod = _modulation(cvec, ada_w, ada_b).reshape(depth, mod_rows, 6, 1, d)

    x = (x_prompt.reshape(n_prompt, d), x_sample.reshape(n_sample, d))
    cache_na_k2 = cache_na_k.reshape(cache_na_k.shape[:3] + (na_heads * na_hd,))
    cache_na_v2 = cache_na_v.reshape(cache_na_v.shape[:3] + (na_heads * na_hd,))
    cache_gqa_k2 = cache_gqa_k.reshape(cache_gqa_k.shape[:3] + (gqa_kv * gqa_hd,))
    cache_gqa_v2 = cache_gqa_v.reshape(cache_gqa_v.shape[:3] + (gqa_kv * gqa_hd,))
    cos_t, sin_t = _rope_tables(dec_seq, gqa_hd)

    new_s5, new_na_k, new_na_v, new_gqa_k, new_gqa_v = [], [], [], [], []
    for i in range(depth):
        kind, slot = i % 3, i // 3
        g1 = norm_g[i, 0].reshape(1, d)
        g2 = norm_g[i, 1].reshape(1, d)
        if kind == 0:
            params = (s5_lam_re[slot], s5_lam_im[slot], s5_log_dt[slot], s5_b_re[slot], s5_b_im[slot],
                      s5_c_re[slot], s5_c_im[slot], s5_d[slot], s5_glu_w[slot], s5_glu_b[slot])
            x, st = _s5_mixer(x, g1, mod, i, tok_half, params, state_s5[:, slot], (batch, seq, dec_batch, dec_seq))
            new_s5.append(st)
        elif kind == 1:
            assert not isinstance(x, tuple)
            q, k_p, k_s, v_p, v_s = _na_qkv(x, g1, mod, i, na_w_qkv[slot], na_q_norm[slot], na_k_norm[slot], tok)
            o_p = _na_ctx_attn(q, k_p, v_p, batch, seq, na_hd)
            bias = _na_bias_table(na_rpb[slot], dec_seq)
            o_s = _na_lat_attn(q, k_s, v_s, cache_na_k2, cache_na_v2, slot, bias, n_prompt, dec_batch, dec_seq, na_hd)
            x = _proj_res(x, o_p, o_s, na_w_o[slot], mod, i, tok)
            new_na_k.append(k_p.reshape(batch, seq, na_heads, na_hd))
            new_na_v.append(v_p.reshape(batch, seq, na_heads, na_hd))
        else:
            assert not isinstance(x, tuple)
            q, k_p, k_s, v_p, v_s = _gqa_qkv(x, g1, mod, i, gqa_w_qkv[slot], gqa_q_norm[slot], gqa_k_norm[slot],
                                             cos_t, sin_t, gqa_kv, tok_half)
            o_p = _gqa_ctx_attn(q, k_p, v_p, batch, seq, gqa_kv)
            o_s = _gqa_lat_attn(q, k_s, v_s, cache_gqa_k2, cache_gqa_v2, slot, n_prompt, dec_batch, dec_seq, gqa_kv)
            x = _proj_res(x, o_p, o_s, gqa_w_o[slot], mod, i, tok)
            new_gqa_k.append(k_p.reshape(batch, seq, gqa_kv, gqa_hd))
            new_gqa_v.append(v_p.reshape(batch, seq, gqa_kv, gqa_hd))
        x = _mlp(x, g2, mod, i, mlp_w1[i], mlp_w2[i], tok, split_out=(i == depth - 1))
    y_p, y_s = x
    return (y_p.reshape(batch, seq, d), y_s.reshape(dec_batch, dec_seq, d),
            jnp.stack(new_s5, axis=1), jnp.stack(new_na_k, axis=1), jnp.stack(new_na_v, axis=1),
            jnp.stack(new_gqa_k, axis=1), jnp.stack(new_gqa_v, axis=1))
```

```python
import functools
import math

import jax
import jax.numpy as jnp
from jax import lax
from jax.experimental import pallas as pl
from jax.experimental.pallas import tpu as pltpu

F32 = jnp.float32
BF16 = jnp.bfloat16

EPS = 1e-6
GRID_W = 64
S5_GROUP = 16
NA_KH = 8
NA_KW = 16
ROPE_THETA = 10000.0
S5_CHUNK = 16
LOG2E = math.log2(math.e)
LANES = 128
SUBLANES = 8
VMEM_LIMIT = 56 * 1024 * 1024


def _cparams(*sem):
    return pltpu.CompilerParams(dimension_semantics=sem, vmem_limit_bytes=VMEM_LIMIT)


def _pow2_tile(pref, *ns):
    t = pref
    while any(n % t for n in ns):
        t //= 2
    return t


def _normmod(x, g, sc, sh):
    ms = jnp.mean(x * x, axis=-1, keepdims=True)
    y = x * lax.rsqrt(ms + EPS) * g
    return y * (1.0 + sc) + sh


class _Tok:
    def __init__(self, n_prompt, n_sample, dec_seq, pref):
        self.tm = _pow2_tile(pref, n_prompt, dec_seq)
        self.n = n_prompt + n_sample
        self.tiles = self.n // self.tm
        self.np_tiles = n_prompt // self.tm
        self.tps = dec_seq // self.tm

    def mod_row(self, i):
        return jnp.where(i < self.np_tiles, 0, 1 + (i - self.np_tiles) // self.tps)

    def prompt_spec(self, d, nargs):
        last = self.np_tiles - 1
        if nargs == 1:
            return pl.BlockSpec((self.tm, d), lambda i: (jnp.minimum(i, last), 0))
        return pl.BlockSpec((self.tm, d), lambda i, j: (jnp.minimum(i, last), 0))

    def sample_spec(self, d, nargs):
        first = self.np_tiles
        if nargs == 1:
            return pl.BlockSpec((self.tm, d), lambda i: (jnp.maximum(i - first, 0), 0))
        return pl.BlockSpec((self.tm, d), lambda i, j: (jnp.maximum(i - first, 0), 0))

    def mod_spec(self, layer, which, d, nargs):
        if nargs == 1:
            return pl.BlockSpec((None, None, None, 1, d), lambda i: (layer, self.mod_row(i), which, 0, 0))
        return pl.BlockSpec((None, None, None, 1, d), lambda i, j: (layer, self.mod_row(i), which, 0, 0))


def _mod_kernel(c_ref, w_ref, b_ref, o_ref):
    c = c_ref[...]
    s = c * jax.nn.sigmoid(c)
    o_ref[...] = jnp.dot(s.astype(BF16), w_ref[...].astype(BF16), preferred_element_type=F32) + b_ref[...]


def _modulation(cvec, ada_w, ada_b):
    depth, d, d6 = ada_w.shape
    rows = cvec.shape[0]
    tn = _pow2_tile(2048, d6) if d6 % 2048 == 0 else d
    return pl.pallas_call(
        _mod_kernel,
        grid=(depth, d6 // tn),
        in_specs=[pl.BlockSpec((rows, d), lambda l, j: (0, 0)),
                  pl.BlockSpec((None, d, tn), lambda l, j: (l, 0, j)),
                  pl.BlockSpec((None, 1, tn), lambda l, j: (l, 0, j))],
        out_specs=pl.BlockSpec((None, rows, tn), lambda l, j: (l, 0, j)),
        out_shape=jax.ShapeDtypeStruct((depth, rows, d6), F32),
        compiler_params=_cparams("arbitrary", "arbitrary"),
        name="adaln_modulation",
    )(cvec, ada_w, ada_b.reshape(depth, 1, d6))


def _read_x(x_refs, np_tiles):
    if len(x_refs) == 1:
        return x_refs[0][...]
    return jnp.where(pl.program_id(0) < np_tiles, x_refs[0][...], x_refs[1][...])


def _x_specs(x, tok, nargs):
    if isinstance(x, tuple):
        d = x[0].shape[1]
        return [tok.prompt_spec(d, nargs), tok.sample_spec(d, nargs)], x
    d = x.shape[1]
    return [pl.BlockSpec((tok.tm, d), (lambda i: (i, 0)) if nargs == 1 else (lambda i, j: (i, 0)))], (x,)


def _s5_pre_kernel(*refs, n_x, np_tiles):
    x_refs, (g_ref, sc_ref, sh_ref, h_ref, h_scr) = refs[:n_x], refs[n_x:]
    h = _normmod(_read_x(x_refs, np_tiles), g_ref[...], sc_ref[...], sh_ref[...])
    nc = h_ref.shape[1]
    for a in range(h_scr.shape[0]):
        sl = slice(a * LANES, (a + 1) * LANES)
        h_scr[a] = h[:, sl]
        for j in range(S5_CHUNK):
            h_ref[j, :, sl] = h_scr[a, pl.ds(j, nc, stride=S5_CHUNK), :]


def _s5_pre(x, g, mod, layer, tok):
    x_specs, xs = _x_specs(x, tok, 1)
    d = xs[0].shape[1]
    nc = tok.tm // S5_CHUNK
    return pl.pallas_call(
        functools.partial(_s5_pre_kernel, n_x=len(xs), np_tiles=tok.np_tiles),
        grid=(tok.tiles,),
        in_specs=x_specs + [pl.BlockSpec((1, d), lambda i: (0, 0)),
                            tok.mod_spec(layer, 1, d, 1),
                            tok.mod_spec(layer, 0, d, 1)],
        out_specs=pl.BlockSpec((S5_CHUNK, nc, d), lambda i: (0, i, 0)),
        out_shape=jax.ShapeDtypeStruct((S5_CHUNK, tok.n // S5_CHUNK, d), F32),
        scratch_shapes=[pltpu.VMEM((d // LANES, tok.tm, LANES), F32)],
        compiler_params=_cparams("arbitrary"),
        name="s5_pre",
    )(*xs, g, mod, mod)


def _s5_scan_kernel(h_ref, wi_ref, ws_ref, w2t_ref, at_ref, h0_ref, y_ref, fin_ref, xt_scr, yt_scr, *chain_scr,
                    streams, unroll):
    ngrp = wi_ref.shape[0]
    p2 = at_ref.shape[-1]
    fwd = lax.broadcasted_iota(jnp.int32, (1, p2), 1) < (p2 // 2)
    for j in range(S5_CHUNK):
        xt_scr[j] = h_ref[j].T.astype(BF16)

    def chain(r, stream, s_scr, e_scr):
        row0, n_chunks, nb, use_h0, write_fin = stream
        cols = slice(row0, row0 + n_chunks * nb)
        c0 = pl.multiple_of(r * S5_GROUP, S5_GROUP)
        rmat = jnp.concatenate([xt_scr[j, pl.ds(c0, S5_GROUP), cols] for j in range(S5_CHUNK)], axis=0)
        zs = jnp.dot(ws_ref[r], rmat, preferred_element_type=F32)
        s_scr[0] = zs[0:p2].T
        s_scr[1] = zs[p2:].T
        ar = at_ref[r, 0:1, :]
        ai = at_ref[r, 1:2, :]
        if use_h0:
            hr, hi = h0_ref[r, 0], h0_ref[r, 1]
        else:
            hr = hi = jnp.zeros((nb, p2), F32)
        for t in range(n_chunks):
            ft = pl.ds(t, nb, stride=n_chunks)
            bt = pl.ds(n_chunks - 1 - t, nb, stride=n_chunks)
            e_scr[0, ft, :] = hr
            e_scr[1, bt, :] = hr
            e_scr[2, ft, :] = hi
            e_scr[3, bt, :] = hi
            in_re = jnp.where(fwd, s_scr[0, ft, :], s_scr[0, bt, :])
            in_im = jnp.where(fwd, s_scr[1, ft, :], s_scr[1, bt, :])
            hr, hi = ar * hr - ai * hi + in_re, ar * hi + ai * hr + in_im
        if write_fin:
            fin_ref[r, 0] = hr
            fin_ref[r, 1] = hi
        e = jnp.concatenate([jnp.where(fwd, e_scr[0], e_scr[1]), jnp.where(fwd, e_scr[2], e_scr[3])],
                            axis=1).astype(BF16)
        yt = (jnp.dot(wi_ref[r], rmat, preferred_element_type=F32)
              + lax.dot_general(w2t_ref[r], e, (((1,), (1,)), ((), ())), preferred_element_type=F32))
        for i in range(S5_CHUNK):
            yt_scr[i, pl.ds(c0, S5_GROUP), cols] = yt[i * S5_GROUP:(i + 1) * S5_GROUP]

    def body(rb, carry):
        for u in range(unroll):
            for si, stream in enumerate(streams):
                k = 2 * (u * len(streams) + si)
                chain(rb * unroll + u, stream, chain_scr[k], chain_scr[k + 1])
        return carry

    lax.fori_loop(0, ngrp // unroll, body, 0)
    for i in range(S5_CHUNK):
        y_ref[i] = yt_scr[i].T


def _s5_scan(hperm, w_intra, w_sum, w2t, at, h0, nb_fin, streams):
    t, nrows, d = hperm.shape
    g, tk, p4 = w2t.shape
    p2 = at.shape[-1]
    gl = LANES // S5_GROUP
    nb0 = h0.shape[2]
    unroll = 2
    assert sum(s[1] * s[2] for s in streams) == nrows and gl % unroll == 0
    chain_scr = []
    for _ in range(unroll):
        for s in streams:
            chain_scr += [pltpu.VMEM((2, s[1] * s[2], p2), F32), pltpu.VMEM((4, s[1] * s[2], p2), F32)]
    return pl.pallas_call(
        functools.partial(_s5_scan_kernel, streams=streams, unroll=unroll),
        grid=(d // LANES,),
        in_specs=[pl.BlockSpec((t, nrows, LANES), lambda a: (0, 0, a)),
                  pl.BlockSpec((gl, tk, tk), lambda a: (a, 0, 0)),
                  pl.BlockSpec((gl, p4, tk), lambda a: (a, 0, 0)),
                  pl.BlockSpec((gl, tk, p4), lambda a: (a, 0, 0)),
                  pl.BlockSpec((gl, 2, p2), lambda a: (a, 0, 0)),
                  pl.BlockSpec((gl, 2, nb0, p2), lambda a: (a, 0, 0, 0))],
        out_specs=[pl.BlockSpec((t, nrows, LANES), lambda a: (0, 0, a)),
                   pl.BlockSpec((gl, 2, nb_fin, p2), lambda a: (a, 0, 0, 0))],
        out_shape=[jax.ShapeDtypeStruct((t, nrows, d), F32),
                   jax.ShapeDtypeStruct((g, 2, nb_fin, p2), F32)],
        scratch_shapes=[pltpu.VMEM((t, LANES, nrows), BF16), pltpu.VMEM((t, LANES, nrows), F32)] + chain_scr,
        compiler_params=_cparams("arbitrary"),
        name="s5_scan",
    )(hperm, w_intra, w_sum, w2t, at, h0)


def _s5_post_kernel(*refs, n_x, np_tiles):
    x_refs = refs[:n_x]
    y_ref, g_ref, sc_ref, sh_ref, gate_ref, dsk_ref, wa_ref, wg_ref, ba_ref, bg_ref, o_ref, y_scr = refs[n_x:]
    nc = y_ref.shape[1]
    for a in range(y_scr.shape[0]):
        for j in range(S5_CHUNK):
            y_scr[a, pl.ds(j, nc, stride=S5_CHUNK), :] = y_ref[j, :, a * LANES:(a + 1) * LANES]
    x = _read_x(x_refs, np_tiles)
    h = _normmod(x, g_ref[...], sc_ref[...], sh_ref[...])
    y = h * dsk_ref[...] + jnp.concatenate([y_scr[a] for a in range(y_scr.shape[0])], axis=1)
    a = jax.nn.gelu(y).astype(BF16)
    za = jnp.dot(a, wa_ref[...].astype(BF16), preferred_element_type=F32) + ba_ref[...]
    zg = jnp.dot(a, wg_ref[...].astype(BF16), preferred_element_type=F32) + bg_ref[...]
    o_ref[...] = x + gate_ref[...] * (za * jax.nn.sigmoid(zg))


def _s5_post(x, y, g, mod, layer, dsk, glu_w, slot, glu_b, tok):
    x_specs, xs = _x_specs(x, tok, 1)
    n, d = tok.n, xs[0].shape[1]
    row = lambda i: (i, 0)
    fixed = lambda i: (0, 0)
    gb2 = glu_b.reshape(1, 2 * d)
    return pl.pallas_call(
        functools.partial(_s5_post_kernel, n_x=len(xs), np_tiles=tok.np_tiles),
        grid=(tok.tiles,),
        in_specs=x_specs + [
                  pl.BlockSpec((S5_CHUNK, tok.tm // S5_CHUNK, d), lambda i: (0, i, 0)),
                  pl.BlockSpec((1, d), fixed),
                  tok.mod_spec(layer, 1, d, 1),
                  tok.mod_spec(layer, 0, d, 1),
                  tok.mod_spec(layer, 2, d, 1),
                  pl.BlockSpec((1, d), fixed),
                  pl.BlockSpec((None, d, d), lambda i: (slot, 0, 0)),
                  pl.BlockSpec((None, d, d), lambda i: (slot, 0, 1)),
                  pl.BlockSpec((1, d), lambda i: (0, 0)),
                  pl.BlockSpec((1, d), lambda i: (0, 1))],
        out_specs=pl.BlockSpec((tok.tm, d), row),
        out_shape=jax.ShapeDtypeStruct((n, d), F32),
        scratch_shapes=[pltpu.VMEM((d // LANES, tok.tm, LANES), F32)],
        compiler_params=_cparams("arbitrary"),
        name="s5_post",
    )(*xs, y, g, mod, mod, mod, dsk, glu_w, glu_w, gb2, gb2)


def _na_qkv_kernel(x_ref, g_ref, sc_ref, sh_ref, w_ref, qn_ref, kn_ref, q_ref, kp_ref, ks_ref, vp_ref, vs_ref, h_scr, *,
                   hd, np_tiles):
    is_prompt = pl.program_id(0) < np_tiles
    j = pl.program_id(1)

    @pl.when(j == 0)
    def _():
        h_scr[...] = _normmod(x_ref[...], g_ref[...], sc_ref[...], sh_ref[...]).astype(BF16)

    z = jnp.dot(h_scr[...], w_ref[...].astype(BF16), preferred_element_type=F32)
    d = z.shape[1]

    def head_norm(gain_ref, out_ref):
        lo = lax.broadcasted_iota(jnp.int32, (1, LANES), 1) < hd
        for s in range(d // LANES):
            seg = z[:, s * LANES:(s + 1) * LANES]
            sq = seg * seg
            s_lo = jnp.sum(jnp.where(lo, sq, 0.0), axis=-1, keepdims=True)
            s_hi = jnp.sum(jnp.where(lo, 0.0, sq), axis=-1, keepdims=True)
            ms = jnp.where(lo, s_lo, s_hi) / hd
            out_ref[:, s * LANES:(s + 1) * LANES] = (seg * lax.rsqrt(ms + EPS) * gain_ref[...]).astype(out_ref.dtype)

    @pl.when(j == 0)
    def _():
        head_norm(qn_ref, q_ref)

    @pl.when(jnp.logical_and(j == 1, is_prompt))
    def _():
        head_norm(kn_ref, kp_ref)

    @pl.when(jnp.logical_and(j == 1, jnp.logical_not(is_prompt)))
    def _():
        head_norm(kn_ref, ks_ref)

    @pl.when(jnp.logical_and(j == 2, is_prompt))
    def _():
        vp_ref[...] = z

    @pl.when(jnp.logical_and(j == 2, jnp.logical_not(is_prompt)))
    def _():
        vs_ref[...] = z.astype(vs_ref.dtype)


def _na_qkv(x, g, mod, layer, w_qkv, slot, qn, kn, tok):
    n, d = x.shape
    hd = qn.shape[-1]
    assert 2 * hd == LANES
    row = lambda i, j: (i, 0)
    fixed = lambda i, j: (0, 0)
    qn2 = jnp.tile(qn, 2).reshape(1, LANES)
    kn2 = jnp.tile(kn, 2).reshape(1, LANES)
    n_p, n_s = tok.np_tiles * tok.tm, n - tok.np_tiles * tok.tm
    return pl.pallas_call(
        functools.partial(_na_qkv_kernel, hd=hd, np_tiles=tok.np_tiles),
        grid=(tok.tiles, 3),
        in_specs=[pl.BlockSpec((tok.tm, d), row),
                  pl.BlockSpec((1, d), fixed),
                  tok.mod_spec(layer, 1, d, 2),
                  tok.mod_spec(layer, 0, d, 2),
                  pl.BlockSpec((None, d, d), lambda i, j: (slot, 0, j)),
                  pl.BlockSpec((1, LANES), fixed),
                  pl.BlockSpec((1, LANES), fixed)],
        out_specs=[pl.BlockSpec((tok.tm, d), row),
                   tok.prompt_spec(d, 2), tok.sample_spec(d, 2), tok.prompt_spec(d, 2), tok.sample_spec(d, 2)],
        out_shape=[jax.ShapeDtypeStruct((n, d), BF16),
                   jax.ShapeDtypeStruct((n_p, d), F32), jax.ShapeDtypeStruct((n_s, d), BF16),
                   jax.ShapeDtypeStruct((n_p, d), F32), jax.ShapeDtypeStruct((n_s, d), BF16)],
        scratch_shapes=[pltpu.VMEM((tok.tm, d), BF16)],
        compiler_params=_cparams("arbitrary", "arbitrary"),
        name="na_qkv",
    )(x, g, mod, mod, w_qkv, qn2, kn2)


def _softmax2_pv(s_parts, v_parts):
    m = s_parts[0].max(axis=-1, keepdims=True)
    for s in s_parts[1:]:
        m = jnp.maximum(m, s.max(axis=-1, keepdims=True))
    den = 0.0
    acc = 0.0
    for s, v in zip(s_parts, v_parts):
        e = jnp.exp2(s - m)
        den = den + e.sum(axis=-1, keepdims=True)
        acc = acc + jnp.dot(e.astype(BF16), v, preferred_element_type=F32)
    return acc / den


def _qk(q, k):
    return lax.dot_general(q, k, (((1,), (1,)), ((), ())), preferred_element_type=F32)


def _na_ctx_attn_kernel(q_ref, k_ref, v_ref, o_ref, *, hd):
    lo = lax.broadcasted_iota(jnp.int32, (1, LANES), 1) < hd
    scale2 = hd ** -0.5 * LOG2E
    for s in range(q_ref.shape[1] // LANES):
        sl = slice(s * LANES, (s + 1) * LANES)
        q = q_ref[:, sl]
        k = k_ref[:, sl].astype(BF16)
        v = v_ref[:, sl].astype(BF16)
        zero = jnp.zeros_like(q)
        o_lo = _softmax2_pv([_qk(jnp.where(lo, q, zero), k) * scale2], [v])
        o_hi = _softmax2_pv([_qk(jnp.where(lo, zero, q), k) * scale2], [v])
        o_ref[:, sl] = jnp.where(lo, o_lo, o_hi).astype(o_ref.dtype)


def _na_ctx_attn(q, k, v, batch, seq, hd):
    d = q.shape[1]
    spec = pl.BlockSpec((seq, d), lambda b: (b, 0))
    return pl.pallas_call(
        functools.partial(_na_ctx_attn_kernel, hd=hd),
        grid=(batch,),
        in_specs=[spec, spec, spec],
        out_specs=spec,
        out_shape=jax.ShapeDtypeStruct((batch * seq, d), BF16),
        compiler_params=_cparams("arbitrary"),
        name="na_ctx_attn",
    )(q, k, v)


def _na_lat_attn_kernel(q_ref, kl_ref, vl_ref, kc_ref, vc_ref, tab_ref, o_ref, kc_scr, vc_scr, *, hd, rows, kh, tq):
    lane = lax.broadcasted_iota(jnp.int32, (1, LANES), 1)
    lo = lane < hd
    upper = (lane >= GRID_W).astype(jnp.int32)
    scale2 = hd ** -0.5 * LOG2E
    q_rows = tq // GRID_W
    w_rows = min(rows, kh + q_rows + (kh + q_rows) % 2)
    kc_scr[...] = kc_ref[...].astype(BF16)
    vc_scr[...] = vc_ref[...].astype(BF16)

    def qblock(qb, carry):
        r0 = qb * q_rows
        w0 = jnp.minimum(jnp.clip(r0 - kh // 2, 0, rows - kh) // 2 * 2, rows - w_rows)
        k0 = pl.multiple_of(w0 * GRID_W, 2 * GRID_W)
        q0 = pl.multiple_of(qb * tq, tq)
        q = q_ref[pl.ds(q0, tq), :]
        kl = kl_ref[pl.ds(k0, w_rows * GRID_W), :]
        vl = vl_ref[pl.ds(k0, w_rows * GRID_W), :]
        zero = jnp.zeros_like(q)

        def bias(hh):
            blocks = []
            for r in range(q_rows):
                rq = r0 + r
                rs = jnp.clip(rq - kh // 2, 0, rows - kh)
                tiles = []
                for kp in range(w_rows // 2):
                    rk = w0 + 2 * kp + upper
                    ok = (rk >= rs) & (rk < rs + kh)
                    tiles.append(jnp.where(ok, tab_ref[hh, w0 + 2 * kp - rq + rows - 1], -jnp.inf))
                blocks.append(jnp.concatenate(tiles, axis=1))
            return jnp.concatenate(blocks, axis=0)

        outs = []
        for hh in range(2):
            qm = jnp.where(lo, q, zero) if hh == 0 else jnp.where(lo, zero, q)
            s_loc = _qk(qm, kl) * scale2 + bias(hh)
            s_ctx = _qk(qm, kc_scr[...]) * scale2
            outs.append(_softmax2_pv([s_loc, s_ctx], [vl, vc_scr[...]]))
        o_ref[pl.ds(q0, tq), :] = jnp.where(lo, outs[0], outs[1]).astype(o_ref.dtype)
        return carry

    lax.fori_loop(0, q_ref.shape[0] // tq, qblock, 0)


def _na_lat_attn(q, k, v, cache_k, cache_v, slot, tab, n_prompt, dec_batch, dec_seq, hd):
    d = q.shape[1]
    past = cache_k.shape[2]
    rows = dec_seq // GRID_W
    kh = min(NA_KH, rows)
    tq = _pow2_tile(256, dec_seq)
    assert tq % GRID_W == 0 and rows % 2 == 0 and 2 * GRID_W == LANES and n_prompt % dec_seq == 0
    b0 = n_prompt // dec_seq
    return pl.pallas_call(
        functools.partial(_na_lat_attn_kernel, hd=hd, rows=rows, kh=kh, tq=tq),
        grid=(d // LANES, dec_batch),
        in_specs=[pl.BlockSpec((dec_seq, LANES), lambda hp, b: (b0 + b, hp)),
                  pl.BlockSpec((dec_seq, LANES), lambda hp, b: (b, hp)),
                  pl.BlockSpec((dec_seq, LANES), lambda hp, b: (b, hp)),
                  pl.BlockSpec((None, None, past, LANES), lambda hp, b: (b, slot, 0, hp)),
                  pl.BlockSpec((None, None, past, LANES), lambda hp, b: (b, slot, 0, hp)),
                  pl.BlockSpec((2,) + tab.shape[1:], lambda hp, b: (hp, 0, 0, 0))],
        out_specs=pl.BlockSpec((dec_seq, LANES), lambda hp, b: (b, hp)),
        out_shape=jax.ShapeDtypeStruct((dec_batch * dec_seq, d), BF16),
        scratch_shapes=[pltpu.VMEM((past, LANES), BF16), pltpu.VMEM((past, LANES), BF16)],
        compiler_params=_cparams("arbitrary", "arbitrary"),
        name="na_lat_attn",
    )(q, k, v, cache_k, cache_v, tab)


def _na_bias_table(rpb, dec_seq):
    rows = dec_seq // GRID_W
    kh = min(NA_KH, rows)
    nh, nd, nc = rpb.shape
    c = jnp.arange(GRID_W)
    cs = jnp.clip(c - NA_KW // 2, 0, GRID_W - NA_KW)
    col_ok = (c[None, :] >= cs[:, None]) & (c[None, :] < cs[:, None] + NA_KW)
    dcol = c[None, :] - c[:, None] + NA_KW - 1
    onehot = (dcol[None] == jnp.arange(nc)[:, None, None]).astype(F32)
    tmp = jnp.einsum('hdc,cxy->hdxy', rpb.astype(F32), onehot, precision=lax.Precision.HIGHEST)
    tmp = jnp.where(col_ok[None, None], tmp * LOG2E, -jnp.inf)
    n_e = 2 * rows - 2
    front = rows - kh
    back = n_e + 1 - nd - front
    pad = lambda n: jnp.full((nh, n, GRID_W, GRID_W), -jnp.inf, F32)
    ext = jnp.concatenate([pad(front), tmp, pad(back)], axis=1)
    return jnp.concatenate([ext[:, :-1], ext[:, 1:]], axis=-1)


def _gqa_qkv_kernel(x_ref, g_ref, sc_ref, sh_ref, w_ref, qn_ref, kn_ref, cos_ref, sin_ref,
                    q_ref, kp_ref, ks_ref, vp_ref, vs_ref, *, nq, nk, np_tiles):
    i = pl.program_id(0)
    h = _normmod(x_ref[...], g_ref[...], sc_ref[...], sh_ref[...]).astype(BF16)
    z = jnp.dot(h, w_ref[...].astype(BF16), preferred_element_type=F32)
    is_sample = i >= np_tiles
    cos = cos_ref[...]
    sin = sin_ref[...]

    def norm_rope(seg, gain):
        ms = jnp.mean(seg * seg, axis=-1, keepdims=True)
        y = seg * lax.rsqrt(ms + EPS) * gain
        roped = y * cos + pltpu.roll(y, LANES // 2, 1) * sin
        return jnp.where(is_sample, roped, y)

    for hh in range(nq):
        sl = slice(hh * LANES, (hh + 1) * LANES)
        q_ref[:, sl] = norm_rope(z[:, sl], qn_ref[...]).astype(q_ref.dtype)
    k = jnp.concatenate([norm_rope(z[:, (nq + hh) * LANES:(nq + hh + 1) * LANES], kn_ref[...]) for hh in range(nk)],
                        axis=1)
    v = z[:, (nq + nk) * LANES:]

    @pl.when(i < np_tiles)
    def _():
        kp_ref[...] = k
        vp_ref[...] = v

    @pl.when(is_sample)
    def _():
        ks_ref[...] = k.astype(ks_ref.dtype)
        vs_ref[...] = v.astype(vs_ref.dtype)


def _gqa_qkv(x, g, mod, layer, w_qkv, slot, qn, kn, cos_t, sin_t, nk, tok):
    n, d = x.shape
    hd = qn.shape[-1]
    assert hd == LANES
    nq = d // hd
    dk = nk * hd
    row = lambda i: (i, 0)
    fixed = lambda i: (0, 0)
    pos = lambda i: (jnp.maximum(i - tok.np_tiles, 0) % tok.tps, 0)
    n_p, n_s = tok.np_tiles * tok.tm, n - tok.np_tiles * tok.tm
    return pl.pallas_call(
        functools.partial(_gqa_qkv_kernel, nq=nq, nk=nk, np_tiles=tok.np_tiles),
        grid=(tok.tiles,),
        in_specs=[pl.BlockSpec((tok.tm, d), row),
                  pl.BlockSpec((1, d), fixed),
                  tok.mod_spec(layer, 1, d, 1),
                  tok.mod_spec(layer, 0, d, 1),
                  pl.BlockSpec((None, d, d + 2 * dk), lambda i: (slot, 0, 0)),
                  pl.BlockSpec((1, hd), fixed),
                  pl.BlockSpec((1, hd), fixed),
                  pl.BlockSpec((tok.tm, hd), pos),
                  pl.BlockSpec((tok.tm, hd), pos)],
        out_specs=[pl.BlockSpec((tok.tm, d), row),
                   tok.prompt_spec(dk, 1), tok.sample_spec(dk, 1), tok.prompt_spec(dk, 1), tok.sample_spec(dk, 1)],
        out_shape=[jax.ShapeDtypeStruct((n, d), BF16),
                   jax.ShapeDtypeStruct((n_p, dk), F32), jax.ShapeDtypeStruct((n_s, dk), BF16),
                   jax.ShapeDtypeStruct((n_p, dk), F32), jax.ShapeDtypeStruct((n_s, dk), BF16)],
        compiler_params=_cparams("arbitrary"),
        name="gqa_qkv",
    )(x, g, mod, mod, w_qkv, qn.reshape(1, hd), kn.reshape(1, hd), cos_t, sin_t)


def _rope_tables(dec_seq, hd):
    t = jnp.arange(dec_seq)
    row = (t // GRID_W).astype(F32)
    col = (t % GRID_W).astype(F32)
    half = hd // 2
    inv = ROPE_THETA ** (-jnp.arange(0, half, 2, dtype=F32) / half)
    ang = jnp.concatenate([row[:, None] * inv, col[:, None] * inv], axis=-1)
    cos, sin = jnp.cos(ang), jnp.sin(ang)
    return jnp.concatenate([cos, cos], axis=-1), jnp.concatenate([-sin, sin], axis=-1)


def _gqa_attn_kernel(q_ref, *refs, rep, nk, n_kv):
    k_refs, v_refs, o_ref = refs[:n_kv], refs[n_kv:2 * n_kv], refs[2 * n_kv]
    tq = q_ref.shape[0]
    scale2 = LANES ** -0.5 * LOG2E
    for kv in range(nk):
        sl = slice(kv * LANES, (kv + 1) * LANES)
        ks = [r[:, sl].astype(BF16) for r in k_refs]
        vs = [r[:, sl].astype(BF16) for r in v_refs]
        qs = jnp.concatenate([q_ref[:, (kv * rep + r) * LANES:(kv * rep + r + 1) * LANES] for r in range(rep)], axis=0)
        o = _softmax2_pv([_qk(qs, k) * scale2 for k in ks], vs)
        for r in range(rep):
            o_ref[:, (kv * rep + r) * LANES:(kv * rep + r + 1) * LANES] = o[r * tq:(r + 1) * tq].astype(o_ref.dtype)


def _gqa_ctx_attn(q, k, v, batch, seq, nk):
    d = q.shape[1]
    dk = k.shape[1]
    rep = d // dk
    qspec = pl.BlockSpec((seq, d), lambda b: (b, 0))
    kspec = pl.BlockSpec((seq, dk), lambda b: (b, 0))
    return pl.pallas_call(
        functools.partial(_gqa_attn_kernel, rep=rep, nk=nk, n_kv=1),
        grid=(batch,),
        in_specs=[qspec, kspec, kspec],
        out_specs=qspec,
        out_shape=jax.ShapeDtypeStruct((batch * seq, d), BF16),
        compiler_params=_cparams("arbitrary"),
        name="gqa_ctx_attn",
    )(q, k, v)


def _gqa_lat_attn(q, k, v, cache_k, cache_v, slot, n_prompt, dec_batch, dec_seq, nk):
    d = q.shape[1]
    dk = k.shape[1]
    rep = d // dk
    past = cache_k.shape[2]
    tq = _pow2_tile(256, dec_seq, n_prompt)
    nqb = dec_seq // tq
    q0 = n_prompt // tq
    lspec = pl.BlockSpec((dec_seq, dk), lambda b, qb: (b, 0))
    cspec = pl.BlockSpec((None, None, past, dk), lambda b, qb: (b, slot, 0, 0))
    return pl.pallas_call(
        functools.partial(_gqa_attn_kernel, rep=rep, nk=nk, n_kv=2),
        grid=(dec_batch, nqb),
        in_specs=[pl.BlockSpec((tq, d), lambda b, qb: (q0 + b * nqb + qb, 0)),
                  lspec, cspec, lspec, cspec],
        out_specs=pl.BlockSpec((tq, d), lambda b, qb: (b * nqb + qb, 0)),
        out_shape=jax.ShapeDtypeStruct((dec_batch * dec_seq, d), BF16),
        compiler_params=_cparams("arbitrary", "arbitrary"),
        name="gqa_lat_attn",
    )(q, k, cache_k, v, cache_v)


def _proj_res_kernel(x_ref, op_ref, os_ref, w_ref, gate_ref, out_ref, *, np_tiles):
    o = jnp.where(pl.program_id(0) < np_tiles, op_ref[...], os_ref[...])
    y = jnp.dot(o, w_ref[...].astype(BF16), preferred_element_type=F32)
    out_ref[...] = x_ref[...] + gate_ref[...] * y


def _proj_res(x, o_p, o_s, w_o, slot, mod, layer, tok):
    n, d = x.shape
    row = lambda i: (i, 0)
    return pl.pallas_call(
        functools.partial(_proj_res_kernel, np_tiles=tok.np_tiles),
        grid=(tok.tiles,),
        in_specs=[pl.BlockSpec((tok.tm, d), row),
                  tok.prompt_spec(d, 1),
                  tok.sample_spec(d, 1),
                  pl.BlockSpec((None, d, d), lambda i: (slot, 0, 0)),
                  tok.mod_spec(layer, 2, d, 1)],
        out_specs=pl.BlockSpec((tok.tm, d), row),
        out_shape=jax.ShapeDtypeStruct((n, d), F32),
        compiler_params=_cparams("arbitrary"),
        name="attn_proj_res",
    )(x, o_p, o_s, w_o, mod)


def _mlp_kernel(x_ref, g_ref, sc_ref, sh_ref, gate_ref, w1_ref, w2_ref, *rest, np_tiles):
    out_refs, (h_scr, acc_scr) = rest[:-2], rest[-2:]
    i = pl.program_id(0)
    j = pl.program_id(1)
    last = j == pl.num_programs(1) - 1

    @pl.when(j == 0)
    def _():
        h_scr[...] = _normmod(x_ref[...], g_ref[...], sc_ref[...], sh_ref[...]).astype(BF16)
        acc_scr[...] = jnp.zeros_like(acc_scr)

    a = jnp.maximum(jnp.dot(h_scr[...], w1_ref[...].astype(BF16), preferred_element_type=F32), 0.0)
    acc_scr[...] += jnp.dot((a * a).astype(BF16), w2_ref[...].astype(BF16), preferred_element_type=F32)

    def result():
        return x_ref[...] + gate_ref[...] * acc_scr[...]

    if len(out_refs) == 1:
        @pl.when(last)
        def _():
            out_refs[0][...] = result()
    else:
        @pl.when(jnp.logical_and(last, i < np_tiles))
        def _():
            out_refs[0][...] = result()

        @pl.when(jnp.logical_and(last, i >= np_tiles))
        def _():
            out_refs[1][...] = result()


def _mlp(x, g, mod, layer, w1, w2, tok, split_out):
    n, d = x.shape
    f = w1.shape[2]
    tf = _pow2_tile(512, f)
    row = lambda i, j: (i, 0)
    if split_out:
        n_p = tok.np_tiles * tok.tm
        out_specs = [tok.prompt_spec(d, 2), tok.sample_spec(d, 2)]
        out_shape = [jax.ShapeDtypeStruct((n_p, d), F32), jax.ShapeDtypeStruct((n - n_p, d), F32)]
    else:
        out_specs = pl.BlockSpec((tok.tm, d), row)
        out_shape = jax.ShapeDtypeStruct((n, d), F32)
    return pl.pallas_call(
        functools.partial(_mlp_kernel, np_tiles=tok.np_tiles),
        grid=(tok.tiles, f // tf),
        in_specs=[pl.BlockSpec((tok.tm, d), row),
                  pl.BlockSpec((1, d), lambda i, j: (0, 0)),
                  tok.mod_spec(layer, 4, d, 2),
                  tok.mod_spec(layer, 3, d, 2),
                  tok.mod_spec(layer, 5, d, 2),
                  pl.BlockSpec((None, d, tf), lambda i, j: (layer, 0, j)),
                  pl.BlockSpec((None, tf, d), lambda i, j: (layer, j, 0))],
        out_specs=out_specs,
        out_shape=out_shape,
        scratch_shapes=[pltpu.VMEM((tok.tm, d), BF16), pltpu.VMEM((tok.tm, d), F32)],
        compiler_params=_cparams("arbitrary", "arbitrary"),
        name="mlp",
    )(x, g, mod, mod, mod, w1, w2)


def _s5_tables(lam_re, lam_im, log_dt, b_re, b_im, c_re, c_im):
    t = S5_CHUNK
    g, p, k = b_re.shape[1:]
    dt = jnp.exp(log_dt.astype(F32))[:, :, None]
    lr, li = lam_re.astype(F32), lam_im.astype(F32)
    ar, ai = lr * dt, li * dt
    mag = jnp.exp(ar)
    abr, abi = mag * jnp.cos(ai), mag * jnp.sin(ai)
    nr, ni = abr - 1.0, abi
    den = lr * lr + li * li
    f_re = (nr * lr + ni * li) / den
    f_im = (ni * lr - nr * li) / den
    bbr = f_re[..., None] * b_re - f_im[..., None] * b_im
    bbi = f_re[..., None] * b_im + f_im[..., None] * b_re
    n = jnp.arange(t + 1, dtype=F32)[None, None, :, None]
    pm = jnp.exp(n * ar[:, :, None, :])
    pr, pi = pm * jnp.cos(n * ai[:, :, None, :]), pm * jnp.sin(n * ai[:, :, None, :])
    cr, ci = c_re.astype(F32), c_im.astype(F32)

    def summ(pw_r, pw_i, d):
        pw_r = pw_r.transpose(0, 2, 1)[:, :, :, None]
        pw_i = pw_i.transpose(0, 2, 1)[:, :, :, None]
        sr = pw_r * bbr[d][:, :, None, :] - pw_i * bbi[d][:, :, None, :]
        si = pw_r * bbi[d][:, :, None, :] + pw_i * bbr[d][:, :, None, :]
        return sr.reshape(g, p, t * k), si.reshape(g, p, t * k)

    sfr, sfi = summ(pr[0, :, :t][:, ::-1], pi[0, :, :t][:, ::-1], 0)
    sbr, sbi = summ(pr[1, :, :t], pi[1, :, :t], 1)
    w_sum = jnp.concatenate([sfr, sbr, sfi, sbi], axis=1)
    w_intra = _s5_toeplitz(jnp.concatenate([cr[0], -ci[0]], axis=-1), jnp.concatenate([cr[1], -ci[1]], axis=-1),
                           jnp.concatenate([sfr, sfi], axis=1), jnp.concatenate([sbr, sbi], axis=1))
    car = cr[:, :, None] * pr[:, :, 1:, None, :] - ci[:, :, None] * pi[:, :, 1:, None, :]
    cai = cr[:, :, None] * pi[:, :, 1:, None, :] + ci[:, :, None] * pr[:, :, 1:, None, :]
    rd = lambda a: a.reshape(g, t * k, p)
    w2t = jnp.concatenate([rd(car[0]), rd(car[1][:, ::-1]), -rd(cai[0]), -rd(cai[1][:, ::-1])], axis=2)
    at = jnp.stack([jnp.concatenate([pr[0, :, t], pr[1, :, t]], axis=-1),
                    jnp.concatenate([pi[0, :, t], pi[1, :, t]], axis=-1)], axis=1)
    return w_intra, w_sum.astype(BF16), w2t.astype(BF16), at


def _s5_toeplitz_kernel(cf_ref, cb_ref, abf_ref, abb_ref, o_ref):
    hi = lax.Precision.HIGHEST
    gl, k, _ = cf_ref.shape
    tk = abf_ref.shape[2]
    for g in range(gl):
        kf = jnp.dot(cf_ref[g], abf_ref[g], precision=hi, preferred_element_type=F32)
        kb = jnp.dot(cb_ref[g], abb_ref[g], precision=hi, preferred_element_type=F32)
        z = jnp.zeros_like(kf)
        krev = jnp.concatenate([kf, z], axis=1) + pltpu.roll(jnp.concatenate([z, kb], axis=1), 2 * tk - k, 1)
        for i in range(S5_CHUNK):
            sh = (S5_CHUNK - 1 - i) * k
            win = krev if sh == 0 else pltpu.roll(krev, 2 * tk - sh, 1)
            o_ref[g, i * k:(i + 1) * k, :] = win[:, 0:tk].astype(o_ref.dtype)


def _s5_toeplitz(cf, cb, abf, abb):
    g, k, p2 = cf.shape
    tk = abf.shape[2]
    gl = _pow2_tile(8, g)
    return pl.pallas_call(
        _s5_toeplitz_kernel,
        grid=(g // gl,),
        in_specs=[pl.BlockSpec((gl, k, p2), lambda a: (a, 0, 0)),
                  pl.BlockSpec((gl, k, p2), lambda a: (a, 0, 0)),
                  pl.BlockSpec((gl, p2, tk), lambda a: (a, 0, 0)),
                  pl.BlockSpec((gl, p2, tk), lambda a: (a, 0, 0))],
        out_specs=pl.BlockSpec((gl, tk, tk), lambda a: (a, 0, 0)),
        out_shape=jax.ShapeDtypeStruct((g, tk, tk), BF16),
        compiler_params=_cparams("arbitrary"),
        name="s5_toeplitz",
    )(cf, cb, abf, abb)


def _s5_mixer(x, g, mod, layer, tok, params, h0, dims):
    (lam_re, lam_im, log_dt, b_re, b_im, c_re, c_im, d_skip, glu_w, slot, glu_b) = params
    batch, seq, dec_batch, dec_seq = dims
    n_prompt = batch * seq
    d = d_skip.shape[0]
    ngrp = d // S5_GROUP
    p = lam_re.shape[-1]
    w_intra, w_sum, w2t, at = _s5_tables(lam_re, lam_im, log_dt, b_re, b_im, c_re, c_im)
    hperm = _s5_pre(x, g, mod, layer, tok)
    h0g = h0.astype(F32).transpose(3, 2, 0, 1, 4).reshape(ngrp, 2, dec_batch, 2 * p)
    streams = ((0, seq // S5_CHUNK, batch, False, True),
               (n_prompt // S5_CHUNK, dec_seq // S5_CHUNK, dec_batch, True, False))
    yperm, fin = _s5_scan(hperm, w_intra, w_sum, w2t, at, h0g, batch, streams)
    x_new = _s5_post(x, yperm, g, mod, layer, d_skip.reshape(1, d), glu_w, slot, glu_b, tok)
    st = fin.reshape(ngrp, 2, batch, 2, p).transpose(2, 3, 1, 0, 4)
    return x_new, st


def kernel(x_prompt, x_sample, state_s5, cache_na_k, cache_na_v, cache_gqa_k, cache_gqa_v, c, c_ctx, norm_g, ada_w, ada_b, mlp_w1, mlp_w2, s5_lam_re, s5_lam_im, s5_log_dt, s5_b_re, s5_b_im, s5_c_re, s5_c_im, s5_d, s5_glu_w, s5_glu_b, na_w_qkv, na_q_norm, na_k_norm, na_rpb, na_w_o, gqa_w_qkv, gqa_q_norm, gqa_k_norm, gqa_w_o):
    batch, seq, d = x_prompt.shape
    dec_batch, dec_seq, _ = x_sample.shape
    depth = ada_w.shape[0]
    n_prompt = batch * seq
    n_sample = dec_batch * dec_seq
    na_heads, na_hd = cache_na_k.shape[3], cache_na_k.shape[4]
    gqa_kv, gqa_hd = cache_gqa_k.shape[3], cache_gqa_k.shape[4]
    assert n_prompt % dec_seq == 0

    tok = _Tok(n_prompt, n_sample, dec_seq, 1024)
    tok_half = _Tok(n_prompt, n_sample, dec_seq, 512)

    mod_rows = -(-(1 + dec_batch) // SUBLANES) * SUBLANES
    cvec = jnp.concatenate([c_ctx[None, :], c, jnp.zeros((mod_rows - 1 - dec_batch, d), F32)], axis=0)
    mod = _modulation(cvec, ada_w, ada_b).reshape(depth, mod_rows, 6, 1, d)

    x = (x_prompt.reshape(n_prompt, d), x_sample.reshape(n_sample, d))
    cache_na_k2 = cache_na_k.reshape(cache_na_k.shape[:3] + (na_heads * na_hd,))
    cache_na_v2 = cache_na_v.reshape(cache_na_v.shape[:3] + (na_heads * na_hd,))
    cache_gqa_k2 = cache_gqa_k.reshape(cache_gqa_k.shape[:3] + (gqa_kv * gqa_hd,))
    cache_gqa_v2 = cache_gqa_v.reshape(cache_gqa_v.shape[:3] + (gqa_kv * gqa_hd,))
    cos_t, sin_t = _rope_tables(dec_seq, gqa_hd)

    new_s5, new_na_k, new_na_v, new_gqa_k, new_gqa_v = [], [], [], [], []
    for i in range(depth):
        kind, slot = i % 3, i // 3
        g1 = norm_g[i, 0].reshape(1, d)
        g2 = norm_g[i, 1].reshape(1, d)
        if kind == 0:
            params = (s5_lam_re[slot], s5_lam_im[slot], s5_log_dt[slot], s5_b_re[slot], s5_b_im[slot],
                      s5_c_re[slot], s5_c_im[slot], s5_d[slot], s5_glu_w, slot, s5_glu_b[slot])
            x, st = _s5_mixer(x, g1, mod, i, tok_half, params, state_s5[:, slot], (batch, seq, dec_batch, dec_seq))
            new_s5.append(st)
        elif kind == 1:
            assert not isinstance(x, tuple)
            q, k_p, k_s, v_p, v_s = _na_qkv(x, g1, mod, i, na_w_qkv, slot, na_q_norm[slot], na_k_norm[slot], tok)
            o_p = _na_ctx_attn(q, k_p, v_p, batch, seq, na_hd)
            bias = _na_bias_table(na_rpb[slot], dec_seq)
            o_s = _na_lat_attn(q, k_s, v_s, cache_na_k2, cache_na_v2, slot, bias, n_prompt, dec_batch, dec_seq, na_hd)
            x = _proj_res(x, o_p, o_s, na_w_o, slot, mod, i, tok)
            new_na_k.append(k_p.reshape(batch, seq, na_heads, na_hd))
            new_na_v.append(v_p.reshape(batch, seq, na_heads, na_hd))
        else:
            assert not isinstance(x, tuple)
            q, k_p, k_s, v_p, v_s = _gqa_qkv(x, g1, mod, i, gqa_w_qkv, slot, gqa_q_norm[slot], gqa_k_norm[slot],
                                             cos_t, sin_t, gqa_kv, tok_half)
            o_p = _gqa_ctx_attn(q, k_p, v_p, batch, seq, gqa_kv)
            o_s = _gqa_lat_attn(q, k_s, v_s, cache_gqa_k2, cache_gqa_v2, slot, n_prompt, dec_batch, dec_seq, gqa_kv)
            x = _proj_res(x, o_p, o_s, gqa_w_o, slot, mod, i, tok)
            new_gqa_k.append(k_p.reshape(batch, seq, gqa_kv, gqa_hd))
            new_gqa_v.append(v_p.reshape(batch, seq, gqa_kv, gqa_hd))
        x = _mlp(x, g2, mod, i, mlp_w1, mlp_w2, tok, split_out=(i == depth - 1))
    y_p, y_s = x
    return (y_p.reshape(batch, seq, d), y_s.reshape(dec_batch, dec_seq, d),
            jnp.stack(new_s5, axis=1), jnp.stack(new_na_k, axis=1), jnp.stack(new_na_v, axis=1),
            jnp.stack(new_gqa_k, axis=1), jnp.stack(new_gqa_v, axis=1))
```

```python
import functools
import math

import jax
import jax.numpy as jnp
from jax import lax
from jax.experimental import pallas as pl
from jax.experimental.pallas import tpu as pltpu

F32 = jnp.float32
BF16 = jnp.bfloat16

EPS = 1e-6
GRID_W = 64
S5_GROUP = 16
NA_KH = 8
NA_KW = 16
ROPE_THETA = 10000.0
S5_CHUNK = 16
LOG2E = math.log2(math.e)
LANES = 128
SUBLANES = 8
VMEM_LIMIT = 56 * 1024 * 1024


def _cparams(*sem):
    return pltpu.CompilerParams(dimension_semantics=sem, vmem_limit_bytes=VMEM_LIMIT)


def _pow2_tile(pref, *ns):
    t = pref
    while any(n % t for n in ns):
        t //= 2
    return t


def _normmod(x, g, sc, sh):
    ms = jnp.mean(x * x, axis=-1, keepdims=True)
    y = x * lax.rsqrt(ms + EPS) * g
    return y * (1.0 + sc) + sh


class _Tok:
    def __init__(self, n_prompt, n_sample, dec_seq, pref):
        self.tm = _pow2_tile(pref, n_prompt, dec_seq)
        self.n = n_prompt + n_sample
        self.tiles = self.n // self.tm
        self.np_tiles = n_prompt // self.tm
        self.tps = dec_seq // self.tm

    def mod_row(self, i):
        return jnp.where(i < self.np_tiles, 0, 1 + (i - self.np_tiles) // self.tps)

    def prompt_spec(self, d, nargs):
        last = self.np_tiles - 1
        if nargs == 1:
            return pl.BlockSpec((self.tm, d), lambda i: (jnp.minimum(i, last), 0))
        return pl.BlockSpec((self.tm, d), lambda i, j: (jnp.minimum(i, last), 0))

    def sample_spec(self, d, nargs):
        first = self.np_tiles
        if nargs == 1:
            return pl.BlockSpec((self.tm, d), lambda i: (jnp.maximum(i - first, 0), 0))
        return pl.BlockSpec((self.tm, d), lambda i, j: (jnp.maximum(i - first, 0), 0))

    def mod_spec(self, layer, which, d, nargs):
        if nargs == 1:
            return pl.BlockSpec((None, None, None, 1, d), lambda i: (layer, self.mod_row(i), which, 0, 0))
        return pl.BlockSpec((None, None, None, 1, d), lambda i, j: (layer, self.mod_row(i), which, 0, 0))


def _mod_kernel(c_ref, w_ref, b_ref, o_ref):
    c = c_ref[...]
    s = c * jax.nn.sigmoid(c)
    o_ref[...] = jnp.dot(s.astype(BF16), w_ref[...].astype(BF16), preferred_element_type=F32) + b_ref[...]


def _modulation(cvec, ada_w, ada_b):
    depth, d, d6 = ada_w.shape
    rows = cvec.shape[0]
    tn = _pow2_tile(2048, d6) if d6 % 2048 == 0 else d
    return pl.pallas_call(
        _mod_kernel,
        grid=(depth, d6 // tn),
        in_specs=[pl.BlockSpec((rows, d), lambda l, j: (0, 0)),
                  pl.BlockSpec((None, d, tn), lambda l, j: (l, 0, j)),
                  pl.BlockSpec((None, 1, tn), lambda l, j: (l, 0, j))],
        out_specs=pl.BlockSpec((None, rows, tn), lambda l, j: (l, 0, j)),
        out_shape=jax.ShapeDtypeStruct((depth, rows, d6), F32),
        compiler_params=_cparams("arbitrary", "arbitrary"),
        name="adaln_modulation",
    )(cvec, ada_w, ada_b.reshape(depth, 1, d6))


def _read_x(x_refs, np_tiles):
    if len(x_refs) == 1:
        return x_refs[0][...]
    return jnp.where(pl.program_id(0) < np_tiles, x_refs[0][...], x_refs[1][...])


def _x_specs(x, tok, nargs):
    if isinstance(x, tuple):
        d = x[0].shape[1]
        return [tok.prompt_spec(d, nargs), tok.sample_spec(d, nargs)], x
    d = x.shape[1]
    return [pl.BlockSpec((tok.tm, d), (lambda i: (i, 0)) if nargs == 1 else (lambda i, j: (i, 0)))], (x,)


def _s5_pre_kernel(*refs, n_x, np_tiles):
    x_refs, (g_ref, sc_ref, sh_ref, h_ref, h_scr) = refs[:n_x], refs[n_x:]
    h = _normmod(_read_x(x_refs, np_tiles), g_ref[...], sc_ref[...], sh_ref[...])
    nc = h_ref.shape[1]
    for a in range(h_scr.shape[0]):
        sl = slice(a * LANES, (a + 1) * LANES)
        h_scr[a] = h[:, sl]
        for j in range(S5_CHUNK):
            h_ref[j, :, sl] = h_scr[a, pl.ds(j, nc, stride=S5_CHUNK), :]


def _s5_pre(x, g, mod, layer, tok):
    x_specs, xs = _x_specs(x, tok, 1)
    d = xs[0].shape[1]
    nc = tok.tm // S5_CHUNK
    return pl.pallas_call(
        functools.partial(_s5_pre_kernel, n_x=len(xs), np_tiles=tok.np_tiles),
        grid=(tok.tiles,),
        in_specs=x_specs + [pl.BlockSpec((1, d), lambda i: (0, 0)),
                            tok.mod_spec(layer, 1, d, 1),
                            tok.mod_spec(layer, 0, d, 1)],
        out_specs=pl.BlockSpec((S5_CHUNK, nc, d), lambda i: (0, i, 0)),
        out_shape=jax.ShapeDtypeStruct((S5_CHUNK, tok.n // S5_CHUNK, d), F32),
        scratch_shapes=[pltpu.VMEM((d // LANES, tok.tm, LANES), F32)],
        compiler_params=_cparams("arbitrary"),
        name="s5_pre",
    )(*xs, g, mod, mod)


def _s5_scan_kernel(h_ref, wi_ref, ws_ref, w2t_ref, at_ref, h0_ref, *rest, streams, unroll):
    ns = len(streams)
    perm_refs, (y_ref, fin_ref, xt_scr, yt_scr), chain_scr = rest[:ns], rest[ns:ns + 4], rest[ns + 4:]
    ngrp = wi_ref.shape[0]
    p2 = at_ref.shape[-1]
    fwd = lax.broadcasted_iota(jnp.int32, (1, p2), 1) < (p2 // 2)
    for j in range(S5_CHUNK):
        xt_scr[j] = h_ref[j].T.astype(BF16)

    def chain(r, stream, perm_ref, s_scr, e_scr):
        row0, n_chunks, nb, use_h0, write_fin = stream
        cols = slice(row0, row0 + n_chunks * nb)
        c0 = pl.multiple_of(r * S5_GROUP, S5_GROUP)
        rmat = jnp.concatenate([xt_scr[j, pl.ds(c0, S5_GROUP), cols] for j in range(S5_CHUNK)], axis=0)
        rmat_cm = jnp.dot(rmat, perm_ref[...], preferred_element_type=F32).astype(BF16)
        zs = jnp.dot(ws_ref[r], rmat_cm, preferred_element_type=F32)
        s_scr[0] = zs[0:p2].T
        s_scr[1] = zs[p2:].T
        ar = at_ref[r, 0:1, :]
        ai = at_ref[r, 1:2, :]
        if use_h0:
            hr, hi = h0_ref[r, 0], h0_ref[r, 1]
        else:
            hr = hi = jnp.zeros((nb, p2), F32)
        for t in range(n_chunks):
            ft = slice(t * nb, (t + 1) * nb)
            bt = slice((n_chunks - 1 - t) * nb, (n_chunks - t) * nb)
            e_scr[0, ft, :] = hr
            e_scr[1, bt, :] = hr
            e_scr[2, ft, :] = hi
            e_scr[3, bt, :] = hi
            in_re = jnp.where(fwd, s_scr[0, ft, :], s_scr[0, bt, :])
            in_im = jnp.where(fwd, s_scr[1, ft, :], s_scr[1, bt, :])
            hr, hi = ar * hr - ai * hi + in_re, ar * hi + ai * hr + in_im
        if write_fin:
            fin_ref[r, 0] = hr
            fin_ref[r, 1] = hi
        e_cm = jnp.concatenate([jnp.where(fwd, e_scr[0], e_scr[1]), jnp.where(fwd, e_scr[2], e_scr[3])],
                               axis=1).astype(BF16)
        e = jnp.dot(perm_ref[...], e_cm, preferred_element_type=F32).astype(BF16)
        yt = (jnp.dot(wi_ref[r], rmat, preferred_element_type=F32)
              + lax.dot_general(w2t_ref[r], e, (((1,), (1,)), ((), ())), preferred_element_type=F32))
        for i in range(S5_CHUNK):
            yt_scr[i, pl.ds(c0, S5_GROUP), cols] = yt[i * S5_GROUP:(i + 1) * S5_GROUP]

    def body(rb, carry):
        for u in range(unroll):
            for si, stream in enumerate(streams):
                k = 2 * (u * ns + si)
                chain(rb * unroll + u, stream, perm_refs[si], chain_scr[k], chain_scr[k + 1])
        return carry

    lax.fori_loop(0, ngrp // unroll, body, 0)
    for i in range(S5_CHUNK):
        y_ref[i] = yt_scr[i].T


def _s5_scan(hperm, w_intra, w_sum, w2t, at, h0, nb_fin, streams):
    t, nrows, d = hperm.shape
    g, tk, p4 = w2t.shape
    p2 = at.shape[-1]
    gl = LANES // S5_GROUP
    nb0 = h0.shape[2]
    unroll = 2
    assert sum(s[1] * s[2] for s in streams) == nrows and gl % unroll == 0
    chain_scr = []
    for _ in range(unroll):
        for s in streams:
            chain_scr += [pltpu.VMEM((2, s[1] * s[2], p2), F32), pltpu.VMEM((4, s[1] * s[2], p2), F32)]
    perms = []
    for _, n_chunks, nb, _, _ in streams:
        m = jnp.arange(n_chunks * nb)
        perms.append((m[:, None] == ((m % nb) * n_chunks + m // nb)[None, :]).astype(BF16))
    return pl.pallas_call(
        functools.partial(_s5_scan_kernel, streams=streams, unroll=unroll),
        grid=(d // LANES,),
        in_specs=[pl.BlockSpec((t, nrows, LANES), lambda a: (0, 0, a)),
                  pl.BlockSpec((gl, tk, tk), lambda a: (a, 0, 0)),
                  pl.BlockSpec((gl, p4, tk), lambda a: (a, 0, 0)),
                  pl.BlockSpec((gl, tk, p4), lambda a: (a, 0, 0)),
                  pl.BlockSpec((gl, 2, p2), lambda a: (a, 0, 0)),
                  pl.BlockSpec((gl, 2, nb0, p2), lambda a: (a, 0, 0, 0))]
                 + [pl.BlockSpec(pm.shape, lambda a: (0, 0)) for pm in perms],
        out_specs=[pl.BlockSpec((t, nrows, LANES), lambda a: (0, 0, a)),
                   pl.BlockSpec((gl, 2, nb_fin, p2), lambda a: (a, 0, 0, 0))],
        out_shape=[jax.ShapeDtypeStruct((t, nrows, d), F32),
                   jax.ShapeDtypeStruct((g, 2, nb_fin, p2), F32)],
        scratch_shapes=[pltpu.VMEM((t, LANES, nrows), BF16), pltpu.VMEM((t, LANES, nrows), F32)] + chain_scr,
        compiler_params=_cparams("arbitrary"),
        name="s5_scan",
    )(hperm, w_intra, w_sum, w2t, at, h0, *perms)


def _s5_post_kernel(*refs, n_x, np_tiles):
    x_refs = refs[:n_x]
    (y_ref, g_ref, sc_ref, sh_ref, gate_ref, dsk_ref, wa_ref, wg_ref, ba_ref, bg_ref, o_ref,
     y_scr, wa_scr, wg_scr) = refs[n_x:]
    _cast_once(wa_ref, wa_scr)
    _cast_once(wg_ref, wg_scr)
    nc = y_ref.shape[1]
    for a in range(y_scr.shape[0]):
        for j in range(S5_CHUNK):
            y_scr[a, pl.ds(j, nc, stride=S5_CHUNK), :] = y_ref[j, :, a * LANES:(a + 1) * LANES]
    x = _read_x(x_refs, np_tiles)
    h = _normmod(x, g_ref[...], sc_ref[...], sh_ref[...])
    y = h * dsk_ref[...] + jnp.concatenate([y_scr[a] for a in range(y_scr.shape[0])], axis=1)
    a = jax.nn.gelu(y).astype(BF16)
    za = jnp.dot(a, wa_scr[...], preferred_element_type=F32) + ba_ref[...]
    zg = jnp.dot(a, wg_scr[...], preferred_element_type=F32) + bg_ref[...]
    o_ref[...] = x + gate_ref[...] * (za * jax.nn.sigmoid(zg))


def _s5_post(x, y, g, mod, layer, dsk, glu_w, slot, glu_b, tok):
    x_specs, xs = _x_specs(x, tok, 1)
    n, d = tok.n, xs[0].shape[1]
    row = lambda i: (i, 0)
    fixed = lambda i: (0, 0)
    gb2 = glu_b.reshape(1, 2 * d)
    return pl.pallas_call(
        functools.partial(_s5_post_kernel, n_x=len(xs), np_tiles=tok.np_tiles),
        grid=(tok.tiles,),
        in_specs=x_specs + [
                  pl.BlockSpec((S5_CHUNK, tok.tm // S5_CHUNK, d), lambda i: (0, i, 0)),
                  pl.BlockSpec((1, d), fixed),
                  tok.mod_spec(layer, 1, d, 1),
                  tok.mod_spec(layer, 0, d, 1),
                  tok.mod_spec(layer, 2, d, 1),
                  pl.BlockSpec((1, d), fixed),
                  _resident((None, d, d), lambda i: (slot, 0, 0)),
                  _resident((None, d, d), lambda i: (slot, 0, 1)),
                  pl.BlockSpec((1, d), lambda i: (0, 0)),
                  pl.BlockSpec((1, d), lambda i: (0, 1))],
        out_specs=pl.BlockSpec((tok.tm, d), row),
        out_shape=jax.ShapeDtypeStruct((n, d), F32),
        scratch_shapes=[pltpu.VMEM((d // LANES, tok.tm, LANES), F32),
                        pltpu.VMEM((d, d), BF16), pltpu.VMEM((d, d), BF16)],
        compiler_params=_cparams("arbitrary"),
        name="s5_post",
    )(*xs, y, g, mod, mod, mod, dsk, glu_w, glu_w, gb2, gb2)


def _cast_once(w_ref, w_scr):
    @pl.when(pl.program_id(0) == 0)
    def _():
        w_scr[...] = w_ref[...].astype(w_scr.dtype)


def _resident(block_shape, index_map):
    return pl.BlockSpec(block_shape, index_map, pipeline_mode=pl.Buffered(1))


def _na_qkv_kernel(x_ref, g_ref, sc_ref, sh_ref, w_ref, qn_ref, kn_ref, q_ref, kp_ref, ks_ref, vp_ref, vs_ref, w_scr, *,
                   hd, np_tiles):
    i = pl.program_id(0)
    _cast_once(w_ref, w_scr)
    h = _normmod(x_ref[...], g_ref[...], sc_ref[...], sh_ref[...]).astype(BF16)
    d = h.shape[1]
    lo = lax.broadcasted_iota(jnp.int32, (1, LANES), 1) < hd

    def head_norm(z, gain):
        outs = []
        for s in range(d // LANES):
            seg = z[:, s * LANES:(s + 1) * LANES]
            sq = seg * seg
            s_lo = jnp.sum(jnp.where(lo, sq, 0.0), axis=-1, keepdims=True)
            s_hi = jnp.sum(jnp.where(lo, 0.0, sq), axis=-1, keepdims=True)
            ms = jnp.where(lo, s_lo, s_hi) / hd
            outs.append(seg * lax.rsqrt(ms + EPS) * gain)
        return jnp.concatenate(outs, axis=1)

    proj = lambda part: jnp.dot(h, w_scr[:, part * d:(part + 1) * d], preferred_element_type=F32)
    q_ref[...] = head_norm(proj(0), qn_ref[...]).astype(q_ref.dtype)
    k = head_norm(proj(1), kn_ref[...])
    v = proj(2)

    @pl.when(i < np_tiles)
    def _():
        kp_ref[...] = k
        vp_ref[...] = v

    @pl.when(i >= np_tiles)
    def _():
        ks_ref[...] = k.astype(ks_ref.dtype)
        vs_ref[...] = v.astype(vs_ref.dtype)


def _na_qkv(x, g, mod, layer, w_qkv, slot, qn, kn, tok):
    n, d = x.shape
    hd = qn.shape[-1]
    assert 2 * hd == LANES
    row = lambda i: (i, 0)
    fixed = lambda i: (0, 0)
    qn2 = jnp.tile(qn, 2).reshape(1, LANES)
    kn2 = jnp.tile(kn, 2).reshape(1, LANES)
    n_p, n_s = tok.np_tiles * tok.tm, n - tok.np_tiles * tok.tm
    return pl.pallas_call(
        functools.partial(_na_qkv_kernel, hd=hd, np_tiles=tok.np_tiles),
        grid=(tok.tiles,),
        in_specs=[pl.BlockSpec((tok.tm, d), row),
                  pl.BlockSpec((1, d), fixed),
                  tok.mod_spec(layer, 1, d, 1),
                  tok.mod_spec(layer, 0, d, 1),
                  _resident((None, d, 3 * d), lambda i: (slot, 0, 0)),
                  pl.BlockSpec((1, LANES), fixed),
                  pl.BlockSpec((1, LANES), fixed)],
        out_specs=[pl.BlockSpec((tok.tm, d), row),
                   tok.prompt_spec(d, 1), tok.sample_spec(d, 1), tok.prompt_spec(d, 1), tok.sample_spec(d, 1)],
        out_shape=[jax.ShapeDtypeStruct((n, d), BF16),
                   jax.ShapeDtypeStruct((n_p, d), F32), jax.ShapeDtypeStruct((n_s, d), BF16),
                   jax.ShapeDtypeStruct((n_p, d), F32), jax.ShapeDtypeStruct((n_s, d), BF16)],
        scratch_shapes=[pltpu.VMEM((d, 3 * d), BF16)],
        compiler_params=_cparams("arbitrary"),
        name="na_qkv",
    )(x, g, mod, mod, w_qkv, qn2, kn2)


def _softmax2_pv(s_parts, v_parts):
    m = s_parts[0].max(axis=-1, keepdims=True)
    for s in s_parts[1:]:
        m = jnp.maximum(m, s.max(axis=-1, keepdims=True))
    den = 0.0
    acc = 0.0
    for s, v in zip(s_parts, v_parts):
        e = jnp.exp2(s - m)
        den = den + e.sum(axis=-1, keepdims=True)
        acc = acc + jnp.dot(e.astype(BF16), v, preferred_element_type=F32)
    return acc / den


def _qk(q, k):
    return lax.dot_general(q, k, (((1,), (1,)), ((), ())), preferred_element_type=F32)


def _na_ctx_attn_kernel(q_ref, k_ref, v_ref, o_ref, *, hd):
    lo = lax.broadcasted_iota(jnp.int32, (1, LANES), 1) < hd
    scale2 = hd ** -0.5 * LOG2E
    for s in range(q_ref.shape[1] // LANES):
        sl = slice(s * LANES, (s + 1) * LANES)
        q = q_ref[:, sl]
        k = k_ref[:, sl].astype(BF16)
        v = v_ref[:, sl].astype(BF16)
        zero = jnp.zeros_like(q)
        o_lo = _softmax2_pv([_qk(jnp.where(lo, q, zero), k) * scale2], [v])
        o_hi = _softmax2_pv([_qk(jnp.where(lo, zero, q), k) * scale2], [v])
        o_ref[:, sl] = jnp.where(lo, o_lo, o_hi).astype(o_ref.dtype)


def _na_ctx_attn(q, k, v, batch, seq, hd):
    d = q.shape[1]
    spec = pl.BlockSpec((seq, d), lambda b: (b, 0))
    return pl.pallas_call(
        functools.partial(_na_ctx_attn_kernel, hd=hd),
        grid=(batch,),
        in_specs=[spec, spec, spec],
        out_specs=spec,
        out_shape=jax.ShapeDtypeStruct((batch * seq, d), BF16),
        compiler_params=_cparams("arbitrary"),
        name="na_ctx_attn",
    )(q, k, v)


def _na_lat_attn_kernel(q_ref, kl_ref, vl_ref, kc_ref, vc_ref, tab_ref, o_ref, kc_scr, vc_scr, *, hd, rows, kh, tq):
    lane = lax.broadcasted_iota(jnp.int32, (1, LANES), 1)
    lo = lane < hd
    upper = (lane >= GRID_W).astype(jnp.int32)
    scale2 = hd ** -0.5 * LOG2E
    q_rows = tq // GRID_W
    w_rows = min(rows, kh + q_rows + (kh + q_rows) % 2)
    kc_scr[...] = kc_ref[...].astype(BF16)
    vc_scr[...] = vc_ref[...].astype(BF16)

    def qblock(qb, carry):
        r0 = qb * q_rows
        w0 = jnp.minimum(jnp.clip(r0 - kh // 2, 0, rows - kh) // 2 * 2, rows - w_rows)
        k0 = pl.multiple_of(w0 * GRID_W, 2 * GRID_W)
        q0 = pl.multiple_of(qb * tq, tq)
        q = q_ref[pl.ds(q0, tq), :]
        kl = kl_ref[pl.ds(k0, w_rows * GRID_W), :]
        vl = vl_ref[pl.ds(k0, w_rows * GRID_W), :]
        zero = jnp.zeros_like(q)

        def bias(hh):
            blocks = []
            for r in range(q_rows):
                rq = r0 + r
                rs = jnp.clip(rq - kh // 2, 0, rows - kh)
                tiles = []
                for kp in range(w_rows // 2):
                    rk = w0 + 2 * kp + upper
                    ok = (rk >= rs) & (rk < rs + kh)
                    tiles.append(jnp.where(ok, tab_ref[hh, w0 + 2 * kp - rq + rows - 1], -jnp.inf))
                blocks.append(jnp.concatenate(tiles, axis=1))
            return jnp.concatenate(blocks, axis=0)

        outs = []
        for hh in range(2):
            qm = jnp.where(lo, q, zero) if hh == 0 else jnp.where(lo, zero, q)
            s_loc = _qk(qm, kl) * scale2 + bias(hh)
            s_ctx = _qk(qm, kc_scr[...]) * scale2
            outs.append(_softmax2_pv([s_loc, s_ctx], [vl, vc_scr[...]]))
        o_ref[pl.ds(q0, tq), :] = jnp.where(lo, outs[0], outs[1]).astype(o_ref.dtype)
        return carry

    lax.fori_loop(0, q_ref.shape[0] // tq, qblock, 0)


def _na_lat_attn(q, k, v, cache_k, cache_v, slot, tab, n_prompt, dec_batch, dec_seq, hd):
    d = q.shape[1]
    past = cache_k.shape[2]
    rows = dec_seq // GRID_W
    kh = min(NA_KH, rows)
    tq = _pow2_tile(256, dec_seq)
    assert tq % GRID_W == 0 and rows % 2 == 0 and 2 * GRID_W == LANES and n_prompt % dec_seq == 0
    b0 = n_prompt // dec_seq
    return pl.pallas_call(
        functools.partial(_na_lat_attn_kernel, hd=hd, rows=rows, kh=kh, tq=tq),
        grid=(d // LANES, dec_batch),
        in_specs=[pl.BlockSpec((dec_seq, LANES), lambda hp, b: (b0 + b, hp)),
                  pl.BlockSpec((dec_seq, LANES), lambda hp, b: (b, hp)),
                  pl.BlockSpec((dec_seq, LANES), lambda hp, b: (b, hp)),
                  pl.BlockSpec((None, None, past, LANES), lambda hp, b: (b, slot, 0, hp)),
                  pl.BlockSpec((None, None, past, LANES), lambda hp, b: (b, slot, 0, hp)),
                  pl.BlockSpec((2,) + tab.shape[1:], lambda hp, b: (hp, 0, 0, 0))],
        out_specs=pl.BlockSpec((dec_seq, LANES), lambda hp, b: (b, hp)),
        out_shape=jax.ShapeDtypeStruct((dec_batch * dec_seq, d), BF16),
        scratch_shapes=[pltpu.VMEM((past, LANES), BF16), pltpu.VMEM((past, LANES), BF16)],
        compiler_params=_cparams("arbitrary", "arbitrary"),
        name="na_lat_attn",
    )(q, k, v, cache_k, cache_v, tab)


def _na_bias_table(rpb, dec_seq):
    rows = dec_seq // GRID_W
    kh = min(NA_KH, rows)
    nh, nd, nc = rpb.shape
    c = jnp.arange(GRID_W)
    cs = jnp.clip(c - NA_KW // 2, 0, GRID_W - NA_KW)
    col_ok = (c[None, :] >= cs[:, None]) & (c[None, :] < cs[:, None] + NA_KW)
    dcol = c[None, :] - c[:, None] + NA_KW - 1
    onehot = (dcol[None] == jnp.arange(nc)[:, None, None]).astype(F32)
    tmp = jnp.einsum('hdc,cxy->hdxy', rpb.astype(F32), onehot, precision=lax.Precision.HIGHEST)
    tmp = jnp.where(col_ok[None, None], tmp * LOG2E, -jnp.inf)
    n_e = 2 * rows - 2
    front = rows - kh
    back = n_e + 1 - nd - front
    pad = lambda n: jnp.full((nh, n, GRID_W, GRID_W), -jnp.inf, F32)
    ext = jnp.concatenate([pad(front), tmp, pad(back)], axis=1)
    return jnp.concatenate([ext[:, :-1], ext[:, 1:]], axis=-1)


def _gqa_qkv_kernel(x_ref, g_ref, sc_ref, sh_ref, w_ref, qn_ref, kn_ref, cos_ref, sin_ref,
                    q_ref, kp_ref, ks_ref, vp_ref, vs_ref, w_scr, *, nq, nk, np_tiles):
    i = pl.program_id(0)
    _cast_once(w_ref, w_scr)
    h = _normmod(x_ref[...], g_ref[...], sc_ref[...], sh_ref[...]).astype(BF16)
    z = jnp.dot(h, w_scr[...], preferred_element_type=F32)
    is_sample = i >= np_tiles
    cos = cos_ref[...]
    sin = sin_ref[...]

    def norm_rope(seg, gain):
        ms = jnp.mean(seg * seg, axis=-1, keepdims=True)
        y = seg * lax.rsqrt(ms + EPS) * gain
        roped = y * cos + pltpu.roll(y, LANES // 2, 1) * sin
        return jnp.where(is_sample, roped, y)

    for hh in range(nq):
        sl = slice(hh * LANES, (hh + 1) * LANES)
        q_ref[:, sl] = norm_rope(z[:, sl], qn_ref[...]).astype(q_ref.dtype)
    k = jnp.concatenate([norm_rope(z[:, (nq + hh) * LANES:(nq + hh + 1) * LANES], kn_ref[...]) for hh in range(nk)],
                        axis=1)
    v = z[:, (nq + nk) * LANES:]

    @pl.when(i < np_tiles)
    def _():
        kp_ref[...] = k
        vp_ref[...] = v

    @pl.when(is_sample)
    def _():
        ks_ref[...] = k.astype(ks_ref.dtype)
        vs_ref[...] = v.astype(vs_ref.dtype)


def _gqa_qkv(x, g, mod, layer, w_qkv, slot, qn, kn, cos_t, sin_t, nk, tok):
    n, d = x.shape
    hd = qn.shape[-1]
    assert hd == LANES
    nq = d // hd
    dk = nk * hd
    row = lambda i: (i, 0)
    fixed = lambda i: (0, 0)
    pos = lambda i: (jnp.maximum(i - tok.np_tiles, 0) % tok.tps, 0)
    n_p, n_s = tok.np_tiles * tok.tm, n - tok.np_tiles * tok.tm
    return pl.pallas_call(
        functools.partial(_gqa_qkv_kernel, nq=nq, nk=nk, np_tiles=tok.np_tiles),
        grid=(tok.tiles,),
        in_specs=[pl.BlockSpec((tok.tm, d), row),
                  pl.BlockSpec((1, d), fixed),
                  tok.mod_spec(layer, 1, d, 1),
                  tok.mod_spec(layer, 0, d, 1),
                  _resident((None, d, d + 2 * dk), lambda i: (slot, 0, 0)),
                  pl.BlockSpec((1, hd), fixed),
                  pl.BlockSpec((1, hd), fixed),
                  pl.BlockSpec((tok.tm, hd), pos),
                  pl.BlockSpec((tok.tm, hd), pos)],
        out_specs=[pl.BlockSpec((tok.tm, d), row),
                   tok.prompt_spec(dk, 1), tok.sample_spec(dk, 1), tok.prompt_spec(dk, 1), tok.sample_spec(dk, 1)],
        out_shape=[jax.ShapeDtypeStruct((n, d), BF16),
                   jax.ShapeDtypeStruct((n_p, dk), F32), jax.ShapeDtypeStruct((n_s, dk), BF16),
                   jax.ShapeDtypeStruct((n_p, dk), F32), jax.ShapeDtypeStruct((n_s, dk), BF16)],
        scratch_shapes=[pltpu.VMEM((d, d + 2 * dk), BF16)],
        compiler_params=_cparams("arbitrary"),
        name="gqa_qkv",
    )(x, g, mod, mod, w_qkv, qn.reshape(1, hd), kn.reshape(1, hd), cos_t, sin_t)


def _rope_tables(dec_seq, hd):
    t = jnp.arange(dec_seq)
    row = (t // GRID_W).astype(F32)
    col = (t % GRID_W).astype(F32)
    half = hd // 2
    inv = ROPE_THETA ** (-jnp.arange(0, half, 2, dtype=F32) / half)
    ang = jnp.concatenate([row[:, None] * inv, col[:, None] * inv], axis=-1)
    cos, sin = jnp.cos(ang), jnp.sin(ang)
    return jnp.concatenate([cos, cos], axis=-1), jnp.concatenate([-sin, sin], axis=-1)


def _gqa_attn_kernel(q_ref, *refs, rep, nk, n_kv):
    k_refs, v_refs, o_ref = refs[:n_kv], refs[n_kv:2 * n_kv], refs[2 * n_kv]
    tq = q_ref.shape[0]
    scale2 = LANES ** -0.5 * LOG2E
    for kv in range(nk):
        sl = slice(kv * LANES, (kv + 1) * LANES)
        ks = [r[:, sl].astype(BF16) for r in k_refs]
        vs = [r[:, sl].astype(BF16) for r in v_refs]
        qs = jnp.concatenate([q_ref[:, (kv * rep + r) * LANES:(kv * rep + r + 1) * LANES] for r in range(rep)], axis=0)
        o = _softmax2_pv([_qk(qs, k) * scale2 for k in ks], vs)
        for r in range(rep):
            o_ref[:, (kv * rep + r) * LANES:(kv * rep + r + 1) * LANES] = o[r * tq:(r + 1) * tq].astype(o_ref.dtype)


def _gqa_ctx_attn(q, k, v, batch, seq, nk):
    d = q.shape[1]
    dk = k.shape[1]
    rep = d // dk
    qspec = pl.BlockSpec((seq, d), lambda b: (b, 0))
    kspec = pl.BlockSpec((seq, dk), lambda b: (b, 0))
    return pl.pallas_call(
        functools.partial(_gqa_attn_kernel, rep=rep, nk=nk, n_kv=1),
        grid=(batch,),
        in_specs=[qspec, kspec, kspec],
        out_specs=qspec,
        out_shape=jax.ShapeDtypeStruct((batch * seq, d), BF16),
        compiler_params=_cparams("arbitrary"),
        name="gqa_ctx_attn",
    )(q, k, v)


def _gqa_lat_attn(q, k, v, cache_k, cache_v, slot, n_prompt, dec_batch, dec_seq, nk):
    d = q.shape[1]
    dk = k.shape[1]
    rep = d // dk
    past = cache_k.shape[2]
    tq = _pow2_tile(256, dec_seq, n_prompt)
    nqb = dec_seq // tq
    q0 = n_prompt // tq
    lspec = pl.BlockSpec((dec_seq, dk), lambda b, qb: (b, 0))
    cspec = pl.BlockSpec((None, None, past, dk), lambda b, qb: (b, slot, 0, 0))
    return pl.pallas_call(
        functools.partial(_gqa_attn_kernel, rep=rep, nk=nk, n_kv=2),
        grid=(dec_batch, nqb),
        in_specs=[pl.BlockSpec((tq, d), lambda b, qb: (q0 + b * nqb + qb, 0)),
                  lspec, cspec, lspec, cspec],
        out_specs=pl.BlockSpec((tq, d), lambda b, qb: (b * nqb + qb, 0)),
        out_shape=jax.ShapeDtypeStruct((dec_batch * dec_seq, d), BF16),
        compiler_params=_cparams("arbitrary", "arbitrary"),
        name="gqa_lat_attn",
    )(q, k, cache_k, v, cache_v)


def _proj_res_kernel(x_ref, op_ref, os_ref, w_ref, gate_ref, out_ref, w_scr, *, np_tiles):
    _cast_once(w_ref, w_scr)
    o = jnp.where(pl.program_id(0) < np_tiles, op_ref[...], os_ref[...])
    y = jnp.dot(o, w_scr[...], preferred_element_type=F32)
    out_ref[...] = x_ref[...] + gate_ref[...] * y


def _proj_res(x, o_p, o_s, w_o, slot, mod, layer, tok):
    n, d = x.shape
    row = lambda i: (i, 0)
    return pl.pallas_call(
        functools.partial(_proj_res_kernel, np_tiles=tok.np_tiles),
        grid=(tok.tiles,),
        in_specs=[pl.BlockSpec((tok.tm, d), row),
                  tok.prompt_spec(d, 1),
                  tok.sample_spec(d, 1),
                  _resident((None, d, d), lambda i: (slot, 0, 0)),
                  tok.mod_spec(layer, 2, d, 1)],
        out_specs=pl.BlockSpec((tok.tm, d), row),
        out_shape=jax.ShapeDtypeStruct((n, d), F32),
        scratch_shapes=[pltpu.VMEM((d, d), BF16)],
        compiler_params=_cparams("arbitrary"),
        name="attn_proj_res",
    )(x, o_p, o_s, w_o, mod)


def _mlp_kernel(x_ref, g_ref, sc_ref, sh_ref, gate_ref, w1_ref, w2_ref, *rest, np_tiles):
    out_refs, (h_scr, acc_scr) = rest[:-2], rest[-2:]
    i = pl.program_id(0)
    j = pl.program_id(1)
    last = j == pl.num_programs(1) - 1

    @pl.when(j == 0)
    def _():
        h_scr[...] = _normmod(x_ref[...], g_ref[...], sc_ref[...], sh_ref[...]).astype(BF16)
        acc_scr[...] = jnp.zeros_like(acc_scr)

    a = jnp.maximum(jnp.dot(h_scr[...], w1_ref[...].astype(BF16), preferred_element_type=F32), 0.0)
    acc_scr[...] += jnp.dot((a * a).astype(BF16), w2_ref[...].astype(BF16), preferred_element_type=F32)

    def result():
        return x_ref[...] + gate_ref[...] * acc_scr[...]

    if len(out_refs) == 1:
        @pl.when(last)
        def _():
            out_refs[0][...] = result()
    else:
        @pl.when(jnp.logical_and(last, i < np_tiles))
        def _():
            out_refs[0][...] = result()

        @pl.when(jnp.logical_and(last, i >= np_tiles))
        def _():
            out_refs[1][...] = result()


def _mlp(x, g, mod, layer, w1, w2, tok, split_out):
    n, d = x.shape
    f = w1.shape[2]
    tf = _pow2_tile(512, f)
    row = lambda i, j: (i, 0)
    if split_out:
        n_p = tok.np_tiles * tok.tm
        out_specs = [tok.prompt_spec(d, 2), tok.sample_spec(d, 2)]
        out_shape = [jax.ShapeDtypeStruct((n_p, d), F32), jax.ShapeDtypeStruct((n - n_p, d), F32)]
    else:
        out_specs = pl.BlockSpec((tok.tm, d), row)
        out_shape = jax.ShapeDtypeStruct((n, d), F32)
    return pl.pallas_call(
        functools.partial(_mlp_kernel, np_tiles=tok.np_tiles),
        grid=(tok.tiles, f // tf),
        in_specs=[pl.BlockSpec((tok.tm, d), row),
                  pl.BlockSpec((1, d), lambda i, j: (0, 0)),
                  tok.mod_spec(layer, 4, d, 2),
                  tok.mod_spec(layer, 3, d, 2),
                  tok.mod_spec(layer, 5, d, 2),
                  pl.BlockSpec((None, d, tf), lambda i, j: (layer, 0, j)),
                  pl.BlockSpec((None, tf, d), lambda i, j: (layer, j, 0))],
        out_specs=out_specs,
        out_shape=out_shape,
        scratch_shapes=[pltpu.VMEM((tok.tm, d), BF16), pltpu.VMEM((tok.tm, d), F32)],
        compiler_params=_cparams("arbitrary", "arbitrary"),
        name="mlp",
    )(x, g, mod, mod, mod, w1, w2)


def _s5_tables(lam_re, lam_im, log_dt, b_re, b_im, c_re, c_im):
    t = S5_CHUNK
    g, p, k = b_re.shape[1:]
    dt = jnp.exp(log_dt.astype(F32))[:, :, None]
    lr, li = lam_re.astype(F32), lam_im.astype(F32)
    ar, ai = lr * dt, li * dt
    mag = jnp.exp(ar)
    abr, abi = mag * jnp.cos(ai), mag * jnp.sin(ai)
    nr, ni = abr - 1.0, abi
    den = lr * lr + li * li
    f_re = (nr * lr + ni * li) / den
    f_im = (ni * lr - nr * li) / den
    bbr = f_re[..., None] * b_re - f_im[..., None] * b_im
    bbi = f_re[..., None] * b_im + f_im[..., None] * b_re
    n = jnp.arange(t + 1, dtype=F32)[None, None, :, None]
    pm = jnp.exp(n * ar[:, :, None, :])
    pr, pi = pm * jnp.cos(n * ai[:, :, None, :]), pm * jnp.sin(n * ai[:, :, None, :])
    cr, ci = c_re.astype(F32), c_im.astype(F32)

    def summ(pw_r, pw_i, d):
        pw_r = pw_r.transpose(0, 2, 1)[:, :, :, None]
        pw_i = pw_i.transpose(0, 2, 1)[:, :, :, None]
        sr = pw_r * bbr[d][:, :, None, :] - pw_i * bbi[d][:, :, None, :]
        si = pw_r * bbi[d][:, :, None, :] + pw_i * bbr[d][:, :, None, :]
        return sr.reshape(g, p, t * k), si.reshape(g, p, t * k)

    sfr, sfi = summ(pr[0, :, :t][:, ::-1], pi[0, :, :t][:, ::-1], 0)
    sbr, sbi = summ(pr[1, :, :t], pi[1, :, :t], 1)
    w_sum = jnp.concatenate([sfr, sbr, sfi, sbi], axis=1)
    w_intra = _s5_toeplitz(jnp.concatenate([cr[0], -ci[0]], axis=-1), jnp.concatenate([cr[1], -ci[1]], axis=-1),
                           jnp.concatenate([sfr, sfi], axis=1), jnp.concatenate([sbr, sbi], axis=1))
    car = cr[:, :, None] * pr[:, :, 1:, None, :] - ci[:, :, None] * pi[:, :, 1:, None, :]
    cai = cr[:, :, None] * pi[:, :, 1:, None, :] + ci[:, :, None] * pr[:, :, 1:, None, :]
    rd = lambda a: a.reshape(g, t * k, p)
    w2t = jnp.concatenate([rd(car[0]), rd(car[1][:, ::-1]), -rd(cai[0]), -rd(cai[1][:, ::-1])], axis=2)
    at = jnp.stack([jnp.concatenate([pr[0, :, t], pr[1, :, t]], axis=-1),
                    jnp.concatenate([pi[0, :, t], pi[1, :, t]], axis=-1)], axis=1)
    return w_intra, w_sum.astype(BF16), w2t.astype(BF16), at


def _s5_toeplitz_kernel(cf_ref, cb_ref, abf_ref, abb_ref, o_ref):
    hi = lax.Precision.HIGHEST
    gl, k, _ = cf_ref.shape
    tk = abf_ref.shape[2]
    for g in range(gl):
        kf = jnp.dot(cf_ref[g], abf_ref[g], precision=hi, preferred_element_type=F32)
        kb = jnp.dot(cb_ref[g], abb_ref[g], precision=hi, preferred_element_type=F32)
        z = jnp.zeros_like(kf)
        krev = jnp.concatenate([kf, z], axis=1) + pltpu.roll(jnp.concatenate([z, kb], axis=1), 2 * tk - k, 1)
        for i in range(S5_CHUNK):
            sh = (S5_CHUNK - 1 - i) * k
            win = krev if sh == 0 else pltpu.roll(krev, 2 * tk - sh, 1)
            o_ref[g, i * k:(i + 1) * k, :] = win[:, 0:tk].astype(o_ref.dtype)


def _s5_toeplitz(cf, cb, abf, abb):
    g, k, p2 = cf.shape
    tk = abf.shape[2]
    gl = _pow2_tile(8, g)
    return pl.pallas_call(
        _s5_toeplitz_kernel,
        grid=(g // gl,),
        in_specs=[pl.BlockSpec((gl, k, p2), lambda a: (a, 0, 0)),
                  pl.BlockSpec((gl, k, p2), lambda a: (a, 0, 0)),
                  pl.BlockSpec((gl, p2, tk), lambda a: (a, 0, 0)),
                  pl.BlockSpec((gl, p2, tk), lambda a: (a, 0, 0))],
        out_specs=pl.BlockSpec((gl, tk, tk), lambda a: (a, 0, 0)),
        out_shape=jax.ShapeDtypeStruct((g, tk, tk), BF16),
        compiler_params=_cparams("arbitrary"),
        name="s5_toeplitz",
    )(cf, cb, abf, abb)


def _s5_mixer(x, g, mod, layer, tok, params, h0, dims):
    (lam_re, lam_im, log_dt, b_re, b_im, c_re, c_im, d_skip, glu_w, slot, glu_b) = params
    batch, seq, dec_batch, dec_seq = dims
    n_prompt = batch * seq
    d = d_skip.shape[0]
    ngrp = d // S5_GROUP
    p = lam_re.shape[-1]
    w_intra, w_sum, w2t, at = _s5_tables(lam_re, lam_im, log_dt, b_re, b_im, c_re, c_im)
    hperm = _s5_pre(x, g, mod, layer, tok)
    h0g = h0.astype(F32).transpose(3, 2, 0, 1, 4).reshape(ngrp, 2, dec_batch, 2 * p)
    streams = ((0, seq // S5_CHUNK, batch, False, True),
               (n_prompt // S5_CHUNK, dec_seq // S5_CHUNK, dec_batch, True, False))
    yperm, fin = _s5_scan(hperm, w_intra, w_sum, w2t, at, h0g, batch, streams)
    x_new = _s5_post(x, yperm, g, mod, layer, d_skip.reshape(1, d), glu_w, slot, glu_b, tok)
    st = fin.reshape(ngrp, 2, batch, 2, p).transpose(2, 3, 1, 0, 4)
    return x_new, st


def kernel(x_prompt, x_sample, state_s5, cache_na_k, cache_na_v, cache_gqa_k, cache_gqa_v, c, c_ctx, norm_g, ada_w, ada_b, mlp_w1, mlp_w2, s5_lam_re, s5_lam_im, s5_log_dt, s5_b_re, s5_b_im, s5_c_re, s5_c_im, s5_d, s5_glu_w, s5_glu_b, na_w_qkv, na_q_norm, na_k_norm, na_rpb, na_w_o, gqa_w_qkv, gqa_q_norm, gqa_k_norm, gqa_w_o):
    batch, seq, d = x_prompt.shape
    dec_batch, dec_seq, _ = x_sample.shape
    depth = ada_w.shape[0]
    n_prompt = batch * seq
    n_sample = dec_batch * dec_seq
    na_heads, na_hd = cache_na_k.shape[3], cache_na_k.shape[4]
    gqa_kv, gqa_hd = cache_gqa_k.shape[3], cache_gqa_k.shape[4]
    assert n_prompt % dec_seq == 0

    tok = _Tok(n_prompt, n_sample, dec_seq, 1024)
    tok_half = _Tok(n_prompt, n_sample, dec_seq, 512)

    mod_rows = -(-(1 + dec_batch) // SUBLANES) * SUBLANES
    cvec = jnp.concatenate([c_ctx[None, :], c, jnp.zeros((mod_rows - 1 - dec_batch, d), F32)], axis=0)
    mod = _modulation(cvec, ada_w, ada_b).reshape(depth, mod_rows, 6, 1, d)

    x = (x_prompt.reshape(n_prompt, d), x_sample.reshape(n_sample, d))
    cache_na_k2 = cache_na_k.reshape(cache_na_k.shape[:3] + (na_heads * na_hd,))
    cache_na_v2 = cache_na_v.reshape(cache_na_v.shape[:3] + (na_heads * na_hd,))
    cache_gqa_k2 = cache_gqa_k.reshape(cache_gqa_k.shape[:3] + (gqa_kv * gqa_hd,))
    cache_gqa_v2 = cache_gqa_v.reshape(cache_gqa_v.shape[:3] + (gqa_kv * gqa_hd,))
    cos_t, sin_t = _rope_tables(dec_seq, gqa_hd)

    new_s5, new_na_k, new_na_v, new_gqa_k, new_gqa_v = [], [], [], [], []
    for i in range(depth):
        kind, slot = i % 3, i // 3
        g1 = norm_g[i, 0].reshape(1, d)
        g2 = norm_g[i, 1].reshape(1, d)
        if kind == 0:
            params = (s5_lam_re[slot], s5_lam_im[slot], s5_log_dt[slot], s5_b_re[slot], s5_b_im[slot],
                      s5_c_re[slot], s5_c_im[slot], s5_d[slot], s5_glu_w, slot, s5_glu_b[slot])
            x, st = _s5_mixer(x, g1, mod, i, tok_half, params, state_s5[:, slot], (batch, seq, dec_batch, dec_seq))
            new_s5.append(st)
        elif kind == 1:
            assert not isinstance(x, tuple)
            q, k_p, k_s, v_p, v_s = _na_qkv(x, g1, mod, i, na_w_qkv, slot, na_q_norm[slot], na_k_norm[slot], tok_half)
            o_p = _na_ctx_attn(q, k_p, v_p, batch, seq, na_hd)
            bias = _na_bias_table(na_rpb[slot], dec_seq)
            o_s = _na_lat_attn(q, k_s, v_s, cache_na_k2, cache_na_v2, slot, bias, n_prompt, dec_batch, dec_seq, na_hd)
            x = _proj_res(x, o_p, o_s, na_w_o, slot, mod, i, tok)
            new_na_k.append(k_p.reshape(batch, seq, na_heads, na_hd))
            new_na_v.append(v_p.reshape(batch, seq, na_heads, na_hd))
        else:
            assert not isinstance(x, tuple)
            q, k_p, k_s, v_p, v_s = _gqa_qkv(x, g1, mod, i, gqa_w_qkv, slot, gqa_q_norm[slot], gqa_k_norm[slot],
                                             cos_t, sin_t, gqa_kv, tok_half)
            o_p = _gqa_ctx_attn(q, k_p, v_p, batch, seq, gqa_kv)
            o_s = _gqa_lat_attn(q, k_s, v_s, cache_gqa_k2, cache_gqa_v2, slot, n_prompt, dec_batch, dec_seq, gqa_kv)
            x = _proj_res(x, o_p, o_s, gqa_w_o, slot, mod, i, tok)
            new_gqa_k.append(k_p.reshape(batch, seq, gqa_kv, gqa_hd))
            new_gqa_v.append(v_p.reshape(batch, seq, gqa_kv, gqa_hd))
        x = _mlp(x, g2, mod, i, mlp_w1, mlp_w2, tok, split_out=(i == depth - 1))
    y_p, y_s = x
    return (y_p.reshape(batch, seq, d), y_s.reshape(dec_batch, dec_seq, d),
            jnp.stack(new_s5, axis=1), jnp.stack(new_na_k, axis=1), jnp.stack(new_na_v, axis=1),
            jnp.stack(new_gqa_k, axis=1), jnp.stack(new_gqa_v, axis=1))
```

```python
import functools
import math

import jax
import jax.numpy as jnp
from jax import lax
from jax.experimental import pallas as pl
from jax.experimental.pallas import tpu as pltpu

F32 = jnp.float32
BF16 = jnp.bfloat16

EPS = 1e-6
GRID_W = 64
S5_GROUP = 16
NA_KH = 8
NA_KW = 16
ROPE_THETA = 10000.0
S5_CHUNK = 16
LOG2E = math.log2(math.e)
LANES = 128
SUBLANES = 8
VMEM_LIMIT = 56 * 1024 * 1024


def _cparams(*sem):
    return pltpu.CompilerParams(dimension_semantics=sem, vmem_limit_bytes=VMEM_LIMIT)


def _pow2_tile(pref, *ns):
    t = pref
    while any(n % t for n in ns):
        t //= 2
    return t


def _normmod(x, g, sc, sh):
    ms = jnp.mean(x * x, axis=-1, keepdims=True)
    y = x * lax.rsqrt(ms + EPS) * g
    return y * (1.0 + sc) + sh


class _Tok:
    def __init__(self, n_prompt, n_sample, dec_seq, pref):
        self.tm = _pow2_tile(pref, n_prompt, dec_seq)
        self.n = n_prompt + n_sample
        self.tiles = self.n // self.tm
        self.np_tiles = n_prompt // self.tm
        self.tps = dec_seq // self.tm

    def mod_row(self, i):
        return jnp.where(i < self.np_tiles, 0, 1 + (i - self.np_tiles) // self.tps)

    def prompt_spec(self, d, nargs):
        last = self.np_tiles - 1
        if nargs == 1:
            return pl.BlockSpec((self.tm, d), lambda i: (jnp.minimum(i, last), 0))
        return pl.BlockSpec((self.tm, d), lambda i, j: (jnp.minimum(i, last), 0))

    def sample_spec(self, d, nargs):
        first = self.np_tiles
        if nargs == 1:
            return pl.BlockSpec((self.tm, d), lambda i: (jnp.maximum(i - first, 0), 0))
        return pl.BlockSpec((self.tm, d), lambda i, j: (jnp.maximum(i - first, 0), 0))

    def mod_spec(self, layer, which, d, nargs):
        if nargs == 1:
            return pl.BlockSpec((None, None, None, 1, d), lambda i: (layer, self.mod_row(i), which, 0, 0))
        return pl.BlockSpec((None, None, None, 1, d), lambda i, j: (layer, self.mod_row(i), which, 0, 0))


def _mod_kernel(c_ref, w_ref, b_ref, o_ref):
    c = c_ref[...]
    s = c * jax.nn.sigmoid(c)
    o_ref[...] = jnp.dot(s.astype(BF16), w_ref[...].astype(BF16), preferred_element_type=F32) + b_ref[...]


def _modulation(cvec, ada_w, ada_b):
    depth, d, d6 = ada_w.shape
    rows = cvec.shape[0]
    tn = _pow2_tile(2048, d6) if d6 % 2048 == 0 else d
    return pl.pallas_call(
        _mod_kernel,
        grid=(depth, d6 // tn),
        in_specs=[pl.BlockSpec((rows, d), lambda l, j: (0, 0)),
                  pl.BlockSpec((None, d, tn), lambda l, j: (l, 0, j)),
                  pl.BlockSpec((None, 1, tn), lambda l, j: (l, 0, j))],
        out_specs=pl.BlockSpec((None, rows, tn), lambda l, j: (l, 0, j)),
        out_shape=jax.ShapeDtypeStruct((depth, rows, d6), F32),
        compiler_params=_cparams("arbitrary", "arbitrary"),
        name="adaln_modulation",
    )(cvec, ada_w, ada_b.reshape(depth, 1, d6))


def _read_x(x_refs, np_tiles):
    if len(x_refs) == 1:
        return x_refs[0][...]
    return jnp.where(pl.program_id(0) < np_tiles, x_refs[0][...], x_refs[1][...])


def _x_specs(x, tok, nargs):
    if isinstance(x, tuple):
        d = x[0].shape[1]
        return [tok.prompt_spec(d, nargs), tok.sample_spec(d, nargs)], x
    d = x.shape[1]
    return [pl.BlockSpec((tok.tm, d), (lambda i: (i, 0)) if nargs == 1 else (lambda i, j: (i, 0)))], (x,)


def _s5_pre_kernel(*refs, n_x, np_tiles):
    x_refs, (g_ref, sc_ref, sh_ref, h_ref, h_scr) = refs[:n_x], refs[n_x:]
    h = _normmod(_read_x(x_refs, np_tiles), g_ref[...], sc_ref[...], sh_ref[...])
    nc = h_ref.shape[1]
    for a in range(h_scr.shape[0]):
        sl = slice(a * LANES, (a + 1) * LANES)
        h_scr[a] = h[:, sl]
        for j in range(S5_CHUNK):
            h_ref[j, :, sl] = h_scr[a, pl.ds(j, nc, stride=S5_CHUNK), :]


def _s5_pre(x, g, mod, layer, tok):
    x_specs, xs = _x_specs(x, tok, 1)
    d = xs[0].shape[1]
    nc = tok.tm // S5_CHUNK
    return pl.pallas_call(
        functools.partial(_s5_pre_kernel, n_x=len(xs), np_tiles=tok.np_tiles),
        grid=(tok.tiles,),
        in_specs=x_specs + [pl.BlockSpec((1, d), lambda i: (0, 0)),
                            tok.mod_spec(layer, 1, d, 1),
                            tok.mod_spec(layer, 0, d, 1)],
        out_specs=pl.BlockSpec((S5_CHUNK, nc, d), lambda i: (0, i, 0)),
        out_shape=jax.ShapeDtypeStruct((S5_CHUNK, tok.n // S5_CHUNK, d), F32),
        scratch_shapes=[pltpu.VMEM((d // LANES, tok.tm, LANES), F32)],
        compiler_params=_cparams("arbitrary"),
        name="s5_pre",
    )(*xs, g, mod, mod)


def _s5_scan_kernel(h_ref, wi_ref, ws_ref, w2t_ref, at_ref, h0_ref, *rest, streams, unroll):
    ns = len(streams)
    perm_refs, (y_ref, fin_ref, xt_scr, yt_scr), chain_scr = rest[:ns], rest[ns:ns + 4], rest[ns + 4:]
    ngrp = wi_ref.shape[0]
    p2 = at_ref.shape[-1]
    fwd = lax.broadcasted_iota(jnp.int32, (1, p2), 1) < (p2 // 2)
    for j in range(S5_CHUNK):
        xt_scr[j] = h_ref[j].T.astype(BF16)

    def chain(r, stream, perm_ref, s_scr, e_scr):
        row0, n_chunks, nb, use_h0, write_fin = stream
        cols = slice(row0, row0 + n_chunks * nb)
        c0 = pl.multiple_of(r * S5_GROUP, S5_GROUP)
        rmat = jnp.concatenate([xt_scr[j, pl.ds(c0, S5_GROUP), cols] for j in range(S5_CHUNK)], axis=0)
        rmat_cm = jnp.dot(rmat, perm_ref[...], preferred_element_type=F32).astype(BF16)
        zs = jnp.dot(ws_ref[r], rmat_cm, preferred_element_type=F32)
        s_scr[0] = zs[0:p2].T
        s_scr[1] = zs[p2:].T
        ar = at_ref[r, 0:1, :]
        ai = at_ref[r, 1:2, :]
        if use_h0:
            hr, hi = h0_ref[r, 0], h0_ref[r, 1]
        else:
            hr = hi = jnp.zeros((nb, p2), F32)
        for t in range(n_chunks):
            ft = slice(t * nb, (t + 1) * nb)
            bt = slice((n_chunks - 1 - t) * nb, (n_chunks - t) * nb)
            e_scr[0, ft, :] = hr
            e_scr[1, bt, :] = hr
            e_scr[2, ft, :] = hi
            e_scr[3, bt, :] = hi
            in_re = jnp.where(fwd, s_scr[0, ft, :], s_scr[0, bt, :])
            in_im = jnp.where(fwd, s_scr[1, ft, :], s_scr[1, bt, :])
            hr, hi = ar * hr - ai * hi + in_re, ar * hi + ai * hr + in_im
        if write_fin:
            fin_ref[r, 0] = hr
            fin_ref[r, 1] = hi
        e_cm = jnp.concatenate([jnp.where(fwd, e_scr[0], e_scr[1]), jnp.where(fwd, e_scr[2], e_scr[3])],
                               axis=1).astype(BF16)
        e = jnp.dot(perm_ref[...], e_cm, preferred_element_type=F32).astype(BF16)
        yt = (jnp.dot(wi_ref[r], rmat, preferred_element_type=F32)
              + lax.dot_general(w2t_ref[r], e, (((1,), (1,)), ((), ())), preferred_element_type=F32))
        for i in range(S5_CHUNK):
            yt_scr[i, pl.ds(c0, S5_GROUP), cols] = yt[i * S5_GROUP:(i + 1) * S5_GROUP]

    def body(rb, carry):
        for u in range(unroll):
            for si, stream in enumerate(streams):
                k = 2 * (u * ns + si)
                chain(rb * unroll + u, stream, perm_refs[si], chain_scr[k], chain_scr[k + 1])
        return carry

    lax.fori_loop(0, ngrp // unroll, body, 0)
    for i in range(S5_CHUNK):
        y_ref[i] = yt_scr[i].T


def _s5_scan(hperm, w_intra, w_sum, w2t, at, h0, nb_fin, streams):
    t, nrows, d = hperm.shape
    g, tk, p4 = w2t.shape
    p2 = at.shape[-1]
    gl = LANES // S5_GROUP
    nb0 = h0.shape[2]
    unroll = 4
    assert sum(s[1] * s[2] for s in streams) == nrows and gl % unroll == 0
    chain_scr = []
    for _ in range(unroll):
        for s in streams:
            chain_scr += [pltpu.VMEM((2, s[1] * s[2], p2), F32), pltpu.VMEM((4, s[1] * s[2], p2), F32)]
    perms = []
    for _, n_chunks, nb, _, _ in streams:
        m = jnp.arange(n_chunks * nb)
        perms.append((m[:, None] == ((m % nb) * n_chunks + m // nb)[None, :]).astype(BF16))
    return pl.pallas_call(
        functools.partial(_s5_scan_kernel, streams=streams, unroll=unroll),
        grid=(d // LANES,),
        in_specs=[pl.BlockSpec((t, nrows, LANES), lambda a: (0, 0, a)),
                  pl.BlockSpec((gl, tk, tk), lambda a: (a, 0, 0)),
                  pl.BlockSpec((gl, p4, tk), lambda a: (a, 0, 0)),
                  pl.BlockSpec((gl, tk, p4), lambda a: (a, 0, 0)),
                  pl.BlockSpec((gl, 2, p2), lambda a: (a, 0, 0)),
                  pl.BlockSpec((gl, 2, nb0, p2), lambda a: (a, 0, 0, 0))]
                 + [pl.BlockSpec(pm.shape, lambda a: (0, 0)) for pm in perms],
        out_specs=[pl.BlockSpec((t, nrows, LANES), lambda a: (0, 0, a)),
                   pl.BlockSpec((gl, 2, nb_fin, p2), lambda a: (a, 0, 0, 0))],
        out_shape=[jax.ShapeDtypeStruct((t, nrows, d), F32),
                   jax.ShapeDtypeStruct((g, 2, nb_fin, p2), F32)],
        scratch_shapes=[pltpu.VMEM((t, LANES, nrows), BF16), pltpu.VMEM((t, LANES, nrows), F32)] + chain_scr,
        compiler_params=_cparams("arbitrary"),
        name="s5_scan",
    )(hperm, w_intra, w_sum, w2t, at, h0, *perms)


def _s5_post_kernel(*refs, n_x, np_tiles):
    x_refs = refs[:n_x]
    (y_ref, g_ref, sc_ref, sh_ref, gate_ref, dsk_ref, wa_ref, wg_ref, ba_ref, bg_ref, o_ref,
     y_scr, wa_scr, wg_scr) = refs[n_x:]
    _cast_once(wa_ref, wa_scr)
    _cast_once(wg_ref, wg_scr)
    nc = y_ref.shape[1]
    for a in range(y_scr.shape[0]):
        for j in range(S5_CHUNK):
            y_scr[a, pl.ds(j, nc, stride=S5_CHUNK), :] = y_ref[j, :, a * LANES:(a + 1) * LANES]
    x = _read_x(x_refs, np_tiles)
    h = _normmod(x, g_ref[...], sc_ref[...], sh_ref[...])
    y = h * dsk_ref[...] + jnp.concatenate([y_scr[a] for a in range(y_scr.shape[0])], axis=1)
    a = jax.nn.gelu(y).astype(BF16)
    za = jnp.dot(a, wa_scr[...], preferred_element_type=F32) + ba_ref[...]
    zg = jnp.dot(a, wg_scr[...], preferred_element_type=F32) + bg_ref[...]
    o_ref[...] = x + gate_ref[...] * (za * jax.nn.sigmoid(zg))


def _s5_post(x, y, g, mod, layer, dsk, glu_w, slot, glu_b, tok):
    x_specs, xs = _x_specs(x, tok, 1)
    n, d = tok.n, xs[0].shape[1]
    row = lambda i: (i, 0)
    fixed = lambda i: (0, 0)
    gb2 = glu_b.reshape(1, 2 * d)
    return pl.pallas_call(
        functools.partial(_s5_post_kernel, n_x=len(xs), np_tiles=tok.np_tiles),
        grid=(tok.tiles,),
        in_specs=x_specs + [
                  pl.BlockSpec((S5_CHUNK, tok.tm // S5_CHUNK, d), lambda i: (0, i, 0)),
                  pl.BlockSpec((1, d), fixed),
                  tok.mod_spec(layer, 1, d, 1),
                  tok.mod_spec(layer, 0, d, 1),
                  tok.mod_spec(layer, 2, d, 1),
                  pl.BlockSpec((1, d), fixed),
                  _resident((None, d, d), lambda i: (slot, 0, 0)),
                  _resident((None, d, d), lambda i: (slot, 0, 1)),
                  pl.BlockSpec((1, d), lambda i: (0, 0)),
                  pl.BlockSpec((1, d), lambda i: (0, 1))],
        out_specs=pl.BlockSpec((tok.tm, d), row),
        out_shape=jax.ShapeDtypeStruct((n, d), F32),
        scratch_shapes=[pltpu.VMEM((d // LANES, tok.tm, LANES), F32),
                        pltpu.VMEM((d, d), BF16), pltpu.VMEM((d, d), BF16)],
        compiler_params=_cparams("arbitrary"),
        name="s5_post",
    )(*xs, y, g, mod, mod, mod, dsk, glu_w, glu_w, gb2, gb2)


def _cast_once(w_ref, w_scr):
    @pl.when(pl.program_id(0) == 0)
    def _():
        w_scr[...] = w_ref[...].astype(w_scr.dtype)


def _resident(block_shape, index_map):
    return pl.BlockSpec(block_shape, index_map, pipeline_mode=pl.Buffered(1))


def _na_qkv_kernel(x_ref, g_ref, sc_ref, sh_ref, w_ref, qn_ref, kn_ref, q_ref, kp_ref, ks_ref, vp_ref, vs_ref, w_scr, *,
                   hd, np_tiles):
    i = pl.program_id(0)
    _cast_once(w_ref, w_scr)
    h = _normmod(x_ref[...], g_ref[...], sc_ref[...], sh_ref[...]).astype(BF16)
    d = h.shape[1]
    lo = lax.broadcasted_iota(jnp.int32, (1, LANES), 1) < hd

    def head_norm(z, gain):
        outs = []
        for s in range(d // LANES):
            seg = z[:, s * LANES:(s + 1) * LANES]
            sq = seg * seg
            s_lo = jnp.sum(jnp.where(lo, sq, 0.0), axis=-1, keepdims=True)
            s_hi = jnp.sum(jnp.where(lo, 0.0, sq), axis=-1, keepdims=True)
            ms = jnp.where(lo, s_lo, s_hi) / hd
            outs.append(seg * lax.rsqrt(ms + EPS) * gain)
        return jnp.concatenate(outs, axis=1)

    proj = lambda part: jnp.dot(h, w_scr[:, part * d:(part + 1) * d], preferred_element_type=F32)
    q_ref[...] = (head_norm(proj(0), qn_ref[...]) * (hd ** -0.5 * LOG2E)).astype(q_ref.dtype)
    k = head_norm(proj(1), kn_ref[...])
    v = proj(2)

    @pl.when(i < np_tiles)
    def _():
        kp_ref[...] = k
        vp_ref[...] = v

    @pl.when(i >= np_tiles)
    def _():
        ks_ref[...] = k.astype(ks_ref.dtype)
        vs_ref[...] = v.astype(vs_ref.dtype)


def _na_qkv(x, g, mod, layer, w_qkv, slot, qn, kn, tok):
    n, d = x.shape
    hd = qn.shape[-1]
    assert 2 * hd == LANES
    row = lambda i: (i, 0)
    fixed = lambda i: (0, 0)
    qn2 = jnp.tile(qn, 2).reshape(1, LANES)
    kn2 = jnp.tile(kn, 2).reshape(1, LANES)
    n_p, n_s = tok.np_tiles * tok.tm, n - tok.np_tiles * tok.tm
    return pl.pallas_call(
        functools.partial(_na_qkv_kernel, hd=hd, np_tiles=tok.np_tiles),
        grid=(tok.tiles,),
        in_specs=[pl.BlockSpec((tok.tm, d), row),
                  pl.BlockSpec((1, d), fixed),
                  tok.mod_spec(layer, 1, d, 1),
                  tok.mod_spec(layer, 0, d, 1),
                  _resident((None, d, 3 * d), lambda i: (slot, 0, 0)),
                  pl.BlockSpec((1, LANES), fixed),
                  pl.BlockSpec((1, LANES), fixed)],
        out_specs=[pl.BlockSpec((tok.tm, d), row),
                   tok.prompt_spec(d, 1), tok.sample_spec(d, 1), tok.prompt_spec(d, 1), tok.sample_spec(d, 1)],
        out_shape=[jax.ShapeDtypeStruct((n, d), BF16),
                   jax.ShapeDtypeStruct((n_p, d), F32), jax.ShapeDtypeStruct((n_s, d), BF16),
                   jax.ShapeDtypeStruct((n_p, d), F32), jax.ShapeDtypeStruct((n_s, d), BF16)],
        scratch_shapes=[pltpu.VMEM((d, 3 * d), BF16)],
        compiler_params=_cparams("arbitrary"),
        name="na_qkv",
    )(x, g, mod, mod, w_qkv, qn2, kn2)


def _softmax2_pv(s_parts, v_parts):
    m = s_parts[0].max(axis=-1, keepdims=True)
    for s in s_parts[1:]:
        m = jnp.maximum(m, s.max(axis=-1, keepdims=True))
    den = 0.0
    acc = 0.0
    for s, v in zip(s_parts, v_parts):
        e = jnp.exp2(s - m)
        den = den + e.sum(axis=-1, keepdims=True)
        acc = acc + jnp.dot(e.astype(BF16), v, preferred_element_type=F32)
    return acc / den


def _qk(q, k):
    return lax.dot_general(q, k, (((1,), (1,)), ((), ())), preferred_element_type=F32)


def _na_ctx_attn_kernel(q_ref, k_ref, v_ref, o_ref, *, hd):
    lo = lax.broadcasted_iota(jnp.int32, (1, LANES), 1) < hd
    for s in range(q_ref.shape[1] // LANES):
        sl = slice(s * LANES, (s + 1) * LANES)
        q = q_ref[:, sl]
        k = k_ref[:, sl].astype(BF16)
        v = v_ref[:, sl].astype(BF16)
        zero = jnp.zeros_like(q)
        o_lo = _softmax2_pv([_qk(jnp.where(lo, q, zero), k)], [v])
        o_hi = _softmax2_pv([_qk(jnp.where(lo, zero, q), k)], [v])
        o_ref[:, sl] = jnp.where(lo, o_lo, o_hi).astype(o_ref.dtype)


def _na_ctx_attn(q, k, v, batch, seq, hd):
    d = q.shape[1]
    spec = pl.BlockSpec((seq, d), lambda b: (b, 0))
    return pl.pallas_call(
        functools.partial(_na_ctx_attn_kernel, hd=hd),
        grid=(batch,),
        in_specs=[spec, spec, spec],
        out_specs=spec,
        out_shape=jax.ShapeDtypeStruct((batch * seq, d), BF16),
        compiler_params=_cparams("arbitrary"),
        name="na_ctx_attn",
    )(q, k, v)


def _na_window(r0, q_rows, rows, kh):
    w_rows = min(rows, kh + q_rows + (kh + q_rows) % 2)
    rs = min(max(r0 - kh // 2, 0), rows - kh)
    return min(rs // 2 * 2, rows - w_rows), w_rows


def _na_lat_attn_kernel(q_ref, kl_ref, vl_ref, kc_ref, vc_ref, tab_ref, o_ref, kc_scr, vc_scr, bias_scr, *,
                        hd, rows, kh, tq):
    lane = lax.broadcasted_iota(jnp.int32, (1, LANES), 1)
    lo = lane < hd
    lower = lane < GRID_W
    q_rows = tq // GRID_W
    n_qb = q_ref.shape[0] // tq
    w_cols = bias_scr.shape[2]
    kc_scr[...] = kc_ref[...].astype(BF16)
    vc_scr[...] = vc_ref[...].astype(BF16)

    @pl.when(pl.program_id(1) == 0)
    def _():
        neg = jnp.full((GRID_W, LANES), -jnp.inf, F32)
        for hh in range(2):
            for rq in range(rows):
                w0, w_rows = _na_window(rq // q_rows * q_rows, q_rows, rows, kh)
                rs = min(max(rq - kh // 2, 0), rows - kh)
                for kp in range(w_rows // 2):
                    rk = w0 + 2 * kp
                    ok0, ok1 = rs <= rk < rs + kh, rs <= rk + 1 < rs + kh
                    tile = tab_ref[hh, rk - rq + rows - 1]
                    if not (ok0 and ok1):
                        tile = jnp.where(lower if ok0 else jnp.logical_not(lower), tile, neg) if (ok0 or ok1) else neg
                    bias_scr[hh, rq * GRID_W:(rq + 1) * GRID_W, kp * LANES:(kp + 1) * LANES] = tile

    def qblock(qb, carry):
        r0 = qb * q_rows
        w0 = jnp.minimum(jnp.clip(r0 - kh // 2, 0, rows - kh) // 2 * 2, rows - w_cols // GRID_W)
        k0 = pl.multiple_of(w0 * GRID_W, 2 * GRID_W)
        q0 = pl.multiple_of(qb * tq, tq)
        q = q_ref[pl.ds(q0, tq), :]
        kl = kl_ref[pl.ds(k0, w_cols), :]
        vl = vl_ref[pl.ds(k0, w_cols), :]
        zero = jnp.zeros_like(q)
        outs = []
        for hh in range(2):
            qm = jnp.where(lo, q, zero) if hh == 0 else jnp.where(lo, zero, q)
            s_loc = _qk(qm, kl) + bias_scr[hh, pl.ds(q0, tq), :]
            s_ctx = _qk(qm, kc_scr[...])
            outs.append(_softmax2_pv([s_loc, s_ctx], [vl, vc_scr[...]]))
        o_ref[pl.ds(q0, tq), :] = jnp.where(lo, outs[0], outs[1]).astype(o_ref.dtype)
        return carry

    lax.fori_loop(0, n_qb, qblock, 0)


def _na_lat_attn(q, k, v, cache_k, cache_v, slot, tab, n_prompt, dec_batch, dec_seq, hd):
    d = q.shape[1]
    past = cache_k.shape[2]
    rows = dec_seq // GRID_W
    kh = min(NA_KH, rows)
    tq = _pow2_tile(256, dec_seq)
    assert tq % GRID_W == 0 and rows % 2 == 0 and 2 * GRID_W == LANES and n_prompt % dec_seq == 0
    b0 = n_prompt // dec_seq
    w_rows = _na_window(0, tq // GRID_W, rows, kh)[1]
    return pl.pallas_call(
        functools.partial(_na_lat_attn_kernel, hd=hd, rows=rows, kh=kh, tq=tq),
        grid=(d // LANES, dec_batch),
        in_specs=[pl.BlockSpec((dec_seq, LANES), lambda hp, b: (b0 + b, hp)),
                  pl.BlockSpec((dec_seq, LANES), lambda hp, b: (b, hp)),
                  pl.BlockSpec((dec_seq, LANES), lambda hp, b: (b, hp)),
                  pl.BlockSpec((None, None, past, LANES), lambda hp, b: (b, slot, 0, hp)),
                  pl.BlockSpec((None, None, past, LANES), lambda hp, b: (b, slot, 0, hp)),
                  pl.BlockSpec((2,) + tab.shape[1:], lambda hp, b: (hp, 0, 0, 0))],
        out_specs=pl.BlockSpec((dec_seq, LANES), lambda hp, b: (b, hp)),
        out_shape=jax.ShapeDtypeStruct((dec_batch * dec_seq, d), BF16),
        scratch_shapes=[pltpu.VMEM((past, LANES), BF16), pltpu.VMEM((past, LANES), BF16),
                        pltpu.VMEM((2, dec_seq, w_rows * GRID_W), F32)],
        compiler_params=_cparams("arbitrary", "arbitrary"),
        name="na_lat_attn",
    )(q, k, v, cache_k, cache_v, tab)


def _na_bias_table(rpb, dec_seq):
    rows = dec_seq // GRID_W
    kh = min(NA_KH, rows)
    nh, nd, nc = rpb.shape
    c = jnp.arange(GRID_W)
    cs = jnp.clip(c - NA_KW // 2, 0, GRID_W - NA_KW)
    col_ok = (c[None, :] >= cs[:, None]) & (c[None, :] < cs[:, None] + NA_KW)
    dcol = c[None, :] - c[:, None] + NA_KW - 1
    onehot = (dcol[None] == jnp.arange(nc)[:, None, None]).astype(F32)
    tmp = jnp.einsum('hdc,cxy->hdxy', rpb.astype(F32), onehot, precision=lax.Precision.HIGHEST)
    tmp = jnp.where(col_ok[None, None], tmp * LOG2E, -jnp.inf)
    n_e = 2 * rows - 2
    front = rows - kh
    back = n_e + 1 - nd - front
    pad = lambda n: jnp.full((nh, n, GRID_W, GRID_W), -jnp.inf, F32)
    ext = jnp.concatenate([pad(front), tmp, pad(back)], axis=1)
    return jnp.concatenate([ext[:, :-1], ext[:, 1:]], axis=-1)


def _gqa_qkv_kernel(x_ref, g_ref, sc_ref, sh_ref, w_ref, qn_ref, kn_ref, cos_ref, sin_ref,
                    q_ref, kp_ref, ks_ref, vp_ref, vs_ref, w_scr, *, nq, nk, np_tiles):
    i = pl.program_id(0)
    _cast_once(w_ref, w_scr)
    h = _normmod(x_ref[...], g_ref[...], sc_ref[...], sh_ref[...]).astype(BF16)
    z = jnp.dot(h, w_scr[...], preferred_element_type=F32)
    is_sample = i >= np_tiles
    cos = cos_ref[...]
    sin = sin_ref[...]

    def norm_rope(seg, gain):
        ms = jnp.mean(seg * seg, axis=-1, keepdims=True)
        y = seg * lax.rsqrt(ms + EPS) * gain
        roped = y * cos + pltpu.roll(y, LANES // 2, 1) * sin
        return jnp.where(is_sample, roped, y)

    for hh in range(nq):
        sl = slice(hh * LANES, (hh + 1) * LANES)
        q_ref[:, sl] = (norm_rope(z[:, sl], qn_ref[...]) * (LANES ** -0.5 * LOG2E)).astype(q_ref.dtype)
    k = jnp.concatenate([norm_rope(z[:, (nq + hh) * LANES:(nq + hh + 1) * LANES], kn_ref[...]) for hh in range(nk)],
                        axis=1)
    v = z[:, (nq + nk) * LANES:]

    @pl.when(i < np_tiles)
    def _():
        kp_ref[...] = k
        vp_ref[...] = v

    @pl.when(is_sample)
    def _():
        ks_ref[...] = k.astype(ks_ref.dtype)
        vs_ref[...] = v.astype(vs_ref.dtype)


def _gqa_qkv(x, g, mod, layer, w_qkv, slot, qn, kn, cos_t, sin_t, nk, tok):
    n, d = x.shape
    hd = qn.shape[-1]
    assert hd == LANES
    nq = d // hd
    dk = nk * hd
    row = lambda i: (i, 0)
    fixed = lambda i: (0, 0)
    pos = lambda i: (jnp.maximum(i - tok.np_tiles, 0) % tok.tps, 0)
    n_p, n_s = tok.np_tiles * tok.tm, n - tok.np_tiles * tok.tm
    return pl.pallas_call(
        functools.partial(_gqa_qkv_kernel, nq=nq, nk=nk, np_tiles=tok.np_tiles),
        grid=(tok.tiles,),
        in_specs=[pl.BlockSpec((tok.tm, d), row),
                  pl.BlockSpec((1, d), fixed),
                  tok.mod_spec(layer, 1, d, 1),
                  tok.mod_spec(layer, 0, d, 1),
                  _resident((None, d, d + 2 * dk), lambda i: (slot, 0, 0)),
                  pl.BlockSpec((1, hd), fixed),
                  pl.BlockSpec((1, hd), fixed),
                  pl.BlockSpec((tok.tm, hd), pos),
                  pl.BlockSpec((tok.tm, hd), pos)],
        out_specs=[pl.BlockSpec((tok.tm, d), row),
                   tok.prompt_spec(dk, 1), tok.sample_spec(dk, 1), tok.prompt_spec(dk, 1), tok.sample_spec(dk, 1)],
        out_shape=[jax.ShapeDtypeStruct((n, d), BF16),
                   jax.ShapeDtypeStruct((n_p, dk), F32), jax.ShapeDtypeStruct((n_s, dk), BF16),
                   jax.ShapeDtypeStruct((n_p, dk), F32), jax.ShapeDtypeStruct((n_s, dk), BF16)],
        scratch_shapes=[pltpu.VMEM((d, d + 2 * dk), BF16)],
        compiler_params=_cparams("arbitrary"),
        name="gqa_qkv",
    )(x, g, mod, mod, w_qkv, qn.reshape(1, hd), kn.reshape(1, hd), cos_t, sin_t)


def _rope_tables(dec_seq, hd):
    t = jnp.arange(dec_seq)
    row = (t // GRID_W).astype(F32)
    col = (t % GRID_W).astype(F32)
    half = hd // 2
    inv = ROPE_THETA ** (-jnp.arange(0, half, 2, dtype=F32) / half)
    ang = jnp.concatenate([row[:, None] * inv, col[:, None] * inv], axis=-1)
    cos, sin = jnp.cos(ang), jnp.sin(ang)
    return jnp.concatenate([cos, cos], axis=-1), jnp.concatenate([-sin, sin], axis=-1)


def _gqa_attn_kernel(q_ref, *refs, rep, nk, n_kv):
    k_refs, v_refs, o_ref = refs[:n_kv], refs[n_kv:2 * n_kv], refs[2 * n_kv]
    tq = q_ref.shape[0]
    for kv in range(nk):
        sl = slice(kv * LANES, (kv + 1) * LANES)
        ks = [r[:, sl].astype(BF16) for r in k_refs]
        vs = [r[:, sl].astype(BF16) for r in v_refs]
        qs = jnp.concatenate([q_ref[:, (kv * rep + r) * LANES:(kv * rep + r + 1) * LANES] for r in range(rep)], axis=0)
        o = _softmax2_pv([_qk(qs, k) for k in ks], vs)
        for r in range(rep):
            o_ref[:, (kv * rep + r) * LANES:(kv * rep + r + 1) * LANES] = o[r * tq:(r + 1) * tq].astype(o_ref.dtype)


def _gqa_ctx_attn(q, k, v, batch, seq, nk):
    d = q.shape[1]
    dk = k.shape[1]
    rep = d // dk
    qspec = pl.BlockSpec((seq, d), lambda b: (b, 0))
    kspec = pl.BlockSpec((seq, dk), lambda b: (b, 0))
    return pl.pallas_call(
        functools.partial(_gqa_attn_kernel, rep=rep, nk=nk, n_kv=1),
        grid=(batch,),
        in_specs=[qspec, kspec, kspec],
        out_specs=qspec,
        out_shape=jax.ShapeDtypeStruct((batch * seq, d), BF16),
        compiler_params=_cparams("arbitrary"),
        name="gqa_ctx_attn",
    )(q, k, v)


def _gqa_lat_attn(q, k, v, cache_k, cache_v, slot, n_prompt, dec_batch, dec_seq, nk):
    d = q.shape[1]
    dk = k.shape[1]
    rep = d // dk
    past = cache_k.shape[2]
    tq = _pow2_tile(256, dec_seq, n_prompt)
    nqb = dec_seq // tq
    q0 = n_prompt // tq
    lspec = pl.BlockSpec((dec_seq, dk), lambda b, qb: (b, 0))
    cspec = pl.BlockSpec((None, None, past, dk), lambda b, qb: (b, slot, 0, 0))
    return pl.pallas_call(
        functools.partial(_gqa_attn_kernel, rep=rep, nk=nk, n_kv=2),
        grid=(dec_batch, nqb),
        in_specs=[pl.BlockSpec((tq, d), lambda b, qb: (q0 + b * nqb + qb, 0)),
                  lspec, cspec, lspec, cspec],
        out_specs=pl.BlockSpec((tq, d), lambda b, qb: (b * nqb + qb, 0)),
        out_shape=jax.ShapeDtypeStruct((dec_batch * dec_seq, d), BF16),
        compiler_params=_cparams("arbitrary", "arbitrary"),
        name="gqa_lat_attn",
    )(q, k, cache_k, v, cache_v)


def _proj_res_kernel(x_ref, op_ref, os_ref, w_ref, gate_ref, out_ref, w_scr, *, np_tiles):
    _cast_once(w_ref, w_scr)
    o = jnp.where(pl.program_id(0) < np_tiles, op_ref[...], os_ref[...])
    y = jnp.dot(o, w_scr[...], preferred_element_type=F32)
    out_ref[...] = x_ref[...] + gate_ref[...] * y


def _proj_res(x, o_p, o_s, w_o, slot, mod, layer, tok):
    n, d = x.shape
    row = lambda i: (i, 0)
    return pl.pallas_call(
        functools.partial(_proj_res_kernel, np_tiles=tok.np_tiles),
        grid=(tok.tiles,),
        in_specs=[pl.BlockSpec((tok.tm, d), row),
                  tok.prompt_spec(d, 1),
                  tok.sample_spec(d, 1),
                  _resident((None, d, d), lambda i: (slot, 0, 0)),
                  tok.mod_spec(layer, 2, d, 1)],
        out_specs=pl.BlockSpec((tok.tm, d), row),
        out_shape=jax.ShapeDtypeStruct((n, d), F32),
        scratch_shapes=[pltpu.VMEM((d, d), BF16)],
        compiler_params=_cparams("arbitrary"),
        name="attn_proj_res",
    )(x, o_p, o_s, w_o, mod)


def _mlp_kernel(x_ref, g_ref, sc_ref, sh_ref, gate_ref, w1_ref, w2_ref, *rest, np_tiles):
    out_refs, (h_scr, acc_scr) = rest[:-2], rest[-2:]
    i = pl.program_id(0)
    j = pl.program_id(1)
    last = j == pl.num_programs(1) - 1

    @pl.when(j == 0)
    def _():
        h_scr[...] = _normmod(x_ref[...], g_ref[...], sc_ref[...], sh_ref[...]).astype(BF16)
        acc_scr[...] = jnp.zeros_like(acc_scr)

    a = jnp.maximum(jnp.dot(h_scr[...], w1_ref[...].astype(BF16), preferred_element_type=F32), 0.0)
    acc_scr[...] += jnp.dot((a * a).astype(BF16), w2_ref[...].astype(BF16), preferred_element_type=F32)

    def result():
        return x_ref[...] + gate_ref[...] * acc_scr[...]

    if len(out_refs) == 1:
        @pl.when(last)
        def _():
            out_refs[0][...] = result()
    else:
        @pl.when(jnp.logical_and(last, i < np_tiles))
        def _():
            out_refs[0][...] = result()

        @pl.when(jnp.logical_and(last, i >= np_tiles))
        def _():
            out_refs[1][...] = result()


def _mlp(x, g, mod, layer, w1, w2, tok, split_out):
    n, d = x.shape
    f = w1.shape[2]
    tf = _pow2_tile(512, f)
    row = lambda i, j: (i, 0)
    if split_out:
        n_p = tok.np_tiles * tok.tm
        out_specs = [tok.prompt_spec(d, 2), tok.sample_spec(d, 2)]
        out_shape = [jax.ShapeDtypeStruct((n_p, d), F32), jax.ShapeDtypeStruct((n - n_p, d), F32)]
    else:
        out_specs = pl.BlockSpec((tok.tm, d), row)
        out_shape = jax.ShapeDtypeStruct((n, d), F32)
    return pl.pallas_call(
        functools.partial(_mlp_kernel, np_tiles=tok.np_tiles),
        grid=(tok.tiles, f // tf),
        in_specs=[pl.BlockSpec((tok.tm, d), row),
                  pl.BlockSpec((1, d), lambda i, j: (0, 0)),
                  tok.mod_spec(layer, 4, d, 2),
                  tok.mod_spec(layer, 3, d, 2),
                  tok.mod_spec(layer, 5, d, 2),
                  pl.BlockSpec((None, d, tf), lambda i, j: (layer, 0, j)),
                  pl.BlockSpec((None, tf, d), lambda i, j: (layer, j, 0))],
        out_specs=out_specs,
        out_shape=out_shape,
        scratch_shapes=[pltpu.VMEM((tok.tm, d), BF16), pltpu.VMEM((tok.tm, d), F32)],
        compiler_params=_cparams("arbitrary", "arbitrary"),
        name="mlp",
    )(x, g, mod, mod, mod, w1, w2)


def _s5_tables(lam_re, lam_im, log_dt, b_re, b_im, c_re, c_im):
    t = S5_CHUNK
    g, p, k = b_re.shape[1:]
    dt = jnp.exp(log_dt.astype(F32))[:, :, None]
    lr, li = lam_re.astype(F32), lam_im.astype(F32)
    ar, ai = lr * dt, li * dt
    mag = jnp.exp(ar)
    abr, abi = mag * jnp.cos(ai), mag * jnp.sin(ai)
    nr, ni = abr - 1.0, abi
    den = lr * lr + li * li
    f_re = (nr * lr + ni * li) / den
    f_im = (ni * lr - nr * li) / den
    bbr = f_re[..., None] * b_re - f_im[..., None] * b_im
    bbi = f_re[..., None] * b_im + f_im[..., None] * b_re
    n = jnp.arange(t + 1, dtype=F32)[None, None, :, None]
    pm = jnp.exp(n * ar[:, :, None, :])
    pr, pi = pm * jnp.cos(n * ai[:, :, None, :]), pm * jnp.sin(n * ai[:, :, None, :])
    cr, ci = c_re.astype(F32), c_im.astype(F32)

    def summ(pw_r, pw_i, d):
        pw_r = pw_r.transpose(0, 2, 1)[:, :, :, None]
        pw_i = pw_i.transpose(0, 2, 1)[:, :, :, None]
        sr = pw_r * bbr[d][:, :, None, :] - pw_i * bbi[d][:, :, None, :]
        si = pw_r * bbi[d][:, :, None, :] + pw_i * bbr[d][:, :, None, :]
        return sr.reshape(g, p, t * k), si.reshape(g, p, t * k)

    sfr, sfi = summ(pr[0, :, :t][:, ::-1], pi[0, :, :t][:, ::-1], 0)
    sbr, sbi = summ(pr[1, :, :t], pi[1, :, :t], 1)
    w_intra, w_sum = _s5_toeplitz(jnp.concatenate([cr[0], -ci[0]], axis=-1),
                                  jnp.concatenate([cr[1], -ci[1]], axis=-1), sfr, sfi, sbr, sbi)
    car = cr[:, :, None] * pr[:, :, 1:, None, :] - ci[:, :, None] * pi[:, :, 1:, None, :]
    cai = cr[:, :, None] * pi[:, :, 1:, None, :] + ci[:, :, None] * pr[:, :, 1:, None, :]
    rd = lambda a: a.reshape(g, t * k, p)
    w2t = jnp.concatenate([rd(car[0]), rd(car[1][:, ::-1]), -rd(cai[0]), -rd(cai[1][:, ::-1])], axis=2)
    at = jnp.stack([jnp.concatenate([pr[0, :, t], pr[1, :, t]], axis=-1),
                    jnp.concatenate([pi[0, :, t], pi[1, :, t]], axis=-1)], axis=1)
    return w_intra, w_sum, w2t.astype(BF16), at


def _s5_toeplitz_kernel(cf_ref, cb_ref, sfr_ref, sfi_ref, sbr_ref, sbi_ref, o_ref, ws_ref):
    hi = lax.Precision.HIGHEST
    gl, k, _ = cf_ref.shape
    tk = sfr_ref.shape[2]
    for g in range(gl):
        abf = jnp.concatenate([sfr_ref[g], sfi_ref[g]], axis=0)
        abb = jnp.concatenate([sbr_ref[g], sbi_ref[g]], axis=0)
        ws_ref[g] = jnp.concatenate([sfr_ref[g], sbr_ref[g], sfi_ref[g], sbi_ref[g]], axis=0).astype(ws_ref.dtype)
        kf = jnp.dot(cf_ref[g], abf, precision=hi, preferred_element_type=F32)
        kb = jnp.dot(cb_ref[g], abb, precision=hi, preferred_element_type=F32)
        z = jnp.zeros_like(kf)
        krev = jnp.concatenate([kf, z], axis=1) + pltpu.roll(jnp.concatenate([z, kb], axis=1), 2 * tk - k, 1)
        for i in range(S5_CHUNK):
            sh = (S5_CHUNK - 1 - i) * k
            win = krev if sh == 0 else pltpu.roll(krev, 2 * tk - sh, 1)
            o_ref[g, i * k:(i + 1) * k, :] = win[:, 0:tk].astype(o_ref.dtype)


def _s5_toeplitz(cf, cb, sfr, sfi, sbr, sbi):
    g, k, p2 = cf.shape
    p, tk = sfr.shape[1:]
    gl = _pow2_tile(8, g)
    cspec = pl.BlockSpec((gl, k, p2), lambda a: (a, 0, 0))
    sspec = pl.BlockSpec((gl, p, tk), lambda a: (a, 0, 0))
    return pl.pallas_call(
        _s5_toeplitz_kernel,
        grid=(g // gl,),
        in_specs=[cspec, cspec, sspec, sspec, sspec, sspec],
        out_specs=[pl.BlockSpec((gl, tk, tk), lambda a: (a, 0, 0)), pl.BlockSpec((gl, 4 * p, tk), lambda a: (a, 0, 0))],
        out_shape=[jax.ShapeDtypeStruct((g, tk, tk), BF16), jax.ShapeDtypeStruct((g, 4 * p, tk), BF16)],
        compiler_params=_cparams("arbitrary"),
        name="s5_toeplitz",
    )(cf, cb, sfr, sfi, sbr, sbi)


def _s5_mixer(x, g, mod, layer, tok, params, h0, dims):
    (lam_re, lam_im, log_dt, b_re, b_im, c_re, c_im, d_skip, glu_w, slot, glu_b) = params
    batch, seq, dec_batch, dec_seq = dims
    n_prompt = batch * seq
    d = d_skip.shape[0]
    ngrp = d // S5_GROUP
    p = lam_re.shape[-1]
    w_intra, w_sum, w2t, at = _s5_tables(lam_re, lam_im, log_dt, b_re, b_im, c_re, c_im)
    hperm = _s5_pre(x, g, mod, layer, tok)
    h0g = h0.astype(F32).transpose(3, 2, 0, 1, 4).reshape(ngrp, 2, dec_batch, 2 * p)
    streams = ((0, seq // S5_CHUNK, batch, False, True),
               (n_prompt // S5_CHUNK, dec_seq // S5_CHUNK, dec_batch, True, False))
    yperm, fin = _s5_scan(hperm, w_intra, w_sum, w2t, at, h0g, batch, streams)
    x_new = _s5_post(x, yperm, g, mod, layer, d_skip.reshape(1, d), glu_w, slot, glu_b, tok)
    st = fin.reshape(ngrp, 2, batch, 2, p).transpose(2, 3, 1, 0, 4)
    return x_new, st


def kernel(x_prompt, x_sample, state_s5, cache_na_k, cache_na_v, cache_gqa_k, cache_gqa_v, c, c_ctx, norm_g, ada_w, ada_b, mlp_w1, mlp_w2, s5_lam_re, s5_lam_im, s5_log_dt, s5_b_re, s5_b_im, s5_c_re, s5_c_im, s5_d, s5_glu_w, s5_glu_b, na_w_qkv, na_q_norm, na_k_norm, na_rpb, na_w_o, gqa_w_qkv, gqa_q_norm, gqa_k_norm, gqa_w_o):
    batch, seq, d = x_prompt.shape
    dec_batch, dec_seq, _ = x_sample.shape
    depth = ada_w.shape[0]
    n_prompt = batch * seq
    n_sample = dec_batch * dec_seq
    na_heads, na_hd = cache_na_k.shape[3], cache_na_k.shape[4]
    gqa_kv, gqa_hd = cache_gqa_k.shape[3], cache_gqa_k.shape[4]
    assert n_prompt % dec_seq == 0

    tok = _Tok(n_prompt, n_sample, dec_seq, 1024)
    tok_half = _Tok(n_prompt, n_sample, dec_seq, 512)

    mod_rows = -(-(1 + dec_batch) // SUBLANES) * SUBLANES
    cvec = jnp.concatenate([c_ctx[None, :], c, jnp.zeros((mod_rows - 1 - dec_batch, d), F32)], axis=0)
    mod = _modulation(cvec, ada_w, ada_b).reshape(depth, mod_rows, 6, 1, d)

    x = (x_prompt.reshape(n_prompt, d), x_sample.reshape(n_sample, d))
    cache_na_k2 = cache_na_k.reshape(cache_na_k.shape[:3] + (na_heads * na_hd,))
    cache_na_v2 = cache_na_v.reshape(cache_na_v.shape[:3] + (na_heads * na_hd,))
    cache_gqa_k2 = cache_gqa_k.reshape(cache_gqa_k.shape[:3] + (gqa_kv * gqa_hd,))
    cache_gqa_v2 = cache_gqa_v.reshape(cache_gqa_v.shape[:3] + (gqa_kv * gqa_hd,))
    cos_t, sin_t = _rope_tables(dec_seq, gqa_hd)

    new_s5, new_na_k, new_na_v, new_gqa_k, new_gqa_v = [], [], [], [], []
    for i in range(depth):
        kind, slot = i % 3, i // 3
        g1 = norm_g[i, 0].reshape(1, d)
        g2 = norm_g[i, 1].reshape(1, d)
        if kind == 0:
            params = (s5_lam_re[slot], s5_lam_im[slot], s5_log_dt[slot], s5_b_re[slot], s5_b_im[slot],
                      s5_c_re[slot], s5_c_im[slot], s5_d[slot], s5_glu_w, slot, s5_glu_b[slot])
            x, st = _s5_mixer(x, g1, mod, i, tok_half, params, state_s5[:, slot], (batch, seq, dec_batch, dec_seq))
            new_s5.append(st)
        elif kind == 1:
            assert not isinstance(x, tuple)
            q, k_p, k_s, v_p, v_s = _na_qkv(x, g1, mod, i, na_w_qkv, slot, na_q_norm[slot], na_k_norm[slot], tok_half)
            o_p = _na_ctx_attn(q, k_p, v_p, batch, seq, na_hd)
            bias = _na_bias_table(na_rpb[slot], dec_seq)
            o_s = _na_lat_attn(q, k_s, v_s, cache_na_k2, cache_na_v2, slot, bias, n_prompt, dec_batch, dec_seq, na_hd)
            x = _proj_res(x, o_p, o_s, na_w_o, slot, mod, i, tok)
            new_na_k.append(k_p.reshape(batch, seq, na_heads, na_hd))
            new_na_v.append(v_p.reshape(batch, seq, na_heads, na_hd))
        else:
            assert not isinstance(x, tuple)
            q, k_p, k_s, v_p, v_s = _gqa_qkv(x, g1, mod, i, gqa_w_qkv, slot, gqa_q_norm[slot], gqa_k_norm[slot],
                                             cos_t, sin_t, gqa_kv, tok_half)
            o_p = _gqa_ctx_attn(q, k_p, v_p, batch, seq, gqa_kv)
            o_s = _gqa_lat_attn(q, k_s, v_s, cache_gqa_k2, cache_gqa_v2, slot, n_prompt, dec_batch, dec_seq, gqa_kv)
            x = _proj_res(x, o_p, o_s, gqa_w_o, slot, mod, i, tok)
            new_gqa_k.append(k_p.reshape(batch, seq, gqa_kv, gqa_hd))
            new_gqa_v.append(v_p.reshape(batch, seq, gqa_kv, gqa_hd))
        x = _mlp(x, g2, mod, i, mlp_w1, mlp_w2, tok, split_out=(i == depth - 1))
    y_p, y_s = x
    return (y_p.reshape(batch, seq, d), y_s.reshape(dec_batch, dec_seq, d),
            jnp.stack(new_s5, axis=1), jnp.stack(new_na_k, axis=1), jnp.stack(new_na_v, axis=1),
            jnp.stack(new_gqa_k, axis=1), jnp.stack(new_gqa_v, axis=1))
```

```python
import functools
import math

import jax
import jax.numpy as jnp
from jax import lax
from jax.experimental import pallas as pl
from jax.experimental.pallas import tpu as pltpu

F32 = jnp.float32
BF16 = jnp.bfloat16

EPS = 1e-6
GRID_W = 64
S5_GROUP = 16
NA_KH = 8
NA_KW = 16
ROPE_THETA = 10000.0
S5_CHUNK = 16
LOG2E = math.log2(math.e)
LANES = 128
SUBLANES = 8
VMEM_LIMIT = 56 * 1024 * 1024


def _cparams(*sem):
    return pltpu.CompilerParams(dimension_semantics=sem, vmem_limit_bytes=VMEM_LIMIT)


def _pow2_tile(pref, *ns):
    t = pref
    while any(n % t for n in ns):
        t //= 2
    return t


def _normmod(x, g, sc, sh):
    ms = jnp.mean(x * x, axis=-1, keepdims=True)
    y = x * lax.rsqrt(ms + EPS) * g
    return y * (1.0 + sc) + sh


class _Tok:
    def __init__(self, n_prompt, n_sample, dec_seq, pref):
        self.tm = _pow2_tile(pref, n_prompt, dec_seq)
        self.n = n_prompt + n_sample
        self.tiles = self.n // self.tm
        self.np_tiles = n_prompt // self.tm
        self.tps = dec_seq // self.tm

    def mod_row(self, i):
        return jnp.where(i < self.np_tiles, 0, 1 + (i - self.np_tiles) // self.tps)

    def prompt_spec(self, d, nargs):
        last = self.np_tiles - 1
        if nargs == 1:
            return pl.BlockSpec((self.tm, d), lambda i: (jnp.minimum(i, last), 0))
        return pl.BlockSpec((self.tm, d), lambda i, j: (jnp.minimum(i, last), 0))

    def sample_spec(self, d, nargs):
        first = self.np_tiles
        if nargs == 1:
            return pl.BlockSpec((self.tm, d), lambda i: (jnp.maximum(i - first, 0), 0))
        return pl.BlockSpec((self.tm, d), lambda i, j: (jnp.maximum(i - first, 0), 0))

    def mod_spec(self, layer, which, d, nargs):
        if nargs == 1:
            return pl.BlockSpec((None, None, None, 1, d), lambda i: (layer, self.mod_row(i), which, 0, 0))
        return pl.BlockSpec((None, None, None, 1, d), lambda i, j: (layer, self.mod_row(i), which, 0, 0))


def _mod_kernel(c_ref, w_ref, b_ref, o_ref):
    c = c_ref[...]
    s = c * jax.nn.sigmoid(c)
    o_ref[...] = jnp.dot(s.astype(BF16), w_ref[...].astype(BF16), preferred_element_type=F32) + b_ref[...]


def _modulation(cvec, ada_w, ada_b):
    depth, d, d6 = ada_w.shape
    rows = cvec.shape[0]
    tn = _pow2_tile(2048, d6) if d6 % 2048 == 0 else d
    return pl.pallas_call(
        _mod_kernel,
        grid=(depth, d6 // tn),
        in_specs=[pl.BlockSpec((rows, d), lambda l, j: (0, 0)),
                  pl.BlockSpec((None, d, tn), lambda l, j: (l, 0, j)),
                  pl.BlockSpec((None, 1, tn), lambda l, j: (l, 0, j))],
        out_specs=pl.BlockSpec((None, rows, tn), lambda l, j: (l, 0, j)),
        out_shape=jax.ShapeDtypeStruct((depth, rows, d6), F32),
        compiler_params=_cparams("arbitrary", "arbitrary"),
        name="adaln_modulation",
    )(cvec, ada_w, ada_b.reshape(depth, 1, d6))


def _read_x(x_refs, np_tiles):
    if len(x_refs) == 1:
        return x_refs[0][...]
    return jnp.where(pl.program_id(0) < np_tiles, x_refs[0][...], x_refs[1][...])


def _x_specs(x, tok, nargs):
    if isinstance(x, tuple):
        d = x[0].shape[1]
        return [tok.prompt_spec(d, nargs), tok.sample_spec(d, nargs)], x
    d = x.shape[1]
    return [pl.BlockSpec((tok.tm, d), (lambda i: (i, 0)) if nargs == 1 else (lambda i, j: (i, 0)))], (x,)


def _s5_pre_kernel(*refs, n_x, np_tiles):
    x_refs, (g_ref, sc_ref, sh_ref, h_ref, h_scr) = refs[:n_x], refs[n_x:]
    h = _normmod(_read_x(x_refs, np_tiles), g_ref[...], sc_ref[...], sh_ref[...])
    nc = h_ref.shape[1]
    for a in range(h_scr.shape[0]):
        sl = slice(a * LANES, (a + 1) * LANES)
        h_scr[a] = h[:, sl]
        for j in range(S5_CHUNK):
            h_ref[j, :, sl] = h_scr[a, pl.ds(j, nc, stride=S5_CHUNK), :]


def _s5_pre(x, g, mod, layer, tok):
    x_specs, xs = _x_specs(x, tok, 1)
    d = xs[0].shape[1]
    nc = tok.tm // S5_CHUNK
    return pl.pallas_call(
        functools.partial(_s5_pre_kernel, n_x=len(xs), np_tiles=tok.np_tiles),
        grid=(tok.tiles,),
        in_specs=x_specs + [pl.BlockSpec((1, d), lambda i: (0, 0)),
                            tok.mod_spec(layer, 1, d, 1),
                            tok.mod_spec(layer, 0, d, 1)],
        out_specs=pl.BlockSpec((S5_CHUNK, nc, d), lambda i: (0, i, 0)),
        out_shape=jax.ShapeDtypeStruct((S5_CHUNK, tok.n // S5_CHUNK, d), F32),
        scratch_shapes=[pltpu.VMEM((d // LANES, tok.tm, LANES), F32)],
        compiler_params=_cparams("arbitrary"),
        name="s5_pre",
    )(*xs, g, mod, mod)


def _s5_scan_kernel(h_ref, wi_ref, ws_ref, w2t_ref, at_ref, h0_ref, *rest, streams, unroll):
    ns = len(streams)
    perm_refs, (y_ref, fin_ref, xt_scr, yt_scr), chain_scr = rest[:ns], rest[ns:ns + 4], rest[ns + 4:]
    ngrp = wi_ref.shape[0]
    p2 = at_ref.shape[-1]
    fwd = lax.broadcasted_iota(jnp.int32, (1, p2), 1) < (p2 // 2)
    for j in range(S5_CHUNK):
        xt_scr[j] = h_ref[j].T.astype(BF16)

    def load_rmat(r, stream):
        row0, n_chunks, nb = stream[:3]
        cols = slice(row0, row0 + n_chunks * nb)
        c0 = pl.multiple_of(r * S5_GROUP, S5_GROUP)
        return jnp.concatenate([xt_scr[j, pl.ds(c0, S5_GROUP), cols] for j in range(S5_CHUNK)], axis=0), c0, cols

    def summaries(r, stream, perm_ref, s_scr):
        nb, use_h0 = stream[2], stream[3]
        rmat, _, _ = load_rmat(r, stream)
        rmat_cm = jnp.dot(rmat, perm_ref[...], preferred_element_type=F32).astype(BF16)
        zs = jnp.dot(ws_ref[r], rmat_cm, preferred_element_type=F32)
        s_scr[0] = zs[0:p2].T
        s_scr[1] = zs[p2:].T
        if use_h0:
            return h0_ref[r, 0], h0_ref[r, 1]
        return jnp.zeros((nb, p2), F32), jnp.zeros((nb, p2), F32)

    def scan_step(t, decay, stream, state, s_scr, e_scr):
        n_chunks, nb = stream[1], stream[2]
        hr, hi = state
        ar, ai = decay
        ft = slice(t * nb, (t + 1) * nb)
        bt = slice((n_chunks - 1 - t) * nb, (n_chunks - t) * nb)
        e_scr[0, ft, :] = hr
        e_scr[1, bt, :] = hr
        e_scr[2, ft, :] = hi
        e_scr[3, bt, :] = hi
        in_re = jnp.where(fwd, s_scr[0, ft, :], s_scr[0, bt, :])
        in_im = jnp.where(fwd, s_scr[1, ft, :], s_scr[1, bt, :])
        return ar * hr - ai * hi + in_re, ar * hi + ai * hr + in_im

    def outputs(r, stream, perm_ref, state, e_scr):
        rmat, c0, cols = load_rmat(r, stream)
        if stream[4]:
            fin_ref[r, 0] = state[0]
            fin_ref[r, 1] = state[1]
        e_cm = jnp.concatenate([jnp.where(fwd, e_scr[0], e_scr[1]), jnp.where(fwd, e_scr[2], e_scr[3])],
                               axis=1).astype(BF16)
        e = jnp.dot(perm_ref[...], e_cm, preferred_element_type=F32).astype(BF16)
        yt = (jnp.dot(wi_ref[r], rmat, preferred_element_type=F32)
              + lax.dot_general(w2t_ref[r], e, (((1,), (1,)), ((), ())), preferred_element_type=F32))
        for i in range(S5_CHUNK):
            yt_scr[i, pl.ds(c0, S5_GROUP), cols] = yt[i * S5_GROUP:(i + 1) * S5_GROUP]

    def body(rb, carry):
        chains = []
        for u in range(unroll):
            for si, stream in enumerate(streams):
                k = 2 * (u * ns + si)
                chains.append((rb * unroll + u, stream, perm_refs[si], chain_scr[k], chain_scr[k + 1]))
        states = [summaries(r, stream, pm, s_scr) for r, stream, pm, s_scr, _ in chains]
        decays = [(at_ref[r, 0:1, :], at_ref[r, 1:2, :]) for r, _, _, _, _ in chains]
        for t in range(max(stream[1] for stream in streams)):
            for ci, (_, stream, _, s_scr, e_scr) in enumerate(chains):
                if t < stream[1]:
                    states[ci] = scan_step(t, decays[ci], stream, states[ci], s_scr, e_scr)
        for (r, stream, pm, _, e_scr), state in zip(chains, states):
            outputs(r, stream, pm, state, e_scr)
        return carry

    lax.fori_loop(0, ngrp // unroll, body, 0)
    for i in range(S5_CHUNK):
        y_ref[i] = yt_scr[i].T


def _s5_scan(hperm, w_intra, w_sum, w2t, at, h0, nb_fin, streams):
    t, nrows, d = hperm.shape
    g, tk, p4 = w2t.shape
    p2 = at.shape[-1]
    gl = LANES // S5_GROUP
    nb0 = h0.shape[2]
    unroll = 4
    assert sum(s[1] * s[2] for s in streams) == nrows and gl % unroll == 0
    chain_scr = []
    for _ in range(unroll):
        for s in streams:
            chain_scr += [pltpu.VMEM((2, s[1] * s[2], p2), F32), pltpu.VMEM((4, s[1] * s[2], p2), F32)]
    perms = []
    for _, n_chunks, nb, _, _ in streams:
        m = jnp.arange(n_chunks * nb)
        perms.append((m[:, None] == ((m % nb) * n_chunks + m // nb)[None, :]).astype(BF16))
    return pl.pallas_call(
        functools.partial(_s5_scan_kernel, streams=streams, unroll=unroll),
        grid=(d // LANES,),
        in_specs=[pl.BlockSpec((t, nrows, LANES), lambda a: (0, 0, a)),
                  pl.BlockSpec((gl, tk, tk), lambda a: (a, 0, 0)),
                  pl.BlockSpec((gl, p4, tk), lambda a: (a, 0, 0)),
                  pl.BlockSpec((gl, tk, p4), lambda a: (a, 0, 0)),
                  pl.BlockSpec((gl, 2, p2), lambda a: (a, 0, 0)),
                  pl.BlockSpec((gl, 2, nb0, p2), lambda a: (a, 0, 0, 0))]
                 + [pl.BlockSpec(pm.shape, lambda a: (0, 0)) for pm in perms],
        out_specs=[pl.BlockSpec((t, nrows, LANES), lambda a: (0, 0, a)),
                   pl.BlockSpec((gl, 2, nb_fin, p2), lambda a: (a, 0, 0, 0))],
        out_shape=[jax.ShapeDtypeStruct((t, nrows, d), F32),
                   jax.ShapeDtypeStruct((g, 2, nb_fin, p2), F32)],
        scratch_shapes=[pltpu.VMEM((t, LANES, nrows), BF16), pltpu.VMEM((t, LANES, nrows), F32)] + chain_scr,
        compiler_params=_cparams("arbitrary"),
        name="s5_scan",
    )(hperm, w_intra, w_sum, w2t, at, h0, *perms)


def _s5_post_kernel(*refs, n_x, np_tiles):
    x_refs = refs[:n_x]
    (y_ref, g_ref, sc_ref, sh_ref, gate_ref, dsk_ref, wa_ref, wg_ref, ba_ref, bg_ref, o_ref,
     y_scr, wa_scr, wg_scr) = refs[n_x:]
    _cast_once(wa_ref, wa_scr)
    _cast_once(wg_ref, wg_scr)
    nc = y_ref.shape[1]
    for a in range(y_scr.shape[0]):
        for j in range(S5_CHUNK):
            y_scr[a, pl.ds(j, nc, stride=S5_CHUNK), :] = y_ref[j, :, a * LANES:(a + 1) * LANES]
    x = _read_x(x_refs, np_tiles)
    h = _normmod(x, g_ref[...], sc_ref[...], sh_ref[...])
    y = h * dsk_ref[...] + jnp.concatenate([y_scr[a] for a in range(y_scr.shape[0])], axis=1)
    a = jax.nn.gelu(y).astype(BF16)
    za = jnp.dot(a, wa_scr[...], preferred_element_type=F32) + ba_ref[...]
    zg = jnp.dot(a, wg_scr[...], preferred_element_type=F32) + bg_ref[...]
    o_ref[...] = x + gate_ref[...] * (za * jax.nn.sigmoid(zg))


def _s5_post(x, y, g, mod, layer, dsk, glu_w, slot, glu_b, tok):
    x_specs, xs = _x_specs(x, tok, 1)
    n, d = tok.n, xs[0].shape[1]
    row = lambda i: (i, 0)
    fixed = lambda i: (0, 0)
    gb2 = glu_b.reshape(1, 2 * d)
    return pl.pallas_call(
        functools.partial(_s5_post_kernel, n_x=len(xs), np_tiles=tok.np_tiles),
        grid=(tok.tiles,),
        in_specs=x_specs + [
                  pl.BlockSpec((S5_CHUNK, tok.tm // S5_CHUNK, d), lambda i: (0, i, 0)),
                  pl.BlockSpec((1, d), fixed),
                  tok.mod_spec(layer, 1, d, 1),
                  tok.mod_spec(layer, 0, d, 1),
                  tok.mod_spec(layer, 2, d, 1),
                  pl.BlockSpec((1, d), fixed),
                  _resident((None, d, d), lambda i: (slot, 0, 0)),
                  _resident((None, d, d), lambda i: (slot, 0, 1)),
                  pl.BlockSpec((1, d), lambda i: (0, 0)),
                  pl.BlockSpec((1, d), lambda i: (0, 1))],
        out_specs=pl.BlockSpec((tok.tm, d), row),
        out_shape=jax.ShapeDtypeStruct((n, d), F32),
        scratch_shapes=[pltpu.VMEM((d // LANES, tok.tm, LANES), F32),
                        pltpu.VMEM((d, d), BF16), pltpu.VMEM((d, d), BF16)],
        compiler_params=_cparams("arbitrary"),
        name="s5_post",
    )(*xs, y, g, mod, mod, mod, dsk, glu_w, glu_w, gb2, gb2)


def _cast_once(w_ref, w_scr):
    @pl.when(pl.program_id(0) == 0)
    def _():
        w_scr[...] = w_ref[...].astype(w_scr.dtype)


def _resident(block_shape, index_map):
    return pl.BlockSpec(block_shape, index_map, pipeline_mode=pl.Buffered(1))


def _na_qkv_kernel(x_ref, g_ref, sc_ref, sh_ref, w_ref, qn_ref, kn_ref, q_ref, kp_ref, ks_ref, vp_ref, vs_ref, w_scr, *,
                   hd, np_tiles):
    i = pl.program_id(0)
    _cast_once(w_ref, w_scr)
    h = _normmod(x_ref[...], g_ref[...], sc_ref[...], sh_ref[...]).astype(BF16)
    d = h.shape[1]
    lo = lax.broadcasted_iota(jnp.int32, (1, LANES), 1) < hd

    def head_norm(z, gain):
        outs = []
        for s in range(d // LANES):
            seg = z[:, s * LANES:(s + 1) * LANES]
            sq = seg * seg
            s_lo = jnp.sum(jnp.where(lo, sq, 0.0), axis=-1, keepdims=True)
            s_hi = jnp.sum(jnp.where(lo, 0.0, sq), axis=-1, keepdims=True)
            ms = jnp.where(lo, s_lo, s_hi) / hd
            outs.append(seg * lax.rsqrt(ms + EPS) * gain)
        return jnp.concatenate(outs, axis=1)

    proj = lambda part: jnp.dot(h, w_scr[:, part * d:(part + 1) * d], preferred_element_type=F32)
    q_ref[...] = (head_norm(proj(0), qn_ref[...]) * (hd ** -0.5 * LOG2E)).astype(q_ref.dtype)
    k = head_norm(proj(1), kn_ref[...])
    v = proj(2)

    @pl.when(i < np_tiles)
    def _():
        kp_ref[...] = k
        vp_ref[...] = v

    @pl.when(i >= np_tiles)
    def _():
        ks_ref[...] = k.astype(ks_ref.dtype)
        vs_ref[...] = v.astype(vs_ref.dtype)


def _na_qkv(x, g, mod, layer, w_qkv, slot, qn, kn, tok):
    n, d = x.shape
    hd = qn.shape[-1]
    assert 2 * hd == LANES
    row = lambda i: (i, 0)
    fixed = lambda i: (0, 0)
    qn2 = jnp.tile(qn, 2).reshape(1, LANES)
    kn2 = jnp.tile(kn, 2).reshape(1, LANES)
    n_p, n_s = tok.np_tiles * tok.tm, n - tok.np_tiles * tok.tm
    return pl.pallas_call(
        functools.partial(_na_qkv_kernel, hd=hd, np_tiles=tok.np_tiles),
        grid=(tok.tiles,),
        in_specs=[pl.BlockSpec((tok.tm, d), row),
                  pl.BlockSpec((1, d), fixed),
                  tok.mod_spec(layer, 1, d, 1),
                  tok.mod_spec(layer, 0, d, 1),
                  _resident((None, d, 3 * d), lambda i: (slot, 0, 0)),
                  pl.BlockSpec((1, LANES), fixed),
                  pl.BlockSpec((1, LANES), fixed)],
        out_specs=[pl.BlockSpec((tok.tm, d), row),
                   tok.prompt_spec(d, 1), tok.sample_spec(d, 1), tok.prompt_spec(d, 1), tok.sample_spec(d, 1)],
        out_shape=[jax.ShapeDtypeStruct((n, d), BF16),
                   jax.ShapeDtypeStruct((n_p, d), F32), jax.ShapeDtypeStruct((n_s, d), BF16),
                   jax.ShapeDtypeStruct((n_p, d), F32), jax.ShapeDtypeStruct((n_s, d), BF16)],
        scratch_shapes=[pltpu.VMEM((d, 3 * d), BF16)],
        compiler_params=_cparams("arbitrary"),
        name="na_qkv",
    )(x, g, mod, mod, w_qkv, qn2, kn2)


def _softmax2_pv(s_parts, v_parts):
    m = s_parts[0].max(axis=-1, keepdims=True)
    for s in s_parts[1:]:
        m = jnp.maximum(m, s.max(axis=-1, keepdims=True))
    den = 0.0
    acc = 0.0
    for s, v in zip(s_parts, v_parts):
        e = jnp.exp2(s - m)
        den = den + e.sum(axis=-1, keepdims=True)
        acc = acc + jnp.dot(e.astype(BF16), v, preferred_element_type=F32)
    return acc / den


def _qk(q, k):
    return lax.dot_general(q, k, (((1,), (1,)), ((), ())), preferred_element_type=F32)


def _na_ctx_attn_kernel(q_ref, k_ref, v_ref, o_ref, *, hd):
    lo = lax.broadcasted_iota(jnp.int32, (1, LANES), 1) < hd
    for s in range(q_ref.shape[1] // LANES):
        sl = slice(s * LANES, (s + 1) * LANES)
        q = q_ref[:, sl]
        k = k_ref[:, sl].astype(BF16)
        v = v_ref[:, sl].astype(BF16)
        zero = jnp.zeros_like(q)
        s_lo = _qk(jnp.where(lo, q, zero), k)
        s_hi = _qk(jnp.where(lo, zero, q), k)
        o_ref[:, sl] = jnp.where(lo, _softmax2_pv([s_lo], [v]), _softmax2_pv([s_hi], [v])).astype(o_ref.dtype)


def _na_ctx_attn(q, k, v, batch, seq, hd):
    d = q.shape[1]
    spec = pl.BlockSpec((seq, d), lambda b: (b, 0))
    return pl.pallas_call(
        functools.partial(_na_ctx_attn_kernel, hd=hd),
        grid=(batch,),
        in_specs=[spec, spec, spec],
        out_specs=spec,
        out_shape=jax.ShapeDtypeStruct((batch * seq, d), BF16),
        compiler_params=_cparams("arbitrary"),
        name="na_ctx_attn",
    )(q, k, v)


def _na_window(r0, q_rows, rows, kh):
    w_rows = min(rows, kh + q_rows + (kh + q_rows) % 2)
    rs = min(max(r0 - kh // 2, 0), rows - kh)
    return min(rs // 2 * 2, rows - w_rows), w_rows


def _na_lat_attn_kernel(q_ref, kl_ref, vl_ref, kc_ref, vc_ref, tab_ref, o_ref, kc_scr, vc_scr, bias_scr, *,
                        hd, rows, kh, tq):
    lane = lax.broadcasted_iota(jnp.int32, (1, LANES), 1)
    lo = lane < hd
    lower = lane < GRID_W
    q_rows = tq // GRID_W
    n_qb = q_ref.shape[0] // tq
    w_cols = bias_scr.shape[2]
    kc_scr[...] = kc_ref[...].astype(BF16)
    vc_scr[...] = vc_ref[...].astype(BF16)

    @pl.when(pl.program_id(1) == 0)
    def _():
        neg = jnp.full((GRID_W, LANES), -jnp.inf, F32)
        for hh in range(2):
            for rq in range(rows):
                w0, w_rows = _na_window(rq // q_rows * q_rows, q_rows, rows, kh)
                rs = min(max(rq - kh // 2, 0), rows - kh)
                for kp in range(w_rows // 2):
                    rk = w0 + 2 * kp
                    ok0, ok1 = rs <= rk < rs + kh, rs <= rk + 1 < rs + kh
                    tile = tab_ref[hh, rk - rq + rows - 1]
                    if not (ok0 and ok1):
                        tile = jnp.where(lower if ok0 else jnp.logical_not(lower), tile, neg) if (ok0 or ok1) else neg
                    bias_scr[hh, rq * GRID_W:(rq + 1) * GRID_W, kp * LANES:(kp + 1) * LANES] = tile

    def qblock(qb, carry):
        r0 = qb * q_rows
        w0 = jnp.minimum(jnp.clip(r0 - kh // 2, 0, rows - kh) // 2 * 2, rows - w_cols // GRID_W)
        k0 = pl.multiple_of(w0 * GRID_W, 2 * GRID_W)
        q0 = pl.multiple_of(qb * tq, tq)
        q = q_ref[pl.ds(q0, tq), :]
        kl = kl_ref[pl.ds(k0, w_cols), :]
        vl = vl_ref[pl.ds(k0, w_cols), :]
        zero = jnp.zeros_like(q)
        scores = []
        for hh in range(2):
            qm = jnp.where(lo, q, zero) if hh == 0 else jnp.where(lo, zero, q)
            scores.append([_qk(qm, kl) + bias_scr[hh, pl.ds(q0, tq), :], _qk(qm, kc_scr[...])])
        outs = [_softmax2_pv(s, [vl, vc_scr[...]]) for s in scores]
        o_ref[pl.ds(q0, tq), :] = jnp.where(lo, outs[0], outs[1]).astype(o_ref.dtype)
        return carry

    lax.fori_loop(0, n_qb, qblock, 0)


def _na_lat_attn(q, k, v, cache_k, cache_v, slot, tab, n_prompt, dec_batch, dec_seq, hd):
    d = q.shape[1]
    past = cache_k.shape[2]
    rows = dec_seq // GRID_W
    kh = min(NA_KH, rows)
    tq = _pow2_tile(256, dec_seq)
    assert tq % GRID_W == 0 and rows % 2 == 0 and 2 * GRID_W == LANES and n_prompt % dec_seq == 0
    b0 = n_prompt // dec_seq
    w_rows = _na_window(0, tq // GRID_W, rows, kh)[1]
    return pl.pallas_call(
        functools.partial(_na_lat_attn_kernel, hd=hd, rows=rows, kh=kh, tq=tq),
        grid=(d // LANES, dec_batch),
        in_specs=[pl.BlockSpec((dec_seq, LANES), lambda hp, b: (b0 + b, hp)),
                  pl.BlockSpec((dec_seq, LANES), lambda hp, b: (b, hp)),
                  pl.BlockSpec((dec_seq, LANES), lambda hp, b: (b, hp)),
                  pl.BlockSpec((None, None, past, LANES), lambda hp, b: (b, slot, 0, hp)),
                  pl.BlockSpec((None, None, past, LANES), lambda hp, b: (b, slot, 0, hp)),
                  pl.BlockSpec((2,) + tab.shape[1:], lambda hp, b: (hp, 0, 0, 0))],
        out_specs=pl.BlockSpec((dec_seq, LANES), lambda hp, b: (b, hp)),
        out_shape=jax.ShapeDtypeStruct((dec_batch * dec_seq, d), BF16),
        scratch_shapes=[pltpu.VMEM((past, LANES), BF16), pltpu.VMEM((past, LANES), BF16),
                        pltpu.VMEM((2, dec_seq, w_rows * GRID_W), F32)],
        compiler_params=_cparams("arbitrary", "arbitrary"),
        name="na_lat_attn",
    )(q, k, v, cache_k, cache_v, tab)


def _na_bias_table(rpb, dec_seq):
    rows = dec_seq // GRID_W
    kh = min(NA_KH, rows)
    nh, nd, nc = rpb.shape
    c = jnp.arange(GRID_W)
    cs = jnp.clip(c - NA_KW // 2, 0, GRID_W - NA_KW)
    col_ok = (c[None, :] >= cs[:, None]) & (c[None, :] < cs[:, None] + NA_KW)
    dcol = c[None, :] - c[:, None] + NA_KW - 1
    onehot = (dcol[None] == jnp.arange(nc)[:, None, None]).astype(F32)
    tmp = jnp.einsum('hdc,cxy->hdxy', rpb.astype(F32), onehot, precision=lax.Precision.HIGHEST)
    tmp = jnp.where(col_ok[None, None], tmp * LOG2E, -jnp.inf)
    n_e = 2 * rows - 2
    front = rows - kh
    back = n_e + 1 - nd - front
    pad = lambda n: jnp.full((nh, n, GRID_W, GRID_W), -jnp.inf, F32)
    ext = jnp.concatenate([pad(front), tmp, pad(back)], axis=1)
    return jnp.concatenate([ext[:, :-1], ext[:, 1:]], axis=-1)


def _gqa_qkv_kernel(x_ref, g_ref, sc_ref, sh_ref, w_ref, qn_ref, kn_ref, cos_ref, sin_ref,
                    q_ref, kp_ref, ks_ref, vp_ref, vs_ref, w_scr, *, nq, nk, np_tiles):
    i = pl.program_id(0)
    _cast_once(w_ref, w_scr)
    h = _normmod(x_ref[...], g_ref[...], sc_ref[...], sh_ref[...]).astype(BF16)
    z = jnp.dot(h, w_scr[...], preferred_element_type=F32)
    is_sample = i >= np_tiles
    cos = cos_ref[...]
    sin = sin_ref[...]

    def norm_rope(seg, gain):
        ms = jnp.mean(seg * seg, axis=-1, keepdims=True)
        y = seg * lax.rsqrt(ms + EPS) * gain
        roped = y * cos + pltpu.roll(y, LANES // 2, 1) * sin
        return jnp.where(is_sample, roped, y)

    for hh in range(nq):
        sl = slice(hh * LANES, (hh + 1) * LANES)
        q_ref[:, sl] = (norm_rope(z[:, sl], qn_ref[...]) * (LANES ** -0.5 * LOG2E)).astype(q_ref.dtype)
    k = jnp.concatenate([norm_rope(z[:, (nq + hh) * LANES:(nq + hh + 1) * LANES], kn_ref[...]) for hh in range(nk)],
                        axis=1)
    v = z[:, (nq + nk) * LANES:]

    @pl.when(i < np_tiles)
    def _():
        kp_ref[...] = k
        vp_ref[...] = v

    @pl.when(is_sample)
    def _():
        ks_ref[...] = k.astype(ks_ref.dtype)
        vs_ref[...] = v.astype(vs_ref.dtype)


def _gqa_qkv(x, g, mod, layer, w_qkv, slot, qn, kn, cos_t, sin_t, nk, tok):
    n, d = x.shape
    hd = qn.shape[-1]
    assert hd == LANES
    nq = d // hd
    dk = nk * hd
    row = lambda i: (i, 0)
    fixed = lambda i: (0, 0)
    pos = lambda i: (jnp.maximum(i - tok.np_tiles, 0) % tok.tps, 0)
    n_p, n_s = tok.np_tiles * tok.tm, n - tok.np_tiles * tok.tm
    return pl.pallas_call(
        functools.partial(_gqa_qkv_kernel, nq=nq, nk=nk, np_tiles=tok.np_tiles),
        grid=(tok.tiles,),
        in_specs=[pl.BlockSpec((tok.tm, d), row),
                  pl.BlockSpec((1, d), fixed),
                  tok.mod_spec(layer, 1, d, 1),
                  tok.mod_spec(layer, 0, d, 1),
                  _resident((None, d, d + 2 * dk), lambda i: (slot, 0, 0)),
                  pl.BlockSpec((1, hd), fixed),
                  pl.BlockSpec((1, hd), fixed),
                  pl.BlockSpec((tok.tm, hd), pos),
                  pl.BlockSpec((tok.tm, hd), pos)],
        out_specs=[pl.BlockSpec((tok.tm, d), row),
                   tok.prompt_spec(dk, 1), tok.sample_spec(dk, 1), tok.prompt_spec(dk, 1), tok.sample_spec(dk, 1)],
        out_shape=[jax.ShapeDtypeStruct((n, d), BF16),
                   jax.ShapeDtypeStruct((n_p, dk), F32), jax.ShapeDtypeStruct((n_s, dk), BF16),
                   jax.ShapeDtypeStruct((n_p, dk), F32), jax.ShapeDtypeStruct((n_s, dk), BF16)],
        scratch_shapes=[pltpu.VMEM((d, d + 2 * dk), BF16)],
        compiler_params=_cparams("arbitrary"),
        name="gqa_qkv",
    )(x, g, mod, mod, w_qkv, qn.reshape(1, hd), kn.reshape(1, hd), cos_t, sin_t)


def _rope_tables(dec_seq, hd):
    t = jnp.arange(dec_seq)
    row = (t // GRID_W).astype(F32)
    col = (t % GRID_W).astype(F32)
    half = hd // 2
    inv = ROPE_THETA ** (-jnp.arange(0, half, 2, dtype=F32) / half)
    ang = jnp.concatenate([row[:, None] * inv, col[:, None] * inv], axis=-1)
    cos, sin = jnp.cos(ang), jnp.sin(ang)
    return jnp.concatenate([cos, cos], axis=-1), jnp.concatenate([-sin, sin], axis=-1)


def _gqa_attn_kernel(q_ref, *refs, rep, nk, n_kv):
    k_refs, v_refs, o_ref = refs[:n_kv], refs[n_kv:2 * n_kv], refs[2 * n_kv]
    tq = q_ref.shape[0]
    scores, values = [], []
    for kv in range(nk):
        sl = slice(kv * LANES, (kv + 1) * LANES)
        qs = jnp.concatenate([q_ref[:, (kv * rep + r) * LANES:(kv * rep + r + 1) * LANES] for r in range(rep)], axis=0)
        scores.append([_qk(qs, r[:, sl].astype(BF16)) for r in k_refs])
        values.append([r[:, sl].astype(BF16) for r in v_refs])
    for kv in range(nk):
        o = _softmax2_pv(scores[kv], values[kv])
        for r in range(rep):
            o_ref[:, (kv * rep + r) * LANES:(kv * rep + r + 1) * LANES] = o[r * tq:(r + 1) * tq].astype(o_ref.dtype)


def _gqa_ctx_attn(q, k, v, batch, seq, nk):
    d = q.shape[1]
    dk = k.shape[1]
    rep = d // dk
    qspec = pl.BlockSpec((seq, d), lambda b: (b, 0))
    kspec = pl.BlockSpec((seq, dk), lambda b: (b, 0))
    return pl.pallas_call(
        functools.partial(_gqa_attn_kernel, rep=rep, nk=nk, n_kv=1),
        grid=(batch,),
        in_specs=[qspec, kspec, kspec],
        out_specs=qspec,
        out_shape=jax.ShapeDtypeStruct((batch * seq, d), BF16),
        compiler_params=_cparams("arbitrary"),
        name="gqa_ctx_attn",
    )(q, k, v)


def _gqa_lat_attn(q, k, v, cache_k, cache_v, slot, n_prompt, dec_batch, dec_seq, nk):
    d = q.shape[1]
    dk = k.shape[1]
    rep = d // dk
    past = cache_k.shape[2]
    tq = _pow2_tile(256, dec_seq, n_prompt)
    nqb = dec_seq // tq
    q0 = n_prompt // tq
    lspec = pl.BlockSpec((dec_seq, dk), lambda b, qb: (b, 0))
    cspec = pl.BlockSpec((None, None, past, dk), lambda b, qb: (b, slot, 0, 0))
    return pl.pallas_call(
        functools.partial(_gqa_attn_kernel, rep=rep, nk=nk, n_kv=2),
        grid=(dec_batch, nqb),
        in_specs=[pl.BlockSpec((tq, d), lambda b, qb: (q0 + b * nqb + qb, 0)),
                  lspec, cspec, lspec, cspec],
        out_specs=pl.BlockSpec((tq, d), lambda b, qb: (b * nqb + qb, 0)),
        out_shape=jax.ShapeDtypeStruct((dec_batch * dec_seq, d), BF16),
        compiler_params=_cparams("arbitrary", "arbitrary"),
        name="gqa_lat_attn",
    )(q, k, cache_k, v, cache_v)


def _proj_res_kernel(x_ref, op_ref, os_ref, w_ref, gate_ref, out_ref, w_scr, *, np_tiles):
    _cast_once(w_ref, w_scr)
    o = jnp.where(pl.program_id(0) < np_tiles, op_ref[...], os_ref[...])
    y = jnp.dot(o, w_scr[...], preferred_element_type=F32)
    out_ref[...] = x_ref[...] + gate_ref[...] * y


def _proj_res(x, o_p, o_s, w_o, slot, mod, layer, tok):
    n, d = x.shape
    row = lambda i: (i, 0)
    return pl.pallas_call(
        functools.partial(_proj_res_kernel, np_tiles=tok.np_tiles),
        grid=(tok.tiles,),
        in_specs=[pl.BlockSpec((tok.tm, d), row),
                  tok.prompt_spec(d, 1),
                  tok.sample_spec(d, 1),
                  _resident((None, d, d), lambda i: (slot, 0, 0)),
                  tok.mod_spec(layer, 2, d, 1)],
        out_specs=pl.BlockSpec((tok.tm, d), row),
        out_shape=jax.ShapeDtypeStruct((n, d), F32),
        scratch_shapes=[pltpu.VMEM((d, d), BF16)],
        compiler_params=_cparams("arbitrary"),
        name="attn_proj_res",
    )(x, o_p, o_s, w_o, mod)


def _mlp_kernel(x_ref, g_ref, sc_ref, sh_ref, gate_ref, w1_ref, w2_ref, *rest, np_tiles):
    out_refs, (h_scr, acc_scr) = rest[:-2], rest[-2:]
    i = pl.program_id(0)
    j = pl.program_id(1)
    last = j == pl.num_programs(1) - 1

    @pl.when(j == 0)
    def _():
        h_scr[...] = _normmod(x_ref[...], g_ref[...], sc_ref[...], sh_ref[...]).astype(BF16)
        acc_scr[...] = jnp.zeros_like(acc_scr)

    a = jnp.maximum(jnp.dot(h_scr[...], w1_ref[...].astype(BF16), preferred_element_type=F32), 0.0)
    acc_scr[...] += jnp.dot((a * a).astype(BF16), w2_ref[...].astype(BF16), preferred_element_type=F32)

    def result():
        return x_ref[...] + gate_ref[...] * acc_scr[...]

    if len(out_refs) == 1:
        @pl.when(last)
        def _():
            out_refs[0][...] = result()
    else:
        @pl.when(jnp.logical_and(last, i < np_tiles))
        def _():
            out_refs[0][...] = result()

        @pl.when(jnp.logical_and(last, i >= np_tiles))
        def _():
            out_refs[1][...] = result()


def _mlp(x, g, mod, layer, w1, w2, tok, split_out):
    n, d = x.shape
    f = w1.shape[2]
    tf = _pow2_tile(512, f)
    row = lambda i, j: (i, 0)
    if split_out:
        n_p = tok.np_tiles * tok.tm
        out_specs = [tok.prompt_spec(d, 2), tok.sample_spec(d, 2)]
        out_shape = [jax.ShapeDtypeStruct((n_p, d), F32), jax.ShapeDtypeStruct((n - n_p, d), F32)]
    else:
        out_specs = pl.BlockSpec((tok.tm, d), row)
        out_shape = jax.ShapeDtypeStruct((n, d), F32)
    return pl.pallas_call(
        functools.partial(_mlp_kernel, np_tiles=tok.np_tiles),
        grid=(tok.tiles, f // tf),
        in_specs=[pl.BlockSpec((tok.tm, d), row),
                  pl.BlockSpec((1, d), lambda i, j: (0, 0)),
                  tok.mod_spec(layer, 4, d, 2),
                  tok.mod_spec(layer, 3, d, 2),
                  tok.mod_spec(layer, 5, d, 2),
                  pl.BlockSpec((None, d, tf), lambda i, j: (layer, 0, j)),
                  pl.BlockSpec((None, tf, d), lambda i, j: (layer, j, 0))],
        out_specs=out_specs,
        out_shape=out_shape,
        scratch_shapes=[pltpu.VMEM((tok.tm, d), BF16), pltpu.VMEM((tok.tm, d), F32)],
        compiler_params=_cparams("arbitrary", "arbitrary"),
        name="mlp",
    )(x, g, mod, mod, mod, w1, w2)


def _s5_tables(lam_re, lam_im, log_dt, b_re, b_im, c_re, c_im):
    t = S5_CHUNK
    g, p, k = b_re.shape[1:]
    dt = jnp.exp(log_dt.astype(F32))[:, :, None]
    lr, li = lam_re.astype(F32), lam_im.astype(F32)
    ar, ai = lr * dt, li * dt
    mag = jnp.exp(ar)
    abr, abi = mag * jnp.cos(ai), mag * jnp.sin(ai)
    nr, ni = abr - 1.0, abi
    den = lr * lr + li * li
    f_re = (nr * lr + ni * li) / den
    f_im = (ni * lr - nr * li) / den
    bbr = f_re[..., None] * b_re - f_im[..., None] * b_im
    bbi = f_re[..., None] * b_im + f_im[..., None] * b_re
    n = jnp.arange(t + 1, dtype=F32)[None, None, :, None]
    pm = jnp.exp(n * ar[:, :, None, :])
    pr, pi = pm * jnp.cos(n * ai[:, :, None, :]), pm * jnp.sin(n * ai[:, :, None, :])
    cr, ci = c_re.astype(F32), c_im.astype(F32)

    def summ(pw_r, pw_i, d):
        pw_r = pw_r.transpose(0, 2, 1)[:, :, :, None]
        pw_i = pw_i.transpose(0, 2, 1)[:, :, :, None]
        sr = pw_r * bbr[d][:, :, None, :] - pw_i * bbi[d][:, :, None, :]
        si = pw_r * bbi[d][:, :, None, :] + pw_i * bbr[d][:, :, None, :]
        return sr.reshape(g, p, t * k), si.reshape(g, p, t * k)

    sfr, sfi = summ(pr[0, :, :t][:, ::-1], pi[0, :, :t][:, ::-1], 0)
    sbr, sbi = summ(pr[1, :, :t], pi[1, :, :t], 1)
    w_intra, w_sum = _s5_toeplitz(jnp.concatenate([cr[0], -ci[0]], axis=-1),
                                  jnp.concatenate([cr[1], -ci[1]], axis=-1), sfr, sfi, sbr, sbi)
    car = cr[:, :, None] * pr[:, :, 1:, None, :] - ci[:, :, None] * pi[:, :, 1:, None, :]
    cai = cr[:, :, None] * pi[:, :, 1:, None, :] + ci[:, :, None] * pr[:, :, 1:, None, :]
    rd = lambda a: a.reshape(g, t * k, p)
    w2t = jnp.concatenate([rd(car[0]), rd(car[1][:, ::-1]), -rd(cai[0]), -rd(cai[1][:, ::-1])], axis=2)
    at = jnp.stack([jnp.concatenate([pr[0, :, t], pr[1, :, t]], axis=-1),
                    jnp.concatenate([pi[0, :, t], pi[1, :, t]], axis=-1)], axis=1)
    return w_intra, w_sum, w2t.astype(BF16), at


def _s5_toeplitz_kernel(cf_ref, cb_ref, sfr_ref, sfi_ref, sbr_ref, sbi_ref, o_ref, ws_ref):
    hi = lax.Precision.HIGHEST
    gl, k, _ = cf_ref.shape
    tk = sfr_ref.shape[2]
    for g in range(gl):
        abf = jnp.concatenate([sfr_ref[g], sfi_ref[g]], axis=0)
        abb = jnp.concatenate([sbr_ref[g], sbi_ref[g]], axis=0)
        ws_ref[g] = jnp.concatenate([sfr_ref[g], sbr_ref[g], sfi_ref[g], sbi_ref[g]], axis=0).astype(ws_ref.dtype)
        kf = jnp.dot(cf_ref[g], abf, precision=hi, preferred_element_type=F32)
        kb = jnp.dot(cb_ref[g], abb, precision=hi, preferred_element_type=F32)
        z = jnp.zeros_like(kf)
        krev = jnp.concatenate([kf, z], axis=1) + pltpu.roll(jnp.concatenate([z, kb], axis=1), 2 * tk - k, 1)
        for i in range(S5_CHUNK):
            sh = (S5_CHUNK - 1 - i) * k
            win = krev if sh == 0 else pltpu.roll(krev, 2 * tk - sh, 1)
            o_ref[g, i * k:(i + 1) * k, :] = win[:, 0:tk].astype(o_ref.dtype)


def _s5_toeplitz(cf, cb, sfr, sfi, sbr, sbi):
    g, k, p2 = cf.shape
    p, tk = sfr.shape[1:]
    gl = _pow2_tile(8, g)
    cspec = pl.BlockSpec((gl, k, p2), lambda a: (a, 0, 0))
    sspec = pl.BlockSpec((gl, p, tk), lambda a: (a, 0, 0))
    return pl.pallas_call(
        _s5_toeplitz_kernel,
        grid=(g // gl,),
        in_specs=[cspec, cspec, sspec, sspec, sspec, sspec],
        out_specs=[pl.BlockSpec((gl, tk, tk), lambda a: (a, 0, 0)), pl.BlockSpec((gl, 4 * p, tk), lambda a: (a, 0, 0))],
        out_shape=[jax.ShapeDtypeStruct((g, tk, tk), BF16), jax.ShapeDtypeStruct((g, 4 * p, tk), BF16)],
        compiler_params=_cparams("arbitrary"),
        name="s5_toeplitz",
    )(cf, cb, sfr, sfi, sbr, sbi)


def _s5_mixer(x, g, mod, layer, tok, params, h0, dims):
    (lam_re, lam_im, log_dt, b_re, b_im, c_re, c_im, d_skip, glu_w, slot, glu_b) = params
    batch, seq, dec_batch, dec_seq = dims
    n_prompt = batch * seq
    d = d_skip.shape[0]
    ngrp = d // S5_GROUP
    p = lam_re.shape[-1]
    w_intra, w_sum, w2t, at = _s5_tables(lam_re, lam_im, log_dt, b_re, b_im, c_re, c_im)
    hperm = _s5_pre(x, g, mod, layer, tok)
    h0g = h0.astype(F32).transpose(3, 2, 0, 1, 4).reshape(ngrp, 2, dec_batch, 2 * p)
    streams = ((0, seq // S5_CHUNK, batch, False, True),
               (n_prompt // S5_CHUNK, dec_seq // S5_CHUNK, dec_batch, True, False))
    yperm, fin = _s5_scan(hperm, w_intra, w_sum, w2t, at, h0g, batch, streams)
    x_new = _s5_post(x, yperm, g, mod, layer, d_skip.reshape(1, d), glu_w, slot, glu_b, tok)
    st = fin.reshape(ngrp, 2, batch, 2, p).transpose(2, 3, 1, 0, 4)
    return x_new, st


def kernel(x_prompt, x_sample, state_s5, cache_na_k, cache_na_v, cache_gqa_k, cache_gqa_v, c, c_ctx, norm_g, ada_w, ada_b, mlp_w1, mlp_w2, s5_lam_re, s5_lam_im, s5_log_dt, s5_b_re, s5_b_im, s5_c_re, s5_c_im, s5_d, s5_glu_w, s5_glu_b, na_w_qkv, na_q_norm, na_k_norm, na_rpb, na_w_o, gqa_w_qkv, gqa_q_norm, gqa_k_norm, gqa_w_o):
    batch, seq, d = x_prompt.shape
    dec_batch, dec_seq, _ = x_sample.shape
    depth = ada_w.shape[0]
    n_prompt = batch * seq
    n_sample = dec_batch * dec_seq
    na_heads, na_hd = cache_na_k.shape[3], cache_na_k.shape[4]
    gqa_kv, gqa_hd = cache_gqa_k.shape[3], cache_gqa_k.shape[4]
    assert n_prompt % dec_seq == 0

    tok = _Tok(n_prompt, n_sample, dec_seq, 1024)
    tok_half = _Tok(n_prompt, n_sample, dec_seq, 512)

    mod_rows = -(-(1 + dec_batch) // SUBLANES) * SUBLANES
    cvec = jnp.concatenate([c_ctx[None, :], c, jnp.zeros((mod_rows - 1 - dec_batch, d), F32)], axis=0)
    mod = _modulation(cvec, ada_w, ada_b).reshape(depth, mod_rows, 6, 1, d)

    x = (x_prompt.reshape(n_prompt, d), x_sample.reshape(n_sample, d))
    cache_na_k2 = cache_na_k.reshape(cache_na_k.shape[:3] + (na_heads * na_hd,))
    cache_na_v2 = cache_na_v.reshape(cache_na_v.shape[:3] + (na_heads * na_hd,))
    cache_gqa_k2 = cache_gqa_k.reshape(cache_gqa_k.shape[:3] + (gqa_kv * gqa_hd,))
    cache_gqa_v2 = cache_gqa_v.reshape(cache_gqa_v.shape[:3] + (gqa_kv * gqa_hd,))
    cos_t, sin_t = _rope_tables(dec_seq, gqa_hd)

    new_s5, new_na_k, new_na_v, new_gqa_k, new_gqa_v = [], [], [], [], []
    for i in range(depth):
        kind, slot = i % 3, i // 3
        g1 = norm_g[i, 0].reshape(1, d)
        g2 = norm_g[i, 1].reshape(1, d)
        if kind == 0:
            params = (s5_lam_re[slot], s5_lam_im[slot], s5_log_dt[slot], s5_b_re[slot], s5_b_im[slot],
                      s5_c_re[slot], s5_c_im[slot], s5_d[slot], s5_glu_w, slot, s5_glu_b[slot])
            x, st = _s5_mixer(x, g1, mod, i, tok_half, params, state_s5[:, slot], (batch, seq, dec_batch, dec_seq))
            new_s5.append(st)
        elif kind == 1:
            assert not isinstance(x, tuple)
            q, k_p, k_s, v_p, v_s = _na_qkv(x, g1, mod, i, na_w_qkv, slot, na_q_norm[slot], na_k_norm[slot], tok_half)
            o_p = _na_ctx_attn(q, k_p, v_p, batch, seq, na_hd)
            bias = _na_bias_table(na_rpb[slot], dec_seq)
            o_s = _na_lat_attn(q, k_s, v_s, cache_na_k2, cache_na_v2, slot, bias, n_prompt, dec_batch, dec_seq, na_hd)
            x = _proj_res(x, o_p, o_s, na_w_o, slot, mod, i, tok)
            new_na_k.append(k_p.reshape(batch, seq, na_heads, na_hd))
            new_na_v.append(v_p.reshape(batch, seq, na_heads, na_hd))
        else:
            assert not isinstance(x, tuple)
            q, k_p, k_s, v_p, v_s = _gqa_qkv(x, g1, mod, i, gqa_w_qkv, slot, gqa_q_norm[slot], gqa_k_norm[slot],
                                             cos_t, sin_t, gqa_kv, tok_half)
            o_p = _gqa_ctx_attn(q, k_p, v_p, batch, seq, gqa_kv)
            o_s = _gqa_lat_attn(q, k_s, v_s, cache_gqa_k2, cache_gqa_v2, slot, n_prompt, dec_batch, dec_seq, gqa_kv)
            x = _proj_res(x, o_p, o_s, gqa_w_o, slot, mod, i, tok)
            new_gqa_k.append(k_p.reshape(batch, seq, gqa_kv, gqa_hd))
            new_gqa_v.append(v_p.reshape(batch, seq, gqa_kv, gqa_hd))
        x = _mlp(x, g2, mod, i, mlp_w1, mlp_w2, tok, split_out=(i == depth - 1))
    y_p, y_s = x
    return (y_p.reshape(batch, seq, d), y_s.reshape(dec_batch, dec_seq, d),
            jnp.stack(new_s5, axis=1), jnp.stack(new_na_k, axis=1), jnp.stack(new_na_v, axis=1),
            jnp.stack(new_gqa_k, axis=1), jnp.stack(new_gqa_v, axis=1))
```

```python
import functools
import math

import jax
import jax.numpy as jnp
from jax import lax
from jax.experimental import pallas as pl
from jax.experimental.pallas import tpu as pltpu

F32 = jnp.float32
BF16 = jnp.bfloat16

EPS = 1e-6
GRID_W = 64
S5_GROUP = 16
NA_KH = 8
NA_KW = 16
ROPE_THETA = 10000.0
S5_CHUNK = 16
LOG2E = math.log2(math.e)
LANES = 128
SUBLANES = 8
VMEM_LIMIT = 56 * 1024 * 1024


def _cparams(*sem):
    return pltpu.CompilerParams(dimension_semantics=sem, vmem_limit_bytes=VMEM_LIMIT)


def _pow2_tile(pref, *ns):
    t = pref
    while any(n % t for n in ns):
        t //= 2
    return t


def _row_halves(n):
    if n % 2 or n // 2 < LANES:
        return [slice(0, n)]
    return [slice(0, n // 2), slice(n // 2, n)]


def _software_pipeline(parts, before, matmuls, after):
    prev = None
    for p in parts:
        z = matmuls(before(p))
        if prev is not None:
            after(*prev)
        prev = (p, z)
    after(*prev)


def _normmod(x, g, sc, sh):
    ms = jnp.mean(x * x, axis=-1, keepdims=True)
    y = x * lax.rsqrt(ms + EPS) * g
    return y * (1.0 + sc) + sh


class _Tok:
    def __init__(self, n_prompt, n_sample, dec_seq, pref):
        self.tm = _pow2_tile(pref, n_prompt, dec_seq)
        self.n = n_prompt + n_sample
        self.tiles = self.n // self.tm
        self.np_tiles = n_prompt // self.tm
        self.tps = dec_seq // self.tm

    def mod_row(self, i):
        return jnp.where(i < self.np_tiles, 0, 1 + (i - self.np_tiles) // self.tps)

    def prompt_spec(self, d, nargs):
        last = self.np_tiles - 1
        if nargs == 1:
            return pl.BlockSpec((self.tm, d), lambda i: (jnp.minimum(i, last), 0))
        return pl.BlockSpec((self.tm, d), lambda i, j: (jnp.minimum(i, last), 0))

    def sample_spec(self, d, nargs):
        first = self.np_tiles
        if nargs == 1:
            return pl.BlockSpec((self.tm, d), lambda i: (jnp.maximum(i - first, 0), 0))
        return pl.BlockSpec((self.tm, d), lambda i, j: (jnp.maximum(i - first, 0), 0))

    def mod_spec(self, layer, which, d, nargs):
        if nargs == 1:
            return pl.BlockSpec((None, None, None, 1, d), lambda i: (layer, self.mod_row(i), which, 0, 0))
        return pl.BlockSpec((None, None, None, 1, d), lambda i, j: (layer, self.mod_row(i), which, 0, 0))


def _mod_kernel(c_ref, w_ref, b_ref, o_ref):
    c = c_ref[...]
    s = c * jax.nn.sigmoid(c)
    o_ref[...] = jnp.dot(s.astype(BF16), w_ref[...].astype(BF16), preferred_element_type=F32) + b_ref[...]


def _modulation(cvec, ada_w, ada_b):
    depth, d, d6 = ada_w.shape
    rows = cvec.shape[0]
    tn = _pow2_tile(2048, d6) if d6 % 2048 == 0 else d
    return pl.pallas_call(
        _mod_kernel,
        grid=(depth, d6 // tn),
        in_specs=[pl.BlockSpec((rows, d), lambda l, j: (0, 0)),
                  pl.BlockSpec((None, d, tn), lambda l, j: (l, 0, j)),
                  pl.BlockSpec((None, 1, tn), lambda l, j: (l, 0, j))],
        out_specs=pl.BlockSpec((None, rows, tn), lambda l, j: (l, 0, j)),
        out_shape=jax.ShapeDtypeStruct((depth, rows, d6), F32),
        compiler_params=_cparams("arbitrary", "arbitrary"),
        name="adaln_modulation",
    )(cvec, ada_w, ada_b.reshape(depth, 1, d6))


def _read_x(x_refs, np_tiles, rows=slice(None)):
    if len(x_refs) == 1:
        return x_refs[0][rows, :]
    return jnp.where(pl.program_id(0) < np_tiles, x_refs[0][rows, :], x_refs[1][rows, :])


def _x_specs(x, tok, nargs):
    if isinstance(x, tuple):
        d = x[0].shape[1]
        return [tok.prompt_spec(d, nargs), tok.sample_spec(d, nargs)], x
    d = x.shape[1]
    return [pl.BlockSpec((tok.tm, d), (lambda i: (i, 0)) if nargs == 1 else (lambda i, j: (i, 0)))], (x,)


def _s5_pre_kernel(*refs, n_x, np_tiles):
    x_refs, (g_ref, sc_ref, sh_ref, h_ref, h_scr) = refs[:n_x], refs[n_x:]
    h = _normmod(_read_x(x_refs, np_tiles), g_ref[...], sc_ref[...], sh_ref[...])
    nc = h_ref.shape[1]
    for a in range(h_scr.shape[0]):
        sl = slice(a * LANES, (a + 1) * LANES)
        h_scr[a] = h[:, sl]
        for j in range(S5_CHUNK):
            h_ref[j, :, sl] = h_scr[a, pl.ds(j, nc, stride=S5_CHUNK), :]


def _s5_pre(x, g, mod, layer, tok):
    x_specs, xs = _x_specs(x, tok, 1)
    d = xs[0].shape[1]
    nc = tok.tm // S5_CHUNK
    return pl.pallas_call(
        functools.partial(_s5_pre_kernel, n_x=len(xs), np_tiles=tok.np_tiles),
        grid=(tok.tiles,),
        in_specs=x_specs + [pl.BlockSpec((1, d), lambda i: (0, 0)),
                            tok.mod_spec(layer, 1, d, 1),
                            tok.mod_spec(layer, 0, d, 1)],
        out_specs=pl.BlockSpec((S5_CHUNK, nc, d), lambda i: (0, i, 0)),
        out_shape=jax.ShapeDtypeStruct((S5_CHUNK, tok.n // S5_CHUNK, d), F32),
        scratch_shapes=[pltpu.VMEM((d // LANES, tok.tm, LANES), F32)],
        compiler_params=_cparams("arbitrary"),
        name="s5_pre",
    )(*xs, g, mod, mod)


def _s5_scan_kernel(h_ref, wi_ref, ws_ref, w2t_ref, at_ref, h0_ref, *rest, streams, unroll):
    ns = len(streams)
    perm_refs, (y_ref, fin_ref, xt_scr, yt_scr), chain_scr = rest[:ns], rest[ns:ns + 4], rest[ns + 4:]
    ngrp = wi_ref.shape[0]
    p2 = at_ref.shape[-1]
    fwd = lax.broadcasted_iota(jnp.int32, (1, p2), 1) < (p2 // 2)
    for j in range(S5_CHUNK):
        xt_scr[j] = h_ref[j].T.astype(BF16)

    def load_rmat(r, stream):
        row0, n_chunks, nb = stream[:3]
        cols = slice(row0, row0 + n_chunks * nb)
        c0 = pl.multiple_of(r * S5_GROUP, S5_GROUP)
        return jnp.concatenate([xt_scr[j, pl.ds(c0, S5_GROUP), cols] for j in range(S5_CHUNK)], axis=0), c0, cols

    def summaries(r, stream, perm_ref, s_scr):
        nb, use_h0 = stream[2], stream[3]
        rmat, _, _ = load_rmat(r, stream)
        rmat_cm = jnp.dot(rmat, perm_ref[...], preferred_element_type=F32).astype(BF16)
        zs = jnp.dot(ws_ref[r], rmat_cm, preferred_element_type=F32)
        s_scr[0] = zs[0:p2].T
        s_scr[1] = zs[p2:].T
        if use_h0:
            return h0_ref[r, 0], h0_ref[r, 1]
        return jnp.zeros((nb, p2), F32), jnp.zeros((nb, p2), F32)

    def scan_step(t, decay, stream, state, s_scr, e_scr):
        n_chunks, nb = stream[1], stream[2]
        hr, hi = state
        ar, ai = decay
        ft = slice(t * nb, (t + 1) * nb)
        bt = slice((n_chunks - 1 - t) * nb, (n_chunks - t) * nb)
        e_scr[0, ft, :] = hr
        e_scr[1, bt, :] = hr
        e_scr[2, ft, :] = hi
        e_scr[3, bt, :] = hi
        in_re = jnp.where(fwd, s_scr[0, ft, :], s_scr[0, bt, :])
        in_im = jnp.where(fwd, s_scr[1, ft, :], s_scr[1, bt, :])
        return ar * hr - ai * hi + in_re, ar * hi + ai * hr + in_im

    def outputs(r, stream, perm_ref, state, e_scr):
        rmat, c0, cols = load_rmat(r, stream)
        if stream[4]:
            fin_ref[r, 0] = state[0]
            fin_ref[r, 1] = state[1]
        e_cm = jnp.concatenate([jnp.where(fwd, e_scr[0], e_scr[1]), jnp.where(fwd, e_scr[2], e_scr[3])],
                               axis=1).astype(BF16)
        e = jnp.dot(perm_ref[...], e_cm, preferred_element_type=F32).astype(BF16)
        yt = (jnp.dot(wi_ref[r], rmat, preferred_element_type=F32)
              + lax.dot_general(w2t_ref[r], e, (((1,), (1,)), ((), ())), preferred_element_type=F32))
        for i in range(S5_CHUNK):
            yt_scr[i, pl.ds(c0, S5_GROUP), cols] = yt[i * S5_GROUP:(i + 1) * S5_GROUP]

    def body(rb, carry):
        chains = []
        for u in range(unroll):
            for si, stream in enumerate(streams):
                k = 2 * (u * ns + si)
                chains.append((rb * unroll + u, stream, perm_refs[si], chain_scr[k], chain_scr[k + 1]))
        states = [summaries(r, stream, pm, s_scr) for r, stream, pm, s_scr, _ in chains]
        decays = [(at_ref[r, 0:1, :], at_ref[r, 1:2, :]) for r, _, _, _, _ in chains]
        for t in range(max(stream[1] for stream in streams)):
            for ci, (_, stream, _, s_scr, e_scr) in enumerate(chains):
                if t < stream[1]:
                    states[ci] = scan_step(t, decays[ci], stream, states[ci], s_scr, e_scr)
        for (r, stream, pm, _, e_scr), state in zip(chains, states):
            outputs(r, stream, pm, state, e_scr)
        return carry

    lax.fori_loop(0, ngrp // unroll, body, 0)
    for i in range(S5_CHUNK):
        y_ref[i] = yt_scr[i].T


def _s5_scan(hperm, w_intra, w_sum, w2t, at, h0, nb_fin, streams):
    t, nrows, d = hperm.shape
    g, tk, p4 = w2t.shape
    p2 = at.shape[-1]
    gl = LANES // S5_GROUP
    nb0 = h0.shape[2]
    unroll = 4
    assert sum(s[1] * s[2] for s in streams) == nrows and gl % unroll == 0
    chain_scr = []
    for _ in range(unroll):
        for s in streams:
            chain_scr += [pltpu.VMEM((2, s[1] * s[2], p2), F32), pltpu.VMEM((4, s[1] * s[2], p2), F32)]
    perms = []
    for _, n_chunks, nb, _, _ in streams:
        m = jnp.arange(n_chunks * nb)
        perms.append((m[:, None] == ((m % nb) * n_chunks + m // nb)[None, :]).astype(BF16))
    return pl.pallas_call(
        functools.partial(_s5_scan_kernel, streams=streams, unroll=unroll),
        grid=(d // LANES,),
        in_specs=[pl.BlockSpec((t, nrows, LANES), lambda a: (0, 0, a)),
                  pl.BlockSpec((gl, tk, tk), lambda a: (a, 0, 0)),
                  pl.BlockSpec((gl, p4, tk), lambda a: (a, 0, 0)),
                  pl.BlockSpec((gl, tk, p4), lambda a: (a, 0, 0)),
                  pl.BlockSpec((gl, 2, p2), lambda a: (a, 0, 0)),
                  pl.BlockSpec((gl, 2, nb0, p2), lambda a: (a, 0, 0, 0))]
                 + [pl.BlockSpec(pm.shape, lambda a: (0, 0)) for pm in perms],
        out_specs=[pl.BlockSpec((t, nrows, LANES), lambda a: (0, 0, a)),
                   pl.BlockSpec((gl, 2, nb_fin, p2), lambda a: (a, 0, 0, 0))],
        out_shape=[jax.ShapeDtypeStruct((t, nrows, d), F32),
                   jax.ShapeDtypeStruct((g, 2, nb_fin, p2), F32)],
        scratch_shapes=[pltpu.VMEM((t, LANES, nrows), BF16), pltpu.VMEM((t, LANES, nrows), F32)] + chain_scr,
        compiler_params=_cparams("arbitrary"),
        name="s5_scan",
    )(hperm, w_intra, w_sum, w2t, at, h0, *perms)


def _s5_post_kernel(*refs, n_x, np_tiles):
    x_refs = refs[:n_x]
    (y_ref, g_ref, sc_ref, sh_ref, gate_ref, dsk_ref, wa_ref, wg_ref, ba_ref, bg_ref, o_ref,
     y_scr, wa_scr, wg_scr) = refs[n_x:]
    _cast_once(wa_ref, wa_scr)
    _cast_once(wg_ref, wg_scr)
    def before(rows):
        c0, nc = rows.start // S5_CHUNK, (rows.stop - rows.start) // S5_CHUNK
        for a in range(y_scr.shape[0]):
            for j in range(S5_CHUNK):
                y_scr[a, pl.ds(rows.start + j, nc, stride=S5_CHUNK), :] = y_ref[j, c0:c0 + nc, a * LANES:(a + 1) * LANES]
        x = _read_x(x_refs, np_tiles, rows)
        h = _normmod(x, g_ref[...], sc_ref[...], sh_ref[...])
        y = h * dsk_ref[...] + jnp.concatenate([y_scr[a, rows, :] for a in range(y_scr.shape[0])], axis=1)
        return jax.nn.gelu(y).astype(BF16)

    def matmuls(a):
        return (jnp.dot(a, wa_scr[...], preferred_element_type=F32), jnp.dot(a, wg_scr[...], preferred_element_type=F32))

    def after(rows, z):
        za, zg = z[0] + ba_ref[...], z[1] + bg_ref[...]
        o_ref[rows, :] = _read_x(x_refs, np_tiles, rows) + gate_ref[...] * (za * jax.nn.sigmoid(zg))

    _software_pipeline(_row_halves(o_ref.shape[0]), before, matmuls, after)


def _s5_post(x, y, g, mod, layer, dsk, glu_w, slot, glu_b, tok):
    x_specs, xs = _x_specs(x, tok, 1)
    n, d = tok.n, xs[0].shape[1]
    row = lambda i: (i, 0)
    fixed = lambda i: (0, 0)
    gb2 = glu_b.reshape(1, 2 * d)
    return pl.pallas_call(
        functools.partial(_s5_post_kernel, n_x=len(xs), np_tiles=tok.np_tiles),
        grid=(tok.tiles,),
        in_specs=x_specs + [
                  pl.BlockSpec((S5_CHUNK, tok.tm // S5_CHUNK, d), lambda i: (0, i, 0)),
                  pl.BlockSpec((1, d), fixed),
                  tok.mod_spec(layer, 1, d, 1),
                  tok.mod_spec(layer, 0, d, 1),
                  tok.mod_spec(layer, 2, d, 1),
                  pl.BlockSpec((1, d), fixed),
                  _resident((None, d, d), lambda i: (slot, 0, 0)),
                  _resident((None, d, d), lambda i: (slot, 0, 1)),
                  pl.BlockSpec((1, d), lambda i: (0, 0)),
                  pl.BlockSpec((1, d), lambda i: (0, 1))],
        out_specs=pl.BlockSpec((tok.tm, d), row),
        out_shape=jax.ShapeDtypeStruct((n, d), F32),
        scratch_shapes=[pltpu.VMEM((d // LANES, tok.tm, LANES), F32),
                        pltpu.VMEM((d, d), BF16), pltpu.VMEM((d, d), BF16)],
        compiler_params=_cparams("arbitrary"),
        name="s5_post",
    )(*xs, y, g, mod, mod, mod, dsk, glu_w, glu_w, gb2, gb2)


def _cast_once(w_ref, w_scr):
    @pl.when(pl.program_id(0) == 0)
    def _():
        w_scr[...] = w_ref[...].astype(w_scr.dtype)


def _resident(block_shape, index_map):
    return pl.BlockSpec(block_shape, index_map, pipeline_mode=pl.Buffered(1))


def _na_qkv_kernel(x_ref, g_ref, sc_ref, sh_ref, w_ref, qn_ref, kn_ref, q_ref, kp_ref, ks_ref, vp_ref, vs_ref, w_scr, *,
                   hd, np_tiles):
    i = pl.program_id(0)
    _cast_once(w_ref, w_scr)
    h = _normmod(x_ref[...], g_ref[...], sc_ref[...], sh_ref[...]).astype(BF16)
    d = h.shape[1]
    lo = lax.broadcasted_iota(jnp.int32, (1, LANES), 1) < hd

    def head_norm(z, gain):
        outs = []
        for s in range(d // LANES):
            seg = z[:, s * LANES:(s + 1) * LANES]
            sq = seg * seg
            s_lo = jnp.sum(jnp.where(lo, sq, 0.0), axis=-1, keepdims=True)
            s_hi = jnp.sum(jnp.where(lo, 0.0, sq), axis=-1, keepdims=True)
            ms = jnp.where(lo, s_lo, s_hi) / hd
            outs.append(seg * lax.rsqrt(ms + EPS) * gain)
        return jnp.concatenate(outs, axis=1)

    proj = lambda part: jnp.dot(h, w_scr[:, part * d:(part + 1) * d], preferred_element_type=F32)
    q_ref[...] = (head_norm(proj(0), qn_ref[...]) * (hd ** -0.5 * LOG2E)).astype(q_ref.dtype)
    k = head_norm(proj(1), kn_ref[...])
    v = proj(2)

    @pl.when(i < np_tiles)
    def _():
        kp_ref[...] = k
        vp_ref[...] = v

    @pl.when(i >= np_tiles)
    def _():
        ks_ref[...] = k.astype(ks_ref.dtype)
        vs_ref[...] = v.astype(vs_ref.dtype)


def _na_qkv(x, g, mod, layer, w_qkv, slot, qn, kn, tok):
    n, d = x.shape
    hd = qn.shape[-1]
    assert 2 * hd == LANES
    row = lambda i: (i, 0)
    fixed = lambda i: (0, 0)
    qn2 = jnp.tile(qn, 2).reshape(1, LANES)
    kn2 = jnp.tile(kn, 2).reshape(1, LANES)
    n_p, n_s = tok.np_tiles * tok.tm, n - tok.np_tiles * tok.tm
    return pl.pallas_call(
        functools.partial(_na_qkv_kernel, hd=hd, np_tiles=tok.np_tiles),
        grid=(tok.tiles,),
        in_specs=[pl.BlockSpec((tok.tm, d), row),
                  pl.BlockSpec((1, d), fixed),
                  tok.mod_spec(layer, 1, d, 1),
                  tok.mod_spec(layer, 0, d, 1),
                  _resident((None, d, 3 * d), lambda i: (slot, 0, 0)),
                  pl.BlockSpec((1, LANES), fixed),
                  pl.BlockSpec((1, LANES), fixed)],
        out_specs=[pl.BlockSpec((tok.tm, d), row),
                   tok.prompt_spec(d, 1), tok.sample_spec(d, 1), tok.prompt_spec(d, 1), tok.sample_spec(d, 1)],
        out_shape=[jax.ShapeDtypeStruct((n, d), BF16),
                   jax.ShapeDtypeStruct((n_p, d), F32), jax.ShapeDtypeStruct((n_s, d), BF16),
                   jax.ShapeDtypeStruct((n_p, d), F32), jax.ShapeDtypeStruct((n_s, d), BF16)],
        scratch_shapes=[pltpu.VMEM((d, 3 * d), BF16)],
        compiler_params=_cparams("arbitrary"),
        name="na_qkv",
    )(x, g, mod, mod, w_qkv, qn2, kn2)


def _softmax2_pv(s_parts, v_parts):
    m = s_parts[0].max(axis=-1, keepdims=True)
    for s in s_parts[1:]:
        m = jnp.maximum(m, s.max(axis=-1, keepdims=True))
    den = 0.0
    acc = 0.0
    for s, v in zip(s_parts, v_parts):
        e = jnp.exp2(s - m)
        den = den + e.sum(axis=-1, keepdims=True)
        acc = acc + jnp.dot(e.astype(BF16), v, preferred_element_type=F32)
    return acc / den


def _qk(q, k):
    return lax.dot_general(q, k, (((1,), (1,)), ((), ())), preferred_element_type=F32)


def _na_ctx_attn_kernel(q_ref, k_ref, v_ref, o_ref, *, hd):
    lo = lax.broadcasted_iota(jnp.int32, (1, LANES), 1) < hd
    for s in range(q_ref.shape[1] // LANES):
        sl = slice(s * LANES, (s + 1) * LANES)
        q = q_ref[:, sl]
        k = k_ref[:, sl].astype(BF16)
        v = v_ref[:, sl].astype(BF16)
        zero = jnp.zeros_like(q)
        o_lo = _softmax2_pv([_qk(jnp.where(lo, q, zero), k)], [v])
        o_hi = _softmax2_pv([_qk(jnp.where(lo, zero, q), k)], [v])
        o_ref[:, sl] = jnp.where(lo, o_lo, o_hi).astype(o_ref.dtype)


def _na_ctx_attn(q, k, v, batch, seq, hd):
    d = q.shape[1]
    spec = pl.BlockSpec((seq, d), lambda b: (b, 0))
    return pl.pallas_call(
        functools.partial(_na_ctx_attn_kernel, hd=hd),
        grid=(batch,),
        in_specs=[spec, spec, spec],
        out_specs=spec,
        out_shape=jax.ShapeDtypeStruct((batch * seq, d), BF16),
        compiler_params=_cparams("arbitrary"),
        name="na_ctx_attn",
    )(q, k, v)


def _na_window(r0, q_rows, rows, kh):
    w_rows = min(rows, kh + q_rows + (kh + q_rows) % 2)
    rs = min(max(r0 - kh // 2, 0), rows - kh)
    return min(rs // 2 * 2, rows - w_rows), w_rows


def _na_lat_attn_kernel(q_ref, kl_ref, vl_ref, kc_ref, vc_ref, tab_ref, o_ref, kc_scr, vc_scr, bias_scr, *,
                        hd, rows, kh, tq):
    lane = lax.broadcasted_iota(jnp.int32, (1, LANES), 1)
    lo = lane < hd
    lower = lane < GRID_W
    q_rows = tq // GRID_W
    n_qb = q_ref.shape[0] // tq
    w_cols = bias_scr.shape[2]
    kc_scr[...] = kc_ref[...].astype(BF16)
    vc_scr[...] = vc_ref[...].astype(BF16)

    @pl.when(pl.program_id(1) == 0)
    def _():
        neg = jnp.full((GRID_W, LANES), -jnp.inf, F32)
        for hh in range(2):
            for rq in range(rows):
                w0, w_rows = _na_window(rq // q_rows * q_rows, q_rows, rows, kh)
                rs = min(max(rq - kh // 2, 0), rows - kh)
                for kp in range(w_rows // 2):
                    rk = w0 + 2 * kp
                    ok0, ok1 = rs <= rk < rs + kh, rs <= rk + 1 < rs + kh
                    tile = tab_ref[hh, rk - rq + rows - 1]
                    if not (ok0 and ok1):
                        tile = jnp.where(lower if ok0 else jnp.logical_not(lower), tile, neg) if (ok0 or ok1) else neg
                    bias_scr[hh, rq * GRID_W:(rq + 1) * GRID_W, kp * LANES:(kp + 1) * LANES] = tile

    def qblock(qb, carry):
        r0 = qb * q_rows
        w0 = jnp.minimum(jnp.clip(r0 - kh // 2, 0, rows - kh) // 2 * 2, rows - w_cols // GRID_W)
        k0 = pl.multiple_of(w0 * GRID_W, 2 * GRID_W)
        q0 = pl.multiple_of(qb * tq, tq)
        q = q_ref[pl.ds(q0, tq), :]
        kl = kl_ref[pl.ds(k0, w_cols), :]
        vl = vl_ref[pl.ds(k0, w_cols), :]
        zero = jnp.zeros_like(q)
        scores = []
        for hh in range(2):
            qm = jnp.where(lo, q, zero) if hh == 0 else jnp.where(lo, zero, q)
            scores.append([_qk(qm, kl) + bias_scr[hh, pl.ds(q0, tq), :], _qk(qm, kc_scr[...])])
        outs = [_softmax2_pv(s, [vl, vc_scr[...]]) for s in scores]
        o_ref[pl.ds(q0, tq), :] = jnp.where(lo, outs[0], outs[1]).astype(o_ref.dtype)
        return carry

    lax.fori_loop(0, n_qb, qblock, 0)


def _na_lat_attn(q, k, v, cache_k, cache_v, slot, tab, n_prompt, dec_batch, dec_seq, hd):
    d = q.shape[1]
    past = cache_k.shape[2]
    rows = dec_seq // GRID_W
    kh = min(NA_KH, rows)
    tq = _pow2_tile(256, dec_seq)
    assert tq % GRID_W == 0 and rows % 2 == 0 and 2 * GRID_W == LANES and n_prompt % dec_seq == 0
    b0 = n_prompt // dec_seq
    w_rows = _na_window(0, tq // GRID_W, rows, kh)[1]
    return pl.pallas_call(
        functools.partial(_na_lat_attn_kernel, hd=hd, rows=rows, kh=kh, tq=tq),
        grid=(d // LANES, dec_batch),
        in_specs=[pl.BlockSpec((dec_seq, LANES), lambda hp, b: (b0 + b, hp)),
                  pl.BlockSpec((dec_seq, LANES), lambda hp, b: (b, hp)),
                  pl.BlockSpec((dec_seq, LANES), lambda hp, b: (b, hp)),
                  pl.BlockSpec((None, None, past, LANES), lambda hp, b: (b, slot, 0, hp)),
                  pl.BlockSpec((None, None, past, LANES), lambda hp, b: (b, slot, 0, hp)),
                  pl.BlockSpec((2,) + tab.shape[1:], lambda hp, b: (hp, 0, 0, 0))],
        out_specs=pl.BlockSpec((dec_seq, LANES), lambda hp, b: (b, hp)),
        out_shape=jax.ShapeDtypeStruct((dec_batch * dec_seq, d), BF16),
        scratch_shapes=[pltpu.VMEM((past, LANES), BF16), pltpu.VMEM((past, LANES), BF16),
                        pltpu.VMEM((2, dec_seq, w_rows * GRID_W), F32)],
        compiler_params=_cparams("arbitrary", "arbitrary"),
        name="na_lat_attn",
    )(q, k, v, cache_k, cache_v, tab)


def _na_bias_table(rpb, dec_seq):
    rows = dec_seq // GRID_W
    kh = min(NA_KH, rows)
    nh, nd, nc = rpb.shape
    c = jnp.arange(GRID_W)
    cs = jnp.clip(c - NA_KW // 2, 0, GRID_W - NA_KW)
    col_ok = (c[None, :] >= cs[:, None]) & (c[None, :] < cs[:, None] + NA_KW)
    dcol = c[None, :] - c[:, None] + NA_KW - 1
    onehot = (dcol[None] == jnp.arange(nc)[:, None, None]).astype(F32)
    tmp = jnp.einsum('hdc,cxy->hdxy', rpb.astype(F32), onehot, precision=lax.Precision.HIGHEST)
    tmp = jnp.where(col_ok[None, None], tmp * LOG2E, -jnp.inf)
    n_e = 2 * rows - 2
    front = rows - kh
    back = n_e + 1 - nd - front
    pad = lambda n: jnp.full((nh, n, GRID_W, GRID_W), -jnp.inf, F32)
    ext = jnp.concatenate([pad(front), tmp, pad(back)], axis=1)
    return jnp.concatenate([ext[:, :-1], ext[:, 1:]], axis=-1)


def _gqa_qkv_kernel(x_ref, g_ref, sc_ref, sh_ref, w_ref, qn_ref, kn_ref, cos_ref, sin_ref,
                    q_ref, kp_ref, ks_ref, vp_ref, vs_ref, w_scr, *, nq, nk, np_tiles):
    i = pl.program_id(0)
    _cast_once(w_ref, w_scr)
    is_sample = i >= np_tiles
    ks, vs = [], []
    for rows in _row_halves(x_ref.shape[0]):
        h = _normmod(x_ref[rows, :], g_ref[...], sc_ref[...], sh_ref[...]).astype(BF16)
        z = jnp.dot(h, w_scr[...], preferred_element_type=F32)
        cos = cos_ref[rows, :]
        sin = sin_ref[rows, :]

        def norm_rope(seg, gain):
            ms = jnp.mean(seg * seg, axis=-1, keepdims=True)
            y = seg * lax.rsqrt(ms + EPS) * gain
            roped = y * cos + pltpu.roll(y, LANES // 2, 1) * sin
            return jnp.where(is_sample, roped, y)

        for hh in range(nq):
            sl = slice(hh * LANES, (hh + 1) * LANES)
            q_ref[rows, sl] = (norm_rope(z[:, sl], qn_ref[...]) * (LANES ** -0.5 * LOG2E)).astype(q_ref.dtype)
        ks.append(jnp.concatenate(
            [norm_rope(z[:, (nq + hh) * LANES:(nq + hh + 1) * LANES], kn_ref[...]) for hh in range(nk)], axis=1))
        vs.append(z[:, (nq + nk) * LANES:])
    k = jnp.concatenate(ks, axis=0)
    v = jnp.concatenate(vs, axis=0)

    @pl.when(i < np_tiles)
    def _():
        kp_ref[...] = k
        vp_ref[...] = v

    @pl.when(is_sample)
    def _():
        ks_ref[...] = k.astype(ks_ref.dtype)
        vs_ref[...] = v.astype(vs_ref.dtype)


def _gqa_qkv(x, g, mod, layer, w_qkv, slot, qn, kn, cos_t, sin_t, nk, tok):
    n, d = x.shape
    hd = qn.shape[-1]
    assert hd == LANES
    nq = d // hd
    dk = nk * hd
    row = lambda i: (i, 0)
    fixed = lambda i: (0, 0)
    pos = lambda i: (jnp.maximum(i - tok.np_tiles, 0) % tok.tps, 0)
    n_p, n_s = tok.np_tiles * tok.tm, n - tok.np_tiles * tok.tm
    return pl.pallas_call(
        functools.partial(_gqa_qkv_kernel, nq=nq, nk=nk, np_tiles=tok.np_tiles),
        grid=(tok.tiles,),
        in_specs=[pl.BlockSpec((tok.tm, d), row),
                  pl.BlockSpec((1, d), fixed),
                  tok.mod_spec(layer, 1, d, 1),
                  tok.mod_spec(layer, 0, d, 1),
                  _resident((None, d, d + 2 * dk), lambda i: (slot, 0, 0)),
                  pl.BlockSpec((1, hd), fixed),
                  pl.BlockSpec((1, hd), fixed),
                  pl.BlockSpec((tok.tm, hd), pos),
                  pl.BlockSpec((tok.tm, hd), pos)],
        out_specs=[pl.BlockSpec((tok.tm, d), row),
                   tok.prompt_spec(dk, 1), tok.sample_spec(dk, 1), tok.prompt_spec(dk, 1), tok.sample_spec(dk, 1)],
        out_shape=[jax.ShapeDtypeStruct((n, d), BF16),
                   jax.ShapeDtypeStruct((n_p, dk), F32), jax.ShapeDtypeStruct((n_s, dk), BF16),
                   jax.ShapeDtypeStruct((n_p, dk), F32), jax.ShapeDtypeStruct((n_s, dk), BF16)],
        scratch_shapes=[pltpu.VMEM((d, d + 2 * dk), BF16)],
        compiler_params=_cparams("arbitrary"),
        name="gqa_qkv",
    )(x, g, mod, mod, w_qkv, qn.reshape(1, hd), kn.reshape(1, hd), cos_t, sin_t)


def _rope_tables(dec_seq, hd):
    t = jnp.arange(dec_seq)
    row = (t // GRID_W).astype(F32)
    col = (t % GRID_W).astype(F32)
    half = hd // 2
    inv = ROPE_THETA ** (-jnp.arange(0, half, 2, dtype=F32) / half)
    ang = jnp.concatenate([row[:, None] * inv, col[:, None] * inv], axis=-1)
    cos, sin = jnp.cos(ang), jnp.sin(ang)
    return jnp.concatenate([cos, cos], axis=-1), jnp.concatenate([-sin, sin], axis=-1)


def _gqa_attn_kernel(q_ref, *refs, rep, nk, n_kv):
    k_refs, v_refs, o_ref = refs[:n_kv], refs[n_kv:2 * n_kv], refs[2 * n_kv]
    tq = q_ref.shape[0]
    scores, values = [], []
    for kv in range(nk):
        sl = slice(kv * LANES, (kv + 1) * LANES)
        qs = jnp.concatenate([q_ref[:, (kv * rep + r) * LANES:(kv * rep + r + 1) * LANES] for r in range(rep)], axis=0)
        scores.append([_qk(qs, r[:, sl].astype(BF16)) for r in k_refs])
        values.append([r[:, sl].astype(BF16) for r in v_refs])
    for kv in range(nk):
        o = _softmax2_pv(scores[kv], values[kv])
        for r in range(rep):
            o_ref[:, (kv * rep + r) * LANES:(kv * rep + r + 1) * LANES] = o[r * tq:(r + 1) * tq].astype(o_ref.dtype)


def _gqa_ctx_attn(q, k, v, batch, seq, nk):
    d = q.shape[1]
    dk = k.shape[1]
    rep = d // dk
    qspec = pl.BlockSpec((seq, d), lambda b: (b, 0))
    kspec = pl.BlockSpec((seq, dk), lambda b: (b, 0))
    return pl.pallas_call(
        functools.partial(_gqa_attn_kernel, rep=rep, nk=nk, n_kv=1),
        grid=(batch,),
        in_specs=[qspec, kspec, kspec],
        out_specs=qspec,
        out_shape=jax.ShapeDtypeStruct((batch * seq, d), BF16),
        compiler_params=_cparams("arbitrary"),
        name="gqa_ctx_attn",
    )(q, k, v)


def _gqa_lat_attn(q, k, v, cache_k, cache_v, slot, n_prompt, dec_batch, dec_seq, nk):
    d = q.shape[1]
    dk = k.shape[1]
    rep = d // dk
    past = cache_k.shape[2]
    tq = _pow2_tile(256, dec_seq, n_prompt)
    nqb = dec_seq // tq
    q0 = n_prompt // tq
    lspec = pl.BlockSpec((dec_seq, dk), lambda b, qb: (b, 0))
    cspec = pl.BlockSpec((None, None, past, dk), lambda b, qb: (b, slot, 0, 0))
    return pl.pallas_call(
        functools.partial(_gqa_attn_kernel, rep=rep, nk=nk, n_kv=2),
        grid=(dec_batch, nqb),
        in_specs=[pl.BlockSpec((tq, d), lambda b, qb: (q0 + b * nqb + qb, 0)),
                  lspec, cspec, lspec, cspec],
        out_specs=pl.BlockSpec((tq, d), lambda b, qb: (b * nqb + qb, 0)),
        out_shape=jax.ShapeDtypeStruct((dec_batch * dec_seq, d), BF16),
        compiler_params=_cparams("arbitrary", "arbitrary"),
        name="gqa_lat_attn",
    )(q, k, cache_k, v, cache_v)


def _proj_res_kernel(x_ref, op_ref, os_ref, w_ref, gate_ref, out_ref, w_scr, *, np_tiles):
    _cast_once(w_ref, w_scr)
    o = jnp.where(pl.program_id(0) < np_tiles, op_ref[...], os_ref[...])
    y = jnp.dot(o, w_scr[...], preferred_element_type=F32)
    out_ref[...] = x_ref[...] + gate_ref[...] * y


def _proj_res(x, o_p, o_s, w_o, slot, mod, layer, tok):
    n, d = x.shape
    row = lambda i: (i, 0)
    return pl.pallas_call(
        functools.partial(_proj_res_kernel, np_tiles=tok.np_tiles),
        grid=(tok.tiles,),
        in_specs=[pl.BlockSpec((tok.tm, d), row),
                  tok.prompt_spec(d, 1),
                  tok.sample_spec(d, 1),
                  _resident((None, d, d), lambda i: (slot, 0, 0)),
                  tok.mod_spec(layer, 2, d, 1)],
        out_specs=pl.BlockSpec((tok.tm, d), row),
        out_shape=jax.ShapeDtypeStruct((n, d), F32),
        scratch_shapes=[pltpu.VMEM((d, d), BF16)],
        compiler_params=_cparams("arbitrary"),
        name="attn_proj_res",
    )(x, o_p, o_s, w_o, mod)


def _mlp_kernel(x_ref, g_ref, sc_ref, sh_ref, gate_ref, w1_ref, w2_ref, *rest, np_tiles):
    out_refs, (h_scr, acc_scr) = rest[:-2], rest[-2:]
    i = pl.program_id(0)
    j = pl.program_id(1)
    last = j == pl.num_programs(1) - 1

    @pl.when(j == 0)
    def _():
        h_scr[...] = _normmod(x_ref[...], g_ref[...], sc_ref[...], sh_ref[...]).astype(BF16)
        acc_scr[...] = jnp.zeros_like(acc_scr)

    a = jnp.maximum(jnp.dot(h_scr[...], w1_ref[...].astype(BF16), preferred_element_type=F32), 0.0)
    acc_scr[...] += jnp.dot((a * a).astype(BF16), w2_ref[...].astype(BF16), preferred_element_type=F32)

    def result():
        return x_ref[...] + gate_ref[...] * acc_scr[...]

    if len(out_refs) == 1:
        @pl.when(last)
        def _():
            out_refs[0][...] = result()
    else:
        @pl.when(jnp.logical_and(last, i < np_tiles))
        def _():
            out_refs[0][...] = result()

        @pl.when(jnp.logical_and(last, i >= np_tiles))
        def _():
            out_refs[1][...] = result()


def _mlp(x, g, mod, layer, w1, w2, tok, split_out):
    n, d = x.shape
    f = w1.shape[2]
    tf = _pow2_tile(1024, f)
    row = lambda i, j: (i, 0)
    if split_out:
        n_p = tok.np_tiles * tok.tm
        out_specs = [tok.prompt_spec(d, 2), tok.sample_spec(d, 2)]
        out_shape = [jax.ShapeDtypeStruct((n_p, d), F32), jax.ShapeDtypeStruct((n - n_p, d), F32)]
    else:
        out_specs = pl.BlockSpec((tok.tm, d), row)
        out_shape = jax.ShapeDtypeStruct((n, d), F32)
    return pl.pallas_call(
        functools.partial(_mlp_kernel, np_tiles=tok.np_tiles),
        grid=(tok.tiles, f // tf),
        in_specs=[pl.BlockSpec((tok.tm, d), row),
                  pl.BlockSpec((1, d), lambda i, j: (0, 0)),
                  tok.mod_spec(layer, 4, d, 2),
                  tok.mod_spec(layer, 3, d, 2),
                  tok.mod_spec(layer, 5, d, 2),
                  pl.BlockSpec((None, d, tf), lambda i, j: (layer, 0, j)),
                  pl.BlockSpec((None, tf, d), lambda i, j: (layer, j, 0))],
        out_specs=out_specs,
        out_shape=out_shape,
        scratch_shapes=[pltpu.VMEM((tok.tm, d), BF16), pltpu.VMEM((tok.tm, d), F32)],
        compiler_params=_cparams("arbitrary", "arbitrary"),
        name="mlp",
    )(x, g, mod, mod, mod, w1, w2)


def _s5_tables(lam_re, lam_im, log_dt, b_re, b_im, c_re, c_im):
    t = S5_CHUNK
    g, p, k = b_re.shape[1:]
    dt = jnp.exp(log_dt.astype(F32))[:, :, None]
    lr, li = lam_re.astype(F32), lam_im.astype(F32)
    ar, ai = lr * dt, li * dt
    mag = jnp.exp(ar)
    abr, abi = mag * jnp.cos(ai), mag * jnp.sin(ai)
    nr, ni = abr - 1.0, abi
    den = lr * lr + li * li
    f_re = (nr * lr + ni * li) / den
    f_im = (ni * lr - nr * li) / den
    bbr = f_re[..., None] * b_re - f_im[..., None] * b_im
    bbi = f_re[..., None] * b_im + f_im[..., None] * b_re
    n = jnp.arange(t + 1, dtype=F32)[None, None, :, None]
    pm = jnp.exp(n * ar[:, :, None, :])
    pr, pi = pm * jnp.cos(n * ai[:, :, None, :]), pm * jnp.sin(n * ai[:, :, None, :])
    cr, ci = c_re.astype(F32), c_im.astype(F32)

    def summ(pw_r, pw_i, d):
        pw_r = pw_r.transpose(0, 2, 1)[:, :, :, None]
        pw_i = pw_i.transpose(0, 2, 1)[:, :, :, None]
        sr = pw_r * bbr[d][:, :, None, :] - pw_i * bbi[d][:, :, None, :]
        si = pw_r * bbi[d][:, :, None, :] + pw_i * bbr[d][:, :, None, :]
        return sr.reshape(g, p, t * k), si.reshape(g, p, t * k)

    sfr, sfi = summ(pr[0, :, :t][:, ::-1], pi[0, :, :t][:, ::-1], 0)
    sbr, sbi = summ(pr[1, :, :t], pi[1, :, :t], 1)
    w_intra, w_sum = _s5_toeplitz(jnp.concatenate([cr[0], -ci[0]], axis=-1),
                                  jnp.concatenate([cr[1], -ci[1]], axis=-1), sfr, sfi, sbr, sbi)
    car = cr[:, :, None] * pr[:, :, 1:, None, :] - ci[:, :, None] * pi[:, :, 1:, None, :]
    cai = cr[:, :, None] * pi[:, :, 1:, None, :] + ci[:, :, None] * pr[:, :, 1:, None, :]
    rd = lambda a: a.reshape(g, t * k, p)
    w2t = jnp.concatenate([rd(car[0]), rd(car[1][:, ::-1]), -rd(cai[0]), -rd(cai[1][:, ::-1])], axis=2)
    at = jnp.stack([jnp.concatenate([pr[0, :, t], pr[1, :, t]], axis=-1),
                    jnp.concatenate([pi[0, :, t], pi[1, :, t]], axis=-1)], axis=1)
    return w_intra, w_sum, w2t.astype(BF16), at


def _s5_toeplitz_kernel(cf_ref, cb_ref, sfr_ref, sfi_ref, sbr_ref, sbi_ref, o_ref, ws_ref):
    hi = lax.Precision.HIGHEST
    gl, k, _ = cf_ref.shape
    tk = sfr_ref.shape[2]
    for g in range(gl):
        abf = jnp.concatenate([sfr_ref[g], sfi_ref[g]], axis=0)
        abb = jnp.concatenate([sbr_ref[g], sbi_ref[g]], axis=0)
        ws_ref[g] = jnp.concatenate([sfr_ref[g], sbr_ref[g], sfi_ref[g], sbi_ref[g]], axis=0).astype(ws_ref.dtype)
        kf = jnp.dot(cf_ref[g], abf, precision=hi, preferred_element_type=F32)
        kb = jnp.dot(cb_ref[g], abb, precision=hi, preferred_element_type=F32)
        z = jnp.zeros_like(kf)
        krev = jnp.concatenate([kf, z], axis=1) + pltpu.roll(jnp.concatenate([z, kb], axis=1), 2 * tk - k, 1)
        for i in range(S5_CHUNK):
            sh = (S5_CHUNK - 1 - i) * k
            win = krev if sh == 0 else pltpu.roll(krev, 2 * tk - sh, 1)
            o_ref[g, i * k:(i + 1) * k, :] = win[:, 0:tk].astype(o_ref.dtype)


def _s5_toeplitz(cf, cb, sfr, sfi, sbr, sbi):
    g, k, p2 = cf.shape
    p, tk = sfr.shape[1:]
    gl = _pow2_tile(8, g)
    cspec = pl.BlockSpec((gl, k, p2), lambda a: (a, 0, 0))
    sspec = pl.BlockSpec((gl, p, tk), lambda a: (a, 0, 0))
    return pl.pallas_call(
        _s5_toeplitz_kernel,
        grid=(g // gl,),
        in_specs=[cspec, cspec, sspec, sspec, sspec, sspec],
        out_specs=[pl.BlockSpec((gl, tk, tk), lambda a: (a, 0, 0)), pl.BlockSpec((gl, 4 * p, tk), lambda a: (a, 0, 0))],
        out_shape=[jax.ShapeDtypeStruct((g, tk, tk), BF16), jax.ShapeDtypeStruct((g, 4 * p, tk), BF16)],
        compiler_params=_cparams("arbitrary"),
        name="s5_toeplitz",
    )(cf, cb, sfr, sfi, sbr, sbi)


def _s5_mixer(x, g, mod, layer, tok, params, h0, dims):
    (lam_re, lam_im, log_dt, b_re, b_im, c_re, c_im, d_skip, glu_w, slot, glu_b) = params
    batch, seq, dec_batch, dec_seq = dims
    n_prompt = batch * seq
    d = d_skip.shape[0]
    ngrp = d // S5_GROUP
    p = lam_re.shape[-1]
    w_intra, w_sum, w2t, at = _s5_tables(lam_re, lam_im, log_dt, b_re, b_im, c_re, c_im)
    hperm = _s5_pre(x, g, mod, layer, tok)
    h0g = h0.astype(F32).transpose(3, 2, 0, 1, 4).reshape(ngrp, 2, dec_batch, 2 * p)
    streams = ((0, seq // S5_CHUNK, batch, False, True),
               (n_prompt // S5_CHUNK, dec_seq // S5_CHUNK, dec_batch, True, False))
    yperm, fin = _s5_scan(hperm, w_intra, w_sum, w2t, at, h0g, batch, streams)
    x_new = _s5_post(x, yperm, g, mod, layer, d_skip.reshape(1, d), glu_w, slot, glu_b, tok)
    st = fin.reshape(ngrp, 2, batch, 2, p).transpose(2, 3, 1, 0, 4)
    return x_new, st


def kernel(x_prompt, x_sample, state_s5, cache_na_k, cache_na_v, cache_gqa_k, cache_gqa_v, c, c_ctx, norm_g, ada_w, ada_b, mlp_w1, mlp_w2, s5_lam_re, s5_lam_im, s5_log_dt, s5_b_re, s5_b_im, s5_c_re, s5_c_im, s5_d, s5_glu_w, s5_glu_b, na_w_qkv, na_q_norm, na_k_norm, na_rpb, na_w_o, gqa_w_qkv, gqa_q_norm, gqa_k_norm, gqa_w_o):
    batch, seq, d = x_prompt.shape
    dec_batch, dec_seq, _ = x_sample.shape
    depth = ada_w.shape[0]
    n_prompt = batch * seq
    n_sample = dec_batch * dec_seq
    na_heads, na_hd = cache_na_k.shape[3], cache_na_k.shape[4]
    gqa_kv, gqa_hd = cache_gqa_k.shape[3], cache_gqa_k.shape[4]
    assert n_prompt % dec_seq == 0

    tok = _Tok(n_prompt, n_sample, dec_seq, 1024)
    tok_half = _Tok(n_prompt, n_sample, dec_seq, 512)

    mod_rows = -(-(1 + dec_batch) // SUBLANES) * SUBLANES
    cvec = jnp.concatenate([c_ctx[None, :], c, jnp.zeros((mod_rows - 1 - dec_batch, d), F32)], axis=0)
    mod = _modulation(cvec, ada_w, ada_b).reshape(depth, mod_rows, 6, 1, d)

    x = (x_prompt.reshape(n_prompt, d), x_sample.reshape(n_sample, d))
    cache_na_k2 = cache_na_k.reshape(cache_na_k.shape[:3] + (na_heads * na_hd,))
    cache_na_v2 = cache_na_v.reshape(cache_na_v.shape[:3] + (na_heads * na_hd,))
    cache_gqa_k2 = cache_gqa_k.reshape(cache_gqa_k.shape[:3] + (gqa_kv * gqa_hd,))
    cache_gqa_v2 = cache_gqa_v.reshape(cache_gqa_v.shape[:3] + (gqa_kv * gqa_hd,))
    cos_t, sin_t = _rope_tables(dec_seq, gqa_hd)

    new_s5, new_na_k, new_na_v, new_gqa_k, new_gqa_v = [], [], [], [], []
    for i in range(depth):
        kind, slot = i % 3, i // 3
        g1 = norm_g[i, 0].reshape(1, d)
        g2 = norm_g[i, 1].reshape(1, d)
        if kind == 0:
            params = (s5_lam_re[slot], s5_lam_im[slot], s5_log_dt[slot], s5_b_re[slot], s5_b_im[slot],
                      s5_c_re[slot], s5_c_im[slot], s5_d[slot], s5_glu_w, slot, s5_glu_b[slot])
            x, st = _s5_mixer(x, g1, mod, i, tok_half, params, state_s5[:, slot], (batch, seq, dec_batch, dec_seq))
            new_s5.append(st)
        elif kind == 1:
            assert not isinstance(x, tuple)
            q, k_p, k_s, v_p, v_s = _na_qkv(x, g1, mod, i, na_w_qkv, slot, na_q_norm[slot], na_k_norm[slot], tok_half)
            o_p = _na_ctx_attn(q, k_p, v_p, batch, seq, na_hd)
            bias = _na_bias_table(na_rpb[slot], dec_seq)
            o_s = _na_lat_attn(q, k_s, v_s, cache_na_k2, cache_na_v2, slot, bias, n_prompt, dec_batch, dec_seq, na_hd)
            x = _proj_res(x, o_p, o_s, na_w_o, slot, mod, i, tok)
            new_na_k.append(k_p.reshape(batch, seq, na_heads, na_hd))
            new_na_v.append(v_p.reshape(batch, seq, na_heads, na_hd))
        else:
            assert not isinstance(x, tuple)
            q, k_p, k_s, v_p, v_s = _gqa_qkv(x, g1, mod, i, gqa_w_qkv, slot, gqa_q_norm[slot], gqa_k_norm[slot],
                                             cos_t, sin_t, gqa_kv, tok_half)
            o_p = _gqa_ctx_attn(q, k_p, v_p, batch, seq, gqa_kv)
            o_s = _gqa_lat_attn(q, k_s, v_s, cache_gqa_k2, cache_gqa_v2, slot, n_prompt, dec_batch, dec_seq, gqa_kv)
            x = _proj_res(x, o_p, o_s, gqa_w_o, slot, mod, i, tok)
            new_gqa_k.append(k_p.reshape(batch, seq, gqa_kv, gqa_hd))
            new_gqa_v.append(v_p.reshape(batch, seq, gqa_kv, gqa_hd))
        x = _mlp(x, g2, mod, i, mlp_w1, mlp_w2, tok, split_out=(i == depth - 1))
    y_p, y_s = x
    return (y_p.reshape(batch, seq, d), y_s.reshape(dec_batch, dec_seq, d),
            jnp.stack(new_s5, axis=1), jnp.stack(new_na_k, axis=1), jnp.stack(new_na_v, axis=1),
            jnp.stack(new_gqa_k, axis=1), jnp.stack(new_gqa_v, axis=1))
```

```python
import functools
import math

import jax
import jax.numpy as jnp
from jax import lax
from jax.experimental import pallas as pl
from jax.experimental.pallas import tpu as pltpu

F32 = jnp.float32
BF16 = jnp.bfloat16

EPS = 1e-6
GRID_W = 64
S5_GROUP = 16
NA_KH = 8
NA_KW = 16
ROPE_THETA = 10000.0
S5_CHUNK = 16
LOG2E = math.log2(math.e)
LANES = 128
SUBLANES = 8
VMEM_LIMIT = 56 * 1024 * 1024


def _cparams(*sem):
    return pltpu.CompilerParams(dimension_semantics=sem, vmem_limit_bytes=VMEM_LIMIT)


def _pow2_tile(pref, *ns):
    t = pref
    while any(n % t for n in ns):
        t //= 2
    return t


def _row_halves(n):
    if n % 2 or n // 2 < LANES:
        return [slice(0, n)]
    return [slice(0, n // 2), slice(n // 2, n)]


def _software_pipeline(parts, before, matmuls, after):
    prev = None
    for p in parts:
        z = matmuls(before(p))
        if prev is not None:
            after(*prev)
        prev = (p, z)
    after(*prev)


def _normmod(x, g, sc, sh):
    ms = jnp.mean(x * x, axis=-1, keepdims=True)
    y = x * lax.rsqrt(ms + EPS) * g
    return y * (1.0 + sc) + sh


class _Tok:
    def __init__(self, n_prompt, n_sample, dec_seq, pref):
        self.tm = _pow2_tile(pref, n_prompt, dec_seq)
        self.n = n_prompt + n_sample
        self.tiles = self.n // self.tm
        self.np_tiles = n_prompt // self.tm
        self.tps = dec_seq // self.tm

    def mod_row(self, i):
        return jnp.where(i < self.np_tiles, 0, 1 + (i - self.np_tiles) // self.tps)

    def prompt_spec(self, d, nargs):
        last = self.np_tiles - 1
        if nargs == 1:
            return pl.BlockSpec((self.tm, d), lambda i: (jnp.minimum(i, last), 0))
        return pl.BlockSpec((self.tm, d), lambda i, j: (jnp.minimum(i, last), 0))

    def sample_spec(self, d, nargs):
        first = self.np_tiles
        if nargs == 1:
            return pl.BlockSpec((self.tm, d), lambda i: (jnp.maximum(i - first, 0), 0))
        return pl.BlockSpec((self.tm, d), lambda i, j: (jnp.maximum(i - first, 0), 0))

    def mod_spec(self, layer, which, d, nargs):
        if nargs == 1:
            return pl.BlockSpec((None, None, None, 1, d), lambda i: (layer, self.mod_row(i), which, 0, 0))
        return pl.BlockSpec((None, None, None, 1, d), lambda i, j: (layer, self.mod_row(i), which, 0, 0))


def _mod_kernel(c_ref, w_ref, b_ref, o_ref):
    c = c_ref[...]
    s = c * jax.nn.sigmoid(c)
    o_ref[...] = jnp.dot(s.astype(BF16), w_ref[...].astype(BF16), preferred_element_type=F32) + b_ref[...]


def _modulation(cvec, ada_w, ada_b):
    depth, d, d6 = ada_w.shape
    rows = cvec.shape[0]
    tn = _pow2_tile(2048, d6) if d6 % 2048 == 0 else d
    return pl.pallas_call(
        _mod_kernel,
        grid=(depth, d6 // tn),
        in_specs=[pl.BlockSpec((rows, d), lambda l, j: (0, 0)),
                  pl.BlockSpec((None, d, tn), lambda l, j: (l, 0, j)),
                  pl.BlockSpec((None, 1, tn), lambda l, j: (l, 0, j))],
        out_specs=pl.BlockSpec((None, rows, tn), lambda l, j: (l, 0, j)),
        out_shape=jax.ShapeDtypeStruct((depth, rows, d6), F32),
        compiler_params=_cparams("arbitrary", "arbitrary"),
        name="adaln_modulation",
    )(cvec, ada_w, ada_b.reshape(depth, 1, d6))


def _read_x(x_refs, np_tiles, rows=slice(None), cols=slice(None)):
    if len(x_refs) == 1:
        return x_refs[0][rows, cols]
    return jnp.where(pl.program_id(0) < np_tiles, x_refs[0][rows, cols], x_refs[1][rows, cols])


def _x_specs(x, tok, nargs):
    if isinstance(x, tuple):
        d = x[0].shape[1]
        return [tok.prompt_spec(d, nargs), tok.sample_spec(d, nargs)], x
    d = x.shape[1]
    return [pl.BlockSpec((tok.tm, d), (lambda i: (i, 0)) if nargs == 1 else (lambda i, j: (i, 0)))], (x,)


def _s5_pre_kernel(*refs, n_x, np_tiles):
    x_refs, (g_ref, sc_ref, sh_ref, h_ref, h_scr) = refs[:n_x], refs[n_x:]
    x = _read_x(x_refs, np_tiles)
    rstd = lax.rsqrt(jnp.mean(x * x, axis=-1, keepdims=True) + EPS)
    nc = h_ref.shape[1]
    for a in range(h_scr.shape[0]):
        sl = slice(a * LANES, (a + 1) * LANES)
        h_scr[a] = (x[:, sl] * rstd * g_ref[:, sl]) * (1.0 + sc_ref[:, sl]) + sh_ref[:, sl]
        for j in range(S5_CHUNK):
            h_ref[j, :, sl] = h_scr[a, pl.ds(j, nc, stride=S5_CHUNK), :].astype(h_ref.dtype)


def _s5_pre(x, g, mod, layer, tok):
    x_specs, xs = _x_specs(x, tok, 1)
    d = xs[0].shape[1]
    nc = tok.tm // S5_CHUNK
    return pl.pallas_call(
        functools.partial(_s5_pre_kernel, n_x=len(xs), np_tiles=tok.np_tiles),
        grid=(tok.tiles,),
        in_specs=x_specs + [pl.BlockSpec((1, d), lambda i: (0, 0)),
                            tok.mod_spec(layer, 1, d, 1),
                            tok.mod_spec(layer, 0, d, 1)],
        out_specs=pl.BlockSpec((S5_CHUNK, nc, d), lambda i: (0, i, 0)),
        out_shape=jax.ShapeDtypeStruct((S5_CHUNK, tok.n // S5_CHUNK, d), BF16),
        scratch_shapes=[pltpu.VMEM((d // LANES, tok.tm, LANES), F32)],
        compiler_params=_cparams("arbitrary"),
        name="s5_pre",
    )(*xs, g, mod, mod)


def _s5_scan_kernel(h_ref, wi_ref, ws_ref, w2t_ref, at_ref, h0_ref, *rest, streams, unroll):
    ns = len(streams)
    perm_refs, (y_ref, fin_ref, xt_scr, yt_scr), chain_scr = rest[:ns], rest[ns:ns + 4], rest[ns + 4:]
    ngrp = wi_ref.shape[0]
    p2 = at_ref.shape[-1]
    fwd = lax.broadcasted_iota(jnp.int32, (1, p2), 1) < (p2 // 2)
    for j in range(S5_CHUNK):
        xt_scr[j] = h_ref[j].astype(F32).T.astype(BF16)

    def load_rmat(r, stream):
        row0, n_chunks, nb = stream[:3]
        cols = slice(row0, row0 + n_chunks * nb)
        c0 = pl.multiple_of(r * S5_GROUP, S5_GROUP)
        return jnp.concatenate([xt_scr[j, pl.ds(c0, S5_GROUP), cols] for j in range(S5_CHUNK)], axis=0), c0, cols

    def summaries(r, stream, perm_ref, s_scr):
        nb, use_h0 = stream[2], stream[3]
        rmat, _, _ = load_rmat(r, stream)
        rmat_cm = jnp.dot(rmat, perm_ref[...], preferred_element_type=F32).astype(BF16)
        zs = jnp.dot(ws_ref[r], rmat_cm, preferred_element_type=F32)
        s_scr[0] = zs[0:p2].T
        s_scr[1] = zs[p2:].T
        if use_h0:
            return h0_ref[r, 0], h0_ref[r, 1]
        return jnp.zeros((nb, p2), F32), jnp.zeros((nb, p2), F32)

    def scan_step(t, decay, stream, state, s_scr, e_scr):
        n_chunks, nb = stream[1], stream[2]
        hr, hi = state
        ar, ai = decay
        ft = slice(t * nb, (t + 1) * nb)
        bt = slice((n_chunks - 1 - t) * nb, (n_chunks - t) * nb)
        e_scr[0, ft, :] = hr
        e_scr[1, bt, :] = hr
        e_scr[2, ft, :] = hi
        e_scr[3, bt, :] = hi
        in_re = jnp.where(fwd, s_scr[0, ft, :], s_scr[0, bt, :])
        in_im = jnp.where(fwd, s_scr[1, ft, :], s_scr[1, bt, :])
        return ar * hr - ai * hi + in_re, ar * hi + ai * hr + in_im

    def outputs(r, stream, perm_ref, state, e_scr):
        rmat, c0, cols = load_rmat(r, stream)
        if stream[4]:
            fin_ref[r, 0] = state[0]
            fin_ref[r, 1] = state[1]
        e_cm = jnp.concatenate([jnp.where(fwd, e_scr[0], e_scr[1]), jnp.where(fwd, e_scr[2], e_scr[3])],
                               axis=1).astype(BF16)
        e = jnp.dot(perm_ref[...], e_cm, preferred_element_type=F32).astype(BF16)
        yt = (jnp.dot(wi_ref[r], rmat, preferred_element_type=F32)
              + lax.dot_general(w2t_ref[r], e, (((1,), (1,)), ((), ())), preferred_element_type=F32))
        for i in range(S5_CHUNK):
            yt_scr[i, pl.ds(c0, S5_GROUP), cols] = yt[i * S5_GROUP:(i + 1) * S5_GROUP]

    def body(rb, carry):
        chains = []
        for u in range(unroll):
            for si, stream in enumerate(streams):
                k = 2 * (u * ns + si)
                chains.append((rb * unroll + u, stream, perm_refs[si], chain_scr[k], chain_scr[k + 1]))
        states = [summaries(r, stream, pm, s_scr) for r, stream, pm, s_scr, _ in chains]
        decays = [(at_ref[r, 0:1, :], at_ref[r, 1:2, :]) for r, _, _, _, _ in chains]
        for t in range(max(stream[1] for stream in streams)):
            for ci, (_, stream, _, s_scr, e_scr) in enumerate(chains):
                if t < stream[1]:
                    states[ci] = scan_step(t, decays[ci], stream, states[ci], s_scr, e_scr)
        for (r, stream, pm, _, e_scr), state in zip(chains, states):
            outputs(r, stream, pm, state, e_scr)
        return carry

    lax.fori_loop(0, ngrp // unroll, body, 0)
    for i in range(S5_CHUNK):
        y_ref[i] = yt_scr[i].T


def _s5_scan(hperm, w_intra, w_sum, w2t, at, h0, nb_fin, streams):
    t, nrows, d = hperm.shape
    g, tk, p4 = w2t.shape
    p2 = at.shape[-1]
    gl = LANES // S5_GROUP
    nb0 = h0.shape[2]
    unroll = 4
    assert sum(s[1] * s[2] for s in streams) == nrows and gl % unroll == 0
    chain_scr = []
    for _ in range(unroll):
        for s in streams:
            chain_scr += [pltpu.VMEM((2, s[1] * s[2], p2), F32), pltpu.VMEM((4, s[1] * s[2], p2), F32)]
    perms = []
    for _, n_chunks, nb, _, _ in streams:
        m = jnp.arange(n_chunks * nb)
        perms.append((m[:, None] == ((m % nb) * n_chunks + m // nb)[None, :]).astype(BF16))
    return pl.pallas_call(
        functools.partial(_s5_scan_kernel, streams=streams, unroll=unroll),
        grid=(d // LANES,),
        in_specs=[pl.BlockSpec((t, nrows, LANES), lambda a: (0, 0, a)),
                  pl.BlockSpec((gl, tk, tk), lambda a: (a, 0, 0)),
                  pl.BlockSpec((gl, p4, tk), lambda a: (a, 0, 0)),
                  pl.BlockSpec((gl, tk, p4), lambda a: (a, 0, 0)),
                  pl.BlockSpec((gl, 2, p2), lambda a: (a, 0, 0)),
                  pl.BlockSpec((gl, 2, nb0, p2), lambda a: (a, 0, 0, 0))]
                 + [pl.BlockSpec(pm.shape, lambda a: (0, 0)) for pm in perms],
        out_specs=[pl.BlockSpec((t, nrows, LANES), lambda a: (0, 0, a)),
                   pl.BlockSpec((gl, 2, nb_fin, p2), lambda a: (a, 0, 0, 0))],
        out_shape=[jax.ShapeDtypeStruct((t, nrows, d), F32),
                   jax.ShapeDtypeStruct((g, 2, nb_fin, p2), F32)],
        scratch_shapes=[pltpu.VMEM((t, LANES, nrows), BF16), pltpu.VMEM((t, LANES, nrows), F32)] + chain_scr,
        compiler_params=_cparams("arbitrary"),
        name="s5_scan",
    )(hperm, w_intra, w_sum, w2t, at, h0, *perms)


def _s5_post_kernel(*refs, n_x, np_tiles):
    x_refs = refs[:n_x]
    (y_ref, g_ref, sc_ref, sh_ref, gate_ref, dsk_ref, wa_ref, wg_ref, ba_ref, bg_ref, o_ref,
     y_scr, wa_scr, wg_scr) = refs[n_x:]
    _cast_once(wa_ref, wa_scr)
    _cast_once(wg_ref, wg_scr)
    tm, d = o_ref.shape
    n_cb = max(d // (2 * LANES), 1)
    cw = d // n_cb

    def before(rows):
        c0, nc = rows.start // S5_CHUNK, (rows.stop - rows.start) // S5_CHUNK
        for a in range(y_scr.shape[0]):
            for j in range(S5_CHUNK):
                y_scr[a, pl.ds(rows.start + j, nc, stride=S5_CHUNK), :] = y_ref[j, c0:c0 + nc, a * LANES:(a + 1) * LANES]
        x = _read_x(x_refs, np_tiles, rows)
        h = _normmod(x, g_ref[...], sc_ref[...], sh_ref[...])
        y = h * dsk_ref[...] + jnp.concatenate([y_scr[a, rows, :] for a in range(y_scr.shape[0])], axis=1)
        return jax.nn.gelu(y).astype(BF16)

    def matmuls(a, cb):
        cols = slice(cb * cw, (cb + 1) * cw)
        return (jnp.dot(a, wa_scr[:, cols], preferred_element_type=F32),
                jnp.dot(a, wg_scr[:, cols], preferred_element_type=F32))

    def after(rows, cb, z):
        cols = slice(cb * cw, (cb + 1) * cw)
        za, zg = z[0] + ba_ref[:, cols], z[1] + bg_ref[:, cols]
        x = _read_x(x_refs, np_tiles, rows, cols)
        o_ref[rows, cols] = x + gate_ref[:, cols] * (za * jax.nn.sigmoid(zg))

    def row_chunks(rows):
        step = max((rows.stop - rows.start) // n_cb, S5_CHUNK)
        return [slice(r, min(r + step, rows.stop)) for r in range(rows.start, rows.stop, step)]

    halves = _row_halves(tm)
    a_cur = jnp.concatenate([before(rc) for rc in row_chunks(halves[0])], axis=0)
    z_prev = None
    for hi, rows in enumerate(halves):
        nxt = row_chunks(halves[hi + 1]) if hi + 1 < len(halves) else []
        a_next, z_cur = [], []
        for cb in range(n_cb):
            z_cur.append(matmuls(a_cur, cb))
            if cb < len(nxt):
                a_next.append(before(nxt[cb]))
            if z_prev is not None:
                after(halves[hi - 1], cb, z_prev[cb])
        a_next += [before(rc) for rc in nxt[n_cb:]]
        z_prev = z_cur
        if a_next:
            a_cur = jnp.concatenate(a_next, axis=0)
    for cb in range(n_cb):
        after(halves[-1], cb, z_prev[cb])


def _s5_post(x, y, g, mod, layer, dsk, glu_w, slot, glu_b, tok):
    x_specs, xs = _x_specs(x, tok, 1)
    n, d = tok.n, xs[0].shape[1]
    row = lambda i: (i, 0)
    fixed = lambda i: (0, 0)
    gb2 = glu_b.reshape(1, 2 * d)
    return pl.pallas_call(
        functools.partial(_s5_post_kernel, n_x=len(xs), np_tiles=tok.np_tiles),
        grid=(tok.tiles,),
        in_specs=x_specs + [
                  pl.BlockSpec((S5_CHUNK, tok.tm // S5_CHUNK, d), lambda i: (0, i, 0)),
                  pl.BlockSpec((1, d), fixed),
                  tok.mod_spec(layer, 1, d, 1),
                  tok.mod_spec(layer, 0, d, 1),
                  tok.mod_spec(layer, 2, d, 1),
                  pl.BlockSpec((1, d), fixed),
                  _resident((None, d, d), lambda i: (slot, 0, 0)),
                  _resident((None, d, d), lambda i: (slot, 0, 1)),
                  pl.BlockSpec((1, d), lambda i: (0, 0)),
                  pl.BlockSpec((1, d), lambda i: (0, 1))],
        out_specs=pl.BlockSpec((tok.tm, d), row),
        out_shape=jax.ShapeDtypeStruct((n, d), F32),
        scratch_shapes=[pltpu.VMEM((d // LANES, tok.tm, LANES), F32),
                        pltpu.VMEM((d, d), BF16), pltpu.VMEM((d, d), BF16)],
        compiler_params=_cparams("arbitrary"),
        name="s5_post",
    )(*xs, y, g, mod, mod, mod, dsk, glu_w, glu_w, gb2, gb2)


def _cast_once(w_ref, w_scr):
    @pl.when(pl.program_id(0) == 0)
    def _():
        w_scr[...] = w_ref[...].astype(w_scr.dtype)


def _resident(block_shape, index_map):
    return pl.BlockSpec(block_shape, index_map, pipeline_mode=pl.Buffered(1))


def _na_qkv_kernel(x_ref, g_ref, sc_ref, sh_ref, w_ref, qn_ref, kn_ref, q_ref, kp_ref, ks_ref, vp_ref, vs_ref, w_scr, *,
                   hd, np_tiles):
    i = pl.program_id(0)
    _cast_once(w_ref, w_scr)
    h = _normmod(x_ref[...], g_ref[...], sc_ref[...], sh_ref[...]).astype(BF16)
    d = h.shape[1]
    lo = lax.broadcasted_iota(jnp.int32, (1, LANES), 1) < hd

    def head_norm(z, gain):
        outs = []
        for s in range(d // LANES):
            seg = z[:, s * LANES:(s + 1) * LANES]
            sq = seg * seg
            s_lo = jnp.sum(jnp.where(lo, sq, 0.0), axis=-1, keepdims=True)
            s_hi = jnp.sum(jnp.where(lo, 0.0, sq), axis=-1, keepdims=True)
            ms = jnp.where(lo, s_lo, s_hi) / hd
            outs.append(seg * lax.rsqrt(ms + EPS) * gain)
        return jnp.concatenate(outs, axis=1)

    proj = lambda part: jnp.dot(h, w_scr[:, part * d:(part + 1) * d], preferred_element_type=F32)
    q_ref[...] = (head_norm(proj(0), qn_ref[...]) * (hd ** -0.5 * LOG2E)).astype(q_ref.dtype)
    k = head_norm(proj(1), kn_ref[...])
    v = proj(2)

    @pl.when(i < np_tiles)
    def _():
        kp_ref[...] = k
        vp_ref[...] = v

    @pl.when(i >= np_tiles)
    def _():
        ks_ref[...] = k.astype(ks_ref.dtype)
        vs_ref[...] = v.astype(vs_ref.dtype)


def _na_qkv(x, g, mod, layer, w_qkv, slot, qn, kn, tok):
    n, d = x.shape
    hd = qn.shape[-1]
    assert 2 * hd == LANES
    row = lambda i: (i, 0)
    fixed = lambda i: (0, 0)
    qn2 = jnp.tile(qn, 2).reshape(1, LANES)
    kn2 = jnp.tile(kn, 2).reshape(1, LANES)
    n_p, n_s = tok.np_tiles * tok.tm, n - tok.np_tiles * tok.tm
    return pl.pallas_call(
        functools.partial(_na_qkv_kernel, hd=hd, np_tiles=tok.np_tiles),
        grid=(tok.tiles,),
        in_specs=[pl.BlockSpec((tok.tm, d), row),
                  pl.BlockSpec((1, d), fixed),
                  tok.mod_spec(layer, 1, d, 1),
                  tok.mod_spec(layer, 0, d, 1),
                  _resident((None, d, 3 * d), lambda i: (slot, 0, 0)),
                  pl.BlockSpec((1, LANES), fixed),
                  pl.BlockSpec((1, LANES), fixed)],
        out_specs=[pl.BlockSpec((tok.tm, d), row),
                   tok.prompt_spec(d, 1), tok.sample_spec(d, 1), tok.prompt_spec(d, 1), tok.sample_spec(d, 1)],
        out_shape=[jax.ShapeDtypeStruct((n, d), BF16),
                   jax.ShapeDtypeStruct((n_p, d), F32), jax.ShapeDtypeStruct((n_s, d), BF16),
                   jax.ShapeDtypeStruct((n_p, d), F32), jax.ShapeDtypeStruct((n_s, d), BF16)],
        scratch_shapes=[pltpu.VMEM((d, 3 * d), BF16)],
        compiler_params=_cparams("arbitrary"),
        name="na_qkv",
    )(x, g, mod, mod, w_qkv, qn2, kn2)


def _softmax2_pv(s_parts, v_parts):
    m = s_parts[0].max(axis=-1, keepdims=True)
    for s in s_parts[1:]:
        m = jnp.maximum(m, s.max(axis=-1, keepdims=True))
    den = 0.0
    acc = 0.0
    for s, v in zip(s_parts, v_parts):
        e = jnp.exp2(s - m)
        den = den + e.sum(axis=-1, keepdims=True)
        acc = acc + jnp.dot(e.astype(BF16), v, preferred_element_type=F32)
    return acc / den


def _qk(q, k):
    return lax.dot_general(q, k, (((1,), (1,)), ((), ())), preferred_element_type=F32)


def _na_ctx_attn_kernel(q_ref, k_ref, v_ref, o_ref, *, hd):
    lo = lax.broadcasted_iota(jnp.int32, (1, LANES), 1) < hd
    for s in range(q_ref.shape[1] // LANES):
        sl = slice(s * LANES, (s + 1) * LANES)
        q = q_ref[:, sl]
        k = k_ref[:, sl].astype(BF16)
        v = v_ref[:, sl].astype(BF16)
        zero = jnp.zeros_like(q)
        o_lo = _softmax2_pv([_qk(jnp.where(lo, q, zero), k)], [v])
        o_hi = _softmax2_pv([_qk(jnp.where(lo, zero, q), k)], [v])
        o_ref[:, sl] = jnp.where(lo, o_lo, o_hi).astype(o_ref.dtype)


def _na_ctx_attn(q, k, v, batch, seq, hd):
    d = q.shape[1]
    spec = pl.BlockSpec((seq, d), lambda b: (b, 0))
    return pl.pallas_call(
        functools.partial(_na_ctx_attn_kernel, hd=hd),
        grid=(batch,),
        in_specs=[spec, spec, spec],
        out_specs=spec,
        out_shape=jax.ShapeDtypeStruct((batch * seq, d), BF16),
        compiler_params=_cparams("arbitrary"),
        name="na_ctx_attn",
    )(q, k, v)


def _na_window(r0, q_rows, rows, kh):
    w_rows = min(rows, kh + q_rows + (kh + q_rows) % 2)
    rs = min(max(r0 - kh // 2, 0), rows - kh)
    return min(rs // 2 * 2, rows - w_rows), w_rows


def _na_lat_attn_kernel(q_ref, kl_ref, vl_ref, kc_ref, vc_ref, tab_ref, o_ref, kc_scr, vc_scr, bias_scr, *,
                        hd, rows, kh, tq):
    lane = lax.broadcasted_iota(jnp.int32, (1, LANES), 1)
    lo = lane < hd
    lower = lane < GRID_W
    q_rows = tq // GRID_W
    n_qb = q_ref.shape[0] // tq
    w_cols = bias_scr.shape[2]
    kc_scr[...] = kc_ref[...].astype(BF16)
    vc_scr[...] = vc_ref[...].astype(BF16)

    @pl.when(pl.program_id(1) == 0)
    def _():
        neg = jnp.full((GRID_W, LANES), -jnp.inf, F32)
        for hh in range(2):
            for rq in range(rows):
                w0, w_rows = _na_window(rq // q_rows * q_rows, q_rows, rows, kh)
                rs = min(max(rq - kh // 2, 0), rows - kh)
                for kp in range(w_rows // 2):
                    rk = w0 + 2 * kp
                    ok0, ok1 = rs <= rk < rs + kh, rs <= rk + 1 < rs + kh
                    tile = tab_ref[hh, rk - rq + rows - 1]
                    if not (ok0 and ok1):
                        tile = jnp.where(lower if ok0 else jnp.logical_not(lower), tile, neg) if (ok0 or ok1) else neg
                    bias_scr[hh, rq * GRID_W:(rq + 1) * GRID_W, kp * LANES:(kp + 1) * LANES] = tile

    def qblock(qb, carry):
        r0 = qb * q_rows
        w0 = jnp.minimum(jnp.clip(r0 - kh // 2, 0, rows - kh) // 2 * 2, rows - w_cols // GRID_W)
        k0 = pl.multiple_of(w0 * GRID_W, 2 * GRID_W)
        q0 = pl.multiple_of(qb * tq, tq)
        q = q_ref[pl.ds(q0, tq), :]
        kl = kl_ref[pl.ds(k0, w_cols), :]
        vl = vl_ref[pl.ds(k0, w_cols), :]
        zero = jnp.zeros_like(q)
        scores = []
        for hh in range(2):
            qm = jnp.where(lo, q, zero) if hh == 0 else jnp.where(lo, zero, q)
            scores.append([_qk(qm, kl) + bias_scr[hh, pl.ds(q0, tq), :], _qk(qm, kc_scr[...])])
        outs = [_softmax2_pv(s, [vl, vc_scr[...]]) for s in scores]
        o_ref[pl.ds(q0, tq), :] = jnp.where(lo, outs[0], outs[1]).astype(o_ref.dtype)
        return carry

    lax.fori_loop(0, n_qb, qblock, 0)


def _na_lat_attn(q, k, v, cache_k, cache_v, slot, tab, n_prompt, dec_batch, dec_seq, hd):
    d = q.shape[1]
    past = cache_k.shape[2]
    rows = dec_seq // GRID_W
    kh = min(NA_KH, rows)
    tq = _pow2_tile(256, dec_seq)
    assert tq % GRID_W == 0 and rows % 2 == 0 and 2 * GRID_W == LANES and n_prompt % dec_seq == 0
    b0 = n_prompt // dec_seq
    w_rows = _na_window(0, tq // GRID_W, rows, kh)[1]
    return pl.pallas_call(
        functools.partial(_na_lat_attn_kernel, hd=hd, rows=rows, kh=kh, tq=tq),
        grid=(d // LANES, dec_batch),
        in_specs=[pl.BlockSpec((dec_seq, LANES), lambda hp, b: (b0 + b, hp)),
                  pl.BlockSpec((dec_seq, LANES), lambda hp, b: (b, hp)),
                  pl.BlockSpec((dec_seq, LANES), lambda hp, b: (b, hp)),
                  pl.BlockSpec((None, None, past, LANES), lambda hp, b: (b, slot, 0, hp)),
                  pl.BlockSpec((None, None, past, LANES), lambda hp, b: (b, slot, 0, hp)),
                  pl.BlockSpec((2,) + tab.shape[1:], lambda hp, b: (hp, 0, 0, 0))],
        out_specs=pl.BlockSpec((dec_seq, LANES), lambda hp, b: (b, hp)),
        out_shape=jax.ShapeDtypeStruct((dec_batch * dec_seq, d), BF16),
        scratch_shapes=[pltpu.VMEM((past, LANES), BF16), pltpu.VMEM((past, LANES), BF16),
                        pltpu.VMEM((2, dec_seq, w_rows * GRID_W), F32)],
        compiler_params=_cparams("arbitrary", "arbitrary"),
        name="na_lat_attn",
    )(q, k, v, cache_k, cache_v, tab)


def _na_bias_table(rpb, dec_seq):
    rows = dec_seq // GRID_W
    kh = min(NA_KH, rows)
    nh, nd, nc = rpb.shape
    c = jnp.arange(GRID_W)
    cs = jnp.clip(c - NA_KW // 2, 0, GRID_W - NA_KW)
    col_ok = (c[None, :] >= cs[:, None]) & (c[None, :] < cs[:, None] + NA_KW)
    dcol = c[None, :] - c[:, None] + NA_KW - 1
    onehot = (dcol[None] == jnp.arange(nc)[:, None, None]).astype(F32)
    tmp = jnp.einsum('hdc,cxy->hdxy', rpb.astype(F32), onehot, precision=lax.Precision.HIGHEST)
    tmp = jnp.where(col_ok[None, None], tmp * LOG2E, -jnp.inf)
    n_e = 2 * rows - 2
    front = rows - kh
    back = n_e + 1 - nd - front
    pad = lambda n: jnp.full((nh, n, GRID_W, GRID_W), -jnp.inf, F32)
    ext = jnp.concatenate([pad(front), tmp, pad(back)], axis=1)
    return jnp.concatenate([ext[:, :-1], ext[:, 1:]], axis=-1)


def _gqa_qkv_kernel(x_ref, g_ref, sc_ref, sh_ref, w_ref, qn_ref, kn_ref, cos_ref, sin_ref,
                    q_ref, kp_ref, ks_ref, vp_ref, vs_ref, w_scr, *, nq, nk, np_tiles):
    i = pl.program_id(0)
    _cast_once(w_ref, w_scr)
    is_sample = i >= np_tiles
    ks, vs = [], []
    for rows in _row_halves(x_ref.shape[0]):
        h = _normmod(x_ref[rows, :], g_ref[...], sc_ref[...], sh_ref[...]).astype(BF16)
        z = jnp.dot(h, w_scr[...], preferred_element_type=F32)
        cos = cos_ref[rows, :]
        sin = sin_ref[rows, :]

        def norm_rope(seg, gain):
            ms = jnp.mean(seg * seg, axis=-1, keepdims=True)
            y = seg * lax.rsqrt(ms + EPS) * gain
            roped = y * cos + pltpu.roll(y, LANES // 2, 1) * sin
            return jnp.where(is_sample, roped, y)

        for hh in range(nq):
            sl = slice(hh * LANES, (hh + 1) * LANES)
            q_ref[rows, sl] = (norm_rope(z[:, sl], qn_ref[...]) * (LANES ** -0.5 * LOG2E)).astype(q_ref.dtype)
        ks.append(jnp.concatenate(
            [norm_rope(z[:, (nq + hh) * LANES:(nq + hh + 1) * LANES], kn_ref[...]) for hh in range(nk)], axis=1))
        vs.append(z[:, (nq + nk) * LANES:])
    k = jnp.concatenate(ks, axis=0)
    v = jnp.concatenate(vs, axis=0)

    @pl.when(i < np_tiles)
    def _():
        kp_ref[...] = k
        vp_ref[...] = v

    @pl.when(is_sample)
    def _():
        ks_ref[...] = k.astype(ks_ref.dtype)
        vs_ref[...] = v.astype(vs_ref.dtype)


def _gqa_qkv(x, g, mod, layer, w_qkv, slot, qn, kn, cos_t, sin_t, nk, tok):
    n, d = x.shape
    hd = qn.shape[-1]
    assert hd == LANES
    nq = d // hd
    dk = nk * hd
    row = lambda i: (i, 0)
    fixed = lambda i: (0, 0)
    pos = lambda i: (jnp.maximum(i - tok.np_tiles, 0) % tok.tps, 0)
    n_p, n_s = tok.np_tiles * tok.tm, n - tok.np_tiles * tok.tm
    return pl.pallas_call(
        functools.partial(_gqa_qkv_kernel, nq=nq, nk=nk, np_tiles=tok.np_tiles),
        grid=(tok.tiles,),
        in_specs=[pl.BlockSpec((tok.tm, d), row),
                  pl.BlockSpec((1, d), fixed),
                  tok.mod_spec(layer, 1, d, 1),
                  tok.mod_spec(layer, 0, d, 1),
                  _resident((None, d, d + 2 * dk), lambda i: (slot, 0, 0)),
                  pl.BlockSpec((1, hd), fixed),
                  pl.BlockSpec((1, hd), fixed),
                  pl.BlockSpec((tok.tm, hd), pos),
                  pl.BlockSpec((tok.tm, hd), pos)],
        out_specs=[pl.BlockSpec((tok.tm, d), row),
                   tok.prompt_spec(dk, 1), tok.sample_spec(dk, 1), tok.prompt_spec(dk, 1), tok.sample_spec(dk, 1)],
        out_shape=[jax.ShapeDtypeStruct((n, d), BF16),
                   jax.ShapeDtypeStruct((n_p, dk), F32), jax.ShapeDtypeStruct((n_s, dk), BF16),
                   jax.ShapeDtypeStruct((n_p, dk), F32), jax.ShapeDtypeStruct((n_s, dk), BF16)],
        scratch_shapes=[pltpu.VMEM((d, d + 2 * dk), BF16)],
        compiler_params=_cparams("arbitrary"),
        name="gqa_qkv",
    )(x, g, mod, mod, w_qkv, qn.reshape(1, hd), kn.reshape(1, hd), cos_t, sin_t)


def _rope_tables(dec_seq, hd):
    t = jnp.arange(dec_seq)
    row = (t // GRID_W).astype(F32)
    col = (t % GRID_W).astype(F32)
    half = hd // 2
    inv = ROPE_THETA ** (-jnp.arange(0, half, 2, dtype=F32) / half)
    ang = jnp.concatenate([row[:, None] * inv, col[:, None] * inv], axis=-1)
    cos, sin = jnp.cos(ang), jnp.sin(ang)
    return jnp.concatenate([cos, cos], axis=-1), jnp.concatenate([-sin, sin], axis=-1)


def _gqa_attn_kernel(q_ref, *refs, rep, nk, n_kv):
    k_refs, v_refs, o_ref = refs[:n_kv], refs[n_kv:2 * n_kv], refs[2 * n_kv]
    tq = q_ref.shape[0]
    scores, values = [], []
    for kv in range(nk):
        sl = slice(kv * LANES, (kv + 1) * LANES)
        qs = jnp.concatenate([q_ref[:, (kv * rep + r) * LANES:(kv * rep + r + 1) * LANES] for r in range(rep)], axis=0)
        scores.append([_qk(qs, r[:, sl].astype(BF16)) for r in k_refs])
        values.append([r[:, sl].astype(BF16) for r in v_refs])
    for kv in range(nk):
        o = _softmax2_pv(scores[kv], values[kv])
        for r in range(rep):
            o_ref[:, (kv * rep + r) * LANES:(kv * rep + r + 1) * LANES] = o[r * tq:(r + 1) * tq].astype(o_ref.dtype)


def _gqa_ctx_attn(q, k, v, batch, seq, nk):
    d = q.shape[1]
    dk = k.shape[1]
    rep = d // dk
    qspec = pl.BlockSpec((seq, d), lambda b: (b, 0))
    kspec = pl.BlockSpec((seq, dk), lambda b: (b, 0))
    return pl.pallas_call(
        functools.partial(_gqa_attn_kernel, rep=rep, nk=nk, n_kv=1),
        grid=(batch,),
        in_specs=[qspec, kspec, kspec],
        out_specs=qspec,
        out_shape=jax.ShapeDtypeStruct((batch * seq, d), BF16),
        compiler_params=_cparams("arbitrary"),
        name="gqa_ctx_attn",
    )(q, k, v)


def _gqa_lat_attn(q, k, v, cache_k, cache_v, slot, n_prompt, dec_batch, dec_seq, nk):
    d = q.shape[1]
    dk = k.shape[1]
    rep = d // dk
    past = cache_k.shape[2]
    tq = _pow2_tile(256, dec_seq, n_prompt)
    nqb = dec_seq // tq
    q0 = n_prompt // tq
    lspec = pl.BlockSpec((dec_seq, dk), lambda b, qb: (b, 0))
    cspec = pl.BlockSpec((None, None, past, dk), lambda b, qb: (b, slot, 0, 0))
    return pl.pallas_call(
        functools.partial(_gqa_attn_kernel, rep=rep, nk=nk, n_kv=2),
        grid=(dec_batch, nqb),
        in_specs=[pl.BlockSpec((tq, d), lambda b, qb: (q0 + b * nqb + qb, 0)),
                  lspec, cspec, lspec, cspec],
        out_specs=pl.BlockSpec((tq, d), lambda b, qb: (b * nqb + qb, 0)),
        out_shape=jax.ShapeDtypeStruct((dec_batch * dec_seq, d), BF16),
        compiler_params=_cparams("arbitrary", "arbitrary"),
        name="gqa_lat_attn",
    )(q, k, cache_k, v, cache_v)


def _proj_res_kernel(x_ref, op_ref, os_ref, w_ref, gate_ref, out_ref, w_scr, *, np_tiles):
    _cast_once(w_ref, w_scr)
    o = jnp.where(pl.program_id(0) < np_tiles, op_ref[...], os_ref[...])
    y = jnp.dot(o, w_scr[...], preferred_element_type=F32)
    out_ref[...] = x_ref[...] + gate_ref[...] * y


def _proj_res(x, o_p, o_s, w_o, slot, mod, layer, tok):
    n, d = x.shape
    row = lambda i: (i, 0)
    return pl.pallas_call(
        functools.partial(_proj_res_kernel, np_tiles=tok.np_tiles),
        grid=(tok.tiles,),
        in_specs=[pl.BlockSpec((tok.tm, d), row),
                  tok.prompt_spec(d, 1),
                  tok.sample_spec(d, 1),
                  _resident((None, d, d), lambda i: (slot, 0, 0)),
                  tok.mod_spec(layer, 2, d, 1)],
        out_specs=pl.BlockSpec((tok.tm, d), row),
        out_shape=jax.ShapeDtypeStruct((n, d), F32),
        scratch_shapes=[pltpu.VMEM((d, d), BF16)],
        compiler_params=_cparams("arbitrary"),
        name="attn_proj_res",
    )(x, o_p, o_s, w_o, mod)


def _mlp_kernel(x_ref, g_ref, sc_ref, sh_ref, gate_ref, w1_ref, w2_ref, *rest, np_tiles):
    out_refs, (h_scr, acc_scr) = rest[:-2], rest[-2:]
    i = pl.program_id(0)
    j = pl.program_id(1)
    last = j == pl.num_programs(1) - 1

    @pl.when(j == 0)
    def _():
        h_scr[...] = _normmod(x_ref[...], g_ref[...], sc_ref[...], sh_ref[...]).astype(BF16)
        acc_scr[...] = jnp.zeros_like(acc_scr)

    a = jnp.maximum(jnp.dot(h_scr[...], w1_ref[...].astype(BF16), preferred_element_type=F32), 0.0)
    acc_scr[...] += jnp.dot((a * a).astype(BF16), w2_ref[...].astype(BF16), preferred_element_type=F32)

    def result():
        return x_ref[...] + gate_ref[...] * acc_scr[...]

    if len(out_refs) == 1:
        @pl.when(last)
        def _():
            out_refs[0][...] = result()
    else:
        @pl.when(jnp.logical_and(last, i < np_tiles))
        def _():
            out_refs[0][...] = result()

        @pl.when(jnp.logical_and(last, i >= np_tiles))
        def _():
            out_refs[1][...] = result()


def _mlp(x, g, mod, layer, w1, w2, tok, split_out):
    n, d = x.shape
    f = w1.shape[2]
    tf = _pow2_tile(1024, f)
    row = lambda i, j: (i, 0)
    if split_out:
        n_p = tok.np_tiles * tok.tm
        out_specs = [tok.prompt_spec(d, 2), tok.sample_spec(d, 2)]
        out_shape = [jax.ShapeDtypeStruct((n_p, d), F32), jax.ShapeDtypeStruct((n - n_p, d), F32)]
    else:
        out_specs = pl.BlockSpec((tok.tm, d), row)
        out_shape = jax.ShapeDtypeStruct((n, d), F32)
    return pl.pallas_call(
        functools.partial(_mlp_kernel, np_tiles=tok.np_tiles),
        grid=(tok.tiles, f // tf),
        in_specs=[pl.BlockSpec((tok.tm, d), row),
                  pl.BlockSpec((1, d), lambda i, j: (0, 0)),
                  tok.mod_spec(layer, 4, d, 2),
                  tok.mod_spec(layer, 3, d, 2),
                  tok.mod_spec(layer, 5, d, 2),
                  pl.BlockSpec((None, d, tf), lambda i, j: (layer, 0, j)),
                  pl.BlockSpec((None, tf, d), lambda i, j: (layer, j, 0))],
        out_specs=out_specs,
        out_shape=out_shape,
        scratch_shapes=[pltpu.VMEM((tok.tm, d), BF16), pltpu.VMEM((tok.tm, d), F32)],
        compiler_params=_cparams("arbitrary", "arbitrary"),
        name="mlp",
    )(x, g, mod, mod, mod, w1, w2)


def _s5_tables(lam_re, lam_im, log_dt, b_re, b_im, c_re, c_im):
    t = S5_CHUNK
    g, p, k = b_re.shape[1:]
    dt = jnp.exp(log_dt.astype(F32))[:, :, None]
    lr, li = lam_re.astype(F32), lam_im.astype(F32)
    ar, ai = lr * dt, li * dt
    mag = jnp.exp(ar)
    abr, abi = mag * jnp.cos(ai), mag * jnp.sin(ai)
    nr, ni = abr - 1.0, abi
    den = lr * lr + li * li
    f_re = (nr * lr + ni * li) / den
    f_im = (ni * lr - nr * li) / den
    bbr = f_re[..., None] * b_re - f_im[..., None] * b_im
    bbi = f_re[..., None] * b_im + f_im[..., None] * b_re
    n = jnp.arange(t + 1, dtype=F32)[None, None, :, None]
    pm = jnp.exp(n * ar[:, :, None, :])
    pr, pi = pm * jnp.cos(n * ai[:, :, None, :]), pm * jnp.sin(n * ai[:, :, None, :])
    cr, ci = c_re.astype(F32), c_im.astype(F32)

    def summ(pw_r, pw_i, d):
        pw_r = pw_r.transpose(0, 2, 1)[:, :, :, None]
        pw_i = pw_i.transpose(0, 2, 1)[:, :, :, None]
        sr = pw_r * bbr[d][:, :, None, :] - pw_i * bbi[d][:, :, None, :]
        si = pw_r * bbi[d][:, :, None, :] + pw_i * bbr[d][:, :, None, :]
        return sr.reshape(g, p, t * k), si.reshape(g, p, t * k)

    sfr, sfi = summ(pr[0, :, :t][:, ::-1], pi[0, :, :t][:, ::-1], 0)
    sbr, sbi = summ(pr[1, :, :t], pi[1, :, :t], 1)
    w_intra, w_sum = _s5_toeplitz(jnp.concatenate([cr[0], -ci[0]], axis=-1),
                                  jnp.concatenate([cr[1], -ci[1]], axis=-1), sfr, sfi, sbr, sbi)
    car = cr[:, :, None] * pr[:, :, 1:, None, :] - ci[:, :, None] * pi[:, :, 1:, None, :]
    cai = cr[:, :, None] * pi[:, :, 1:, None, :] + ci[:, :, None] * pr[:, :, 1:, None, :]
    rd = lambda a: a.reshape(g, t * k, p)
    w2t = jnp.concatenate([rd(car[0]), rd(car[1][:, ::-1]), -rd(cai[0]), -rd(cai[1][:, ::-1])], axis=2)
    at = jnp.stack([jnp.concatenate([pr[0, :, t], pr[1, :, t]], axis=-1),
                    jnp.concatenate([pi[0, :, t], pi[1, :, t]], axis=-1)], axis=1)
    return w_intra, w_sum, w2t.astype(BF16), at


def _s5_toeplitz_kernel(cf_ref, cb_ref, sfr_ref, sfi_ref, sbr_ref, sbi_ref, o_ref, ws_ref):
    hi = lax.Precision.HIGHEST
    gl, k, _ = cf_ref.shape
    tk = sfr_ref.shape[2]
    for g in range(gl):
        abf = jnp.concatenate([sfr_ref[g], sfi_ref[g]], axis=0)
        abb = jnp.concatenate([sbr_ref[g], sbi_ref[g]], axis=0)
        ws_ref[g] = jnp.concatenate([sfr_ref[g], sbr_ref[g], sfi_ref[g], sbi_ref[g]], axis=0).astype(ws_ref.dtype)
        kf = jnp.dot(cf_ref[g], abf, precision=hi, preferred_element_type=F32)
        kb = jnp.dot(cb_ref[g], abb, precision=hi, preferred_element_type=F32)
        z = jnp.zeros_like(kf)
        krev = jnp.concatenate([kf, z], axis=1) + pltpu.roll(jnp.concatenate([z, kb], axis=1), 2 * tk - k, 1)
        for i in range(S5_CHUNK):
            sh = (S5_CHUNK - 1 - i) * k
            win = krev if sh == 0 else pltpu.roll(krev, 2 * tk - sh, 1)
            o_ref[g, i * k:(i + 1) * k, :] = win[:, 0:tk].astype(o_ref.dtype)


def _s5_toeplitz(cf, cb, sfr, sfi, sbr, sbi):
    g, k, p2 = cf.shape
    p, tk = sfr.shape[1:]
    gl = _pow2_tile(8, g)
    cspec = pl.BlockSpec((gl, k, p2), lambda a: (a, 0, 0))
    sspec = pl.BlockSpec((gl, p, tk), lambda a: (a, 0, 0))
    return pl.pallas_call(
        _s5_toeplitz_kernel,
        grid=(g // gl,),
        in_specs=[cspec, cspec, sspec, sspec, sspec, sspec],
        out_specs=[pl.BlockSpec((gl, tk, tk), lambda a: (a, 0, 0)), pl.BlockSpec((gl, 4 * p, tk), lambda a: (a, 0, 0))],
        out_shape=[jax.ShapeDtypeStruct((g, tk, tk), BF16), jax.ShapeDtypeStruct((g, 4 * p, tk), BF16)],
        compiler_params=_cparams("arbitrary"),
        name="s5_toeplitz",
    )(cf, cb, sfr, sfi, sbr, sbi)


def _s5_mixer(x, g, mod, layer, tok, params, h0, dims):
    (lam_re, lam_im, log_dt, b_re, b_im, c_re, c_im, d_skip, glu_w, slot, glu_b) = params
    batch, seq, dec_batch, dec_seq = dims
    n_prompt = batch * seq
    d = d_skip.shape[0]
    ngrp = d // S5_GROUP
    p = lam_re.shape[-1]
    w_intra, w_sum, w2t, at = _s5_tables(lam_re, lam_im, log_dt, b_re, b_im, c_re, c_im)
    hperm = _s5_pre(x, g, mod, layer, tok)
    h0g = h0.astype(F32).transpose(3, 2, 0, 1, 4).reshape(ngrp, 2, dec_batch, 2 * p)
    streams = ((0, seq // S5_CHUNK, batch, False, True),
               (n_prompt // S5_CHUNK, dec_seq // S5_CHUNK, dec_batch, True, False))
    yperm, fin = _s5_scan(hperm, w_intra, w_sum, w2t, at, h0g, batch, streams)
    x_new = _s5_post(x, yperm, g, mod, layer, d_skip.reshape(1, d), glu_w, slot, glu_b, tok)
    st = fin.reshape(ngrp, 2, batch, 2, p).transpose(2, 3, 1, 0, 4)
    return x_new, st


def kernel(x_prompt, x_sample, state_s5, cache_na_k, cache_na_v, cache_gqa_k, cache_gqa_v, c, c_ctx, norm_g, ada_w, ada_b, mlp_w1, mlp_w2, s5_lam_re, s5_lam_im, s5_log_dt, s5_b_re, s5_b_im, s5_c_re, s5_c_im, s5_d, s5_glu_w, s5_glu_b, na_w_qkv, na_q_norm, na_k_norm, na_rpb, na_w_o, gqa_w_qkv, gqa_q_norm, gqa_k_norm, gqa_w_o):
    batch, seq, d = x_prompt.shape
    dec_batch, dec_seq, _ = x_sample.shape
    depth = ada_w.shape[0]
    n_prompt = batch * seq
    n_sample = dec_batch * dec_seq
    na_heads, na_hd = cache_na_k.shape[3], cache_na_k.shape[4]
    gqa_kv, gqa_hd = cache_gqa_k.shape[3], cache_gqa_k.shape[4]
    assert n_prompt % dec_seq == 0

    tok = _Tok(n_prompt, n_sample, dec_seq, 1024)
    tok_half = _Tok(n_prompt, n_sample, dec_seq, 512)

    mod_rows = -(-(1 + dec_batch) // SUBLANES) * SUBLANES
    cvec = jnp.concatenate([c_ctx[None, :], c, jnp.zeros((mod_rows - 1 - dec_batch, d), F32)], axis=0)
    mod = _modulation(cvec, ada_w, ada_b).reshape(depth, mod_rows, 6, 1, d)

    x = (x_prompt.reshape(n_prompt, d), x_sample.reshape(n_sample, d))
    cache_na_k2 = cache_na_k.reshape(cache_na_k.shape[:3] + (na_heads * na_hd,))
    cache_na_v2 = cache_na_v.reshape(cache_na_v.shape[:3] + (na_heads * na_hd,))
    cache_gqa_k2 = cache_gqa_k.reshape(cache_gqa_k.shape[:3] + (gqa_kv * gqa_hd,))
    cache_gqa_v2 = cache_gqa_v.reshape(cache_gqa_v.shape[:3] + (gqa_kv * gqa_hd,))
    cos_t, sin_t = _rope_tables(dec_seq, gqa_hd)

    new_s5, new_na_k, new_na_v, new_gqa_k, new_gqa_v = [], [], [], [], []
    for i in range(depth):
        kind, slot = i % 3, i // 3
        g1 = norm_g[i, 0].reshape(1, d)
        g2 = norm_g[i, 1].reshape(1, d)
        if kind == 0:
            params = (s5_lam_re[slot], s5_lam_im[slot], s5_log_dt[slot], s5_b_re[slot], s5_b_im[slot],
                      s5_c_re[slot], s5_c_im[slot], s5_d[slot], s5_glu_w, slot, s5_glu_b[slot])
            x, st = _s5_mixer(x, g1, mod, i, tok_half, params, state_s5[:, slot], (batch, seq, dec_batch, dec_seq))
            new_s5.append(st)
        elif kind == 1:
            assert not isinstance(x, tuple)
            q, k_p, k_s, v_p, v_s = _na_qkv(x, g1, mod, i, na_w_qkv, slot, na_q_norm[slot], na_k_norm[slot], tok_half)
            o_p = _na_ctx_attn(q, k_p, v_p, batch, seq, na_hd)
            bias = _na_bias_table(na_rpb[slot], dec_seq)
            o_s = _na_lat_attn(q, k_s, v_s, cache_na_k2, cache_na_v2, slot, bias, n_prompt, dec_batch, dec_seq, na_hd)
            x = _proj_res(x, o_p, o_s, na_w_o, slot, mod, i, tok)
            new_na_k.append(k_p.reshape(batch, seq, na_heads, na_hd))
            new_na_v.append(v_p.reshape(batch, seq, na_heads, na_hd))
        else:
            assert not isinstance(x, tuple)
            q, k_p, k_s, v_p, v_s = _gqa_qkv(x, g1, mod, i, gqa_w_qkv, slot, gqa_q_norm[slot], gqa_k_norm[slot],
                                             cos_t, sin_t, gqa_kv, tok_half)
            o_p = _gqa_ctx_attn(q, k_p, v_p, batch, seq, gqa_kv)
            o_s = _gqa_lat_attn(q, k_s, v_s, cache_gqa_k2, cache_gqa_v2, slot, n_prompt, dec_batch, dec_seq, gqa_kv)
            x = _proj_res(x, o_p, o_s, gqa_w_o, slot, mod, i, tok)
            new_gqa_k.append(k_p.reshape(batch, seq, gqa_kv, gqa_hd))
            new_gqa_v.append(v_p.reshape(batch, seq, gqa_kv, gqa_hd))
        x = _mlp(x, g2, mod, i, mlp_w1, mlp_w2, tok, split_out=(i == depth - 1))
    y_p, y_s = x
    return (y_p.reshape(batch, seq, d), y_s.reshape(dec_batch, dec_seq, d),
            jnp.stack(new_s5, axis=1), jnp.stack(new_na_k, axis=1), jnp.stack(new_na_v, axis=1),
            jnp.stack(new_gqa_k, axis=1), jnp.stack(new_gqa_v, axis=1))
```

```python
import functools
import math

import jax
import jax.numpy as jnp
from jax import lax
from jax.experimental import pallas as pl
from jax.experimental.pallas import tpu as pltpu

F32 = jnp.float32
BF16 = jnp.bfloat16

EPS = 1e-6
GRID_W = 64
S5_GROUP = 16
NA_KH = 8
NA_KW = 16
ROPE_THETA = 10000.0
S5_CHUNK = 16
LOG2E = math.log2(math.e)
LANES = 128
SUBLANES = 8
VMEM_LIMIT = 56 * 1024 * 1024


def _cparams(*sem):
    return pltpu.CompilerParams(dimension_semantics=sem, vmem_limit_bytes=VMEM_LIMIT)


def _pow2_tile(pref, *ns):
    t = pref
    while any(n % t for n in ns):
        t //= 2
    return t


def _row_halves(n):
    if n % 2 or n // 2 < LANES:
        return [slice(0, n)]
    return [slice(0, n // 2), slice(n // 2, n)]


def _software_pipeline(parts, before, matmuls, after):
    prev = None
    for p in parts:
        z = matmuls(before(p))
        if prev is not None:
            after(*prev)
        prev = (p, z)
    after(*prev)


def _normmod(x, g, sc, sh):
    ms = jnp.mean(x * x, axis=-1, keepdims=True)
    y = x * lax.rsqrt(ms + EPS) * g
    return y * (1.0 + sc) + sh


class _Tok:
    def __init__(self, n_prompt, n_sample, dec_seq, pref):
        self.tm = _pow2_tile(pref, n_prompt, dec_seq)
        self.n = n_prompt + n_sample
        self.tiles = self.n // self.tm
        self.np_tiles = n_prompt // self.tm
        self.tps = dec_seq // self.tm

    def mod_row(self, i):
        return jnp.where(i < self.np_tiles, 0, 1 + (i - self.np_tiles) // self.tps)

    def prompt_spec(self, d, nargs):
        last = self.np_tiles - 1
        if nargs == 1:
            return pl.BlockSpec((self.tm, d), lambda i: (jnp.minimum(i, last), 0))
        return pl.BlockSpec((self.tm, d), lambda i, j: (jnp.minimum(i, last), 0))

    def sample_spec(self, d, nargs):
        first = self.np_tiles
        if nargs == 1:
            return pl.BlockSpec((self.tm, d), lambda i: (jnp.maximum(i - first, 0), 0))
        return pl.BlockSpec((self.tm, d), lambda i, j: (jnp.maximum(i - first, 0), 0))

    def mod_spec(self, layer, which, d, nargs):
        if nargs == 1:
            return pl.BlockSpec((None, None, None, 1, d), lambda i: (layer, self.mod_row(i), which, 0, 0))
        return pl.BlockSpec((None, None, None, 1, d), lambda i, j: (layer, self.mod_row(i), which, 0, 0))


def _mod_kernel(c_ref, w_ref, b_ref, o_ref):
    c = c_ref[...]
    s = c * jax.nn.sigmoid(c)
    o_ref[...] = jnp.dot(s.astype(BF16), w_ref[...].astype(BF16), preferred_element_type=F32) + b_ref[...]


def _modulation(cvec, ada_w, ada_b):
    depth, d, d6 = ada_w.shape
    rows = cvec.shape[0]
    tn = _pow2_tile(2048, d6) if d6 % 2048 == 0 else d
    return pl.pallas_call(
        _mod_kernel,
        grid=(depth, d6 // tn),
        in_specs=[pl.BlockSpec((rows, d), lambda l, j: (0, 0)),
                  pl.BlockSpec((None, d, tn), lambda l, j: (l, 0, j)),
                  pl.BlockSpec((None, 1, tn), lambda l, j: (l, 0, j))],
        out_specs=pl.BlockSpec((None, rows, tn), lambda l, j: (l, 0, j)),
        out_shape=jax.ShapeDtypeStruct((depth, rows, d6), F32),
        compiler_params=_cparams("arbitrary", "arbitrary"),
        name="adaln_modulation",
    )(cvec, ada_w, ada_b.reshape(depth, 1, d6))


def _read_x(x_refs, np_tiles, rows=slice(None), cols=slice(None)):
    if len(x_refs) == 1:
        return x_refs[0][rows, cols]
    return jnp.where(pl.program_id(0) < np_tiles, x_refs[0][rows, cols], x_refs[1][rows, cols])


def _x_specs(x, tok, nargs):
    if isinstance(x, tuple):
        d = x[0].shape[1]
        return [tok.prompt_spec(d, nargs), tok.sample_spec(d, nargs)], x
    d = x.shape[1]
    return [pl.BlockSpec((tok.tm, d), (lambda i: (i, 0)) if nargs == 1 else (lambda i, j: (i, 0)))], (x,)


def _s5_pre_kernel(*refs, n_x, np_tiles):
    x_refs, (g_ref, sc_ref, sh_ref, h_ref, h_scr) = refs[:n_x], refs[n_x:]
    x = _read_x(x_refs, np_tiles)
    rstd = lax.rsqrt(jnp.mean(x * x, axis=-1, keepdims=True) + EPS)
    nc = h_ref.shape[1]
    for a in range(h_scr.shape[0]):
        sl = slice(a * LANES, (a + 1) * LANES)
        h_scr[a] = (x[:, sl] * rstd * g_ref[:, sl]) * (1.0 + sc_ref[:, sl]) + sh_ref[:, sl]
        for j in range(S5_CHUNK):
            h_ref[j, :, sl] = h_scr[a, pl.ds(j, nc, stride=S5_CHUNK), :].astype(h_ref.dtype)


def _s5_pre(x, g, mod, layer, tok):
    x_specs, xs = _x_specs(x, tok, 1)
    d = xs[0].shape[1]
    nc = tok.tm // S5_CHUNK
    return pl.pallas_call(
        functools.partial(_s5_pre_kernel, n_x=len(xs), np_tiles=tok.np_tiles),
        grid=(tok.tiles,),
        in_specs=x_specs + [pl.BlockSpec((1, d), lambda i: (0, 0)),
                            tok.mod_spec(layer, 1, d, 1),
                            tok.mod_spec(layer, 0, d, 1)],
        out_specs=pl.BlockSpec((S5_CHUNK, nc, d), lambda i: (0, i, 0)),
        out_shape=jax.ShapeDtypeStruct((S5_CHUNK, tok.n // S5_CHUNK, d), BF16),
        scratch_shapes=[pltpu.VMEM((d // LANES, tok.tm, LANES), F32)],
        compiler_params=_cparams("arbitrary"),
        name="s5_pre",
    )(*xs, g, mod, mod)


def _s5_scan_kernel(h_ref, wi_ref, ws_ref, w2t_ref, at_ref, h0_ref, *rest, streams, unroll):
    ns = len(streams)
    perm_refs, (y_ref, fin_ref, xt_scr, yt_scr), chain_scr = rest[:ns], rest[ns:ns + 4], rest[ns + 4:]
    ngrp = wi_ref.shape[0]
    p2 = at_ref.shape[-1]
    fwd = lax.broadcasted_iota(jnp.int32, (1, p2), 1) < (p2 // 2)
    for j in range(S5_CHUNK):
        xt_scr[j] = h_ref[j].astype(F32).T.astype(BF16)

    def load_rmat(r, stream):
        row0, n_chunks, nb = stream[:3]
        cols = slice(row0, row0 + n_chunks * nb)
        c0 = pl.multiple_of(r * S5_GROUP, S5_GROUP)
        return jnp.concatenate([xt_scr[j, pl.ds(c0, S5_GROUP), cols] for j in range(S5_CHUNK)], axis=0), c0, cols

    def summaries(r, stream, perm_ref, s_scr):
        nb, use_h0 = stream[2], stream[3]
        rmat, _, _ = load_rmat(r, stream)
        rmat_cm = jnp.dot(rmat, perm_ref[...], preferred_element_type=F32).astype(BF16)
        zs = jnp.dot(ws_ref[r], rmat_cm, preferred_element_type=F32)
        s_scr[0] = zs[0:p2].T
        s_scr[1] = zs[p2:].T
        if use_h0:
            return h0_ref[r, 0], h0_ref[r, 1]
        return jnp.zeros((nb, p2), F32), jnp.zeros((nb, p2), F32)

    def scan_step(t, decay, stream, state, s_scr, e_scr):
        n_chunks, nb = stream[1], stream[2]
        hr, hi = state
        ar, ai = decay
        ft = slice(t * nb, (t + 1) * nb)
        bt = slice((n_chunks - 1 - t) * nb, (n_chunks - t) * nb)
        e_scr[0, ft, :] = hr
        e_scr[1, bt, :] = hr
        e_scr[2, ft, :] = hi
        e_scr[3, bt, :] = hi
        in_re = jnp.where(fwd, s_scr[0, ft, :], s_scr[0, bt, :])
        in_im = jnp.where(fwd, s_scr[1, ft, :], s_scr[1, bt, :])
        return ar * hr - ai * hi + in_re, ar * hi + ai * hr + in_im

    def outputs(r, stream, perm_ref, state, e_scr):
        rmat, c0, cols = load_rmat(r, stream)
        if stream[4]:
            fin_ref[r, 0] = state[0]
            fin_ref[r, 1] = state[1]
        e_cm = jnp.concatenate([jnp.where(fwd, e_scr[0], e_scr[1]), jnp.where(fwd, e_scr[2], e_scr[3])],
                               axis=1).astype(BF16)
        e = jnp.dot(perm_ref[...], e_cm, preferred_element_type=F32).astype(BF16)
        yt = (jnp.dot(wi_ref[r], rmat, preferred_element_type=F32)
              + lax.dot_general(w2t_ref[r], e, (((1,), (1,)), ((), ())), preferred_element_type=F32))
        for i in range(S5_CHUNK):
            yt_scr[i, pl.ds(c0, S5_GROUP), cols] = yt[i * S5_GROUP:(i + 1) * S5_GROUP]

    def body(rb, carry):
        chains = []
        for u in range(unroll):
            for si, stream in enumerate(streams):
                k = 2 * (u * ns + si)
                chains.append((rb * unroll + u, stream, perm_refs[si], chain_scr[k], chain_scr[k + 1]))
        states = [summaries(r, stream, pm, s_scr) for r, stream, pm, s_scr, _ in chains]
        decays = [(at_ref[r, 0:1, :], at_ref[r, 1:2, :]) for r, _, _, _, _ in chains]
        for t in range(max(stream[1] for stream in streams)):
            for ci, (_, stream, _, s_scr, e_scr) in enumerate(chains):
                if t < stream[1]:
                    states[ci] = scan_step(t, decays[ci], stream, states[ci], s_scr, e_scr)
        for (r, stream, pm, _, e_scr), state in zip(chains, states):
            outputs(r, stream, pm, state, e_scr)
        return carry

    lax.fori_loop(0, ngrp // unroll, body, 0)
    for i in range(S5_CHUNK):
        y_ref[i] = yt_scr[i].T


def _s5_scan(hperm, w_intra, w_sum, w2t, at, h0, nb_fin, streams):
    t, nrows, d = hperm.shape
    g, tk, p4 = w2t.shape
    p2 = at.shape[-1]
    gl = LANES // S5_GROUP
    nb0 = h0.shape[2]
    unroll = 4
    assert sum(s[1] * s[2] for s in streams) == nrows and gl % unroll == 0
    chain_scr = []
    for _ in range(unroll):
        for s in streams:
            chain_scr += [pltpu.VMEM((2, s[1] * s[2], p2), F32), pltpu.VMEM((4, s[1] * s[2], p2), F32)]
    perms = []
    for _, n_chunks, nb, _, _ in streams:
        m = jnp.arange(n_chunks * nb)
        perms.append((m[:, None] == ((m % nb) * n_chunks + m // nb)[None, :]).astype(BF16))
    return pl.pallas_call(
        functools.partial(_s5_scan_kernel, streams=streams, unroll=unroll),
        grid=(d // LANES,),
        in_specs=[pl.BlockSpec((t, nrows, LANES), lambda a: (0, 0, a)),
                  pl.BlockSpec((gl, tk, tk), lambda a: (a, 0, 0)),
                  pl.BlockSpec((gl, p4, tk), lambda a: (a, 0, 0)),
                  pl.BlockSpec((gl, tk, p4), lambda a: (a, 0, 0)),
                  pl.BlockSpec((gl, 2, p2), lambda a: (a, 0, 0)),
                  pl.BlockSpec((gl, 2, nb0, p2), lambda a: (a, 0, 0, 0))]
                 + [pl.BlockSpec(pm.shape, lambda a: (0, 0)) for pm in perms],
        out_specs=[pl.BlockSpec((t, nrows, LANES), lambda a: (0, 0, a)),
                   pl.BlockSpec((gl, 2, nb_fin, p2), lambda a: (a, 0, 0, 0))],
        out_shape=[jax.ShapeDtypeStruct((t, nrows, d), F32),
                   jax.ShapeDtypeStruct((g, 2, nb_fin, p2), F32)],
        scratch_shapes=[pltpu.VMEM((t, LANES, nrows), BF16), pltpu.VMEM((t, LANES, nrows), F32)] + chain_scr,
        compiler_params=_cparams("arbitrary"),
        name="s5_scan",
    )(hperm, w_intra, w_sum, w2t, at, h0, *perms)


def _s5_post_kernel(*refs, n_x, np_tiles):
    x_refs = refs[:n_x]
    (y_ref, g_ref, sc_ref, sh_ref, gate_ref, dsk_ref, wa_ref, wg_ref, ba_ref, bg_ref, o_ref,
     y_scr, wa_scr, wg_scr) = refs[n_x:]
    _cast_once(wa_ref, wa_scr)
    _cast_once(wg_ref, wg_scr)
    tm, d = o_ref.shape
    n_cb = max(d // (2 * LANES), 1)
    cw = d // n_cb

    def before(rows):
        c0, nc = rows.start // S5_CHUNK, (rows.stop - rows.start) // S5_CHUNK
        for a in range(y_scr.shape[0]):
            for j in range(S5_CHUNK):
                y_scr[a, pl.ds(rows.start + j, nc, stride=S5_CHUNK), :] = y_ref[j, c0:c0 + nc, a * LANES:(a + 1) * LANES]
        x = _read_x(x_refs, np_tiles, rows)
        h = _normmod(x, g_ref[...], sc_ref[...], sh_ref[...])
        y = h * dsk_ref[...] + jnp.concatenate([y_scr[a, rows, :] for a in range(y_scr.shape[0])], axis=1)
        return jax.nn.gelu(y).astype(BF16)

    def matmuls(a, cb):
        cols = slice(cb * cw, (cb + 1) * cw)
        return (jnp.dot(a, wa_scr[:, cols], preferred_element_type=F32),
                jnp.dot(a, wg_scr[:, cols], preferred_element_type=F32))

    def after(rows, cb, z):
        cols = slice(cb * cw, (cb + 1) * cw)
        za, zg = z[0] + ba_ref[:, cols], z[1] + bg_ref[:, cols]
        x = _read_x(x_refs, np_tiles, rows, cols)
        o_ref[rows, cols] = x + gate_ref[:, cols] * (za * jax.nn.sigmoid(zg))

    def row_chunks(rows):
        step = max((rows.stop - rows.start) // n_cb, S5_CHUNK)
        return [slice(r, min(r + step, rows.stop)) for r in range(rows.start, rows.stop, step)]

    halves = _row_halves(tm)
    a_cur = jnp.concatenate([before(rc) for rc in row_chunks(halves[0])], axis=0)
    z_prev = None
    for hi, rows in enumerate(halves):
        nxt = row_chunks(halves[hi + 1]) if hi + 1 < len(halves) else []
        a_next, z_cur = [], []
        for cb in range(n_cb):
            z_cur.append(matmuls(a_cur, cb))
            if cb < len(nxt):
                a_next.append(before(nxt[cb]))
            if z_prev is not None:
                after(halves[hi - 1], cb, z_prev[cb])
        a_next += [before(rc) for rc in nxt[n_cb:]]
        z_prev = z_cur
        if a_next:
            a_cur = jnp.concatenate(a_next, axis=0)
    for cb in range(n_cb):
        after(halves[-1], cb, z_prev[cb])


def _s5_post(x, y, g, mod, layer, dsk, glu_w, slot, glu_b, tok):
    x_specs, xs = _x_specs(x, tok, 1)
    n, d = tok.n, xs[0].shape[1]
    row = lambda i: (i, 0)
    fixed = lambda i: (0, 0)
    gb2 = glu_b.reshape(1, 2 * d)
    return pl.pallas_call(
        functools.partial(_s5_post_kernel, n_x=len(xs), np_tiles=tok.np_tiles),
        grid=(tok.tiles,),
        in_specs=x_specs + [
                  pl.BlockSpec((S5_CHUNK, tok.tm // S5_CHUNK, d), lambda i: (0, i, 0)),
                  pl.BlockSpec((1, d), fixed),
                  tok.mod_spec(layer, 1, d, 1),
                  tok.mod_spec(layer, 0, d, 1),
                  tok.mod_spec(layer, 2, d, 1),
                  pl.BlockSpec((1, d), fixed),
                  _resident((None, d, d), lambda i: (slot, 0, 0)),
                  _resident((None, d, d), lambda i: (slot, 0, 1)),
                  pl.BlockSpec((1, d), lambda i: (0, 0)),
                  pl.BlockSpec((1, d), lambda i: (0, 1))],
        out_specs=pl.BlockSpec((tok.tm, d), row),
        out_shape=jax.ShapeDtypeStruct((n, d), F32),
        scratch_shapes=[pltpu.VMEM((d // LANES, tok.tm, LANES), F32),
                        pltpu.VMEM((d, d), BF16), pltpu.VMEM((d, d), BF16)],
        compiler_params=_cparams("arbitrary"),
        name="s5_post",
    )(*xs, y, g, mod, mod, mod, dsk, glu_w, glu_w, gb2, gb2)


def _cast_once(w_ref, w_scr):
    @pl.when(pl.program_id(0) == 0)
    def _():
        w_scr[...] = w_ref[...].astype(w_scr.dtype)


def _resident(block_shape, index_map):
    return pl.BlockSpec(block_shape, index_map, pipeline_mode=pl.Buffered(1))


def _na_qkv_kernel(x_ref, g_ref, sc_ref, sh_ref, w_ref, qn_ref, kn_ref, q_ref, kp_ref, ks_ref, vp_ref, vs_ref, w_scr, *,
                   hd, np_tiles):
    i = pl.program_id(0)
    _cast_once(w_ref, w_scr)
    h = _normmod(x_ref[...], g_ref[...], sc_ref[...], sh_ref[...]).astype(BF16)
    d = h.shape[1]
    lo = lax.broadcasted_iota(jnp.int32, (1, LANES), 1) < hd

    def head_norm(z, gain):
        outs = []
        for s in range(d // LANES):
            seg = z[:, s * LANES:(s + 1) * LANES]
            sq = seg * seg
            s_lo = jnp.sum(jnp.where(lo, sq, 0.0), axis=-1, keepdims=True)
            s_hi = jnp.sum(jnp.where(lo, 0.0, sq), axis=-1, keepdims=True)
            ms = jnp.where(lo, s_lo, s_hi) / hd
            outs.append(seg * lax.rsqrt(ms + EPS) * gain)
        return jnp.concatenate(outs, axis=1)

    proj = lambda part: jnp.dot(h, w_scr[:, part * d:(part + 1) * d], preferred_element_type=F32)
    q_ref[...] = (head_norm(proj(0), qn_ref[...]) * (hd ** -0.5 * LOG2E)).astype(q_ref.dtype)
    k = head_norm(proj(1), kn_ref[...])
    v = proj(2)

    @pl.when(i < np_tiles)
    def _():
        kp_ref[...] = k
        vp_ref[...] = v

    @pl.when(i >= np_tiles)
    def _():
        ks_ref[...] = k.astype(ks_ref.dtype)
        vs_ref[...] = v.astype(vs_ref.dtype)


def _na_qkv(x, g, mod, layer, w_qkv, slot, qn, kn, tok):
    n, d = x.shape
    hd = qn.shape[-1]
    assert 2 * hd == LANES
    row = lambda i: (i, 0)
    fixed = lambda i: (0, 0)
    qn2 = jnp.tile(qn, 2).reshape(1, LANES)
    kn2 = jnp.tile(kn, 2).reshape(1, LANES)
    n_p, n_s = tok.np_tiles * tok.tm, n - tok.np_tiles * tok.tm
    return pl.pallas_call(
        functools.partial(_na_qkv_kernel, hd=hd, np_tiles=tok.np_tiles),
        grid=(tok.tiles,),
        in_specs=[pl.BlockSpec((tok.tm, d), row),
                  pl.BlockSpec((1, d), fixed),
                  tok.mod_spec(layer, 1, d, 1),
                  tok.mod_spec(layer, 0, d, 1),
                  _resident((None, d, 3 * d), lambda i: (slot, 0, 0)),
                  pl.BlockSpec((1, LANES), fixed),
                  pl.BlockSpec((1, LANES), fixed)],
        out_specs=[pl.BlockSpec((tok.tm, d), row),
                   tok.prompt_spec(d, 1), tok.sample_spec(d, 1), tok.prompt_spec(d, 1), tok.sample_spec(d, 1)],
        out_shape=[jax.ShapeDtypeStruct((n, d), BF16),
                   jax.ShapeDtypeStruct((n_p, d), F32), jax.ShapeDtypeStruct((n_s, d), BF16),
                   jax.ShapeDtypeStruct((n_p, d), F32), jax.ShapeDtypeStruct((n_s, d), BF16)],
        scratch_shapes=[pltpu.VMEM((d, 3 * d), BF16)],
        compiler_params=_cparams("arbitrary"),
        name="na_qkv",
    )(x, g, mod, mod, w_qkv, qn2, kn2)


def _softmax2_pv(s_parts, v_parts):
    m = s_parts[0].max(axis=-1, keepdims=True)
    for s in s_parts[1:]:
        m = jnp.maximum(m, s.max(axis=-1, keepdims=True))
    den = 0.0
    acc = 0.0
    for s, v in zip(s_parts, v_parts):
        e = jnp.exp2(s - m)
        den = den + e.sum(axis=-1, keepdims=True)
        acc = acc + jnp.dot(e.astype(BF16), v, preferred_element_type=F32)
    return acc / den


def _qk(q, k):
    return lax.dot_general(q, k, (((1,), (1,)), ((), ())), preferred_element_type=F32)


def _na_ctx_attn_kernel(q_ref, k_ref, v_ref, o_ref, *, hd):
    lo = lax.broadcasted_iota(jnp.int32, (1, LANES), 1) < hd
    for s in range(q_ref.shape[1] // LANES):
        sl = slice(s * LANES, (s + 1) * LANES)
        q = q_ref[:, sl]
        k = k_ref[:, sl].astype(BF16)
        v = v_ref[:, sl].astype(BF16)
        zero = jnp.zeros_like(q)
        o_lo = _softmax2_pv([_qk(jnp.where(lo, q, zero), k)], [v])
        o_hi = _softmax2_pv([_qk(jnp.where(lo, zero, q), k)], [v])
        o_ref[:, sl] = jnp.where(lo, o_lo, o_hi).astype(o_ref.dtype)


def _na_ctx_attn(q, k, v, batch, seq, hd):
    d = q.shape[1]
    spec = pl.BlockSpec((seq, d), lambda b: (b, 0))
    return pl.pallas_call(
        functools.partial(_na_ctx_attn_kernel, hd=hd),
        grid=(batch,),
        in_specs=[spec, spec, spec],
        out_specs=spec,
        out_shape=jax.ShapeDtypeStruct((batch * seq, d), BF16),
        compiler_params=_cparams("arbitrary"),
        name="na_ctx_attn",
    )(q, k, v)


def _na_window(r0, q_rows, rows, kh):
    w_rows = min(rows, kh + q_rows + (kh + q_rows) % 2)
    rs = min(max(r0 - kh // 2, 0), rows - kh)
    return min(rs // 2 * 2, rows - w_rows), w_rows


def _na_lat_attn_kernel(q_ref, kl_ref, vl_ref, kc_ref, vc_ref, tab_ref, o_ref, kc_scr, vc_scr, bias_scr, *,
                        hd, rows, kh, tq):
    lane = lax.broadcasted_iota(jnp.int32, (1, LANES), 1)
    lo = lane < hd
    lower = lane < GRID_W
    q_rows = tq // GRID_W
    n_qb = q_ref.shape[0] // tq
    w_cols = bias_scr.shape[2]
    kc_scr[...] = kc_ref[...].astype(BF16)
    vc_scr[...] = vc_ref[...].astype(BF16)

    @pl.when(pl.program_id(1) == 0)
    def _():
        neg = jnp.full((GRID_W, LANES), -jnp.inf, F32)
        for hh in range(2):
            for rq in range(rows):
                w0, w_rows = _na_window(rq // q_rows * q_rows, q_rows, rows, kh)
                rs = min(max(rq - kh // 2, 0), rows - kh)
                for kp in range(w_rows // 2):
                    rk = w0 + 2 * kp
                    ok0, ok1 = rs <= rk < rs + kh, rs <= rk + 1 < rs + kh
                    tile = tab_ref[hh, rk - rq + rows - 1]
                    if not (ok0 and ok1):
                        tile = jnp.where(lower if ok0 else jnp.logical_not(lower), tile, neg) if (ok0 or ok1) else neg
                    bias_scr[hh, rq * GRID_W:(rq + 1) * GRID_W, kp * LANES:(kp + 1) * LANES] = tile

    def qblock(qb, carry):
        r0 = qb * q_rows
        w0 = jnp.minimum(jnp.clip(r0 - kh // 2, 0, rows - kh) // 2 * 2, rows - w_cols // GRID_W)
        k0 = pl.multiple_of(w0 * GRID_W, 2 * GRID_W)
        q0 = pl.multiple_of(qb * tq, tq)
        q = q_ref[pl.ds(q0, tq), :]
        kl = kl_ref[pl.ds(k0, w_cols), :]
        vl = vl_ref[pl.ds(k0, w_cols), :]
        zero = jnp.zeros_like(q)
        scores = []
        for hh in range(2):
            qm = jnp.where(lo, q, zero) if hh == 0 else jnp.where(lo, zero, q)
            scores.append([_qk(qm, kl) + bias_scr[hh, pl.ds(q0, tq), :], _qk(qm, kc_scr[...])])
        outs = [_softmax2_pv(s, [vl, vc_scr[...]]) for s in scores]
        o_ref[pl.ds(q0, tq), :] = jnp.where(lo, outs[0], outs[1]).astype(o_ref.dtype)
        return carry

    lax.fori_loop(0, n_qb, qblock, 0)


def _na_lat_attn(q, k, v, cache_k, cache_v, slot, tab, n_prompt, dec_batch, dec_seq, hd):
    d = q.shape[1]
    past = cache_k.shape[2]
    rows = dec_seq // GRID_W
    kh = min(NA_KH, rows)
    tq = _pow2_tile(256, dec_seq)
    assert tq % GRID_W == 0 and rows % 2 == 0 and 2 * GRID_W == LANES and n_prompt % dec_seq == 0
    b0 = n_prompt // dec_seq
    w_rows = _na_window(0, tq // GRID_W, rows, kh)[1]
    return pl.pallas_call(
        functools.partial(_na_lat_attn_kernel, hd=hd, rows=rows, kh=kh, tq=tq),
        grid=(d // LANES, dec_batch),
        in_specs=[pl.BlockSpec((dec_seq, LANES), lambda hp, b: (b0 + b, hp)),
                  pl.BlockSpec((dec_seq, LANES), lambda hp, b: (b, hp)),
                  pl.BlockSpec((dec_seq, LANES), lambda hp, b: (b, hp)),
                  pl.BlockSpec((None, None, past, LANES), lambda hp, b: (b, slot, 0, hp)),
                  pl.BlockSpec((None, None, past, LANES), lambda hp, b: (b, slot, 0, hp)),
                  pl.BlockSpec((2,) + tab.shape[1:], lambda hp, b: (hp, 0, 0, 0))],
        out_specs=pl.BlockSpec((dec_seq, LANES), lambda hp, b: (b, hp)),
        out_shape=jax.ShapeDtypeStruct((dec_batch * dec_seq, d), BF16),
        scratch_shapes=[pltpu.VMEM((past, LANES), BF16), pltpu.VMEM((past, LANES), BF16),
                        pltpu.VMEM((2, dec_seq, w_rows * GRID_W), F32)],
        compiler_params=_cparams("arbitrary", "arbitrary"),
        name="na_lat_attn",
    )(q, k, v, cache_k, cache_v, tab)


def _na_bias_table(rpb, dec_seq):
    rows = dec_seq // GRID_W
    kh = min(NA_KH, rows)
    nh, nd, nc = rpb.shape
    c = jnp.arange(GRID_W)
    cs = jnp.clip(c - NA_KW // 2, 0, GRID_W - NA_KW)
    col_ok = (c[None, :] >= cs[:, None]) & (c[None, :] < cs[:, None] + NA_KW)
    dcol = c[None, :] - c[:, None] + NA_KW - 1
    onehot = (dcol[None] == jnp.arange(nc)[:, None, None]).astype(F32)
    tmp = jnp.einsum('hdc,cxy->hdxy', rpb.astype(F32), onehot, precision=lax.Precision.HIGHEST)
    tmp = jnp.where(col_ok[None, None], tmp * LOG2E, -jnp.inf)
    n_e = 2 * rows - 2
    front = rows - kh
    back = n_e + 1 - nd - front
    pad = lambda n: jnp.full((nh, n, GRID_W, GRID_W), -jnp.inf, F32)
    ext = jnp.concatenate([pad(front), tmp, pad(back)], axis=1)
    return jnp.concatenate([ext[:, :-1], ext[:, 1:]], axis=-1)


def _gqa_qkv_kernel(x_ref, g_ref, sc_ref, sh_ref, w_ref, qn_ref, kn_ref, cos_ref, sin_ref,
                    q_ref, kp_ref, ks_ref, vp_ref, vs_ref, w_scr, *, nq, nk, np_tiles):
    i = pl.program_id(0)
    _cast_once(w_ref, w_scr)
    is_sample = i >= np_tiles
    ks, vs = [], []
    for rows in _row_halves(x_ref.shape[0]):
        h = _normmod(x_ref[rows, :], g_ref[...], sc_ref[...], sh_ref[...]).astype(BF16)
        z = jnp.dot(h, w_scr[...], preferred_element_type=F32)
        cos = cos_ref[rows, :]
        sin = sin_ref[rows, :]

        def norm_rope(seg, gain):
            ms = jnp.mean(seg * seg, axis=-1, keepdims=True)
            y = seg * lax.rsqrt(ms + EPS) * gain
            roped = y * cos + pltpu.roll(y, LANES // 2, 1) * sin
            return jnp.where(is_sample, roped, y)

        for hh in range(nq):
            sl = slice(hh * LANES, (hh + 1) * LANES)
            q_ref[rows, sl] = (norm_rope(z[:, sl], qn_ref[...]) * (LANES ** -0.5 * LOG2E)).astype(q_ref.dtype)
        ks.append(jnp.concatenate(
            [norm_rope(z[:, (nq + hh) * LANES:(nq + hh + 1) * LANES], kn_ref[...]) for hh in range(nk)], axis=1))
        vs.append(z[:, (nq + nk) * LANES:])
    k = jnp.concatenate(ks, axis=0)
    v = jnp.concatenate(vs, axis=0)

    @pl.when(i < np_tiles)
    def _():
        kp_ref[...] = k
        vp_ref[...] = v

    @pl.when(is_sample)
    def _():
        ks_ref[...] = k.astype(ks_ref.dtype)
        vs_ref[...] = v.astype(vs_ref.dtype)


def _gqa_qkv(x, g, mod, layer, w_qkv, slot, qn, kn, cos_t, sin_t, nk, tok):
    n, d = x.shape
    hd = qn.shape[-1]
    assert hd == LANES
    nq = d // hd
    dk = nk * hd
    row = lambda i: (i, 0)
    fixed = lambda i: (0, 0)
    pos = lambda i: (jnp.maximum(i - tok.np_tiles, 0) % tok.tps, 0)
    n_p, n_s = tok.np_tiles * tok.tm, n - tok.np_tiles * tok.tm
    return pl.pallas_call(
        functools.partial(_gqa_qkv_kernel, nq=nq, nk=nk, np_tiles=tok.np_tiles),
        grid=(tok.tiles,),
        in_specs=[pl.BlockSpec((tok.tm, d), row),
                  pl.BlockSpec((1, d), fixed),
                  tok.mod_spec(layer, 1, d, 1),
                  tok.mod_spec(layer, 0, d, 1),
                  _resident((None, d, d + 2 * dk), lambda i: (slot, 0, 0)),
                  pl.BlockSpec((1, hd), fixed),
                  pl.BlockSpec((1, hd), fixed),
                  pl.BlockSpec((tok.tm, hd), pos),
                  pl.BlockSpec((tok.tm, hd), pos)],
        out_specs=[pl.BlockSpec((tok.tm, d), row),
                   tok.prompt_spec(dk, 1), tok.sample_spec(dk, 1), tok.prompt_spec(dk, 1), tok.sample_spec(dk, 1)],
        out_shape=[jax.ShapeDtypeStruct((n, d), BF16),
                   jax.ShapeDtypeStruct((n_p, dk), F32), jax.ShapeDtypeStruct((n_s, dk), BF16),
                   jax.ShapeDtypeStruct((n_p, dk), F32), jax.ShapeDtypeStruct((n_s, dk), BF16)],
        scratch_shapes=[pltpu.VMEM((d, d + 2 * dk), BF16)],
        compiler_params=_cparams("arbitrary"),
        name="gqa_qkv",
    )(x, g, mod, mod, w_qkv, qn.reshape(1, hd), kn.reshape(1, hd), cos_t, sin_t)


def _rope_tables(dec_seq, hd):
    t = jnp.arange(dec_seq)
    row = (t // GRID_W).astype(F32)
    col = (t % GRID_W).astype(F32)
    half = hd // 2
    inv = ROPE_THETA ** (-jnp.arange(0, half, 2, dtype=F32) / half)
    ang = jnp.concatenate([row[:, None] * inv, col[:, None] * inv], axis=-1)
    cos, sin = jnp.cos(ang), jnp.sin(ang)
    return jnp.concatenate([cos, cos], axis=-1), jnp.concatenate([-sin, sin], axis=-1)


def _gqa_attn_kernel(q_ref, *refs, rep, nk, n_kv):
    k_refs, v_refs, o_ref = refs[:n_kv], refs[n_kv:2 * n_kv], refs[2 * n_kv]
    tq = q_ref.shape[0]
    scores, values = [], []
    for kv in range(nk):
        sl = slice(kv * LANES, (kv + 1) * LANES)
        qs = jnp.concatenate([q_ref[:, (kv * rep + r) * LANES:(kv * rep + r + 1) * LANES] for r in range(rep)], axis=0)
        scores.append([_qk(qs, r[:, sl].astype(BF16)) for r in k_refs])
        values.append([r[:, sl].astype(BF16) for r in v_refs])
    for kv in range(nk):
        o = _softmax2_pv(scores[kv], values[kv])
        for r in range(rep):
            o_ref[:, (kv * rep + r) * LANES:(kv * rep + r + 1) * LANES] = o[r * tq:(r + 1) * tq].astype(o_ref.dtype)


def _gqa_ctx_attn(q, k, v, batch, seq, nk):
    d = q.shape[1]
    dk = k.shape[1]
    rep = d // dk
    qspec = pl.BlockSpec((seq, d), lambda b: (b, 0))
    kspec = pl.BlockSpec((seq, dk), lambda b: (b, 0))
    return pl.pallas_call(
        functools.partial(_gqa_attn_kernel, rep=rep, nk=nk, n_kv=1),
        grid=(batch,),
        in_specs=[qspec, kspec, kspec],
        out_specs=qspec,
        out_shape=jax.ShapeDtypeStruct((batch * seq, d), BF16),
        compiler_params=_cparams("arbitrary"),
        name="gqa_ctx_attn",
    )(q, k, v)


def _gqa_lat_attn(q, k, v, cache_k, cache_v, slot, n_prompt, dec_batch, dec_seq, nk):
    d = q.shape[1]
    dk = k.shape[1]
    rep = d // dk
    past = cache_k.shape[2]
    tq = _pow2_tile(256, dec_seq, n_prompt)
    nqb = dec_seq // tq
    q0 = n_prompt // tq
    lspec = pl.BlockSpec((dec_seq, dk), lambda b, qb: (b, 0))
    cspec = pl.BlockSpec((None, None, past, dk), lambda b, qb: (b, slot, 0, 0))
    return pl.pallas_call(
        functools.partial(_gqa_attn_kernel, rep=rep, nk=nk, n_kv=2),
        grid=(dec_batch, nqb),
        in_specs=[pl.BlockSpec((tq, d), lambda b, qb: (q0 + b * nqb + qb, 0)),
                  lspec, cspec, lspec, cspec],
        out_specs=pl.BlockSpec((tq, d), lambda b, qb: (b * nqb + qb, 0)),
        out_shape=jax.ShapeDtypeStruct((dec_batch * dec_seq, d), BF16),
        compiler_params=_cparams("arbitrary", "arbitrary"),
        name="gqa_lat_attn",
    )(q, k, cache_k, v, cache_v)


def _proj_res_kernel(x_ref, op_ref, os_ref, w_ref, gate_ref, out_ref, w_scr, *, np_tiles):
    _cast_once(w_ref, w_scr)
    o = jnp.where(pl.program_id(0) < np_tiles, op_ref[...], os_ref[...])
    y = jnp.dot(o, w_scr[...], preferred_element_type=F32)
    out_ref[...] = x_ref[...] + gate_ref[...] * y


def _proj_res(x, o_p, o_s, w_o, slot, mod, layer, tok):
    n, d = x.shape
    row = lambda i: (i, 0)
    return pl.pallas_call(
        functools.partial(_proj_res_kernel, np_tiles=tok.np_tiles),
        grid=(tok.tiles,),
        in_specs=[pl.BlockSpec((tok.tm, d), row),
                  tok.prompt_spec(d, 1),
                  tok.sample_spec(d, 1),
                  _resident((None, d, d), lambda i: (slot, 0, 0)),
                  tok.mod_spec(layer, 2, d, 1)],
        out_specs=pl.BlockSpec((tok.tm, d), row),
        out_shape=jax.ShapeDtypeStruct((n, d), F32),
        scratch_shapes=[pltpu.VMEM((d, d), BF16)],
        compiler_params=_cparams("arbitrary"),
        name="attn_proj_res",
    )(x, o_p, o_s, w_o, mod)


def _mlp_kernel(x_ref, g_ref, sc_ref, sh_ref, gate_ref, w1_ref, w2_ref, *rest, np_tiles):
    out_refs, (h_scr, acc_scr) = rest[:-2], rest[-2:]
    i = pl.program_id(0)
    j = pl.program_id(1)
    last = j == pl.num_programs(1) - 1

    @pl.when(j == 0)
    def _():
        h_scr[...] = _normmod(x_ref[...], g_ref[...], sc_ref[...], sh_ref[...]).astype(BF16)
        acc_scr[...] = jnp.zeros_like(acc_scr)

    a = jnp.maximum(jnp.dot(h_scr[...], w1_ref[...].astype(BF16), preferred_element_type=F32), 0.0)
    acc_scr[...] += jnp.dot((a * a).astype(BF16), w2_ref[...].astype(BF16), preferred_element_type=F32)

    def result():
        return x_ref[...] + gate_ref[...] * acc_scr[...]

    if len(out_refs) == 1:
        @pl.when(last)
        def _():
            out_refs[0][...] = result()
    else:
        @pl.when(jnp.logical_and(last, i < np_tiles))
        def _():
            out_refs[0][...] = result()

        @pl.when(jnp.logical_and(last, i >= np_tiles))
        def _():
            out_refs[1][...] = result()


def _mlp(x, g, mod, layer, w1, w2, tok, split_out):
    n, d = x.shape
    f = w1.shape[2]
    tf = _pow2_tile(1024, f)
    row = lambda i, j: (i, 0)
    if split_out:
        n_p = tok.np_tiles * tok.tm
        out_specs = [tok.prompt_spec(d, 2), tok.sample_spec(d, 2)]
        out_shape = [jax.ShapeDtypeStruct((n_p, d), F32), jax.ShapeDtypeStruct((n - n_p, d), F32)]
    else:
        out_specs = pl.BlockSpec((tok.tm, d), row)
        out_shape = jax.ShapeDtypeStruct((n, d), F32)
    return pl.pallas_call(
        functools.partial(_mlp_kernel, np_tiles=tok.np_tiles),
        grid=(tok.tiles, f // tf),
        in_specs=[pl.BlockSpec((tok.tm, d), row),
                  pl.BlockSpec((1, d), lambda i, j: (0, 0)),
                  tok.mod_spec(layer, 4, d, 2),
                  tok.mod_spec(layer, 3, d, 2),
                  tok.mod_spec(layer, 5, d, 2),
                  pl.BlockSpec((None, d, tf), lambda i, j: (layer, 0, j)),
                  pl.BlockSpec((None, tf, d), lambda i, j: (layer, j, 0))],
        out_specs=out_specs,
        out_shape=out_shape,
        scratch_shapes=[pltpu.VMEM((tok.tm, d), BF16), pltpu.VMEM((tok.tm, d), F32)],
        compiler_params=_cparams("arbitrary", "arbitrary"),
        name="mlp",
    )(x, g, mod, mod, mod, w1, w2)


def _s5_tables(lam_re, lam_im, log_dt, b_re, b_im, c_re, c_im):
    t = S5_CHUNK
    g, p, k = b_re.shape[1:]
    dt = jnp.exp(log_dt.astype(F32))[:, :, None]
    lr, li = lam_re.astype(F32), lam_im.astype(F32)
    ar, ai = lr * dt, li * dt
    mag = jnp.exp(ar)
    abr, abi = mag * jnp.cos(ai), mag * jnp.sin(ai)
    nr, ni = abr - 1.0, abi
    den = lr * lr + li * li
    f_re = (nr * lr + ni * li) / den
    f_im = (ni * lr - nr * li) / den
    bbr = f_re[..., None] * b_re - f_im[..., None] * b_im
    bbi = f_re[..., None] * b_im + f_im[..., None] * b_re
    n = jnp.arange(t + 1, dtype=F32)[None, None, :, None]
    pm = jnp.exp(n * ar[:, :, None, :])
    pr, pi = pm * jnp.cos(n * ai[:, :, None, :]), pm * jnp.sin(n * ai[:, :, None, :])
    cr, ci = c_re.astype(F32), c_im.astype(F32)

    assert 2 * p == LANES
    cat = lambda a, b: jnp.concatenate([a, b], axis=-1)
    bt_r, bt_i = bbr.transpose(0, 1, 3, 2), bbi.transpose(0, 1, 3, 2)
    bb2 = jnp.stack([cat(bt_r[0], bt_i[0]), cat(-bt_i[0], bt_r[0]),
                     cat(bt_r[1], bt_i[1]), cat(-bt_i[1], bt_r[1])], axis=1)
    pws = jnp.stack([cat(pr[0, :, :t][:, ::-1], pr[0, :, :t][:, ::-1]), cat(pi[0, :, :t][:, ::-1], pi[0, :, :t][:, ::-1]),
                     cat(pr[1, :, :t], pr[1, :, :t]), cat(pi[1, :, :t], pi[1, :, :t])], axis=1)
    pwr = jnp.stack([cat(pr[0, :, 1:], pr[1, :, 1:][:, ::-1]), cat(pi[0, :, 1:], pi[1, :, 1:][:, ::-1])], axis=1)
    c2 = jnp.stack([cat(cr[0], cr[1]), cat(ci[0], ci[1])], axis=1)
    cfb = jnp.stack([cat(cr[0], -ci[0]), cat(cr[1], -ci[1])], axis=1)
    w_intra, w_sum, w2t = _s5_toeplitz(cfb, bb2, pws, c2, pwr)
    at = jnp.stack([jnp.concatenate([pr[0, :, t], pr[1, :, t]], axis=-1),
                    jnp.concatenate([pi[0, :, t], pi[1, :, t]], axis=-1)], axis=1)
    return w_intra, w_sum, w2t, at


def _s5_toeplitz_kernel(cfb_ref, bb2_ref, pws_ref, c2_ref, pwr_ref, o_ref, ws_ref, w2t_ref):
    hi = lax.Precision.HIGHEST
    gl, _, k, p2 = cfb_ref.shape
    p = p2 // 2
    tk = o_ref.shape[1]
    t = tk // k
    for g in range(gl):
        def ab(d):
            blocks = [bb2_ref[g, 2 * d] * pws_ref[g, 2 * d, j:j + 1, :]
                      + bb2_ref[g, 2 * d + 1] * pws_ref[g, 2 * d + 1, j:j + 1, :] for j in range(t)]
            return jnp.concatenate(blocks, axis=0).T

        abf, abb = ab(0), ab(1)
        ws_ref[g] = jnp.concatenate([abf[0:p], abb[0:p], abf[p:], abb[p:]], axis=0).astype(ws_ref.dtype)
        cr2, ci2 = c2_ref[g, 0], c2_ref[g, 1]
        rows = []
        for i in range(t):
            wr, wi = pwr_ref[g, 0, i:i + 1, :], pwr_ref[g, 1, i:i + 1, :]
            rows.append(jnp.concatenate([cr2 * wr - ci2 * wi, -(cr2 * wi + ci2 * wr)], axis=1))
        w2t_ref[g] = jnp.concatenate(rows, axis=0).astype(w2t_ref.dtype)
        kf = jnp.dot(cfb_ref[g, 0], abf, precision=hi, preferred_element_type=F32)
        kb = jnp.dot(cfb_ref[g, 1], abb, precision=hi, preferred_element_type=F32)
        z = jnp.zeros_like(kf)
        krev = jnp.concatenate([kf, z], axis=1) + pltpu.roll(jnp.concatenate([z, kb], axis=1), 2 * tk - k, 1)
        for i in range(S5_CHUNK):
            sh = (S5_CHUNK - 1 - i) * k
            win = krev if sh == 0 else pltpu.roll(krev, 2 * tk - sh, 1)
            o_ref[g, i * k:(i + 1) * k, :] = win[:, 0:tk].astype(o_ref.dtype)


def _s5_toeplitz(cfb, bb2, pws, c2, pwr):
    g, _, k, p2 = cfb.shape
    t = pws.shape[2]
    tk = t * k
    gl = _pow2_tile(8, g)
    spec = lambda a: pl.BlockSpec((gl,) + a.shape[1:], lambda i: (i, 0, 0, 0))
    out = lambda rows, cols: (pl.BlockSpec((gl, rows, cols), lambda i: (i, 0, 0)),
                              jax.ShapeDtypeStruct((g, rows, cols), BF16))
    outs = [out(tk, tk), out(2 * p2, tk), out(tk, 2 * p2)]
    return pl.pallas_call(
        _s5_toeplitz_kernel,
        grid=(g // gl,),
        in_specs=[spec(a) for a in (cfb, bb2, pws, c2, pwr)],
        out_specs=[o[0] for o in outs],
        out_shape=[o[1] for o in outs],
        compiler_params=_cparams("arbitrary"),
        name="s5_toeplitz",
    )(cfb, bb2, pws, c2, pwr)


def _s5_mixer(x, g, mod, layer, tok, params, h0, dims):
    (lam_re, lam_im, log_dt, b_re, b_im, c_re, c_im, d_skip, glu_w, slot, glu_b) = params
    batch, seq, dec_batch, dec_seq = dims
    n_prompt = batch * seq
    d = d_skip.shape[0]
    ngrp = d // S5_GROUP
    p = lam_re.shape[-1]
    w_intra, w_sum, w2t, at = _s5_tables(lam_re, lam_im, log_dt, b_re, b_im, c_re, c_im)
    hperm = _s5_pre(x, g, mod, layer, tok)
    h0g = h0.astype(F32).transpose(3, 2, 0, 1, 4).reshape(ngrp, 2, dec_batch, 2 * p)
    streams = ((0, seq // S5_CHUNK, batch, False, True),
               (n_prompt // S5_CHUNK, dec_seq // S5_CHUNK, dec_batch, True, False))
    yperm, fin = _s5_scan(hperm, w_intra, w_sum, w2t, at, h0g, batch, streams)
    x_new = _s5_post(x, yperm, g, mod, layer, d_skip.reshape(1, d), glu_w, slot, glu_b, tok)
    st = fin.reshape(ngrp, 2, batch, 2, p).transpose(2, 3, 1, 0, 4)
    return x_new, st


def kernel(x_prompt, x_sample, state_s5, cache_na_k, cache_na_v, cache_gqa_k, cache_gqa_v, c, c_ctx, norm_g, ada_w, ada_b, mlp_w1, mlp_w2, s5_lam_re, s5_lam_im, s5_log_dt, s5_b_re, s5_b_im, s5_c_re, s5_c_im, s5_d, s5_glu_w, s5_glu_b, na_w_qkv, na_q_norm, na_k_norm, na_rpb, na_w_o, gqa_w_qkv, gqa_q_norm, gqa_k_norm, gqa_w_o):
    batch, seq, d = x_prompt.shape
    dec_batch, dec_seq, _ = x_sample.shape
    depth = ada_w.shape[0]
    n_prompt = batch * seq
    n_sample = dec_batch * dec_seq
    na_heads, na_hd = cache_na_k.shape[3], cache_na_k.shape[4]
    gqa_kv, gqa_hd = cache_gqa_k.shape[3], cache_gqa_k.shape[4]
    assert n_prompt % dec_seq == 0

    tok = _Tok(n_prompt, n_sample, dec_seq, 1024)
    tok_half = _Tok(n_prompt, n_sample, dec_seq, 512)

    mod_rows = -(-(1 + dec_batch) // SUBLANES) * SUBLANES
    cvec = jnp.concatenate([c_ctx[None, :], c, jnp.zeros((mod_rows - 1 - dec_batch, d), F32)], axis=0)
    mod = _modulation(cvec, ada_w, ada_b).reshape(depth, mod_rows, 6, 1, d)

    x = (x_prompt.reshape(n_prompt, d), x_sample.reshape(n_sample, d))
    cache_na_k2 = cache_na_k.reshape(cache_na_k.shape[:3] + (na_heads * na_hd,))
    cache_na_v2 = cache_na_v.reshape(cache_na_v.shape[:3] + (na_heads * na_hd,))
    cache_gqa_k2 = cache_gqa_k.reshape(cache_gqa_k.shape[:3] + (gqa_kv * gqa_hd,))
    cache_gqa_v2 = cache_gqa_v.reshape(cache_gqa_v.shape[:3] + (gqa_kv * gqa_hd,))
    cos_t, sin_t = _rope_tables(dec_seq, gqa_hd)

    new_s5, new_na_k, new_na_v, new_gqa_k, new_gqa_v = [], [], [], [], []
    for i in range(depth):
        kind, slot = i % 3, i // 3
        g1 = norm_g[i, 0].reshape(1, d)
        g2 = norm_g[i, 1].reshape(1, d)
        if kind == 0:
            params = (s5_lam_re[slot], s5_lam_im[slot], s5_log_dt[slot], s5_b_re[slot], s5_b_im[slot],
                      s5_c_re[slot], s5_c_im[slot], s5_d[slot], s5_glu_w, slot, s5_glu_b[slot])
            x, st = _s5_mixer(x, g1, mod, i, tok_half, params, state_s5[:, slot], (batch, seq, dec_batch, dec_seq))
            new_s5.append(st)
        elif kind == 1:
            assert not isinstance(x, tuple)
            q, k_p, k_s, v_p, v_s = _na_qkv(x, g1, mod, i, na_w_qkv, slot, na_q_norm[slot], na_k_norm[slot], tok_half)
            o_p = _na_ctx_attn(q, k_p, v_p, batch, seq, na_hd)
            bias = _na_bias_table(na_rpb[slot], dec_seq)
            o_s = _na_lat_attn(q, k_s, v_s, cache_na_k2, cache_na_v2, slot, bias, n_prompt, dec_batch, dec_seq, na_hd)
            x = _proj_res(x, o_p, o_s, na_w_o, slot, mod, i, tok)
            new_na_k.append(k_p.reshape(batch, seq, na_heads, na_hd))
            new_na_v.append(v_p.reshape(batch, seq, na_heads, na_hd))
        else:
            assert not isinstance(x, tuple)
            q, k_p, k_s, v_p, v_s = _gqa_qkv(x, g1, mod, i, gqa_w_qkv, slot, gqa_q_norm[slot], gqa_k_norm[slot],
                                             cos_t, sin_t, gqa_kv, tok_half)
            o_p = _gqa_ctx_attn(q, k_p, v_p, batch, seq, gqa_kv)
            o_s = _gqa_lat_attn(q, k_s, v_s, cache_gqa_k2, cache_gqa_v2, slot, n_prompt, dec_batch, dec_seq, gqa_kv)
            x = _proj_res(x, o_p, o_s, gqa_w_o, slot, mod, i, tok)
            new_gqa_k.append(k_p.reshape(batch, seq, gqa_kv, gqa_hd))
            new_gqa_v.append(v_p.reshape(batch, seq, gqa_kv, gqa_hd))
        x = _mlp(x, g2, mod, i, mlp_w1, mlp_w2, tok, split_out=(i == depth - 1))
    y_p, y_s = x
    return (y_p.reshape(batch, seq, d), y_s.reshape(dec_batch, dec_seq, d),
            jnp.stack(new_s5, axis=1), jnp.stack(new_na_k, axis=1), jnp.stack(new_na_v, axis=1),
            jnp.stack(new_gqa_k, axis=1), jnp.stack(new_gqa_v, axis=1))
```

```python
import functools
import math

import jax
import jax.numpy as jnp
from jax import lax
from jax.experimental import pallas as pl
from jax.experimental.pallas import tpu as pltpu

F32 = jnp.float32
BF16 = jnp.bfloat16

EPS = 1e-6
GRID_W = 64
S5_GROUP = 16
NA_KH = 8
NA_KW = 16
ROPE_THETA = 10000.0
S5_CHUNK = 16
LOG2E = math.log2(math.e)
LANES = 128
SUBLANES = 8
VMEM_LIMIT = 56 * 1024 * 1024


def _cparams(*sem):
    return pltpu.CompilerParams(dimension_semantics=sem, vmem_limit_bytes=VMEM_LIMIT)


def _pow2_tile(pref, *ns):
    t = pref
    while any(n % t for n in ns):
        t //= 2
    return t


def _row_halves(n):
    if n % 2 or n // 2 < LANES:
        return [slice(0, n)]
    return [slice(0, n // 2), slice(n // 2, n)]


def _normmod(x, g, sc, sh):
    ms = jnp.mean(x * x, axis=-1, keepdims=True)
    y = x * lax.rsqrt(ms + EPS) * g
    return y * (1.0 + sc) + sh


class _Tok:
    def __init__(self, n_prompt, n_sample, dec_seq, pref):
        self.tm = _pow2_tile(pref, n_prompt, dec_seq)
        self.n = n_prompt + n_sample
        self.tiles = self.n // self.tm
        self.np_tiles = n_prompt // self.tm
        self.tps = dec_seq // self.tm

    def mod_row(self, i):
        return jnp.where(i < self.np_tiles, 0, 1 + (i - self.np_tiles) // self.tps)

    def prompt_spec(self, d, nargs):
        last = self.np_tiles - 1
        if nargs == 1:
            return pl.BlockSpec((self.tm, d), lambda i: (jnp.minimum(i, last), 0))
        return pl.BlockSpec((self.tm, d), lambda i, j: (jnp.minimum(i, last), 0))

    def sample_spec(self, d, nargs):
        first = self.np_tiles
        if nargs == 1:
            return pl.BlockSpec((self.tm, d), lambda i: (jnp.maximum(i - first, 0), 0))
        return pl.BlockSpec((self.tm, d), lambda i, j: (jnp.maximum(i - first, 0), 0))

    def mod_spec(self, layer, which, d, nargs):
        if nargs == 1:
            return pl.BlockSpec((None, None, None, 1, d), lambda i: (layer, self.mod_row(i), which, 0, 0))
        return pl.BlockSpec((None, None, None, 1, d), lambda i, j: (layer, self.mod_row(i), which, 0, 0))


def _mod_kernel(c_ref, w_ref, b_ref, o_ref):
    c = c_ref[...]
    s = c * jax.nn.sigmoid(c)
    o_ref[...] = jnp.dot(s.astype(BF16), w_ref[...].astype(BF16), preferred_element_type=F32) + b_ref[...]


def _modulation(cvec, ada_w, ada_b):
    depth, d, d6 = ada_w.shape
    rows = cvec.shape[0]
    tn = _pow2_tile(2048, d6) if d6 % 2048 == 0 else d
    return pl.pallas_call(
        _mod_kernel,
        grid=(depth, d6 // tn),
        in_specs=[pl.BlockSpec((rows, d), lambda l, j: (0, 0)),
                  pl.BlockSpec((None, d, tn), lambda l, j: (l, 0, j)),
                  pl.BlockSpec((None, 1, tn), lambda l, j: (l, 0, j))],
        out_specs=pl.BlockSpec((None, rows, tn), lambda l, j: (l, 0, j)),
        out_shape=jax.ShapeDtypeStruct((depth, rows, d6), F32),
        compiler_params=_cparams("arbitrary", "arbitrary"),
        name="adaln_modulation",
    )(cvec, ada_w, ada_b.reshape(depth, 1, d6))


def _read_x(x_refs, np_tiles, rows=slice(None), cols=slice(None)):
    if len(x_refs) == 1:
        return x_refs[0][rows, cols]
    return jnp.where(pl.program_id(0) < np_tiles, x_refs[0][rows, cols], x_refs[1][rows, cols])


def _x_specs(x, tok, nargs):
    if isinstance(x, tuple):
        d = x[0].shape[1]
        return [tok.prompt_spec(d, nargs), tok.sample_spec(d, nargs)], x
    d = x.shape[1]
    return [pl.BlockSpec((tok.tm, d), (lambda i: (i, 0)) if nargs == 1 else (lambda i, j: (i, 0)))], (x,)


def _s5_pre_kernel(*refs, n_x, np_tiles):
    x_refs, (g_ref, sc_ref, sh_ref, h_ref, h_scr) = refs[:n_x], refs[n_x:]
    x = _read_x(x_refs, np_tiles)
    rstd = lax.rsqrt(jnp.mean(x * x, axis=-1, keepdims=True) + EPS)
    nc = h_ref.shape[1]
    for a in range(h_scr.shape[0]):
        sl = slice(a * LANES, (a + 1) * LANES)
        h_scr[a] = (x[:, sl] * rstd * g_ref[:, sl]) * (1.0 + sc_ref[:, sl]) + sh_ref[:, sl]
        for j in range(S5_CHUNK):
            h_ref[j, :, sl] = h_scr[a, pl.ds(j, nc, stride=S5_CHUNK), :].astype(h_ref.dtype)


def _s5_pre(x, g, mod, layer, tok):
    x_specs, xs = _x_specs(x, tok, 1)
    d = xs[0].shape[1]
    nc = tok.tm // S5_CHUNK
    return pl.pallas_call(
        functools.partial(_s5_pre_kernel, n_x=len(xs), np_tiles=tok.np_tiles),
        grid=(tok.tiles,),
        in_specs=x_specs + [pl.BlockSpec((1, d), lambda i: (0, 0)),
                            tok.mod_spec(layer, 1, d, 1),
                            tok.mod_spec(layer, 0, d, 1)],
        out_specs=pl.BlockSpec((S5_CHUNK, nc, d), lambda i: (0, i, 0)),
        out_shape=jax.ShapeDtypeStruct((S5_CHUNK, tok.n // S5_CHUNK, d), BF16),
        scratch_shapes=[pltpu.VMEM((d // LANES, tok.tm, LANES), F32)],
        compiler_params=_cparams("arbitrary"),
        name="s5_pre",
    )(*xs, g, mod, mod)


def _s5_scan_kernel(h_ref, wi_ref, ws_ref, w2t_ref, at_ref, h0_ref, *rest, streams, unroll):
    ns = len(streams)
    perm_refs, (y_ref, fin_ref, xt_scr, yt_scr), chain_scr = rest[:ns], rest[ns:ns + 4], rest[ns + 4:]
    ngrp = wi_ref.shape[0]
    p2 = at_ref.shape[-1]
    fwd = lax.broadcasted_iota(jnp.int32, (1, p2), 1) < (p2 // 2)
    for j in range(S5_CHUNK):
        xt_scr[j] = h_ref[j].astype(F32).T.astype(BF16)

    def load_rmat(r, stream):
        row0, n_chunks, nb = stream[:3]
        cols = slice(row0, row0 + n_chunks * nb)
        c0 = pl.multiple_of(r * S5_GROUP, S5_GROUP)
        return jnp.concatenate([xt_scr[j, pl.ds(c0, S5_GROUP), cols] for j in range(S5_CHUNK)], axis=0), c0, cols

    def summaries(r, stream, perm_ref, s_scr):
        nb, use_h0 = stream[2], stream[3]
        rmat, _, _ = load_rmat(r, stream)
        rmat_cm = jnp.dot(rmat, perm_ref[...], preferred_element_type=F32).astype(BF16)
        zs = jnp.dot(ws_ref[r], rmat_cm, preferred_element_type=F32)
        s_scr[0] = zs[0:p2].T
        s_scr[1] = zs[p2:].T
        if use_h0:
            return h0_ref[r, 0], h0_ref[r, 1]
        return jnp.zeros((nb, p2), F32), jnp.zeros((nb, p2), F32)

    def scan_step(t, decay, stream, state, s_scr, e_scr):
        n_chunks, nb = stream[1], stream[2]
        hr, hi = state
        ar, ai = decay
        ft = slice(t * nb, (t + 1) * nb)
        bt = slice((n_chunks - 1 - t) * nb, (n_chunks - t) * nb)
        e_scr[0, ft, :] = hr
        e_scr[1, bt, :] = hr
        e_scr[2, ft, :] = hi
        e_scr[3, bt, :] = hi
        in_re = jnp.where(fwd, s_scr[0, ft, :], s_scr[0, bt, :])
        in_im = jnp.where(fwd, s_scr[1, ft, :], s_scr[1, bt, :])
        return ar * hr - ai * hi + in_re, ar * hi + ai * hr + in_im

    def outputs(r, stream, perm_ref, state, e_scr):
        rmat, c0, cols = load_rmat(r, stream)
        if stream[4]:
            fin_ref[r, 0] = state[0]
            fin_ref[r, 1] = state[1]
        e_cm = jnp.concatenate([jnp.where(fwd, e_scr[0], e_scr[1]), jnp.where(fwd, e_scr[2], e_scr[3])],
                               axis=1).astype(BF16)
        e = jnp.dot(perm_ref[...], e_cm, preferred_element_type=F32).astype(BF16)
        yt = (jnp.dot(wi_ref[r], rmat, preferred_element_type=F32)
              + lax.dot_general(w2t_ref[r], e, (((1,), (1,)), ((), ())), preferred_element_type=F32))
        for i in range(S5_CHUNK):
            yt_scr[i, pl.ds(c0, S5_GROUP), cols] = yt[i * S5_GROUP:(i + 1) * S5_GROUP]

    def body(rb, carry):
        chains = []
        for u in range(unroll):
            for si, stream in enumerate(streams):
                k = 2 * (u * ns + si)
                chains.append((rb * unroll + u, stream, perm_refs[si], chain_scr[k], chain_scr[k + 1]))
        states = [summaries(r, stream, pm, s_scr) for r, stream, pm, s_scr, _ in chains]
        decays = [(at_ref[r, 0:1, :], at_ref[r, 1:2, :]) for r, _, _, _, _ in chains]
        for t in range(max(stream[1] for stream in streams)):
            for ci, (_, stream, _, s_scr, e_scr) in enumerate(chains):
                if t < stream[1]:
                    states[ci] = scan_step(t, decays[ci], stream, states[ci], s_scr, e_scr)
        for (r, stream, pm, _, e_scr), state in zip(chains, states):
            outputs(r, stream, pm, state, e_scr)
        return carry

    lax.fori_loop(0, ngrp // unroll, body, 0)
    for i in range(S5_CHUNK):
        y_ref[i] = yt_scr[i].T


def _s5_scan(hperm, w_intra, w_sum, w2t, at, h0, nb_fin, streams):
    t, nrows, d = hperm.shape
    g, tk, p4 = w2t.shape
    p2 = at.shape[-1]
    gl = LANES // S5_GROUP
    nb0 = h0.shape[2]
    unroll = 4
    assert sum(s[1] * s[2] for s in streams) == nrows and gl % unroll == 0
    chain_scr = []
    for _ in range(unroll):
        for s in streams:
            chain_scr += [pltpu.VMEM((2, s[1] * s[2], p2), F32), pltpu.VMEM((4, s[1] * s[2], p2), F32)]
    perms = []
    for _, n_chunks, nb, _, _ in streams:
        m = jnp.arange(n_chunks * nb)
        perms.append((m[:, None] == ((m % nb) * n_chunks + m // nb)[None, :]).astype(BF16))
    return pl.pallas_call(
        functools.partial(_s5_scan_kernel, streams=streams, unroll=unroll),
        grid=(d // LANES,),
        in_specs=[pl.BlockSpec((t, nrows, LANES), lambda a: (0, 0, a)),
                  pl.BlockSpec((gl, tk, tk), lambda a: (a, 0, 0)),
                  pl.BlockSpec((gl, p4, tk), lambda a: (a, 0, 0)),
                  pl.BlockSpec((gl, tk, p4), lambda a: (a, 0, 0)),
                  pl.BlockSpec((gl, 2, p2), lambda a: (a, 0, 0)),
                  pl.BlockSpec((gl, 2, nb0, p2), lambda a: (a, 0, 0, 0))]
                 + [pl.BlockSpec(pm.shape, lambda a: (0, 0)) for pm in perms],
        out_specs=[pl.BlockSpec((t, nrows, LANES), lambda a: (0, 0, a)),
                   pl.BlockSpec((gl, 2, nb_fin, p2), lambda a: (a, 0, 0, 0))],
        out_shape=[jax.ShapeDtypeStruct((t, nrows, d), F32),
                   jax.ShapeDtypeStruct((g, 2, nb_fin, p2), F32)],
        scratch_shapes=[pltpu.VMEM((t, LANES, nrows), BF16), pltpu.VMEM((t, LANES, nrows), F32)] + chain_scr,
        compiler_params=_cparams("arbitrary"),
        name="s5_scan",
    )(hperm, w_intra, w_sum, w2t, at, h0, *perms)


def _s5_post_kernel(*refs, n_x, np_tiles):
    x_refs = refs[:n_x]
    (y_ref, g_ref, sc_ref, sh_ref, gate_ref, dsk_ref, wa_ref, wg_ref, ba_ref, bg_ref, o_ref,
     y_scr, wa_scr, wg_scr) = refs[n_x:]
    _cast_once(wa_ref, wa_scr)
    _cast_once(wg_ref, wg_scr)
    tm, d = o_ref.shape
    n_cb = max(d // (2 * LANES), 1)
    cw = d // n_cb

    def before(rows):
        c0, nc = rows.start // S5_CHUNK, (rows.stop - rows.start) // S5_CHUNK
        for a in range(y_scr.shape[0]):
            for j in range(S5_CHUNK):
                y_scr[a, pl.ds(rows.start + j, nc, stride=S5_CHUNK), :] = y_ref[j, c0:c0 + nc, a * LANES:(a + 1) * LANES]
        x = _read_x(x_refs, np_tiles, rows)
        h = _normmod(x, g_ref[...], sc_ref[...], sh_ref[...])
        y = h * dsk_ref[...] + jnp.concatenate([y_scr[a, rows, :] for a in range(y_scr.shape[0])], axis=1)
        return jax.nn.gelu(y).astype(BF16)

    def matmuls(a, cb):
        cols = slice(cb * cw, (cb + 1) * cw)
        return (jnp.dot(a, wa_scr[:, cols], preferred_element_type=F32),
                jnp.dot(a, wg_scr[:, cols], preferred_element_type=F32))

    def after(rows, cb, z):
        cols = slice(cb * cw, (cb + 1) * cw)
        za, zg = z[0] + ba_ref[:, cols], z[1] + bg_ref[:, cols]
        x = _read_x(x_refs, np_tiles, rows, cols)
        o_ref[rows, cols] = x + gate_ref[:, cols] * (za * jax.nn.sigmoid(zg))

    def row_chunks(rows):
        step = max((rows.stop - rows.start) // n_cb, S5_CHUNK)
        return [slice(r, min(r + step, rows.stop)) for r in range(rows.start, rows.stop, step)]

    halves = _row_halves(tm)
    a_cur = jnp.concatenate([before(rc) for rc in row_chunks(halves[0])], axis=0)
    z_prev = None
    for hi, rows in enumerate(halves):
        nxt = row_chunks(halves[hi + 1]) if hi + 1 < len(halves) else []
        a_next, z_cur = [], []
        for cb in range(n_cb):
            z_cur.append(matmuls(a_cur, cb))
            if cb < len(nxt):
                a_next.append(before(nxt[cb]))
            if z_prev is not None:
                after(halves[hi - 1], cb, z_prev[cb])
        a_next += [before(rc) for rc in nxt[n_cb:]]
        z_prev = z_cur
        if a_next:
            a_cur = jnp.concatenate(a_next, axis=0)
    for cb in range(n_cb):
        after(halves[-1], cb, z_prev[cb])


def _s5_post(x, y, g, mod, layer, dsk, glu_w, slot, glu_b, tok):
    x_specs, xs = _x_specs(x, tok, 1)
    n, d = tok.n, xs[0].shape[1]
    row = lambda i: (i, 0)
    fixed = lambda i: (0, 0)
    gb2 = glu_b.reshape(1, 2 * d)
    return pl.pallas_call(
        functools.partial(_s5_post_kernel, n_x=len(xs), np_tiles=tok.np_tiles),
        grid=(tok.tiles,),
        in_specs=x_specs + [
                  pl.BlockSpec((S5_CHUNK, tok.tm // S5_CHUNK, d), lambda i: (0, i, 0)),
                  pl.BlockSpec((1, d), fixed),
                  tok.mod_spec(layer, 1, d, 1),
                  tok.mod_spec(layer, 0, d, 1),
                  tok.mod_spec(layer, 2, d, 1),
                  pl.BlockSpec((1, d), fixed),
                  _resident((None, d, d), lambda i: (slot, 0, 0)),
                  _resident((None, d, d), lambda i: (slot, 0, 1)),
                  pl.BlockSpec((1, d), lambda i: (0, 0)),
                  pl.BlockSpec((1, d), lambda i: (0, 1))],
        out_specs=pl.BlockSpec((tok.tm, d), row),
        out_shape=jax.ShapeDtypeStruct((n, d), F32),
        scratch_shapes=[pltpu.VMEM((d // LANES, tok.tm, LANES), F32),
                        pltpu.VMEM((d, d), BF16), pltpu.VMEM((d, d), BF16)],
        compiler_params=_cparams("arbitrary"),
        name="s5_post",
    )(*xs, y, g, mod, mod, mod, dsk, glu_w, glu_w, gb2, gb2)


def _cast_once(w_ref, w_scr):
    @pl.when(pl.program_id(0) == 0)
    def _():
        w_scr[...] = w_ref[...].astype(w_scr.dtype)


def _resident(block_shape, index_map):
    return pl.BlockSpec(block_shape, index_map, pipeline_mode=pl.Buffered(1))


def _na_qkv_kernel(x_ref, g_ref, sc_ref, sh_ref, w_ref, qn_ref, kn_ref, q_ref, kp_ref, ks_ref, vp_ref, vs_ref, w_scr, *,
                   hd, np_tiles):
    i = pl.program_id(0)
    _cast_once(w_ref, w_scr)
    d = x_ref.shape[1]
    lo = lax.broadcasted_iota(jnp.int32, (1, LANES), 1) < hd

    def head_norm(z, gain):
        outs = []
        for s in range(d // LANES):
            seg = z[:, s * LANES:(s + 1) * LANES]
            sq = seg * seg
            s_lo = jnp.sum(jnp.where(lo, sq, 0.0), axis=-1, keepdims=True)
            s_hi = jnp.sum(jnp.where(lo, 0.0, sq), axis=-1, keepdims=True)
            ms = jnp.where(lo, s_lo, s_hi) / hd
            outs.append(seg * lax.rsqrt(ms + EPS) * gain)
        return jnp.concatenate(outs, axis=1)

    h = _normmod(x_ref[...], g_ref[...], sc_ref[...], sh_ref[...]).astype(BF16)
    proj = lambda part: jnp.dot(h, w_scr[:, part * d:(part + 1) * d], preferred_element_type=F32)
    q_ref[...] = (head_norm(proj(0), qn_ref[...]) * (hd ** -0.5 * LOG2E)).astype(q_ref.dtype)
    k = head_norm(proj(1), kn_ref[...])
    v = proj(2)

    @pl.when(i < np_tiles)
    def _():
        kp_ref[...] = k
        vp_ref[...] = v

    @pl.when(i >= np_tiles)
    def _():
        ks_ref[...] = k.astype(ks_ref.dtype)
        vs_ref[...] = v.astype(vs_ref.dtype)


def _na_qkv(x, g, mod, layer, w_qkv, slot, qn, kn, tok):
    n, d = x.shape
    hd = qn.shape[-1]
    assert 2 * hd == LANES
    row = lambda i: (i, 0)
    fixed = lambda i: (0, 0)
    qn2 = jnp.tile(qn, 2).reshape(1, LANES)
    kn2 = jnp.tile(kn, 2).reshape(1, LANES)
    n_p, n_s = tok.np_tiles * tok.tm, n - tok.np_tiles * tok.tm
    return pl.pallas_call(
        functools.partial(_na_qkv_kernel, hd=hd, np_tiles=tok.np_tiles),
        grid=(tok.tiles,),
        in_specs=[pl.BlockSpec((tok.tm, d), row),
                  pl.BlockSpec((1, d), fixed),
                  tok.mod_spec(layer, 1, d, 1),
                  tok.mod_spec(layer, 0, d, 1),
                  _resident((None, d, 3 * d), lambda i: (slot, 0, 0)),
                  pl.BlockSpec((1, LANES), fixed),
                  pl.BlockSpec((1, LANES), fixed)],
        out_specs=[pl.BlockSpec((tok.tm, d), row),
                   tok.prompt_spec(d, 1), tok.sample_spec(d, 1), tok.prompt_spec(d, 1), tok.sample_spec(d, 1)],
        out_shape=[jax.ShapeDtypeStruct((n, d), BF16),
                   jax.ShapeDtypeStruct((n_p, d), F32), jax.ShapeDtypeStruct((n_s, d), BF16),
                   jax.ShapeDtypeStruct((n_p, d), F32), jax.ShapeDtypeStruct((n_s, d), BF16)],
        scratch_shapes=[pltpu.VMEM((d, 3 * d), BF16)],
        compiler_params=_cparams("arbitrary"),
        name="na_qkv",
    )(x, g, mod, mod, w_qkv, qn2, kn2)


def _softmax2_pv(s_parts, v_parts):
    m = s_parts[0].max(axis=-1, keepdims=True)
    for s in s_parts[1:]:
        m = jnp.maximum(m, s.max(axis=-1, keepdims=True))
    den = 0.0
    acc = 0.0
    for s, v in zip(s_parts, v_parts):
        e = jnp.exp2(s - m)
        den = den + e.sum(axis=-1, keepdims=True)
        acc = acc + jnp.dot(e.astype(BF16), v, preferred_element_type=F32)
    return acc / den


def _qk(q, k):
    return lax.dot_general(q, k, (((1,), (1,)), ((), ())), preferred_element_type=F32)


def _na_ctx_attn_kernel(q_ref, k_ref, v_ref, o_ref, *, hd):
    lo = lax.broadcasted_iota(jnp.int32, (1, LANES), 1) < hd
    for s in range(q_ref.shape[1] // LANES):
        sl = slice(s * LANES, (s + 1) * LANES)
        q = q_ref[:, sl]
        k = k_ref[:, sl].astype(BF16)
        v = v_ref[:, sl].astype(BF16)
        zero = jnp.zeros_like(q)
        o_lo = _softmax2_pv([_qk(jnp.where(lo, q, zero), k)], [v])
        o_hi = _softmax2_pv([_qk(jnp.where(lo, zero, q), k)], [v])
        o_ref[:, sl] = jnp.where(lo, o_lo, o_hi).astype(o_ref.dtype)


def _na_ctx_attn(q, k, v, batch, seq, hd):
    d = q.shape[1]
    spec = pl.BlockSpec((seq, d), lambda b: (b, 0))
    return pl.pallas_call(
        functools.partial(_na_ctx_attn_kernel, hd=hd),
        grid=(batch,),
        in_specs=[spec, spec, spec],
        out_specs=spec,
        out_shape=jax.ShapeDtypeStruct((batch * seq, d), BF16),
        compiler_params=_cparams("arbitrary"),
        name="na_ctx_attn",
    )(q, k, v)


def _na_window(r0, q_rows, rows, kh):
    w_rows = min(rows, kh + q_rows + (kh + q_rows) % 2)
    rs = min(max(r0 - kh // 2, 0), rows - kh)
    return min(rs // 2 * 2, rows - w_rows), w_rows


def _na_lat_attn_kernel(q_ref, kl_ref, vl_ref, kc_ref, vc_ref, tab_ref, o_ref, kc_scr, vc_scr, bias_scr, *,
                        hd, rows, kh, tq):
    lane = lax.broadcasted_iota(jnp.int32, (1, LANES), 1)
    lo = lane < hd
    q_rows = tq // GRID_W
    n_qb = q_ref.shape[0] // tq
    w_cols = bias_scr.shape[2]
    kc_scr[...] = kc_ref[...].astype(BF16)
    vc_scr[...] = vc_ref[...].astype(BF16)

    @pl.when(pl.program_id(1) == 0)
    def _():
        neg = jnp.full((GRID_W, GRID_W), -jnp.inf, F32)
        for hh in range(2):
            for rq in range(rows):
                w0, w_rows = _na_window(rq // q_rows * q_rows, q_rows, rows, kh)
                rs = min(max(rq - kh // 2, 0), rows - kh)
                for kr in range(w_rows):
                    rk = w0 + kr
                    tile = tab_ref[hh, rk - rq + kh - 1] if rs <= rk < rs + kh else neg
                    bias_scr[hh, rq * GRID_W:(rq + 1) * GRID_W, kr * GRID_W:(kr + 1) * GRID_W] = tile

    per_iter = 4 if n_qb % 4 == 0 else 1

    def qblocks(it, carry):
        work = []
        for u in range(per_iter):
            qb = it * per_iter + u
            r0 = qb * q_rows
            w0 = jnp.minimum(jnp.clip(r0 - kh // 2, 0, rows - kh) // 2 * 2, rows - w_cols // GRID_W)
            k0 = pl.multiple_of(w0 * GRID_W, 2 * GRID_W)
            q0 = pl.multiple_of(qb * tq, tq)
            q = q_ref[pl.ds(q0, tq), :]
            kl = kl_ref[pl.ds(k0, w_cols), :]
            zero = jnp.zeros_like(q)
            scores = []
            for hh in range(2):
                qm = jnp.where(lo, q, zero) if hh == 0 else jnp.where(lo, zero, q)
                scores.append([_qk(qm, kl) + bias_scr[hh, pl.ds(q0, tq), :], _qk(qm, kc_scr[...])])
            work.append((q0, k0, scores))
        for q0, k0, scores in work:
            vl = vl_ref[pl.ds(k0, w_cols), :]
            outs = [_softmax2_pv(s, [vl, vc_scr[...]]) for s in scores]
            o_ref[pl.ds(q0, tq), :] = jnp.where(lo, outs[0], outs[1]).astype(o_ref.dtype)
        return carry

    lax.fori_loop(0, n_qb // per_iter, qblocks, 0)


def _na_lat_attn(q, k, v, cache_k, cache_v, slot, tab, n_prompt, dec_batch, dec_seq, hd):
    d = q.shape[1]
    past = cache_k.shape[2]
    rows = dec_seq // GRID_W
    kh = min(NA_KH, rows)
    tq = _pow2_tile(256, dec_seq)
    assert tq % GRID_W == 0 and rows % 2 == 0 and 2 * GRID_W == LANES and n_prompt % dec_seq == 0
    b0 = n_prompt // dec_seq
    w_rows = _na_window(0, tq // GRID_W, rows, kh)[1]
    return pl.pallas_call(
        functools.partial(_na_lat_attn_kernel, hd=hd, rows=rows, kh=kh, tq=tq),
        grid=(d // LANES, dec_batch),
        in_specs=[pl.BlockSpec((dec_seq, LANES), lambda hp, b: (b0 + b, hp)),
                  pl.BlockSpec((dec_seq, LANES), lambda hp, b: (b, hp)),
                  pl.BlockSpec((dec_seq, LANES), lambda hp, b: (b, hp)),
                  pl.BlockSpec((None, None, past, LANES), lambda hp, b: (b, slot, 0, hp)),
                  pl.BlockSpec((None, None, past, LANES), lambda hp, b: (b, slot, 0, hp)),
                  pl.BlockSpec((2,) + tab.shape[1:], lambda hp, b: (hp, 0, 0, 0))],
        out_specs=pl.BlockSpec((dec_seq, LANES), lambda hp, b: (b, hp)),
        out_shape=jax.ShapeDtypeStruct((dec_batch * dec_seq, d), BF16),
        scratch_shapes=[pltpu.VMEM((past, LANES), BF16), pltpu.VMEM((past, LANES), BF16),
                        pltpu.VMEM((2, dec_seq, w_rows * GRID_W), F32)],
        compiler_params=_cparams("arbitrary", "arbitrary"),
        name="na_lat_attn",
    )(q, k, v, cache_k, cache_v, tab)


def _na_bias_table(rpb):
    nh, nd, nc = rpb.shape
    c = jnp.arange(GRID_W)
    cs = jnp.clip(c - NA_KW // 2, 0, GRID_W - NA_KW)
    col_ok = (c[None, :] >= cs[:, None]) & (c[None, :] < cs[:, None] + NA_KW)
    dcol = c[None, :] - c[:, None] + NA_KW - 1
    onehot = (dcol[None] == jnp.arange(nc)[:, None, None]).astype(F32)
    tmp = jnp.einsum('hdc,cxy->hdxy', rpb.astype(F32), onehot, precision=lax.Precision.HIGHEST)
    return jnp.where(col_ok[None, None], tmp * LOG2E, -jnp.inf)


def _gqa_qkv_kernel(x_ref, g_ref, sc_ref, sh_ref, w_ref, qn_ref, kn_ref, cos_ref, sin_ref,
                    q_ref, kp_ref, ks_ref, vp_ref, vs_ref, w_scr, *, nq, nk, np_tiles):
    i = pl.program_id(0)
    _cast_once(w_ref, w_scr)
    is_sample = i >= np_tiles
    ks, vs = [], []
    for rows in _row_halves(x_ref.shape[0]):
        h = _normmod(x_ref[rows, :], g_ref[...], sc_ref[...], sh_ref[...]).astype(BF16)
        z = jnp.dot(h, w_scr[...], preferred_element_type=F32)
        cos = cos_ref[rows, :]
        sin = sin_ref[rows, :]

        def norm_rope(seg, gain):
            ms = jnp.mean(seg * seg, axis=-1, keepdims=True)
            y = seg * lax.rsqrt(ms + EPS) * gain
            roped = y * cos + pltpu.roll(y, LANES // 2, 1) * sin
            return jnp.where(is_sample, roped, y)

        for hh in range(nq):
            sl = slice(hh * LANES, (hh + 1) * LANES)
            q_ref[rows, sl] = (norm_rope(z[:, sl], qn_ref[...]) * (LANES ** -0.5 * LOG2E)).astype(q_ref.dtype)
        ks.append(jnp.concatenate(
            [norm_rope(z[:, (nq + hh) * LANES:(nq + hh + 1) * LANES], kn_ref[...]) for hh in range(nk)], axis=1))
        vs.append(z[:, (nq + nk) * LANES:])
    k = jnp.concatenate(ks, axis=0)
    v = jnp.concatenate(vs, axis=0)

    @pl.when(i < np_tiles)
    def _():
        kp_ref[...] = k
        vp_ref[...] = v

    @pl.when(is_sample)
    def _():
        ks_ref[...] = k.astype(ks_ref.dtype)
        vs_ref[...] = v.astype(vs_ref.dtype)


def _gqa_qkv(x, g, mod, layer, w_qkv, slot, qn, kn, cos_t, sin_t, nk, tok):
    n, d = x.shape
    hd = qn.shape[-1]
    assert hd == LANES
    nq = d // hd
    dk = nk * hd
    row = lambda i: (i, 0)
    fixed = lambda i: (0, 0)
    pos = lambda i: (jnp.maximum(i - tok.np_tiles, 0) % tok.tps, 0)
    n_p, n_s = tok.np_tiles * tok.tm, n - tok.np_tiles * tok.tm
    return pl.pallas_call(
        functools.partial(_gqa_qkv_kernel, nq=nq, nk=nk, np_tiles=tok.np_tiles),
        grid=(tok.tiles,),
        in_specs=[pl.BlockSpec((tok.tm, d), row),
                  pl.BlockSpec((1, d), fixed),
                  tok.mod_spec(layer, 1, d, 1),
                  tok.mod_spec(layer, 0, d, 1),
                  _resident((None, d, d + 2 * dk), lambda i: (slot, 0, 0)),
                  pl.BlockSpec((1, hd), fixed),
                  pl.BlockSpec((1, hd), fixed),
                  pl.BlockSpec((tok.tm, hd), pos),
                  pl.BlockSpec((tok.tm, hd), pos)],
        out_specs=[pl.BlockSpec((tok.tm, d), row),
                   tok.prompt_spec(dk, 1), tok.sample_spec(dk, 1), tok.prompt_spec(dk, 1), tok.sample_spec(dk, 1)],
        out_shape=[jax.ShapeDtypeStruct((n, d), BF16),
                   jax.ShapeDtypeStruct((n_p, dk), F32), jax.ShapeDtypeStruct((n_s, dk), BF16),
                   jax.ShapeDtypeStruct((n_p, dk), F32), jax.ShapeDtypeStruct((n_s, dk), BF16)],
        scratch_shapes=[pltpu.VMEM((d, d + 2 * dk), BF16)],
        compiler_params=_cparams("arbitrary"),
        name="gqa_qkv",
    )(x, g, mod, mod, w_qkv, qn.reshape(1, hd), kn.reshape(1, hd), cos_t, sin_t)


def _rope_tables(dec_seq, hd):
    t = jnp.arange(dec_seq)
    row = (t // GRID_W).astype(F32)
    col = (t % GRID_W).astype(F32)
    half = hd // 2
    inv = ROPE_THETA ** (-jnp.arange(0, half, 2, dtype=F32) / half)
    ang = jnp.concatenate([row[:, None] * inv, col[:, None] * inv], axis=-1)
    cos, sin = jnp.cos(ang), jnp.sin(ang)
    return jnp.concatenate([cos, cos], axis=-1), jnp.concatenate([-sin, sin], axis=-1)


def _gqa_attn_kernel(q_ref, *refs, rep, nk, n_kv):
    k_refs, v_refs, o_ref = refs[:n_kv], refs[n_kv:2 * n_kv], refs[2 * n_kv]
    tq = q_ref.shape[0]
    scores, values = [], []
    for kv in range(nk):
        sl = slice(kv * LANES, (kv + 1) * LANES)
        qs = jnp.concatenate([q_ref[:, (kv * rep + r) * LANES:(kv * rep + r + 1) * LANES] for r in range(rep)], axis=0)
        scores.append([_qk(qs, r[:, sl].astype(BF16)) for r in k_refs])
        values.append([r[:, sl].astype(BF16) for r in v_refs])
    for kv in range(nk):
        o = _softmax2_pv(scores[kv], values[kv])
        for r in range(rep):
            o_ref[:, (kv * rep + r) * LANES:(kv * rep + r + 1) * LANES] = o[r * tq:(r + 1) * tq].astype(o_ref.dtype)


def _gqa_ctx_attn(q, k, v, batch, seq, nk):
    d = q.shape[1]
    dk = k.shape[1]
    rep = d // dk
    qspec = pl.BlockSpec((seq, d), lambda b: (b, 0))
    kspec = pl.BlockSpec((seq, dk), lambda b: (b, 0))
    return pl.pallas_call(
        functools.partial(_gqa_attn_kernel, rep=rep, nk=nk, n_kv=1),
        grid=(batch,),
        in_specs=[qspec, kspec, kspec],
        out_specs=qspec,
        out_shape=jax.ShapeDtypeStruct((batch * seq, d), BF16),
        compiler_params=_cparams("arbitrary"),
        name="gqa_ctx_attn",
    )(q, k, v)


def _gqa_lat_attn(q, k, v, cache_k, cache_v, slot, n_prompt, dec_batch, dec_seq, nk):
    d = q.shape[1]
    dk = k.shape[1]
    rep = d // dk
    past = cache_k.shape[2]
    tq = _pow2_tile(256, dec_seq, n_prompt)
    nqb = dec_seq // tq
    q0 = n_prompt // tq
    lspec = pl.BlockSpec((dec_seq, dk), lambda b, qb: (b, 0))
    cspec = pl.BlockSpec((None, None, past, dk), lambda b, qb: (b, slot, 0, 0))
    return pl.pallas_call(
        functools.partial(_gqa_attn_kernel, rep=rep, nk=nk, n_kv=2),
        grid=(dec_batch, nqb),
        in_specs=[pl.BlockSpec((tq, d), lambda b, qb: (q0 + b * nqb + qb, 0)),
                  lspec, cspec, lspec, cspec],
        out_specs=pl.BlockSpec((tq, d), lambda b, qb: (b * nqb + qb, 0)),
        out_shape=jax.ShapeDtypeStruct((dec_batch * dec_seq, d), BF16),
        compiler_params=_cparams("arbitrary", "arbitrary"),
        name="gqa_lat_attn",
    )(q, k, cache_k, v, cache_v)


def _proj_res_kernel(x_ref, op_ref, os_ref, w_ref, gate_ref, out_ref, w_scr, *, np_tiles):
    _cast_once(w_ref, w_scr)
    o = jnp.where(pl.program_id(0) < np_tiles, op_ref[...], os_ref[...])
    y = jnp.dot(o, w_scr[...], preferred_element_type=F32)
    out_ref[...] = x_ref[...] + gate_ref[...] * y


def _proj_res(x, o_p, o_s, w_o, slot, mod, layer, tok):
    n, d = x.shape
    row = lambda i: (i, 0)
    return pl.pallas_call(
        functools.partial(_proj_res_kernel, np_tiles=tok.np_tiles),
        grid=(tok.tiles,),
        in_specs=[pl.BlockSpec((tok.tm, d), row),
                  tok.prompt_spec(d, 1),
                  tok.sample_spec(d, 1),
                  _resident((None, d, d), lambda i: (slot, 0, 0)),
                  tok.mod_spec(layer, 2, d, 1)],
        out_specs=pl.BlockSpec((tok.tm, d), row),
        out_shape=jax.ShapeDtypeStruct((n, d), F32),
        scratch_shapes=[pltpu.VMEM((d, d), BF16)],
        compiler_params=_cparams("arbitrary"),
        name="attn_proj_res",
    )(x, o_p, o_s, w_o, mod)


def _mlp_kernel(x_ref, g_ref, sc_ref, sh_ref, gate_ref, w1_ref, w2_ref, *rest, np_tiles):
    out_refs, (h_scr, acc_scr) = rest[:-2], rest[-2:]
    i = pl.program_id(0)
    j = pl.program_id(1)
    last = j == pl.num_programs(1) - 1

    @pl.when(j == 0)
    def _():
        h_scr[...] = _normmod(x_ref[...], g_ref[...], sc_ref[...], sh_ref[...]).astype(BF16)
        acc_scr[...] = jnp.zeros_like(acc_scr)

    a = jnp.maximum(jnp.dot(h_scr[...], w1_ref[...].astype(BF16), preferred_element_type=F32), 0.0)
    acc_scr[...] += jnp.dot((a * a).astype(BF16), w2_ref[...].astype(BF16), preferred_element_type=F32)

    def result():
        return x_ref[...] + gate_ref[...] * acc_scr[...]

    if len(out_refs) == 1:
        @pl.when(last)
        def _():
            out_refs[0][...] = result()
    else:
        @pl.when(jnp.logical_and(last, i < np_tiles))
        def _():
            out_refs[0][...] = result()

        @pl.when(jnp.logical_and(last, i >= np_tiles))
        def _():
            out_refs[1][...] = result()


def _mlp(x, g, mod, layer, w1, w2, tok, split_out):
    n, d = x.shape
    f = w1.shape[2]
    tf = _pow2_tile(1024, f)
    row = lambda i, j: (i, 0)
    if split_out:
        n_p = tok.np_tiles * tok.tm
        out_specs = [tok.prompt_spec(d, 2), tok.sample_spec(d, 2)]
        out_shape = [jax.ShapeDtypeStruct((n_p, d), F32), jax.ShapeDtypeStruct((n - n_p, d), F32)]
    else:
        out_specs = pl.BlockSpec((tok.tm, d), row)
        out_shape = jax.ShapeDtypeStruct((n, d), F32)
    return pl.pallas_call(
        functools.partial(_mlp_kernel, np_tiles=tok.np_tiles),
        grid=(tok.tiles, f // tf),
        in_specs=[pl.BlockSpec((tok.tm, d), row),
                  pl.BlockSpec((1, d), lambda i, j: (0, 0)),
                  tok.mod_spec(layer, 4, d, 2),
                  tok.mod_spec(layer, 3, d, 2),
                  tok.mod_spec(layer, 5, d, 2),
                  pl.BlockSpec((None, d, tf), lambda i, j: (layer, 0, j)),
                  pl.BlockSpec((None, tf, d), lambda i, j: (layer, j, 0))],
        out_specs=out_specs,
        out_shape=out_shape,
        scratch_shapes=[pltpu.VMEM((tok.tm, d), BF16), pltpu.VMEM((tok.tm, d), F32)],
        compiler_params=_cparams("arbitrary", "arbitrary"),
        name="mlp",
    )(x, g, mod, mod, mod, w1, w2)


def _s5_tables(lam_re, lam_im, log_dt, b_re, b_im, c_re, c_im):
    t = S5_CHUNK
    g, p, k = b_re.shape[1:]
    dt = jnp.exp(log_dt.astype(F32))[:, :, None]
    lr, li = lam_re.astype(F32), lam_im.astype(F32)
    ar, ai = lr * dt, li * dt
    mag = jnp.exp(ar)
    abr, abi = mag * jnp.cos(ai), mag * jnp.sin(ai)
    nr, ni = abr - 1.0, abi
    den = lr * lr + li * li
    f_re = (nr * lr + ni * li) / den
    f_im = (ni * lr - nr * li) / den
    bbr = f_re[..., None] * b_re - f_im[..., None] * b_im
    bbi = f_re[..., None] * b_im + f_im[..., None] * b_re
    n = jnp.arange(t + 1, dtype=F32)[None, None, :, None]
    pm = jnp.exp(n * ar[:, :, None, :])
    pr, pi = pm * jnp.cos(n * ai[:, :, None, :]), pm * jnp.sin(n * ai[:, :, None, :])
    cr, ci = c_re.astype(F32), c_im.astype(F32)

    assert 2 * p == LANES
    cat = lambda a, b: jnp.concatenate([a, b], axis=-1)
    bt_r, bt_i = bbr.transpose(0, 1, 3, 2), bbi.transpose(0, 1, 3, 2)
    bb2 = jnp.stack([cat(bt_r[0], bt_i[0]), cat(-bt_i[0], bt_r[0]),
                     cat(bt_r[1], bt_i[1]), cat(-bt_i[1], bt_r[1])], axis=1)
    pws = jnp.stack([cat(pr[0, :, :t][:, ::-1], pr[0, :, :t][:, ::-1]), cat(pi[0, :, :t][:, ::-1], pi[0, :, :t][:, ::-1]),
                     cat(pr[1, :, :t], pr[1, :, :t]), cat(pi[1, :, :t], pi[1, :, :t])], axis=1)
    pwr = jnp.stack([cat(pr[0, :, 1:], pr[1, :, 1:][:, ::-1]), cat(pi[0, :, 1:], pi[1, :, 1:][:, ::-1])], axis=1)
    c2 = jnp.stack([cat(cr[0], cr[1]), cat(ci[0], ci[1])], axis=1)
    cfb = jnp.stack([cat(cr[0], -ci[0]), cat(cr[1], -ci[1])], axis=1)
    w_intra, w_sum, w2t = _s5_toeplitz(cfb, bb2, pws, c2, pwr)
    at = jnp.stack([jnp.concatenate([pr[0, :, t], pr[1, :, t]], axis=-1),
                    jnp.concatenate([pi[0, :, t], pi[1, :, t]], axis=-1)], axis=1)
    return w_intra, w_sum, w2t, at


def _s5_toeplitz_kernel(cfb_ref, bb2_ref, pws_ref, c2_ref, pwr_ref, o_ref, ws_ref, w2t_ref):
    hi = lax.Precision.HIGHEST
    gl, _, k, p2 = cfb_ref.shape
    p = p2 // 2
    tk = o_ref.shape[1]
    t = tk // k
    for g in range(gl):
        def ab(d):
            blocks = [bb2_ref[g, 2 * d] * pws_ref[g, 2 * d, j:j + 1, :]
                      + bb2_ref[g, 2 * d + 1] * pws_ref[g, 2 * d + 1, j:j + 1, :] for j in range(t)]
            return jnp.concatenate(blocks, axis=0).T

        abf, abb = ab(0), ab(1)
        ws_ref[g] = jnp.concatenate([abf[0:p], abb[0:p], abf[p:], abb[p:]], axis=0).astype(ws_ref.dtype)
        cr2, ci2 = c2_ref[g, 0], c2_ref[g, 1]
        rows = []
        for i in range(t):
            wr, wi = pwr_ref[g, 0, i:i + 1, :], pwr_ref[g, 1, i:i + 1, :]
            rows.append(jnp.concatenate([cr2 * wr - ci2 * wi, -(cr2 * wi + ci2 * wr)], axis=1))
        w2t_ref[g] = jnp.concatenate(rows, axis=0).astype(w2t_ref.dtype)
        kf = jnp.dot(cfb_ref[g, 0], abf, precision=hi, preferred_element_type=F32)
        kb = jnp.dot(cfb_ref[g, 1], abb, precision=hi, preferred_element_type=F32)
        z = jnp.zeros_like(kf)
        krev = jnp.concatenate([kf, z], axis=1) + pltpu.roll(jnp.concatenate([z, kb], axis=1), 2 * tk - k, 1)
        for i in range(S5_CHUNK):
            sh = (S5_CHUNK - 1 - i) * k
            win = krev if sh == 0 else pltpu.roll(krev, 2 * tk - sh, 1)
            o_ref[g, i * k:(i + 1) * k, :] = win[:, 0:tk].astype(o_ref.dtype)


def _s5_toeplitz(cfb, bb2, pws, c2, pwr):
    g, _, k, p2 = cfb.shape
    t = pws.shape[2]
    tk = t * k
    gl = _pow2_tile(8, g)
    spec = lambda a: pl.BlockSpec((gl,) + a.shape[1:], lambda i: (i, 0, 0, 0))
    out = lambda rows, cols: (pl.BlockSpec((gl, rows, cols), lambda i: (i, 0, 0)),
                              jax.ShapeDtypeStruct((g, rows, cols), BF16))
    outs = [out(tk, tk), out(2 * p2, tk), out(tk, 2 * p2)]
    return pl.pallas_call(
        _s5_toeplitz_kernel,
        grid=(g // gl,),
        in_specs=[spec(a) for a in (cfb, bb2, pws, c2, pwr)],
        out_specs=[o[0] for o in outs],
        out_shape=[o[1] for o in outs],
        compiler_params=_cparams("arbitrary"),
        name="s5_toeplitz",
    )(cfb, bb2, pws, c2, pwr)


def _s5_mixer(x, g, mod, layer, tok, params, h0, dims):
    (lam_re, lam_im, log_dt, b_re, b_im, c_re, c_im, d_skip, glu_w, slot, glu_b) = params
    batch, seq, dec_batch, dec_seq = dims
    n_prompt = batch * seq
    d = d_skip.shape[0]
    ngrp = d // S5_GROUP
    p = lam_re.shape[-1]
    w_intra, w_sum, w2t, at = _s5_tables(lam_re, lam_im, log_dt, b_re, b_im, c_re, c_im)
    hperm = _s5_pre(x, g, mod, layer, tok)
    h0g = h0.astype(F32).transpose(3, 2, 0, 1, 4).reshape(ngrp, 2, dec_batch, 2 * p)
    streams = ((0, seq // S5_CHUNK, batch, False, True),
               (n_prompt // S5_CHUNK, dec_seq // S5_CHUNK, dec_batch, True, False))
    yperm, fin = _s5_scan(hperm, w_intra, w_sum, w2t, at, h0g, batch, streams)
    x_new = _s5_post(x, yperm, g, mod, layer, d_skip.reshape(1, d), glu_w, slot, glu_b, tok)
    st = fin.reshape(ngrp, 2, batch, 2, p).transpose(2, 3, 1, 0, 4)
    return x_new, st


def kernel(x_prompt, x_sample, state_s5, cache_na_k, cache_na_v, cache_gqa_k, cache_gqa_v, c, c_ctx, norm_g, ada_w, ada_b, mlp_w1, mlp_w2, s5_lam_re, s5_lam_im, s5_log_dt, s5_b_re, s5_b_im, s5_c_re, s5_c_im, s5_d, s5_glu_w, s5_glu_b, na_w_qkv, na_q_norm, na_k_norm, na_rpb, na_w_o, gqa_w_qkv, gqa_q_norm, gqa_k_norm, gqa_w_o):
    batch, seq, d = x_prompt.shape
    dec_batch, dec_seq, _ = x_sample.shape
    depth = ada_w.shape[0]
    n_prompt = batch * seq
    n_sample = dec_batch * dec_seq
    na_heads, na_hd = cache_na_k.shape[3], cache_na_k.shape[4]
    gqa_kv, gqa_hd = cache_gqa_k.shape[3], cache_gqa_k.shape[4]
    assert n_prompt % dec_seq == 0

    tok = _Tok(n_prompt, n_sample, dec_seq, 1024)
    tok_half = _Tok(n_prompt, n_sample, dec_seq, 512)

    mod_rows = -(-(1 + dec_batch) // SUBLANES) * SUBLANES
    cvec = jnp.concatenate([c_ctx[None, :], c, jnp.zeros((mod_rows - 1 - dec_batch, d), F32)], axis=0)
    mod = _modulation(cvec, ada_w, ada_b).reshape(depth, mod_rows, 6, 1, d)

    x = (x_prompt.reshape(n_prompt, d), x_sample.reshape(n_sample, d))
    cache_na_k2 = cache_na_k.reshape(cache_na_k.shape[:3] + (na_heads * na_hd,))
    cache_na_v2 = cache_na_v.reshape(cache_na_v.shape[:3] + (na_heads * na_hd,))
    cache_gqa_k2 = cache_gqa_k.reshape(cache_gqa_k.shape[:3] + (gqa_kv * gqa_hd,))
    cache_gqa_v2 = cache_gqa_v.reshape(cache_gqa_v.shape[:3] + (gqa_kv * gqa_hd,))
    cos_t, sin_t = _rope_tables(dec_seq, gqa_hd)

    new_s5, new_na_k, new_na_v, new_gqa_k, new_gqa_v = [], [], [], [], []
    for i in range(depth):
        kind, slot = i % 3, i // 3
        g1 = norm_g[i, 0].reshape(1, d)
        g2 = norm_g[i, 1].reshape(1, d)
        if kind == 0:
            params = (s5_lam_re[slot], s5_lam_im[slot], s5_log_dt[slot], s5_b_re[slot], s5_b_im[slot],
                      s5_c_re[slot], s5_c_im[slot], s5_d[slot], s5_glu_w, slot, s5_glu_b[slot])
            x, st = _s5_mixer(x, g1, mod, i, tok_half, params, state_s5[:, slot], (batch, seq, dec_batch, dec_seq))
            new_s5.append(st)
        elif kind == 1:
            assert not isinstance(x, tuple)
            q, k_p, k_s, v_p, v_s = _na_qkv(x, g1, mod, i, na_w_qkv, slot, na_q_norm[slot], na_k_norm[slot], tok_half)
            o_p = _na_ctx_attn(q, k_p, v_p, batch, seq, na_hd)
            bias = _na_bias_table(na_rpb[slot])
            o_s = _na_lat_attn(q, k_s, v_s, cache_na_k2, cache_na_v2, slot, bias, n_prompt, dec_batch, dec_seq, na_hd)
            x = _proj_res(x, o_p, o_s, na_w_o, slot, mod, i, tok)
            new_na_k.append(k_p.reshape(batch, seq, na_heads, na_hd))
            new_na_v.append(v_p.reshape(batch, seq, na_heads, na_hd))
        else:
            assert not isinstance(x, tuple)
            q, k_p, k_s, v_p, v_s = _gqa_qkv(x, g1, mod, i, gqa_w_qkv, slot, gqa_q_norm[slot], gqa_k_norm[slot],
                                             cos_t, sin_t, gqa_kv, tok_half)
            o_p = _gqa_ctx_attn(q, k_p, v_p, batch, seq, gqa_kv)
            o_s = _gqa_lat_attn(q, k_s, v_s, cache_gqa_k2, cache_gqa_v2, slot, n_prompt, dec_batch, dec_seq, gqa_kv)
            x = _proj_res(x, o_p, o_s, gqa_w_o, slot, mod, i, tok)
            new_gqa_k.append(k_p.reshape(batch, seq, gqa_kv, gqa_hd))
            new_gqa_v.append(v_p.reshape(batch, seq, gqa_kv, gqa_hd))
        x = _mlp(x, g2, mod, i, mlp_w1, mlp_w2, tok, split_out=(i == depth - 1))
    y_p, y_s = x
    return (y_p.reshape(batch, seq, d), y_s.reshape(dec_batch, dec_seq, d),
            jnp.stack(new_s5, axis=1), jnp.stack(new_na_k, axis=1), jnp.stack(new_na_v, axis=1),
            jnp.stack(new_gqa_k, axis=1), jnp.stack(new_gqa_v, axis=1))
```

```python
import functools
import math

import jax
import jax.numpy as jnp
from jax import lax
from jax.experimental import pallas as pl
from jax.experimental.pallas import tpu as pltpu

F32 = jnp.float32
BF16 = jnp.bfloat16

EPS = 1e-6
GRID_W = 64
S5_GROUP = 16
NA_KH = 8
NA_KW = 16
ROPE_THETA = 10000.0
S5_CHUNK = 16
LOG2E = math.log2(math.e)
LANES = 128
SUBLANES = 8
VMEM_LIMIT = 56 * 1024 * 1024


def _cparams(*sem):
    return pltpu.CompilerParams(dimension_semantics=sem, vmem_limit_bytes=VMEM_LIMIT)


def _pow2_tile(pref, *ns):
    t = pref
    while any(n % t for n in ns):
        t //= 2
    return t


def _row_halves(n):
    if n % 2 or n // 2 < LANES:
        return [slice(0, n)]
    return [slice(0, n // 2), slice(n // 2, n)]


def _normmod(x, g, sc, sh):
    ms = jnp.mean(x * x, axis=-1, keepdims=True)
    y = x * lax.rsqrt(ms + EPS) * g
    return y * (1.0 + sc) + sh


class _Tok:
    def __init__(self, n_prompt, n_sample, dec_seq, pref):
        self.tm = _pow2_tile(pref, n_prompt, dec_seq)
        self.n = n_prompt + n_sample
        self.tiles = self.n // self.tm
        self.np_tiles = n_prompt // self.tm
        self.tps = dec_seq // self.tm

    def mod_row(self, i):
        return jnp.where(i < self.np_tiles, 0, 1 + (i - self.np_tiles) // self.tps)

    def prompt_spec(self, d, nargs):
        last = self.np_tiles - 1
        if nargs == 1:
            return pl.BlockSpec((self.tm, d), lambda i: (jnp.minimum(i, last), 0))
        return pl.BlockSpec((self.tm, d), lambda i, j: (jnp.minimum(i, last), 0))

    def sample_spec(self, d, nargs):
        first = self.np_tiles
        if nargs == 1:
            return pl.BlockSpec((self.tm, d), lambda i: (jnp.maximum(i - first, 0), 0))
        return pl.BlockSpec((self.tm, d), lambda i, j: (jnp.maximum(i - first, 0), 0))

    def mod_spec(self, layer, which, d, nargs):
        if nargs == 1:
            return pl.BlockSpec((None, None, None, 1, d), lambda i: (layer, self.mod_row(i), which, 0, 0))
        return pl.BlockSpec((None, None, None, 1, d), lambda i, j: (layer, self.mod_row(i), which, 0, 0))


def _mod_kernel(c_ref, w_ref, b_ref, o_ref):
    c = c_ref[...]
    s = c * jax.nn.sigmoid(c)
    o_ref[...] = jnp.dot(s.astype(BF16), w_ref[...].astype(BF16), preferred_element_type=F32) + b_ref[...]


def _modulation(cvec, ada_w, ada_b):
    depth, d, d6 = ada_w.shape
    rows = cvec.shape[0]
    tn = _pow2_tile(2048, d6) if d6 % 2048 == 0 else d
    return pl.pallas_call(
        _mod_kernel,
        grid=(depth, d6 // tn),
        in_specs=[pl.BlockSpec((rows, d), lambda l, j: (0, 0)),
                  pl.BlockSpec((None, d, tn), lambda l, j: (l, 0, j)),
                  pl.BlockSpec((None, 1, tn), lambda l, j: (l, 0, j))],
        out_specs=pl.BlockSpec((None, rows, tn), lambda l, j: (l, 0, j)),
        out_shape=jax.ShapeDtypeStruct((depth, rows, d6), F32),
        compiler_params=_cparams("arbitrary", "arbitrary"),
        name="adaln_modulation",
    )(cvec, ada_w, ada_b.reshape(depth, 1, d6))


def _read_x(x_refs, np_tiles, rows=slice(None), cols=slice(None)):
    if len(x_refs) == 1:
        return x_refs[0][rows, cols]
    return jnp.where(pl.program_id(0) < np_tiles, x_refs[0][rows, cols], x_refs[1][rows, cols])


def _x_specs(x, tok, nargs):
    if isinstance(x, tuple):
        d = x[0].shape[1]
        return [tok.prompt_spec(d, nargs), tok.sample_spec(d, nargs)], x
    d = x.shape[1]
    return [pl.BlockSpec((tok.tm, d), (lambda i: (i, 0)) if nargs == 1 else (lambda i, j: (i, 0)))], (x,)


def _s5_pre_kernel(*refs, n_x, np_tiles):
    x_refs, (g_ref, sc_ref, sh_ref, h_ref, h_scr) = refs[:n_x], refs[n_x:]
    x = _read_x(x_refs, np_tiles)
    rstd = lax.rsqrt(jnp.mean(x * x, axis=-1, keepdims=True) + EPS)
    nc = h_ref.shape[1]
    for a in range(h_scr.shape[0]):
        sl = slice(a * LANES, (a + 1) * LANES)
        h_scr[a] = (x[:, sl] * rstd * g_ref[:, sl]) * (1.0 + sc_ref[:, sl]) + sh_ref[:, sl]
        for j in range(S5_CHUNK):
            h_ref[j, :, sl] = h_scr[a, pl.ds(j, nc, stride=S5_CHUNK), :].astype(h_ref.dtype)


def _s5_pre(x, g, mod, layer, tok):
    x_specs, xs = _x_specs(x, tok, 1)
    d = xs[0].shape[1]
    nc = tok.tm // S5_CHUNK
    return pl.pallas_call(
        functools.partial(_s5_pre_kernel, n_x=len(xs), np_tiles=tok.np_tiles),
        grid=(tok.tiles,),
        in_specs=x_specs + [pl.BlockSpec((1, d), lambda i: (0, 0)),
                            tok.mod_spec(layer, 1, d, 1),
                            tok.mod_spec(layer, 0, d, 1)],
        out_specs=pl.BlockSpec((S5_CHUNK, nc, d), lambda i: (0, i, 0)),
        out_shape=jax.ShapeDtypeStruct((S5_CHUNK, tok.n // S5_CHUNK, d), BF16),
        scratch_shapes=[pltpu.VMEM((d // LANES, tok.tm, LANES), F32)],
        compiler_params=_cparams("arbitrary"),
        name="s5_pre",
    )(*xs, g, mod, mod)


def _s5_scan_kernel(h_ref, wi_ref, ws_ref, w2t_ref, at_ref, h0_ref, *rest, streams, unroll):
    ns = len(streams)
    perm_refs, (y_ref, fin_ref, xt_scr, yt_scr), chain_scr = rest[:ns], rest[ns:ns + 4], rest[ns + 4:]
    ngrp = wi_ref.shape[0]
    p2 = at_ref.shape[-1]
    fwd = lax.broadcasted_iota(jnp.int32, (1, p2), 1) < (p2 // 2)
    for j in range(S5_CHUNK):
        xt_scr[j] = h_ref[j].astype(F32).T.astype(BF16)

    def load_rmat(r, stream):
        row0, n_chunks, nb = stream[:3]
        cols = slice(row0, row0 + n_chunks * nb)
        c0 = pl.multiple_of(r * S5_GROUP, S5_GROUP)
        return jnp.concatenate([xt_scr[j, pl.ds(c0, S5_GROUP), cols] for j in range(S5_CHUNK)], axis=0), c0, cols

    def summaries(r, stream, perm_ref, s_scr):
        nb, use_h0 = stream[2], stream[3]
        rmat, _, _ = load_rmat(r, stream)
        rmat_cm = jnp.dot(rmat, perm_ref[...], preferred_element_type=F32).astype(BF16)
        zs = jnp.dot(ws_ref[r], rmat_cm, preferred_element_type=F32)
        s_scr[0] = zs[0:p2].T
        s_scr[1] = zs[p2:].T
        if use_h0:
            return h0_ref[r, 0], h0_ref[r, 1]
        return jnp.zeros((nb, p2), F32), jnp.zeros((nb, p2), F32)

    def scan_step(t, decay, stream, state, s_scr, e_scr):
        n_chunks, nb = stream[1], stream[2]
        hr, hi = state
        ar, ai = decay
        ft = slice(t * nb, (t + 1) * nb)
        bt = slice((n_chunks - 1 - t) * nb, (n_chunks - t) * nb)
        e_scr[0, ft, :] = hr
        e_scr[1, bt, :] = hr
        e_scr[2, ft, :] = hi
        e_scr[3, bt, :] = hi
        in_re = jnp.where(fwd, s_scr[0, ft, :], s_scr[0, bt, :])
        in_im = jnp.where(fwd, s_scr[1, ft, :], s_scr[1, bt, :])
        return ar * hr - ai * hi + in_re, ar * hi + ai * hr + in_im

    def outputs(r, stream, perm_ref, state, e_scr):
        rmat, c0, cols = load_rmat(r, stream)
        if stream[4]:
            fin_ref[r, 0] = state[0]
            fin_ref[r, 1] = state[1]
        e_cm = jnp.concatenate([jnp.where(fwd, e_scr[0], e_scr[1]), jnp.where(fwd, e_scr[2], e_scr[3])],
                               axis=1).astype(BF16)
        e = jnp.dot(perm_ref[...], e_cm, preferred_element_type=F32).astype(BF16)
        yt = (jnp.dot(wi_ref[r], rmat, preferred_element_type=F32)
              + lax.dot_general(w2t_ref[r], e, (((1,), (1,)), ((), ())), preferred_element_type=F32))
        for i in range(S5_CHUNK):
            yt_scr[i, pl.ds(c0, S5_GROUP), cols] = yt[i * S5_GROUP:(i + 1) * S5_GROUP]

    def body(rb, carry):
        chains = []
        for u in range(unroll):
            for si, stream in enumerate(streams):
                k = 2 * (u * ns + si)
                chains.append((rb * unroll + u, stream, perm_refs[si], chain_scr[k], chain_scr[k + 1]))
        states = [summaries(r, stream, pm, s_scr) for r, stream, pm, s_scr, _ in chains]
        decays = [(at_ref[r, 0:1, :], at_ref[r, 1:2, :]) for r, _, _, _, _ in chains]
        for t in range(max(stream[1] for stream in streams)):
            for ci, (_, stream, _, s_scr, e_scr) in enumerate(chains):
                if t < stream[1]:
                    states[ci] = scan_step(t, decays[ci], stream, states[ci], s_scr, e_scr)
        for (r, stream, pm, _, e_scr), state in zip(chains, states):
            outputs(r, stream, pm, state, e_scr)
        return carry

    lax.fori_loop(0, ngrp // unroll, body, 0)
    for i in range(S5_CHUNK):
        y_ref[i] = yt_scr[i].T


def _s5_scan(hperm, w_intra, w_sum, w2t, at, h0, nb_fin, streams):
    t, nrows, d = hperm.shape
    g, tk, p4 = w2t.shape
    p2 = at.shape[-1]
    gl = LANES // S5_GROUP
    nb0 = h0.shape[2]
    unroll = 4
    assert sum(s[1] * s[2] for s in streams) == nrows and gl % unroll == 0
    chain_scr = []
    for _ in range(unroll):
        for s in streams:
            chain_scr += [pltpu.VMEM((2, s[1] * s[2], p2), F32), pltpu.VMEM((4, s[1] * s[2], p2), F32)]
    perms = []
    for _, n_chunks, nb, _, _ in streams:
        m = jnp.arange(n_chunks * nb)
        perms.append((m[:, None] == ((m % nb) * n_chunks + m // nb)[None, :]).astype(BF16))
    return pl.pallas_call(
        functools.partial(_s5_scan_kernel, streams=streams, unroll=unroll),
        grid=(d // LANES,),
        in_specs=[pl.BlockSpec((t, nrows, LANES), lambda a: (0, 0, a)),
                  pl.BlockSpec((gl, tk, tk), lambda a: (a, 0, 0)),
                  pl.BlockSpec((gl, p4, tk), lambda a: (a, 0, 0)),
                  pl.BlockSpec((gl, tk, p4), lambda a: (a, 0, 0)),
                  pl.BlockSpec((gl, 2, p2), lambda a: (a, 0, 0)),
                  pl.BlockSpec((gl, 2, nb0, p2), lambda a: (a, 0, 0, 0))]
                 + [pl.BlockSpec(pm.shape, lambda a: (0, 0)) for pm in perms],
        out_specs=[pl.BlockSpec((t, nrows, LANES), lambda a: (0, 0, a)),
                   pl.BlockSpec((gl, 2, nb_fin, p2), lambda a: (a, 0, 0, 0))],
        out_shape=[jax.ShapeDtypeStruct((t, nrows, d), F32),
                   jax.ShapeDtypeStruct((g, 2, nb_fin, p2), F32)],
        scratch_shapes=[pltpu.VMEM((t, LANES, nrows), BF16), pltpu.VMEM((t, LANES, nrows), F32)] + chain_scr,
        compiler_params=_cparams("arbitrary"),
        name="s5_scan",
    )(hperm, w_intra, w_sum, w2t, at, h0, *perms)


def _s5_post_kernel(*refs, n_x, np_tiles):
    x_refs = refs[:n_x]
    (y_ref, g_ref, sc_ref, sh_ref, gate_ref, dsk_ref, wa_ref, wg_ref, ba_ref, bg_ref, o_ref,
     y_scr, wa_scr, wg_scr) = refs[n_x:]
    _cast_once(wa_ref, wa_scr)
    _cast_once(wg_ref, wg_scr)
    tm, d = o_ref.shape
    n_cb = max(d // (2 * LANES), 1)
    cw = d // n_cb

    def before(rows):
        c0, nc = rows.start // S5_CHUNK, (rows.stop - rows.start) // S5_CHUNK
        for a in range(y_scr.shape[0]):
            for j in range(S5_CHUNK):
                y_scr[a, pl.ds(rows.start + j, nc, stride=S5_CHUNK), :] = y_ref[j, c0:c0 + nc, a * LANES:(a + 1) * LANES]
        x = _read_x(x_refs, np_tiles, rows)
        h = _normmod(x, g_ref[...], sc_ref[...], sh_ref[...])
        y = h * dsk_ref[...] + jnp.concatenate([y_scr[a, rows, :] for a in range(y_scr.shape[0])], axis=1)
        return jax.nn.gelu(y).astype(BF16)

    def matmuls(a, cb):
        cols = slice(cb * cw, (cb + 1) * cw)
        return (jnp.dot(a, wa_scr[:, cols], preferred_element_type=F32),
                jnp.dot(a, wg_scr[:, cols], preferred_element_type=F32))

    def after(rows, cb, z):
        cols = slice(cb * cw, (cb + 1) * cw)
        za, zg = z[0] + ba_ref[:, cols], z[1] + bg_ref[:, cols]
        x = _read_x(x_refs, np_tiles, rows, cols)
        o_ref[rows, cols] = x + gate_ref[:, cols] * (za * jax.nn.sigmoid(zg))

    def row_chunks(rows):
        step = max((rows.stop - rows.start) // n_cb, S5_CHUNK)
        return [slice(r, min(r + step, rows.stop)) for r in range(rows.start, rows.stop, step)]

    halves = _row_halves(tm)
    a_cur = jnp.concatenate([before(rc) for rc in row_chunks(halves[0])], axis=0)
    z_prev = None
    for hi, rows in enumerate(halves):
        nxt = row_chunks(halves[hi + 1]) if hi + 1 < len(halves) else []
        a_next, z_cur = [], []
        for cb in range(n_cb):
            z_cur.append(matmuls(a_cur, cb))
            if cb < len(nxt):
                a_next.append(before(nxt[cb]))
            if z_prev is not None:
                after(halves[hi - 1], cb, z_prev[cb])
        a_next += [before(rc) for rc in nxt[n_cb:]]
        z_prev = z_cur
        if a_next:
            a_cur = jnp.concatenate(a_next, axis=0)
    for cb in range(n_cb):
        after(halves[-1], cb, z_prev[cb])


def _s5_post(x, y, g, mod, layer, dsk, glu_w, slot, glu_b, tok):
    x_specs, xs = _x_specs(x, tok, 1)
    n, d = tok.n, xs[0].shape[1]
    row = lambda i: (i, 0)
    fixed = lambda i: (0, 0)
    gb2 = glu_b.reshape(1, 2 * d)
    return pl.pallas_call(
        functools.partial(_s5_post_kernel, n_x=len(xs), np_tiles=tok.np_tiles),
        grid=(tok.tiles,),
        in_specs=x_specs + [
                  pl.BlockSpec((S5_CHUNK, tok.tm // S5_CHUNK, d), lambda i: (0, i, 0)),
                  pl.BlockSpec((1, d), fixed),
                  tok.mod_spec(layer, 1, d, 1),
                  tok.mod_spec(layer, 0, d, 1),
                  tok.mod_spec(layer, 2, d, 1),
                  pl.BlockSpec((1, d), fixed),
                  _resident((None, d, d), lambda i: (slot, 0, 0)),
                  _resident((None, d, d), lambda i: (slot, 0, 1)),
                  pl.BlockSpec((1, d), lambda i: (0, 0)),
                  pl.BlockSpec((1, d), lambda i: (0, 1))],
        out_specs=pl.BlockSpec((tok.tm, d), row),
        out_shape=jax.ShapeDtypeStruct((n, d), F32),
        scratch_shapes=[pltpu.VMEM((d // LANES, tok.tm, LANES), F32),
                        pltpu.VMEM((d, d), BF16), pltpu.VMEM((d, d), BF16)],
        compiler_params=_cparams("arbitrary"),
        name="s5_post",
    )(*xs, y, g, mod, mod, mod, dsk, glu_w, glu_w, gb2, gb2)


def _cast_once(w_ref, w_scr):
    @pl.when(pl.program_id(0) == 0)
    def _():
        w_scr[...] = w_ref[...].astype(w_scr.dtype)


def _resident(block_shape, index_map):
    return pl.BlockSpec(block_shape, index_map, pipeline_mode=pl.Buffered(1))


def _na_qkv_kernel(x_ref, g_ref, sc_ref, sh_ref, w_ref, qn_ref, kn_ref, q_ref, kp_ref, ks_ref, vp_ref, vs_ref, w_scr, *,
                   hd, np_tiles):
    i = pl.program_id(0)
    _cast_once(w_ref, w_scr)
    d = x_ref.shape[1]
    lo = lax.broadcasted_iota(jnp.int32, (1, LANES), 1) < hd

    def head_norm(z, gain):
        outs = []
        for s in range(d // LANES):
            seg = z[:, s * LANES:(s + 1) * LANES]
            sq = seg * seg
            s_lo = jnp.sum(jnp.where(lo, sq, 0.0), axis=-1, keepdims=True)
            s_hi = jnp.sum(jnp.where(lo, 0.0, sq), axis=-1, keepdims=True)
            ms = jnp.where(lo, s_lo, s_hi) / hd
            outs.append(seg * lax.rsqrt(ms + EPS) * gain)
        return jnp.concatenate(outs, axis=1)

    h = _normmod(x_ref[...], g_ref[...], sc_ref[...], sh_ref[...]).astype(BF16)
    proj = lambda part: jnp.dot(h, w_scr[:, part * d:(part + 1) * d], preferred_element_type=F32)
    q_ref[...] = (head_norm(proj(0), qn_ref[...]) * (hd ** -0.5 * LOG2E)).astype(q_ref.dtype)
    k = head_norm(proj(1), kn_ref[...])
    v = proj(2)

    @pl.when(i < np_tiles)
    def _():
        kp_ref[...] = k
        vp_ref[...] = v

    @pl.when(i >= np_tiles)
    def _():
        ks_ref[...] = k.astype(ks_ref.dtype)
        vs_ref[...] = v.astype(vs_ref.dtype)


def _na_qkv(x, g, mod, layer, w_qkv, slot, qn, kn, tok):
    n, d = x.shape
    hd = qn.shape[-1]
    assert 2 * hd == LANES
    row = lambda i: (i, 0)
    fixed = lambda i: (0, 0)
    qn2 = jnp.tile(qn, 2).reshape(1, LANES)
    kn2 = jnp.tile(kn, 2).reshape(1, LANES)
    n_p, n_s = tok.np_tiles * tok.tm, n - tok.np_tiles * tok.tm
    return pl.pallas_call(
        functools.partial(_na_qkv_kernel, hd=hd, np_tiles=tok.np_tiles),
        grid=(tok.tiles,),
        in_specs=[pl.BlockSpec((tok.tm, d), row),
                  pl.BlockSpec((1, d), fixed),
                  tok.mod_spec(layer, 1, d, 1),
                  tok.mod_spec(layer, 0, d, 1),
                  _resident((None, d, 3 * d), lambda i: (slot, 0, 0)),
                  pl.BlockSpec((1, LANES), fixed),
                  pl.BlockSpec((1, LANES), fixed)],
        out_specs=[pl.BlockSpec((tok.tm, d), row),
                   tok.prompt_spec(d, 1), tok.sample_spec(d, 1), tok.prompt_spec(d, 1), tok.sample_spec(d, 1)],
        out_shape=[jax.ShapeDtypeStruct((n, d), BF16),
                   jax.ShapeDtypeStruct((n_p, d), F32), jax.ShapeDtypeStruct((n_s, d), BF16),
                   jax.ShapeDtypeStruct((n_p, d), F32), jax.ShapeDtypeStruct((n_s, d), BF16)],
        scratch_shapes=[pltpu.VMEM((d, 3 * d), BF16)],
        compiler_params=_cparams("arbitrary"),
        name="na_qkv",
    )(x, g, mod, mod, w_qkv, qn2, kn2)


def _softmax2_pv(s_parts, v_parts):
    m = s_parts[0].max(axis=-1, keepdims=True)
    for s in s_parts[1:]:
        m = jnp.maximum(m, s.max(axis=-1, keepdims=True))
    den = 0.0
    acc = 0.0
    for s, v in zip(s_parts, v_parts):
        e = jnp.exp2(s - m)
        den = den + e.sum(axis=-1, keepdims=True)
        acc = acc + jnp.dot(e.astype(BF16), v, preferred_element_type=F32)
    return acc / den


def _qk(q, k):
    return lax.dot_general(q, k, (((1,), (1,)), ((), ())), preferred_element_type=F32)


def _na_ctx_attn_kernel(q_ref, k_ref, v_ref, o_ref, *, hd):
    lo = lax.broadcasted_iota(jnp.int32, (1, LANES), 1) < hd
    for s in range(q_ref.shape[1] // LANES):
        sl = slice(s * LANES, (s + 1) * LANES)
        q = q_ref[:, sl]
        k = k_ref[:, sl].astype(BF16)
        v = v_ref[:, sl].astype(BF16)
        zero = jnp.zeros_like(q)
        q2 = jnp.concatenate([jnp.where(lo, q, zero), jnp.where(lo, zero, q)], axis=0)
        out = _softmax2_pv([_qk(q2, k)], [v])
        n = q.shape[0]
        o_ref[:, sl] = jnp.where(lo, out[0:n], out[n:]).astype(o_ref.dtype)


def _na_ctx_attn(q, k, v, batch, seq, hd):
    d = q.shape[1]
    spec = pl.BlockSpec((seq, d), lambda b: (b, 0))
    return pl.pallas_call(
        functools.partial(_na_ctx_attn_kernel, hd=hd),
        grid=(batch,),
        in_specs=[spec, spec, spec],
        out_specs=spec,
        out_shape=jax.ShapeDtypeStruct((batch * seq, d), BF16),
        compiler_params=_cparams("arbitrary"),
        name="na_ctx_attn",
    )(q, k, v)


def _na_window(r0, q_rows, rows, kh):
    w_rows = min(rows, kh + q_rows + (kh + q_rows) % 2)
    rs = min(max(r0 - kh // 2, 0), rows - kh)
    return min(rs // 2 * 2, rows - w_rows), w_rows


def _na_lat_attn_kernel(q_ref, kl_ref, vl_ref, kc_ref, vc_ref, tab_ref, o_ref, kc_scr, vc_scr, bias_scr, *,
                        hd, rows, kh, tq):
    lane = lax.broadcasted_iota(jnp.int32, (1, LANES), 1)
    lo = lane < hd
    q_rows = tq // GRID_W
    n_qb = q_ref.shape[0] // tq
    w_cols = bias_scr.shape[2]
    kc_scr[...] = kc_ref[...].astype(BF16)
    vc_scr[...] = vc_ref[...].astype(BF16)

    @pl.when(pl.program_id(1) == 0)
    def _():
        neg = jnp.full((GRID_W, GRID_W), -jnp.inf, F32)
        for hh in range(2):
            for rq in range(rows):
                w0, w_rows = _na_window(rq // q_rows * q_rows, q_rows, rows, kh)
                rs = min(max(rq - kh // 2, 0), rows - kh)
                for kr in range(w_rows):
                    rk = w0 + kr
                    tile = tab_ref[hh, rk - rq + kh - 1] if rs <= rk < rs + kh else neg
                    bias_scr[hh, rq * GRID_W:(rq + 1) * GRID_W, kr * GRID_W:(kr + 1) * GRID_W] = tile

    per_iter = 4 if n_qb % 4 == 0 else 1

    def qblocks(it, carry):
        work = []
        for u in range(per_iter):
            qb = it * per_iter + u
            r0 = qb * q_rows
            w0 = jnp.minimum(jnp.clip(r0 - kh // 2, 0, rows - kh) // 2 * 2, rows - w_cols // GRID_W)
            k0 = pl.multiple_of(w0 * GRID_W, 2 * GRID_W)
            q0 = pl.multiple_of(qb * tq, tq)
            q = q_ref[pl.ds(q0, tq), :]
            kl = kl_ref[pl.ds(k0, w_cols), :]
            zero = jnp.zeros_like(q)
            q2 = jnp.concatenate([jnp.where(lo, q, zero), jnp.where(lo, zero, q)], axis=0)
            bias2 = jnp.concatenate([bias_scr[0, pl.ds(q0, tq), :], bias_scr[1, pl.ds(q0, tq), :]], axis=0)
            work.append((q0, k0, [_qk(q2, kl) + bias2, _qk(q2, kc_scr[...])]))
        for q0, k0, scores in work:
            vl = vl_ref[pl.ds(k0, w_cols), :]
            out = _softmax2_pv(scores, [vl, vc_scr[...]])
            o_ref[pl.ds(q0, tq), :] = jnp.where(lo, out[0:tq], out[tq:]).astype(o_ref.dtype)
        return carry

    lax.fori_loop(0, n_qb // per_iter, qblocks, 0)


def _na_lat_attn(q, k, v, cache_k, cache_v, slot, tab, n_prompt, dec_batch, dec_seq, hd):
    d = q.shape[1]
    past = cache_k.shape[2]
    rows = dec_seq // GRID_W
    kh = min(NA_KH, rows)
    tq = _pow2_tile(256, dec_seq)
    assert tq % GRID_W == 0 and rows % 2 == 0 and 2 * GRID_W == LANES and n_prompt % dec_seq == 0
    b0 = n_prompt // dec_seq
    w_rows = _na_window(0, tq // GRID_W, rows, kh)[1]
    return pl.pallas_call(
        functools.partial(_na_lat_attn_kernel, hd=hd, rows=rows, kh=kh, tq=tq),
        grid=(d // LANES, dec_batch),
        in_specs=[pl.BlockSpec((dec_seq, LANES), lambda hp, b: (b0 + b, hp)),
                  pl.BlockSpec((dec_seq, LANES), lambda hp, b: (b, hp)),
                  pl.BlockSpec((dec_seq, LANES), lambda hp, b: (b, hp)),
                  pl.BlockSpec((None, None, past, LANES), lambda hp, b: (b, slot, 0, hp)),
                  pl.BlockSpec((None, None, past, LANES), lambda hp, b: (b, slot, 0, hp)),
                  pl.BlockSpec((2,) + tab.shape[1:], lambda hp, b: (hp, 0, 0, 0))],
        out_specs=pl.BlockSpec((dec_seq, LANES), lambda hp, b: (b, hp)),
        out_shape=jax.ShapeDtypeStruct((dec_batch * dec_seq, d), BF16),
        scratch_shapes=[pltpu.VMEM((past, LANES), BF16), pltpu.VMEM((past, LANES), BF16),
                        pltpu.VMEM((2, dec_seq, w_rows * GRID_W), F32)],
        compiler_params=_cparams("arbitrary", "arbitrary"),
        name="na_lat_attn",
    )(q, k, v, cache_k, cache_v, tab)


def _na_bias_table(rpb):
    nh, nd, nc = rpb.shape
    c = jnp.arange(GRID_W)
    cs = jnp.clip(c - NA_KW // 2, 0, GRID_W - NA_KW)
    col_ok = (c[None, :] >= cs[:, None]) & (c[None, :] < cs[:, None] + NA_KW)
    dcol = c[None, :] - c[:, None] + NA_KW - 1
    onehot = (dcol[None] == jnp.arange(nc)[:, None, None]).astype(F32)
    tmp = jnp.einsum('hdc,cxy->hdxy', rpb.astype(F32), onehot, precision=lax.Precision.HIGHEST)
    return jnp.where(col_ok[None, None], tmp * LOG2E, -jnp.inf)


def _gqa_qkv_kernel(x_ref, g_ref, sc_ref, sh_ref, w_ref, qn_ref, kn_ref, cos_ref, sin_ref,
                    q_ref, kp_ref, ks_ref, vp_ref, vs_ref, w_scr, *, nq, nk, np_tiles):
    i = pl.program_id(0)
    _cast_once(w_ref, w_scr)
    is_sample = i >= np_tiles
    ks, vs = [], []
    for rows in _row_halves(x_ref.shape[0]):
        h = _normmod(x_ref[rows, :], g_ref[...], sc_ref[...], sh_ref[...]).astype(BF16)
        z = jnp.dot(h, w_scr[...], preferred_element_type=F32)
        cos = cos_ref[rows, :]
        sin = sin_ref[rows, :]

        def norm_rope(seg, gain):
            ms = jnp.mean(seg * seg, axis=-1, keepdims=True)
            y = seg * lax.rsqrt(ms + EPS) * gain
            roped = y * cos + pltpu.roll(y, LANES // 2, 1) * sin
            return jnp.where(is_sample, roped, y)

        for hh in range(nq):
            sl = slice(hh * LANES, (hh + 1) * LANES)
            q_ref[rows, sl] = (norm_rope(z[:, sl], qn_ref[...]) * (LANES ** -0.5 * LOG2E)).astype(q_ref.dtype)
        ks.append(jnp.concatenate(
            [norm_rope(z[:, (nq + hh) * LANES:(nq + hh + 1) * LANES], kn_ref[...]) for hh in range(nk)], axis=1))
        vs.append(z[:, (nq + nk) * LANES:])
    k = jnp.concatenate(ks, axis=0)
    v = jnp.concatenate(vs, axis=0)

    @pl.when(i < np_tiles)
    def _():
        kp_ref[...] = k
        vp_ref[...] = v

    @pl.when(is_sample)
    def _():
        ks_ref[...] = k.astype(ks_ref.dtype)
        vs_ref[...] = v.astype(vs_ref.dtype)


def _gqa_qkv(x, g, mod, layer, w_qkv, slot, qn, kn, cos_t, sin_t, nk, tok):
    n, d = x.shape
    hd = qn.shape[-1]
    assert hd == LANES
    nq = d // hd
    dk = nk * hd
    row = lambda i: (i, 0)
    fixed = lambda i: (0, 0)
    pos = lambda i: (jnp.maximum(i - tok.np_tiles, 0) % tok.tps, 0)
    n_p, n_s = tok.np_tiles * tok.tm, n - tok.np_tiles * tok.tm
    return pl.pallas_call(
        functools.partial(_gqa_qkv_kernel, nq=nq, nk=nk, np_tiles=tok.np_tiles),
        grid=(tok.tiles,),
        in_specs=[pl.BlockSpec((tok.tm, d), row),
                  pl.BlockSpec((1, d), fixed),
                  tok.mod_spec(layer, 1, d, 1),
                  tok.mod_spec(layer, 0, d, 1),
                  _resident((None, d, d + 2 * dk), lambda i: (slot, 0, 0)),
                  pl.BlockSpec((1, hd), fixed),
                  pl.BlockSpec((1, hd), fixed),
                  pl.BlockSpec((tok.tm, hd), pos),
                  pl.BlockSpec((tok.tm, hd), pos)],
        out_specs=[pl.BlockSpec((tok.tm, d), row),
                   tok.prompt_spec(dk, 1), tok.sample_spec(dk, 1), tok.prompt_spec(dk, 1), tok.sample_spec(dk, 1)],
        out_shape=[jax.ShapeDtypeStruct((n, d), BF16),
                   jax.ShapeDtypeStruct((n_p, dk), F32), jax.ShapeDtypeStruct((n_s, dk), BF16),
                   jax.ShapeDtypeStruct((n_p, dk), F32), jax.ShapeDtypeStruct((n_s, dk), BF16)],
        scratch_shapes=[pltpu.VMEM((d, d + 2 * dk), BF16)],
        compiler_params=_cparams("arbitrary"),
        name="gqa_qkv",
    )(x, g, mod, mod, w_qkv, qn.reshape(1, hd), kn.reshape(1, hd), cos_t, sin_t)


def _rope_tables(dec_seq, hd):
    t = jnp.arange(dec_seq)
    row = (t // GRID_W).astype(F32)
    col = (t % GRID_W).astype(F32)
    half = hd // 2
    inv = ROPE_THETA ** (-jnp.arange(0, half, 2, dtype=F32) / half)
    ang = jnp.concatenate([row[:, None] * inv, col[:, None] * inv], axis=-1)
    cos, sin = jnp.cos(ang), jnp.sin(ang)
    return jnp.concatenate([cos, cos], axis=-1), jnp.concatenate([-sin, sin], axis=-1)


def _gqa_attn_kernel(q_ref, *refs, rep, nk, n_kv):
    k_refs, v_refs, o_ref = refs[:n_kv], refs[n_kv:2 * n_kv], refs[2 * n_kv]
    tq = q_ref.shape[0]
    scores, values = [], []
    for kv in range(nk):
        sl = slice(kv * LANES, (kv + 1) * LANES)
        qs = jnp.concatenate([q_ref[:, (kv * rep + r) * LANES:(kv * rep + r + 1) * LANES] for r in range(rep)], axis=0)
        scores.append([_qk(qs, r[:, sl].astype(BF16)) for r in k_refs])
        values.append([r[:, sl].astype(BF16) for r in v_refs])
    for kv in range(nk):
        o = _softmax2_pv(scores[kv], values[kv])
        for r in range(rep):
            o_ref[:, (kv * rep + r) * LANES:(kv * rep + r + 1) * LANES] = o[r * tq:(r + 1) * tq].astype(o_ref.dtype)


def _gqa_ctx_attn(q, k, v, batch, seq, nk):
    d = q.shape[1]
    dk = k.shape[1]
    rep = d // dk
    qspec = pl.BlockSpec((seq, d), lambda b: (b, 0))
    kspec = pl.BlockSpec((seq, dk), lambda b: (b, 0))
    return pl.pallas_call(
        functools.partial(_gqa_attn_kernel, rep=rep, nk=nk, n_kv=1),
        grid=(batch,),
        in_specs=[qspec, kspec, kspec],
        out_specs=qspec,
        out_shape=jax.ShapeDtypeStruct((batch * seq, d), BF16),
        compiler_params=_cparams("arbitrary"),
        name="gqa_ctx_attn",
    )(q, k, v)


def _gqa_lat_attn(q, k, v, cache_k, cache_v, slot, n_prompt, dec_batch, dec_seq, nk):
    d = q.shape[1]
    dk = k.shape[1]
    rep = d // dk
    past = cache_k.shape[2]
    tq = _pow2_tile(256, dec_seq, n_prompt)
    nqb = dec_seq // tq
    q0 = n_prompt // tq
    lspec = pl.BlockSpec((dec_seq, dk), lambda b, qb: (b, 0))
    cspec = pl.BlockSpec((None, None, past, dk), lambda b, qb: (b, slot, 0, 0))
    return pl.pallas_call(
        functools.partial(_gqa_attn_kernel, rep=rep, nk=nk, n_kv=2),
        grid=(dec_batch, nqb),
        in_specs=[pl.BlockSpec((tq, d), lambda b, qb: (q0 + b * nqb + qb, 0)),
                  lspec, cspec, lspec, cspec],
        out_specs=pl.BlockSpec((tq, d), lambda b, qb: (b * nqb + qb, 0)),
        out_shape=jax.ShapeDtypeStruct((dec_batch * dec_seq, d), BF16),
        compiler_params=_cparams("arbitrary", "arbitrary"),
        name="gqa_lat_attn",
    )(q, k, cache_k, v, cache_v)


def _proj_res_kernel(x_ref, op_ref, os_ref, w_ref, gate_ref, out_ref, w_scr, *, np_tiles):
    _cast_once(w_ref, w_scr)
    o = jnp.where(pl.program_id(0) < np_tiles, op_ref[...], os_ref[...])
    y = jnp.dot(o, w_scr[...], preferred_element_type=F32)
    out_ref[...] = x_ref[...] + gate_ref[...] * y


def _proj_res(x, o_p, o_s, w_o, slot, mod, layer, tok):
    n, d = x.shape
    row = lambda i: (i, 0)
    return pl.pallas_call(
        functools.partial(_proj_res_kernel, np_tiles=tok.np_tiles),
        grid=(tok.tiles,),
        in_specs=[pl.BlockSpec((tok.tm, d), row),
                  tok.prompt_spec(d, 1),
                  tok.sample_spec(d, 1),
                  _resident((None, d, d), lambda i: (slot, 0, 0)),
                  tok.mod_spec(layer, 2, d, 1)],
        out_specs=pl.BlockSpec((tok.tm, d), row),
        out_shape=jax.ShapeDtypeStruct((n, d), F32),
        scratch_shapes=[pltpu.VMEM((d, d), BF16)],
        compiler_params=_cparams("arbitrary"),
        name="attn_proj_res",
    )(x, o_p, o_s, w_o, mod)


def _mlp_kernel(x_ref, g_ref, sc_ref, sh_ref, gate_ref, w1_ref, w2_ref, *rest, np_tiles):
    out_refs, (h_scr, acc_scr) = rest[:-2], rest[-2:]
    i = pl.program_id(0)
    j = pl.program_id(1)
    last = j == pl.num_programs(1) - 1

    @pl.when(j == 0)
    def _():
        h_scr[...] = _normmod(x_ref[...], g_ref[...], sc_ref[...], sh_ref[...]).astype(BF16)
        acc_scr[...] = jnp.zeros_like(acc_scr)

    a = jnp.maximum(jnp.dot(h_scr[...], w1_ref[...].astype(BF16), preferred_element_type=F32), 0.0)
    acc_scr[...] += jnp.dot((a * a).astype(BF16), w2_ref[...].astype(BF16), preferred_element_type=F32)

    def result():
        return x_ref[...] + gate_ref[...] * acc_scr[...]

    if len(out_refs) == 1:
        @pl.when(last)
        def _():
            out_refs[0][...] = result()
    else:
        @pl.when(jnp.logical_and(last, i < np_tiles))
        def _():
            out_refs[0][...] = result()

        @pl.when(jnp.logical_and(last, i >= np_tiles))
        def _():
            out_refs[1][...] = result()


def _mlp(x, g, mod, layer, w1, w2, tok, split_out):
    n, d = x.shape
    f = w1.shape[2]
    tf = _pow2_tile(1024, f)
    row = lambda i, j: (i, 0)
    if split_out:
        n_p = tok.np_tiles * tok.tm
        out_specs = [tok.prompt_spec(d, 2), tok.sample_spec(d, 2)]
        out_shape = [jax.ShapeDtypeStruct((n_p, d), F32), jax.ShapeDtypeStruct((n - n_p, d), F32)]
    else:
        out_specs = pl.BlockSpec((tok.tm, d), row)
        out_shape = jax.ShapeDtypeStruct((n, d), F32)
    return pl.pallas_call(
        functools.partial(_mlp_kernel, np_tiles=tok.np_tiles),
        grid=(tok.tiles, f // tf),
        in_specs=[pl.BlockSpec((tok.tm, d), row),
                  pl.BlockSpec((1, d), lambda i, j: (0, 0)),
                  tok.mod_spec(layer, 4, d, 2),
                  tok.mod_spec(layer, 3, d, 2),
                  tok.mod_spec(layer, 5, d, 2),
                  pl.BlockSpec((None, d, tf), lambda i, j: (layer, 0, j)),
                  pl.BlockSpec((None, tf, d), lambda i, j: (layer, j, 0))],
        out_specs=out_specs,
        out_shape=out_shape,
        scratch_shapes=[pltpu.VMEM((tok.tm, d), BF16), pltpu.VMEM((tok.tm, d), F32)],
        compiler_params=_cparams("arbitrary", "arbitrary"),
        name="mlp",
    )(x, g, mod, mod, mod, w1, w2)


def _s5_tables(lam_re, lam_im, log_dt, b_re, b_im, c_re, c_im):
    t = S5_CHUNK
    g, p, k = b_re.shape[1:]
    dt = jnp.exp(log_dt.astype(F32))[:, :, None]
    lr, li = lam_re.astype(F32), lam_im.astype(F32)
    ar, ai = lr * dt, li * dt
    mag = jnp.exp(ar)
    abr, abi = mag * jnp.cos(ai), mag * jnp.sin(ai)
    nr, ni = abr - 1.0, abi
    den = lr * lr + li * li
    f_re = (nr * lr + ni * li) / den
    f_im = (ni * lr - nr * li) / den
    bbr = f_re[..., None] * b_re - f_im[..., None] * b_im
    bbi = f_re[..., None] * b_im + f_im[..., None] * b_re
    n = jnp.arange(t + 1, dtype=F32)[None, None, :, None]
    pm = jnp.exp(n * ar[:, :, None, :])
    pr, pi = pm * jnp.cos(n * ai[:, :, None, :]), pm * jnp.sin(n * ai[:, :, None, :])
    cr, ci = c_re.astype(F32), c_im.astype(F32)

    assert 2 * p == LANES
    cat = lambda a, b: jnp.concatenate([a, b], axis=-1)
    bt_r, bt_i = bbr.transpose(0, 1, 3, 2), bbi.transpose(0, 1, 3, 2)
    bb2 = jnp.stack([cat(bt_r[0], bt_i[0]), cat(-bt_i[0], bt_r[0]),
                     cat(bt_r[1], bt_i[1]), cat(-bt_i[1], bt_r[1])], axis=1)
    pws = jnp.stack([cat(pr[0, :, :t][:, ::-1], pr[0, :, :t][:, ::-1]), cat(pi[0, :, :t][:, ::-1], pi[0, :, :t][:, ::-1]),
                     cat(pr[1, :, :t], pr[1, :, :t]), cat(pi[1, :, :t], pi[1, :, :t])], axis=1)
    pwr = jnp.stack([cat(pr[0, :, 1:], pr[1, :, 1:][:, ::-1]), cat(pi[0, :, 1:], pi[1, :, 1:][:, ::-1])], axis=1)
    c2 = jnp.stack([cat(cr[0], cr[1]), cat(ci[0], ci[1])], axis=1)
    cfb = jnp.stack([cat(cr[0], -ci[0]), cat(cr[1], -ci[1])], axis=1)
    w_intra, w_sum, w2t = _s5_toeplitz(cfb, bb2, pws, c2, pwr)
    at = jnp.stack([jnp.concatenate([pr[0, :, t], pr[1, :, t]], axis=-1),
                    jnp.concatenate([pi[0, :, t], pi[1, :, t]], axis=-1)], axis=1)
    return w_intra, w_sum, w2t, at


def _s5_toeplitz_kernel(cfb_ref, bb2_ref, pws_ref, c2_ref, pwr_ref, o_ref, ws_ref, w2t_ref):
    hi = lax.Precision.HIGHEST
    gl, _, k, p2 = cfb_ref.shape
    p = p2 // 2
    tk = o_ref.shape[1]
    t = tk // k
    for g in range(gl):
        def ab(d):
            blocks = [bb2_ref[g, 2 * d] * pws_ref[g, 2 * d, j:j + 1, :]
                      + bb2_ref[g, 2 * d + 1] * pws_ref[g, 2 * d + 1, j:j + 1, :] for j in range(t)]
            return jnp.concatenate(blocks, axis=0).T

        abf, abb = ab(0), ab(1)
        ws_ref[g] = jnp.concatenate([abf[0:p], abb[0:p], abf[p:], abb[p:]], axis=0).astype(ws_ref.dtype)
        cr2, ci2 = c2_ref[g, 0], c2_ref[g, 1]
        rows = []
        for i in range(t):
            wr, wi = pwr_ref[g, 0, i:i + 1, :], pwr_ref[g, 1, i:i + 1, :]
            rows.append(jnp.concatenate([cr2 * wr - ci2 * wi, -(cr2 * wi + ci2 * wr)], axis=1))
        w2t_ref[g] = jnp.concatenate(rows, axis=0).astype(w2t_ref.dtype)
        kf = jnp.dot(cfb_ref[g, 0], abf, precision=hi, preferred_element_type=F32)
        kb = jnp.dot(cfb_ref[g, 1], abb, precision=hi, preferred_element_type=F32)
        z = jnp.zeros_like(kf)
        krev = jnp.concatenate([kf, z], axis=1) + pltpu.roll(jnp.concatenate([z, kb], axis=1), 2 * tk - k, 1)
        for i in range(S5_CHUNK):
            sh = (S5_CHUNK - 1 - i) * k
            win = krev if sh == 0 else pltpu.roll(krev, 2 * tk - sh, 1)
            o_ref[g, i * k:(i + 1) * k, :] = win[:, 0:tk].astype(o_ref.dtype)


def _s5_toeplitz(cfb, bb2, pws, c2, pwr):
    g, _, k, p2 = cfb.shape
    t = pws.shape[2]
    tk = t * k
    gl = _pow2_tile(8, g)
    spec = lambda a: pl.BlockSpec((gl,) + a.shape[1:], lambda i: (i, 0, 0, 0))
    out = lambda rows, cols: (pl.BlockSpec((gl, rows, cols), lambda i: (i, 0, 0)),
                              jax.ShapeDtypeStruct((g, rows, cols), BF16))
    outs = [out(tk, tk), out(2 * p2, tk), out(tk, 2 * p2)]
    return pl.pallas_call(
        _s5_toeplitz_kernel,
        grid=(g // gl,),
        in_specs=[spec(a) for a in (cfb, bb2, pws, c2, pwr)],
        out_specs=[o[0] for o in outs],
        out_shape=[o[1] for o in outs],
        compiler_params=_cparams("arbitrary"),
        name="s5_toeplitz",
    )(cfb, bb2, pws, c2, pwr)


def _s5_mixer(x, g, mod, layer, tok, params, h0, dims):
    (lam_re, lam_im, log_dt, b_re, b_im, c_re, c_im, d_skip, glu_w, slot, glu_b) = params
    batch, seq, dec_batch, dec_seq = dims
    n_prompt = batch * seq
    d = d_skip.shape[0]
    ngrp = d // S5_GROUP
    p = lam_re.shape[-1]
    w_intra, w_sum, w2t, at = _s5_tables(lam_re, lam_im, log_dt, b_re, b_im, c_re, c_im)
    hperm = _s5_pre(x, g, mod, layer, tok)
    h0g = h0.astype(F32).transpose(3, 2, 0, 1, 4).reshape(ngrp, 2, dec_batch, 2 * p)
    streams = ((0, seq // S5_CHUNK, batch, False, True),
               (n_prompt // S5_CHUNK, dec_seq // S5_CHUNK, dec_batch, True, False))
    yperm, fin = _s5_scan(hperm, w_intra, w_sum, w2t, at, h0g, batch, streams)
    x_new = _s5_post(x, yperm, g, mod, layer, d_skip.reshape(1, d), glu_w, slot, glu_b, tok)
    st = fin.reshape(ngrp, 2, batch, 2, p).transpose(2, 3, 1, 0, 4)
    return x_new, st


def kernel(x_prompt, x_sample, state_s5, cache_na_k, cache_na_v, cache_gqa_k, cache_gqa_v, c, c_ctx, norm_g, ada_w, ada_b, mlp_w1, mlp_w2, s5_lam_re, s5_lam_im, s5_log_dt, s5_b_re, s5_b_im, s5_c_re, s5_c_im, s5_d, s5_glu_w, s5_glu_b, na_w_qkv, na_q_norm, na_k_norm, na_rpb, na_w_o, gqa_w_qkv, gqa_q_norm, gqa_k_norm, gqa_w_o):
    batch, seq, d = x_prompt.shape
    dec_batch, dec_seq, _ = x_sample.shape
    depth = ada_w.shape[0]
    n_prompt = batch * seq
    n_sample = dec_batch * dec_seq
    na_heads, na_hd = cache_na_k.shape[3], cache_na_k.shape[4]
    gqa_kv, gqa_hd = cache_gqa_k.shape[3], cache_gqa_k.shape[4]
    assert n_prompt % dec_seq == 0

    tok = _Tok(n_prompt, n_sample, dec_seq, 1024)
    tok_half = _Tok(n_prompt, n_sample, dec_seq, 512)

    mod_rows = -(-(1 + dec_batch) // SUBLANES) * SUBLANES
    cvec = jnp.concatenate([c_ctx[None, :], c, jnp.zeros((mod_rows - 1 - dec_batch, d), F32)], axis=0)
    mod = _modulation(cvec, ada_w, ada_b).reshape(depth, mod_rows, 6, 1, d)

    x = (x_prompt.reshape(n_prompt, d), x_sample.reshape(n_sample, d))
    cache_na_k2 = cache_na_k.reshape(cache_na_k.shape[:3] + (na_heads * na_hd,))
    cache_na_v2 = cache_na_v.reshape(cache_na_v.shape[:3] + (na_heads * na_hd,))
    cache_gqa_k2 = cache_gqa_k.reshape(cache_gqa_k.shape[:3] + (gqa_kv * gqa_hd,))
    cache_gqa_v2 = cache_gqa_v.reshape(cache_gqa_v.shape[:3] + (gqa_kv * gqa_hd,))
    cos_t, sin_t = _rope_tables(dec_seq, gqa_hd)

    new_s5, new_na_k, new_na_v, new_gqa_k, new_gqa_v = [], [], [], [], []
    for i in range(depth):
        kind, slot = i % 3, i // 3
        g1 = norm_g[i, 0].reshape(1, d)
        g2 = norm_g[i, 1].reshape(1, d)
        if kind == 0:
            params = (s5_lam_re[slot], s5_lam_im[slot], s5_log_dt[slot], s5_b_re[slot], s5_b_im[slot],
                      s5_c_re[slot], s5_c_im[slot], s5_d[slot], s5_glu_w, slot, s5_glu_b[slot])
            x, st = _s5_mixer(x, g1, mod, i, tok_half, params, state_s5[:, slot], (batch, seq, dec_batch, dec_seq))
            new_s5.append(st)
        elif kind == 1:
            assert not isinstance(x, tuple)
            q, k_p, k_s, v_p, v_s = _na_qkv(x, g1, mod, i, na_w_qkv, slot, na_q_norm[slot], na_k_norm[slot], tok_half)
            o_p = _na_ctx_attn(q, k_p, v_p, batch, seq, na_hd)
            bias = _na_bias_table(na_rpb[slot])
            o_s = _na_lat_attn(q, k_s, v_s, cache_na_k2, cache_na_v2, slot, bias, n_prompt, dec_batch, dec_seq, na_hd)
            x = _proj_res(x, o_p, o_s, na_w_o, slot, mod, i, tok)
            new_na_k.append(k_p.reshape(batch, seq, na_heads, na_hd))
            new_na_v.append(v_p.reshape(batch, seq, na_heads, na_hd))
        else:
            assert not isinstance(x, tuple)
            q, k_p, k_s, v_p, v_s = _gqa_qkv(x, g1, mod, i, gqa_w_qkv, slot, gqa_q_norm[slot], gqa_k_norm[slot],
                                             cos_t, sin_t, gqa_kv, tok_half)
            o_p = _gqa_ctx_attn(q, k_p, v_p, batch, seq, gqa_kv)
            o_s = _gqa_lat_attn(q, k_s, v_s, cache_gqa_k2, cache_gqa_v2, slot, n_prompt, dec_batch, dec_seq, gqa_kv)
            x = _proj_res(x, o_p, o_s, gqa_w_o, slot, mod, i, tok)
            new_gqa_k.append(k_p.reshape(batch, seq, gqa_kv, gqa_hd))
            new_gqa_v.append(v_p.reshape(batch, seq, gqa_kv, gqa_hd))
        x = _mlp(x, g2, mod, i, mlp_w1, mlp_w2, tok, split_out=(i == depth - 1))
    y_p, y_s = x
    return (y_p.reshape(batch, seq, d), y_s.reshape(dec_batch, dec_seq, d),
            jnp.stack(new_s5, axis=1), jnp.stack(new_na_k, axis=1), jnp.stack(new_na_v, axis=1),
            jnp.stack(new_gqa_k, axis=1), jnp.stack(new_gqa_v, axis=1))
```

```python
import functools
import math

import jax
import jax.numpy as jnp
from jax import lax
from jax.experimental import pallas as pl
from jax.experimental.pallas import tpu as pltpu

F32 = jnp.float32
BF16 = jnp.bfloat16

EPS = 1e-6
GRID_W = 64
S5_GROUP = 16
NA_KH = 8
NA_KW = 16
ROPE_THETA = 10000.0
S5_CHUNK = 16
LOG2E = math.log2(math.e)
LANES = 128
SUBLANES = 8
VMEM_LIMIT = 56 * 1024 * 1024


def _cparams(*sem):
    return pltpu.CompilerParams(dimension_semantics=sem, vmem_limit_bytes=VMEM_LIMIT)


def _pow2_tile(pref, *ns):
    t = pref
    while any(n % t for n in ns):
        t //= 2
    return t


def _row_halves(n):
    if n % 2 or n // 2 < LANES:
        return [slice(0, n)]
    return [slice(0, n // 2), slice(n // 2, n)]


def _normmod(x, g, sc, sh):
    ms = jnp.mean(x * x, axis=-1, keepdims=True)
    y = x * lax.rsqrt(ms + EPS) * g
    return y * (1.0 + sc) + sh


class _Tok:
    def __init__(self, n_prompt, n_sample, dec_seq, pref):
        self.tm = _pow2_tile(pref, n_prompt, dec_seq)
        self.n = n_prompt + n_sample
        self.tiles = self.n // self.tm
        self.np_tiles = n_prompt // self.tm
        self.tps = dec_seq // self.tm

    def mod_row(self, i):
        return jnp.where(i < self.np_tiles, 0, 1 + (i - self.np_tiles) // self.tps)

    def prompt_spec(self, d, nargs):
        last = self.np_tiles - 1
        if nargs == 1:
            return pl.BlockSpec((self.tm, d), lambda i: (jnp.minimum(i, last), 0))
        return pl.BlockSpec((self.tm, d), lambda i, j: (jnp.minimum(i, last), 0))

    def sample_spec(self, d, nargs):
        first = self.np_tiles
        if nargs == 1:
            return pl.BlockSpec((self.tm, d), lambda i: (jnp.maximum(i - first, 0), 0))
        return pl.BlockSpec((self.tm, d), lambda i, j: (jnp.maximum(i - first, 0), 0))

    def mod_spec(self, layer, which, d, nargs):
        if nargs == 1:
            return pl.BlockSpec((None, None, None, 1, d), lambda i: (layer, self.mod_row(i), which, 0, 0))
        return pl.BlockSpec((None, None, None, 1, d), lambda i, j: (layer, self.mod_row(i), which, 0, 0))


def _mod_kernel(c_ref, w_ref, b_ref, o_ref):
    c = c_ref[...]
    s = c * jax.nn.sigmoid(c)
    o_ref[...] = jnp.dot(s.astype(BF16), w_ref[...].astype(BF16), preferred_element_type=F32) + b_ref[...]


def _modulation(cvec, ada_w, ada_b):
    depth, d, d6 = ada_w.shape
    rows = cvec.shape[0]
    tn = _pow2_tile(2048, d6) if d6 % 2048 == 0 else d
    return pl.pallas_call(
        _mod_kernel,
        grid=(depth, d6 // tn),
        in_specs=[pl.BlockSpec((rows, d), lambda l, j: (0, 0)),
                  pl.BlockSpec((None, d, tn), lambda l, j: (l, 0, j)),
                  pl.BlockSpec((None, 1, tn), lambda l, j: (l, 0, j))],
        out_specs=pl.BlockSpec((None, rows, tn), lambda l, j: (l, 0, j)),
        out_shape=jax.ShapeDtypeStruct((depth, rows, d6), F32),
        compiler_params=_cparams("arbitrary", "arbitrary"),
        name="adaln_modulation",
    )(cvec, ada_w, ada_b.reshape(depth, 1, d6))


def _read_x(x_refs, np_tiles, rows=slice(None), cols=slice(None)):
    if len(x_refs) == 1:
        return x_refs[0][rows, cols]
    return jnp.where(pl.program_id(0) < np_tiles, x_refs[0][rows, cols], x_refs[1][rows, cols])


def _x_specs(x, tok, nargs):
    if isinstance(x, tuple):
        d = x[0].shape[1]
        return [tok.prompt_spec(d, nargs), tok.sample_spec(d, nargs)], x
    d = x.shape[1]
    return [pl.BlockSpec((tok.tm, d), (lambda i: (i, 0)) if nargs == 1 else (lambda i, j: (i, 0)))], (x,)


def _s5_pre_kernel(*refs, n_x, np_tiles):
    x_refs, (g_ref, sc_ref, sh_ref, h_ref, h_scr) = refs[:n_x], refs[n_x:]
    x = _read_x(x_refs, np_tiles)
    rstd = lax.rsqrt(jnp.mean(x * x, axis=-1, keepdims=True) + EPS)
    nc = h_ref.shape[1]
    for a in range(h_scr.shape[0]):
        sl = slice(a * LANES, (a + 1) * LANES)
        h_scr[a] = (x[:, sl] * rstd * g_ref[:, sl]) * (1.0 + sc_ref[:, sl]) + sh_ref[:, sl]
        for j in range(S5_CHUNK):
            h_ref[j, :, sl] = h_scr[a, pl.ds(j, nc, stride=S5_CHUNK), :].astype(h_ref.dtype)


def _s5_pre(x, g, mod, layer, tok):
    x_specs, xs = _x_specs(x, tok, 1)
    d = xs[0].shape[1]
    nc = tok.tm // S5_CHUNK
    return pl.pallas_call(
        functools.partial(_s5_pre_kernel, n_x=len(xs), np_tiles=tok.np_tiles),
        grid=(tok.tiles,),
        in_specs=x_specs + [pl.BlockSpec((1, d), lambda i: (0, 0)),
                            tok.mod_spec(layer, 1, d, 1),
                            tok.mod_spec(layer, 0, d, 1)],
        out_specs=pl.BlockSpec((S5_CHUNK, nc, d), lambda i: (0, i, 0)),
        out_shape=jax.ShapeDtypeStruct((S5_CHUNK, tok.n // S5_CHUNK, d), BF16),
        scratch_shapes=[pltpu.VMEM((d // LANES, tok.tm, LANES), F32)],
        compiler_params=_cparams("arbitrary"),
        name="s5_pre",
    )(*xs, g, mod, mod)


def _s5_scan_kernel(h_ref, wi_ref, ws_ref, w2t_ref, at_ref, h0_ref, *rest, streams, unroll):
    ns = len(streams)
    perm_refs, (y_ref, fin_ref, xt_scr, yt_scr), chain_scr = rest[:ns], rest[ns:ns + 4], rest[ns + 4:]
    ngrp = wi_ref.shape[0]
    p2 = at_ref.shape[-1]
    fwd = lax.broadcasted_iota(jnp.int32, (1, p2), 1) < (p2 // 2)
    for j in range(S5_CHUNK):
        xt_scr[j] = h_ref[j].astype(F32).T.astype(BF16)

    def load_rmat(r, stream):
        row0, n_chunks, nb = stream[:3]
        cols = slice(row0, row0 + n_chunks * nb)
        c0 = pl.multiple_of(r * S5_GROUP, S5_GROUP)
        return jnp.concatenate([xt_scr[j, pl.ds(c0, S5_GROUP), cols] for j in range(S5_CHUNK)], axis=0), c0, cols

    def summaries(r, stream, perm_ref, s_scr):
        nb, use_h0 = stream[2], stream[3]
        rmat, _, _ = load_rmat(r, stream)
        rmat_cm = jnp.dot(rmat, perm_ref[...], preferred_element_type=F32).astype(BF16)
        zs = jnp.dot(ws_ref[r], rmat_cm, preferred_element_type=F32)
        s_scr[0] = zs[0:p2].T
        s_scr[1] = zs[p2:].T
        if use_h0:
            return h0_ref[r, 0], h0_ref[r, 1]
        return jnp.zeros((nb, p2), F32), jnp.zeros((nb, p2), F32)

    def scan_step(t, decay, stream, state, s_scr, e_scr):
        n_chunks, nb = stream[1], stream[2]
        hr, hi = state
        ar, ai = decay
        ft = slice(t * nb, (t + 1) * nb)
        bt = slice((n_chunks - 1 - t) * nb, (n_chunks - t) * nb)
        e_scr[0, ft, :] = hr
        e_scr[1, bt, :] = hr
        e_scr[2, ft, :] = hi
        e_scr[3, bt, :] = hi
        in_re = jnp.where(fwd, s_scr[0, ft, :], s_scr[0, bt, :])
        in_im = jnp.where(fwd, s_scr[1, ft, :], s_scr[1, bt, :])
        return ar * hr - ai * hi + in_re, ar * hi + ai * hr + in_im

    def outputs(r, stream, perm_ref, state, e_scr):
        rmat, c0, cols = load_rmat(r, stream)
        if stream[4]:
            fin_ref[r, 0] = state[0]
            fin_ref[r, 1] = state[1]
        e_cm = jnp.concatenate([jnp.where(fwd, e_scr[0], e_scr[1]), jnp.where(fwd, e_scr[2], e_scr[3])],
                               axis=1).astype(BF16)
        e = jnp.dot(perm_ref[...], e_cm, preferred_element_type=F32).astype(BF16)
        yt = (jnp.dot(wi_ref[r], rmat, preferred_element_type=F32)
              + lax.dot_general(w2t_ref[r], e, (((1,), (1,)), ((), ())), preferred_element_type=F32))
        for i in range(S5_CHUNK):
            yt_scr[i, pl.ds(c0, S5_GROUP), cols] = yt[i * S5_GROUP:(i + 1) * S5_GROUP]

    def body(rb, carry):
        chains = []
        for u in range(unroll):
            for si, stream in enumerate(streams):
                k = 2 * (u * ns + si)
                chains.append((rb * unroll + u, stream, perm_refs[si], chain_scr[k], chain_scr[k + 1]))
        states = [summaries(r, stream, pm, s_scr) for r, stream, pm, s_scr, _ in chains]
        decays = [(at_ref[r, 0:1, :], at_ref[r, 1:2, :]) for r, _, _, _, _ in chains]
        for t in range(max(stream[1] for stream in streams)):
            for ci, (_, stream, _, s_scr, e_scr) in enumerate(chains):
                if t < stream[1]:
                    states[ci] = scan_step(t, decays[ci], stream, states[ci], s_scr, e_scr)
        for (r, stream, pm, _, e_scr), state in zip(chains, states):
            outputs(r, stream, pm, state, e_scr)
        return carry

    lax.fori_loop(0, ngrp // unroll, body, 0)
    for i in range(S5_CHUNK):
        y_ref[i] = yt_scr[i].T


def _s5_scan(hperm, w_intra, w_sum, w2t, at, h0, nb_fin, streams):
    t, nrows, d = hperm.shape
    g, tk, p4 = w2t.shape
    p2 = at.shape[-1]
    gl = LANES // S5_GROUP
    nb0 = h0.shape[2]
    unroll = 8
    assert sum(s[1] * s[2] for s in streams) == nrows and gl % unroll == 0
    chain_scr = []
    for _ in range(unroll):
        for s in streams:
            chain_scr += [pltpu.VMEM((2, s[1] * s[2], p2), F32), pltpu.VMEM((4, s[1] * s[2], p2), F32)]
    perms = []
    for _, n_chunks, nb, _, _ in streams:
        m = jnp.arange(n_chunks * nb)
        perms.append((m[:, None] == ((m % nb) * n_chunks + m // nb)[None, :]).astype(BF16))
    return pl.pallas_call(
        functools.partial(_s5_scan_kernel, streams=streams, unroll=unroll),
        grid=(d // LANES,),
        in_specs=[pl.BlockSpec((t, nrows, LANES), lambda a: (0, 0, a)),
                  pl.BlockSpec((gl, tk, tk), lambda a: (a, 0, 0)),
                  pl.BlockSpec((gl, p4, tk), lambda a: (a, 0, 0)),
                  pl.BlockSpec((gl, tk, p4), lambda a: (a, 0, 0)),
                  pl.BlockSpec((gl, 2, p2), lambda a: (a, 0, 0)),
                  pl.BlockSpec((gl, 2, nb0, p2), lambda a: (a, 0, 0, 0))]
                 + [pl.BlockSpec(pm.shape, lambda a: (0, 0)) for pm in perms],
        out_specs=[pl.BlockSpec((t, nrows, LANES), lambda a: (0, 0, a)),
                   pl.BlockSpec((gl, 2, nb_fin, p2), lambda a: (a, 0, 0, 0))],
        out_shape=[jax.ShapeDtypeStruct((t, nrows, d), F32),
                   jax.ShapeDtypeStruct((g, 2, nb_fin, p2), F32)],
        scratch_shapes=[pltpu.VMEM((t, LANES, nrows), BF16), pltpu.VMEM((t, LANES, nrows), F32)] + chain_scr,
        compiler_params=_cparams("arbitrary"),
        name="s5_scan",
    )(hperm, w_intra, w_sum, w2t, at, h0, *perms)


def _s5_post_kernel(*refs, n_x, np_tiles):
    x_refs = refs[:n_x]
    (y_ref, g_ref, sc_ref, sh_ref, gate_ref, dsk_ref, wa_ref, wg_ref, ba_ref, bg_ref, o_ref,
     y_scr, wa_scr, wg_scr) = refs[n_x:]
    _cast_once(wa_ref, wa_scr)
    _cast_once(wg_ref, wg_scr)
    tm, d = o_ref.shape
    n_cb = max(d // (2 * LANES), 1)
    cw = d // n_cb

    def before(rows):
        c0, nc = rows.start // S5_CHUNK, (rows.stop - rows.start) // S5_CHUNK
        for a in range(y_scr.shape[0]):
            for j in range(S5_CHUNK):
                y_scr[a, pl.ds(rows.start + j, nc, stride=S5_CHUNK), :] = y_ref[j, c0:c0 + nc, a * LANES:(a + 1) * LANES]
        x = _read_x(x_refs, np_tiles, rows)
        h = _normmod(x, g_ref[...], sc_ref[...], sh_ref[...])
        y = h * dsk_ref[...] + jnp.concatenate([y_scr[a, rows, :] for a in range(y_scr.shape[0])], axis=1)
        return jax.nn.gelu(y).astype(BF16)

    def matmuls(a, cb):
        cols = slice(cb * cw, (cb + 1) * cw)
        return (jnp.dot(a, wa_scr[:, cols], preferred_element_type=F32),
                jnp.dot(a, wg_scr[:, cols], preferred_element_type=F32))

    def after(rows, cb, z):
        cols = slice(cb * cw, (cb + 1) * cw)
        za, zg = z[0] + ba_ref[:, cols], z[1] + bg_ref[:, cols]
        x = _read_x(x_refs, np_tiles, rows, cols)
        o_ref[rows, cols] = x + gate_ref[:, cols] * (za * jax.nn.sigmoid(zg))

    def row_chunks(rows):
        step = max((rows.stop - rows.start) // n_cb, S5_CHUNK)
        return [slice(r, min(r + step, rows.stop)) for r in range(rows.start, rows.stop, step)]

    halves = _row_halves(tm)
    a_cur = jnp.concatenate([before(rc) for rc in row_chunks(halves[0])], axis=0)
    z_prev = None
    for hi, rows in enumerate(halves):
        nxt = row_chunks(halves[hi + 1]) if hi + 1 < len(halves) else []
        a_next, z_cur = [], []
        for cb in range(n_cb):
            z_cur.append(matmuls(a_cur, cb))
            if cb < len(nxt):
                a_next.append(before(nxt[cb]))
            if z_prev is not None:
                after(halves[hi - 1], cb, z_prev[cb])
        a_next += [before(rc) for rc in nxt[n_cb:]]
        z_prev = z_cur
        if a_next:
            a_cur = jnp.concatenate(a_next, axis=0)
    for cb in range(n_cb):
        after(halves[-1], cb, z_prev[cb])


def _s5_post(x, y, g, mod, layer, dsk, glu_w, slot, glu_b, tok):
    x_specs, xs = _x_specs(x, tok, 1)
    n, d = tok.n, xs[0].shape[1]
    row = lambda i: (i, 0)
    fixed = lambda i: (0, 0)
    gb2 = glu_b.reshape(1, 2 * d)
    return pl.pallas_call(
        functools.partial(_s5_post_kernel, n_x=len(xs), np_tiles=tok.np_tiles),
        grid=(tok.tiles,),
        in_specs=x_specs + [
                  pl.BlockSpec((S5_CHUNK, tok.tm // S5_CHUNK, d), lambda i: (0, i, 0)),
                  pl.BlockSpec((1, d), fixed),
                  tok.mod_spec(layer, 1, d, 1),
                  tok.mod_spec(layer, 0, d, 1),
                  tok.mod_spec(layer, 2, d, 1),
                  pl.BlockSpec((1, d), fixed),
                  _resident((None, d, d), lambda i: (slot, 0, 0)),
                  _resident((None, d, d), lambda i: (slot, 0, 1)),
                  pl.BlockSpec((1, d), lambda i: (0, 0)),
                  pl.BlockSpec((1, d), lambda i: (0, 1))],
        out_specs=pl.BlockSpec((tok.tm, d), row),
        out_shape=jax.ShapeDtypeStruct((n, d), F32),
        scratch_shapes=[pltpu.VMEM((d // LANES, tok.tm, LANES), F32),
                        pltpu.VMEM((d, d), BF16), pltpu.VMEM((d, d), BF16)],
        compiler_params=_cparams("arbitrary"),
        name="s5_post",
    )(*xs, y, g, mod, mod, mod, dsk, glu_w, glu_w, gb2, gb2)


def _cast_once(w_ref, w_scr):
    @pl.when(pl.program_id(0) == 0)
    def _():
        w_scr[...] = w_ref[...].astype(w_scr.dtype)


def _resident(block_shape, index_map):
    return pl.BlockSpec(block_shape, index_map, pipeline_mode=pl.Buffered(1))


def _na_qkv_kernel(x_ref, g_ref, sc_ref, sh_ref, w_ref, qn_ref, kn_ref, q_ref, kp_ref, ks_ref, vp_ref, vs_ref, w_scr, *,
                   hd, np_tiles):
    i = pl.program_id(0)
    _cast_once(w_ref, w_scr)
    d = x_ref.shape[1]
    lo = lax.broadcasted_iota(jnp.int32, (1, LANES), 1) < hd

    def head_norm(z, gain):
        outs = []
        for s in range(d // LANES):
            seg = z[:, s * LANES:(s + 1) * LANES]
            sq = seg * seg
            s_lo = jnp.sum(jnp.where(lo, sq, 0.0), axis=-1, keepdims=True)
            s_hi = jnp.sum(jnp.where(lo, 0.0, sq), axis=-1, keepdims=True)
            ms = jnp.where(lo, s_lo, s_hi) / hd
            outs.append(seg * lax.rsqrt(ms + EPS) * gain)
        return jnp.concatenate(outs, axis=1)

    h = _normmod(x_ref[...], g_ref[...], sc_ref[...], sh_ref[...]).astype(BF16)
    proj = lambda part: jnp.dot(h, w_scr[:, part * d:(part + 1) * d], preferred_element_type=F32)
    q_ref[...] = (head_norm(proj(0), qn_ref[...]) * (hd ** -0.5 * LOG2E)).astype(q_ref.dtype)
    k = head_norm(proj(1), kn_ref[...])
    v = proj(2)

    @pl.when(i < np_tiles)
    def _():
        kp_ref[...] = k
        vp_ref[...] = v

    @pl.when(i >= np_tiles)
    def _():
        ks_ref[...] = k.astype(ks_ref.dtype)
        vs_ref[...] = v.astype(vs_ref.dtype)


def _na_qkv(x, g, mod, layer, w_qkv, slot, qn, kn, tok):
    n, d = x.shape
    hd = qn.shape[-1]
    assert 2 * hd == LANES
    row = lambda i: (i, 0)
    fixed = lambda i: (0, 0)
    qn2 = jnp.tile(qn, 2).reshape(1, LANES)
    kn2 = jnp.tile(kn, 2).reshape(1, LANES)
    n_p, n_s = tok.np_tiles * tok.tm, n - tok.np_tiles * tok.tm
    return pl.pallas_call(
        functools.partial(_na_qkv_kernel, hd=hd, np_tiles=tok.np_tiles),
        grid=(tok.tiles,),
        in_specs=[pl.BlockSpec((tok.tm, d), row),
                  pl.BlockSpec((1, d), fixed),
                  tok.mod_spec(layer, 1, d, 1),
                  tok.mod_spec(layer, 0, d, 1),
                  _resident((None, d, 3 * d), lambda i: (slot, 0, 0)),
                  pl.BlockSpec((1, LANES), fixed),
                  pl.BlockSpec((1, LANES), fixed)],
        out_specs=[pl.BlockSpec((tok.tm, d), row),
                   tok.prompt_spec(d, 1), tok.sample_spec(d, 1), tok.prompt_spec(d, 1), tok.sample_spec(d, 1)],
        out_shape=[jax.ShapeDtypeStruct((n, d), BF16),
                   jax.ShapeDtypeStruct((n_p, d), F32), jax.ShapeDtypeStruct((n_s, d), BF16),
                   jax.ShapeDtypeStruct((n_p, d), F32), jax.ShapeDtypeStruct((n_s, d), BF16)],
        scratch_shapes=[pltpu.VMEM((d, 3 * d), BF16)],
        compiler_params=_cparams("arbitrary"),
        name="na_qkv",
    )(x, g, mod, mod, w_qkv, qn2, kn2)


def _softmax2_pv(s_parts, v_parts):
    m = s_parts[0].max(axis=-1, keepdims=True)
    for s in s_parts[1:]:
        m = jnp.maximum(m, s.max(axis=-1, keepdims=True))
    den = 0.0
    acc = 0.0
    for s, v in zip(s_parts, v_parts):
        e = jnp.exp2(s - m)
        den = den + e.sum(axis=-1, keepdims=True)
        acc = acc + jnp.dot(e.astype(BF16), v, preferred_element_type=F32)
    return acc / den


def _qk(q, k):
    return lax.dot_general(q, k, (((1,), (1,)), ((), ())), preferred_element_type=F32)


def _na_ctx_attn_kernel(q_ref, k_ref, v_ref, o_ref, *, hd):
    lo = lax.broadcasted_iota(jnp.int32, (1, LANES), 1) < hd
    for s in range(q_ref.shape[1] // LANES):
        sl = slice(s * LANES, (s + 1) * LANES)
        q = q_ref[:, sl]
        k = k_ref[:, sl].astype(BF16)
        v = v_ref[:, sl].astype(BF16)
        zero = jnp.zeros_like(q)
        q2 = jnp.concatenate([jnp.where(lo, q, zero), jnp.where(lo, zero, q)], axis=0)
        out = _softmax2_pv([_qk(q2, k)], [v])
        n = q.shape[0]
        o_ref[:, sl] = jnp.where(lo, out[0:n], out[n:]).astype(o_ref.dtype)


def _na_ctx_attn(q, k, v, batch, seq, hd):
    d = q.shape[1]
    spec = pl.BlockSpec((seq, d), lambda b: (b, 0))
    return pl.pallas_call(
        functools.partial(_na_ctx_attn_kernel, hd=hd),
        grid=(batch,),
        in_specs=[spec, spec, spec],
        out_specs=spec,
        out_shape=jax.ShapeDtypeStruct((batch * seq, d), BF16),
        compiler_params=_cparams("arbitrary"),
        name="na_ctx_attn",
    )(q, k, v)


def _na_window(r0, q_rows, rows, kh):
    w_rows = min(rows, kh + q_rows + (kh + q_rows) % 2)
    rs = min(max(r0 - kh // 2, 0), rows - kh)
    return min(rs // 2 * 2, rows - w_rows), w_rows


def _na_lat_attn_kernel(q_ref, kl_ref, vl_ref, kc_ref, vc_ref, tab_ref, o_ref, kc_scr, vc_scr, bias_scr, *,
                        hd, rows, kh, tq):
    lane = lax.broadcasted_iota(jnp.int32, (1, LANES), 1)
    lo = lane < hd
    q_rows = tq // GRID_W
    n_qb = q_ref.shape[0] // tq
    w_cols = bias_scr.shape[2]
    kc_scr[...] = kc_ref[...].astype(BF16)
    vc_scr[...] = vc_ref[...].astype(BF16)

    @pl.when(pl.program_id(1) == 0)
    def _():
        neg = jnp.full((GRID_W, GRID_W), -jnp.inf, F32)
        for hh in range(2):
            for rq in range(rows):
                w0, w_rows = _na_window(rq // q_rows * q_rows, q_rows, rows, kh)
                rs = min(max(rq - kh // 2, 0), rows - kh)
                for kr in range(w_rows):
                    rk = w0 + kr
                    tile = tab_ref[hh, rk - rq + kh - 1] if rs <= rk < rs + kh else neg
                    bias_scr[hh, rq * GRID_W:(rq + 1) * GRID_W, kr * GRID_W:(kr + 1) * GRID_W] = tile

    per_iter = 4 if n_qb % 4 == 0 else 1

    def qblocks(it, carry):
        work = []
        for u in range(per_iter):
            qb = it * per_iter + u
            r0 = qb * q_rows
            w0 = jnp.minimum(jnp.clip(r0 - kh // 2, 0, rows - kh) // 2 * 2, rows - w_cols // GRID_W)
            k0 = pl.multiple_of(w0 * GRID_W, 2 * GRID_W)
            q0 = pl.multiple_of(qb * tq, tq)
            q = q_ref[pl.ds(q0, tq), :]
            kl = kl_ref[pl.ds(k0, w_cols), :]
            zero = jnp.zeros_like(q)
            q2 = jnp.concatenate([jnp.where(lo, q, zero), jnp.where(lo, zero, q)], axis=0)
            bias2 = jnp.concatenate([bias_scr[0, pl.ds(q0, tq), :], bias_scr[1, pl.ds(q0, tq), :]], axis=0)
            work.append((q0, k0, [_qk(q2, kl) + bias2, _qk(q2, kc_scr[...])]))
        for q0, k0, scores in work:
            vl = vl_ref[pl.ds(k0, w_cols), :]
            out = _softmax2_pv(scores, [vl, vc_scr[...]])
            o_ref[pl.ds(q0, tq), :] = jnp.where(lo, out[0:tq], out[tq:]).astype(o_ref.dtype)
        return carry

    lax.fori_loop(0, n_qb // per_iter, qblocks, 0)


def _na_lat_attn(q, k, v, cache_k, cache_v, slot, tab, n_prompt, dec_batch, dec_seq, hd):
    d = q.shape[1]
    past = cache_k.shape[2]
    rows = dec_seq // GRID_W
    kh = min(NA_KH, rows)
    tq = _pow2_tile(256, dec_seq)
    assert tq % GRID_W == 0 and rows % 2 == 0 and 2 * GRID_W == LANES and n_prompt % dec_seq == 0
    b0 = n_prompt // dec_seq
    w_rows = _na_window(0, tq // GRID_W, rows, kh)[1]
    return pl.pallas_call(
        functools.partial(_na_lat_attn_kernel, hd=hd, rows=rows, kh=kh, tq=tq),
        grid=(d // LANES, dec_batch),
        in_specs=[pl.BlockSpec((dec_seq, LANES), lambda hp, b: (b0 + b, hp)),
                  pl.BlockSpec((dec_seq, LANES), lambda hp, b: (b, hp)),
                  pl.BlockSpec((dec_seq, LANES), lambda hp, b: (b, hp)),
                  pl.BlockSpec((None, None, past, LANES), lambda hp, b: (b, slot, 0, hp)),
                  pl.BlockSpec((None, None, past, LANES), lambda hp, b: (b, slot, 0, hp)),
                  pl.BlockSpec((2,) + tab.shape[1:], lambda hp, b: (hp, 0, 0, 0))],
        out_specs=pl.BlockSpec((dec_seq, LANES), lambda hp, b: (b, hp)),
        out_shape=jax.ShapeDtypeStruct((dec_batch * dec_seq, d), BF16),
        scratch_shapes=[pltpu.VMEM((past, LANES), BF16), pltpu.VMEM((past, LANES), BF16),
                        pltpu.VMEM((2, dec_seq, w_rows * GRID_W), F32)],
        compiler_params=_cparams("arbitrary", "arbitrary"),
        name="na_lat_attn",
    )(q, k, v, cache_k, cache_v, tab)


def _na_bias_table(rpb):
    nh, nd, nc = rpb.shape
    c = jnp.arange(GRID_W)
    cs = jnp.clip(c - NA_KW // 2, 0, GRID_W - NA_KW)
    col_ok = (c[None, :] >= cs[:, None]) & (c[None, :] < cs[:, None] + NA_KW)
    dcol = c[None, :] - c[:, None] + NA_KW - 1
    onehot = (dcol[None] == jnp.arange(nc)[:, None, None]).astype(F32)
    tmp = jnp.einsum('hdc,cxy->hdxy', rpb.astype(F32), onehot, precision=lax.Precision.HIGHEST)
    return jnp.where(col_ok[None, None], tmp * LOG2E, -jnp.inf)


def _gqa_qkv_kernel(x_ref, g_ref, sc_ref, sh_ref, w_ref, qn_ref, kn_ref, cos_ref, sin_ref,
                    q_ref, kp_ref, ks_ref, vp_ref, vs_ref, w_scr, *, nq, nk, np_tiles):
    i = pl.program_id(0)
    _cast_once(w_ref, w_scr)
    is_sample = i >= np_tiles
    ks, vs = [], []
    for rows in _row_halves(x_ref.shape[0]):
        h = _normmod(x_ref[rows, :], g_ref[...], sc_ref[...], sh_ref[...]).astype(BF16)
        z = jnp.dot(h, w_scr[...], preferred_element_type=F32)
        cos = cos_ref[rows, :]
        sin = sin_ref[rows, :]

        def norm_rope(seg, gain):
            ms = jnp.mean(seg * seg, axis=-1, keepdims=True)
            y = seg * lax.rsqrt(ms + EPS) * gain
            roped = y * cos + pltpu.roll(y, LANES // 2, 1) * sin
            return jnp.where(is_sample, roped, y)

        for hh in range(nq):
            sl = slice(hh * LANES, (hh + 1) * LANES)
            q_ref[rows, sl] = (norm_rope(z[:, sl], qn_ref[...]) * (LANES ** -0.5 * LOG2E)).astype(q_ref.dtype)
        ks.append(jnp.concatenate(
            [norm_rope(z[:, (nq + hh) * LANES:(nq + hh + 1) * LANES], kn_ref[...]) for hh in range(nk)], axis=1))
        vs.append(z[:, (nq + nk) * LANES:])
    k = jnp.concatenate(ks, axis=0)
    v = jnp.concatenate(vs, axis=0)

    @pl.when(i < np_tiles)
    def _():
        kp_ref[...] = k
        vp_ref[...] = v

    @pl.when(is_sample)
    def _():
        ks_ref[...] = k.astype(ks_ref.dtype)
        vs_ref[...] = v.astype(vs_ref.dtype)


def _gqa_qkv(x, g, mod, layer, w_qkv, slot, qn, kn, cos_t, sin_t, nk, tok):
    n, d = x.shape
    hd = qn.shape[-1]
    assert hd == LANES
    nq = d // hd
    dk = nk * hd
    row = lambda i: (i, 0)
    fixed = lambda i: (0, 0)
    pos = lambda i: (jnp.maximum(i - tok.np_tiles, 0) % tok.tps, 0)
    n_p, n_s = tok.np_tiles * tok.tm, n - tok.np_tiles * tok.tm
    return pl.pallas_call(
        functools.partial(_gqa_qkv_kernel, nq=nq, nk=nk, np_tiles=tok.np_tiles),
        grid=(tok.tiles,),
        in_specs=[pl.BlockSpec((tok.tm, d), row),
                  pl.BlockSpec((1, d), fixed),
                  tok.mod_spec(layer, 1, d, 1),
                  tok.mod_spec(layer, 0, d, 1),
                  _resident((None, d, d + 2 * dk), lambda i: (slot, 0, 0)),
                  pl.BlockSpec((1, hd), fixed),
                  pl.BlockSpec((1, hd), fixed),
                  pl.BlockSpec((tok.tm, hd), pos),
                  pl.BlockSpec((tok.tm, hd), pos)],
        out_specs=[pl.BlockSpec((tok.tm, d), row),
                   tok.prompt_spec(dk, 1), tok.sample_spec(dk, 1), tok.prompt_spec(dk, 1), tok.sample_spec(dk, 1)],
        out_shape=[jax.ShapeDtypeStruct((n, d), BF16),
                   jax.ShapeDtypeStruct((n_p, dk), F32), jax.ShapeDtypeStruct((n_s, dk), BF16),
                   jax.ShapeDtypeStruct((n_p, dk), F32), jax.ShapeDtypeStruct((n_s, dk), BF16)],
        scratch_shapes=[pltpu.VMEM((d, d + 2 * dk), BF16)],
        compiler_params=_cparams("arbitrary"),
        name="gqa_qkv",
    )(x, g, mod, mod, w_qkv, qn.reshape(1, hd), kn.reshape(1, hd), cos_t, sin_t)


def _rope_tables(dec_seq, hd):
    t = jnp.arange(dec_seq)
    row = (t // GRID_W).astype(F32)
    col = (t % GRID_W).astype(F32)
    half = hd // 2
    inv = ROPE_THETA ** (-jnp.arange(0, half, 2, dtype=F32) / half)
    ang = jnp.concatenate([row[:, None] * inv, col[:, None] * inv], axis=-1)
    cos, sin = jnp.cos(ang), jnp.sin(ang)
    return jnp.concatenate([cos, cos], axis=-1), jnp.concatenate([-sin, sin], axis=-1)


def _gqa_attn_kernel(q_ref, *refs, rep, nk, n_kv):
    k_refs, v_refs, o_ref = refs[:n_kv], refs[n_kv:2 * n_kv], refs[2 * n_kv]
    tq = q_ref.shape[0]
    scores, values = [], []
    for kv in range(nk):
        sl = slice(kv * LANES, (kv + 1) * LANES)
        qs = jnp.concatenate([q_ref[:, (kv * rep + r) * LANES:(kv * rep + r + 1) * LANES] for r in range(rep)], axis=0)
        scores.append([_qk(qs, r[:, sl].astype(BF16)) for r in k_refs])
        values.append([r[:, sl].astype(BF16) for r in v_refs])
    for kv in range(nk):
        o = _softmax2_pv(scores[kv], values[kv])
        for r in range(rep):
            o_ref[:, (kv * rep + r) * LANES:(kv * rep + r + 1) * LANES] = o[r * tq:(r + 1) * tq].astype(o_ref.dtype)


def _gqa_ctx_attn(q, k, v, batch, seq, nk):
    d = q.shape[1]
    dk = k.shape[1]
    rep = d // dk
    qspec = pl.BlockSpec((seq, d), lambda b: (b, 0))
    kspec = pl.BlockSpec((seq, dk), lambda b: (b, 0))
    return pl.pallas_call(
        functools.partial(_gqa_attn_kernel, rep=rep, nk=nk, n_kv=1),
        grid=(batch,),
        in_specs=[qspec, kspec, kspec],
        out_specs=qspec,
        out_shape=jax.ShapeDtypeStruct((batch * seq, d), BF16),
        compiler_params=_cparams("arbitrary"),
        name="gqa_ctx_attn",
    )(q, k, v)


def _gqa_lat_attn(q, k, v, cache_k, cache_v, slot, n_prompt, dec_batch, dec_seq, nk):
    d = q.shape[1]
    dk = k.shape[1]
    rep = d // dk
    past = cache_k.shape[2]
    tq = _pow2_tile(256, dec_seq, n_prompt)
    nqb = dec_seq // tq
    q0 = n_prompt // tq
    lspec = pl.BlockSpec((dec_seq, dk), lambda b, qb: (b, 0))
    cspec = pl.BlockSpec((None, None, past, dk), lambda b, qb: (b, slot, 0, 0))
    return pl.pallas_call(
        functools.partial(_gqa_attn_kernel, rep=rep, nk=nk, n_kv=2),
        grid=(dec_batch, nqb),
        in_specs=[pl.BlockSpec((tq, d), lambda b, qb: (q0 + b * nqb + qb, 0)),
                  lspec, cspec, lspec, cspec],
        out_specs=pl.BlockSpec((tq, d), lambda b, qb: (b * nqb + qb, 0)),
        out_shape=jax.ShapeDtypeStruct((dec_batch * dec_seq, d), BF16),
        compiler_params=_cparams("arbitrary", "arbitrary"),
        name="gqa_lat_attn",
    )(q, k, cache_k, v, cache_v)


def _proj_res_kernel(x_ref, op_ref, os_ref, w_ref, gate_ref, out_ref, w_scr, *, np_tiles):
    _cast_once(w_ref, w_scr)
    o = jnp.where(pl.program_id(0) < np_tiles, op_ref[...], os_ref[...])
    y = jnp.dot(o, w_scr[...], preferred_element_type=F32)
    out_ref[...] = x_ref[...] + gate_ref[...] * y


def _proj_res(x, o_p, o_s, w_o, slot, mod, layer, tok):
    n, d = x.shape
    row = lambda i: (i, 0)
    return pl.pallas_call(
        functools.partial(_proj_res_kernel, np_tiles=tok.np_tiles),
        grid=(tok.tiles,),
        in_specs=[pl.BlockSpec((tok.tm, d), row),
                  tok.prompt_spec(d, 1),
                  tok.sample_spec(d, 1),
                  _resident((None, d, d), lambda i: (slot, 0, 0)),
                  tok.mod_spec(layer, 2, d, 1)],
        out_specs=pl.BlockSpec((tok.tm, d), row),
        out_shape=jax.ShapeDtypeStruct((n, d), F32),
        scratch_shapes=[pltpu.VMEM((d, d), BF16)],
        compiler_params=_cparams("arbitrary"),
        name="attn_proj_res",
    )(x, o_p, o_s, w_o, mod)


def _mlp_kernel(x_ref, g_ref, sc_ref, sh_ref, gate_ref, w1_ref, w2_ref, *rest, np_tiles):
    out_refs, (h_scr, acc_scr) = rest[:-2], rest[-2:]
    i = pl.program_id(0)
    j = pl.program_id(1)
    last = j == pl.num_programs(1) - 1

    @pl.when(j == 0)
    def _():
        h_scr[...] = _normmod(x_ref[...], g_ref[...], sc_ref[...], sh_ref[...]).astype(BF16)
        acc_scr[...] = jnp.zeros_like(acc_scr)

    a = jnp.maximum(jnp.dot(h_scr[...], w1_ref[...].astype(BF16), preferred_element_type=F32), 0.0)
    acc_scr[...] += jnp.dot((a * a).astype(BF16), w2_ref[...].astype(BF16), preferred_element_type=F32)

    def result():
        return x_ref[...] + gate_ref[...] * acc_scr[...]

    if len(out_refs) == 1:
        @pl.when(last)
        def _():
            out_refs[0][...] = result()
    else:
        @pl.when(jnp.logical_and(last, i < np_tiles))
        def _():
            out_refs[0][...] = result()

        @pl.when(jnp.logical_and(last, i >= np_tiles))
        def _():
            out_refs[1][...] = result()


def _mlp(x, g, mod, layer, w1, w2, tok, split_out):
    n, d = x.shape
    f = w1.shape[2]
    tf = _pow2_tile(1024, f)
    row = lambda i, j: (i, 0)
    if split_out:
        n_p = tok.np_tiles * tok.tm
        out_specs = [tok.prompt_spec(d, 2), tok.sample_spec(d, 2)]
        out_shape = [jax.ShapeDtypeStruct((n_p, d), F32), jax.ShapeDtypeStruct((n - n_p, d), F32)]
    else:
        out_specs = pl.BlockSpec((tok.tm, d), row)
        out_shape = jax.ShapeDtypeStruct((n, d), F32)
    return pl.pallas_call(
        functools.partial(_mlp_kernel, np_tiles=tok.np_tiles),
        grid=(tok.tiles, f // tf),
        in_specs=[pl.BlockSpec((tok.tm, d), row),
                  pl.BlockSpec((1, d), lambda i, j: (0, 0)),
                  tok.mod_spec(layer, 4, d, 2),
                  tok.mod_spec(layer, 3, d, 2),
                  tok.mod_spec(layer, 5, d, 2),
                  pl.BlockSpec((None, d, tf), lambda i, j: (layer, 0, j)),
                  pl.BlockSpec((None, tf, d), lambda i, j: (layer, j, 0))],
        out_specs=out_specs,
        out_shape=out_shape,
        scratch_shapes=[pltpu.VMEM((tok.tm, d), BF16), pltpu.VMEM((tok.tm, d), F32)],
        compiler_params=_cparams("arbitrary", "arbitrary"),
        name="mlp",
    )(x, g, mod, mod, mod, w1, w2)


def _s5_tables(lam_re, lam_im, log_dt, b_re, b_im, c_re, c_im):
    t = S5_CHUNK
    g, p, k = b_re.shape[1:]
    dt = jnp.exp(log_dt.astype(F32))[:, :, None]
    lr, li = lam_re.astype(F32), lam_im.astype(F32)
    ar, ai = lr * dt, li * dt
    mag = jnp.exp(ar)
    abr, abi = mag * jnp.cos(ai), mag * jnp.sin(ai)
    nr, ni = abr - 1.0, abi
    den = lr * lr + li * li
    f_re = (nr * lr + ni * li) / den
    f_im = (ni * lr - nr * li) / den
    bbr = f_re[..., None] * b_re - f_im[..., None] * b_im
    bbi = f_re[..., None] * b_im + f_im[..., None] * b_re
    n = jnp.arange(t + 1, dtype=F32)[None, None, :, None]
    pm = jnp.exp(n * ar[:, :, None, :])
    pr, pi = pm * jnp.cos(n * ai[:, :, None, :]), pm * jnp.sin(n * ai[:, :, None, :])
    cr, ci = c_re.astype(F32), c_im.astype(F32)

    assert 2 * p == LANES
    cat = lambda a, b: jnp.concatenate([a, b], axis=-1)
    bt_r, bt_i = bbr.transpose(0, 1, 3, 2), bbi.transpose(0, 1, 3, 2)
    bb2 = jnp.stack([cat(bt_r[0], bt_i[0]), cat(-bt_i[0], bt_r[0]),
                     cat(bt_r[1], bt_i[1]), cat(-bt_i[1], bt_r[1])], axis=1)
    pws = jnp.stack([cat(pr[0, :, :t][:, ::-1], pr[0, :, :t][:, ::-1]), cat(pi[0, :, :t][:, ::-1], pi[0, :, :t][:, ::-1]),
                     cat(pr[1, :, :t], pr[1, :, :t]), cat(pi[1, :, :t], pi[1, :, :t])], axis=1)
    pwr = jnp.stack([cat(pr[0, :, 1:], pr[1, :, 1:][:, ::-1]), cat(pi[0, :, 1:], pi[1, :, 1:][:, ::-1])], axis=1)
    c2 = jnp.stack([cat(cr[0], cr[1]), cat(ci[0], ci[1])], axis=1)
    cfb = jnp.stack([cat(cr[0], -ci[0]), cat(cr[1], -ci[1])], axis=1)
    w_intra, w_sum, w2t = _s5_toeplitz(cfb, bb2, pws, c2, pwr)
    at = jnp.stack([jnp.concatenate([pr[0, :, t], pr[1, :, t]], axis=-1),
                    jnp.concatenate([pi[0, :, t], pi[1, :, t]], axis=-1)], axis=1)
    return w_intra, w_sum, w2t, at


def _s5_toeplitz_kernel(cfb_ref, bb2_ref, pws_ref, c2_ref, pwr_ref, o_ref, ws_ref, w2t_ref):
    hi = lax.Precision.HIGHEST
    gl, _, k, p2 = cfb_ref.shape
    p = p2 // 2
    tk = o_ref.shape[1]
    t = tk // k
    for g in range(gl):
        def ab(d):
            blocks = [bb2_ref[g, 2 * d] * pws_ref[g, 2 * d, j:j + 1, :]
                      + bb2_ref[g, 2 * d + 1] * pws_ref[g, 2 * d + 1, j:j + 1, :] for j in range(t)]
            return jnp.concatenate(blocks, axis=0).T

        abf, abb = ab(0), ab(1)
        ws_ref[g] = jnp.concatenate([abf[0:p], abb[0:p], abf[p:], abb[p:]], axis=0).astype(ws_ref.dtype)
        cr2, ci2 = c2_ref[g, 0], c2_ref[g, 1]
        rows = []
        for i in range(t):
            wr, wi = pwr_ref[g, 0, i:i + 1, :], pwr_ref[g, 1, i:i + 1, :]
            rows.append(jnp.concatenate([cr2 * wr - ci2 * wi, -(cr2 * wi + ci2 * wr)], axis=1))
        w2t_ref[g] = jnp.concatenate(rows, axis=0).astype(w2t_ref.dtype)
        kf = jnp.dot(cfb_ref[g, 0], abf, precision=hi, preferred_element_type=F32)
        kb = jnp.dot(cfb_ref[g, 1], abb, precision=hi, preferred_element_type=F32)
        z = jnp.zeros_like(kf)
        krev = jnp.concatenate([kf, z], axis=1) + pltpu.roll(jnp.concatenate([z, kb], axis=1), 2 * tk - k, 1)
        for i in range(S5_CHUNK):
            sh = (S5_CHUNK - 1 - i) * k
            win = krev if sh == 0 else pltpu.roll(krev, 2 * tk - sh, 1)
            o_ref[g, i * k:(i + 1) * k, :] = win[:, 0:tk].astype(o_ref.dtype)


def _s5_toeplitz(cfb, bb2, pws, c2, pwr):
    g, _, k, p2 = cfb.shape
    t = pws.shape[2]
    tk = t * k
    gl = _pow2_tile(8, g)
    spec = lambda a: pl.BlockSpec((gl,) + a.shape[1:], lambda i: (i, 0, 0, 0))
    out = lambda rows, cols: (pl.BlockSpec((gl, rows, cols), lambda i: (i, 0, 0)),
                              jax.ShapeDtypeStruct((g, rows, cols), BF16))
    outs = [out(tk, tk), out(2 * p2, tk), out(tk, 2 * p2)]
    return pl.pallas_call(
        _s5_toeplitz_kernel,
        grid=(g // gl,),
        in_specs=[spec(a) for a in (cfb, bb2, pws, c2, pwr)],
        out_specs=[o[0] for o in outs],
        out_shape=[o[1] for o in outs],
        compiler_params=_cparams("arbitrary"),
        name="s5_toeplitz",
    )(cfb, bb2, pws, c2, pwr)


def _s5_mixer(x, g, mod, layer, tok, tok_post, params, h0, dims):
    (lam_re, lam_im, log_dt, b_re, b_im, c_re, c_im, d_skip, glu_w, slot, glu_b) = params
    batch, seq, dec_batch, dec_seq = dims
    n_prompt = batch * seq
    d = d_skip.shape[0]
    ngrp = d // S5_GROUP
    p = lam_re.shape[-1]
    w_intra, w_sum, w2t, at = _s5_tables(lam_re, lam_im, log_dt, b_re, b_im, c_re, c_im)
    hperm = _s5_pre(x, g, mod, layer, tok)
    h0g = h0.astype(F32).transpose(3, 2, 0, 1, 4).reshape(ngrp, 2, dec_batch, 2 * p)
    streams = ((0, seq // S5_CHUNK, batch, False, True),
               (n_prompt // S5_CHUNK, dec_seq // S5_CHUNK, dec_batch, True, False))
    yperm, fin = _s5_scan(hperm, w_intra, w_sum, w2t, at, h0g, batch, streams)
    x_new = _s5_post(x, yperm, g, mod, layer, d_skip.reshape(1, d), glu_w, slot, glu_b, tok_post)
    st = fin.reshape(ngrp, 2, batch, 2, p).transpose(2, 3, 1, 0, 4)
    return x_new, st


def kernel(x_prompt, x_sample, state_s5, cache_na_k, cache_na_v, cache_gqa_k, cache_gqa_v, c, c_ctx, norm_g, ada_w, ada_b, mlp_w1, mlp_w2, s5_lam_re, s5_lam_im, s5_log_dt, s5_b_re, s5_b_im, s5_c_re, s5_c_im, s5_d, s5_glu_w, s5_glu_b, na_w_qkv, na_q_norm, na_k_norm, na_rpb, na_w_o, gqa_w_qkv, gqa_q_norm, gqa_k_norm, gqa_w_o):
    batch, seq, d = x_prompt.shape
    dec_batch, dec_seq, _ = x_sample.shape
    depth = ada_w.shape[0]
    n_prompt = batch * seq
    n_sample = dec_batch * dec_seq
    na_heads, na_hd = cache_na_k.shape[3], cache_na_k.shape[4]
    gqa_kv, gqa_hd = cache_gqa_k.shape[3], cache_gqa_k.shape[4]
    assert n_prompt % dec_seq == 0

    tok = _Tok(n_prompt, n_sample, dec_seq, 1024)
    tok_half = _Tok(n_prompt, n_sample, dec_seq, 512)

    mod_rows = -(-(1 + dec_batch) // SUBLANES) * SUBLANES
    cvec = jnp.concatenate([c_ctx[None, :], c, jnp.zeros((mod_rows - 1 - dec_batch, d), F32)], axis=0)
    mod = _modulation(cvec, ada_w, ada_b).reshape(depth, mod_rows, 6, 1, d)

    x = (x_prompt.reshape(n_prompt, d), x_sample.reshape(n_sample, d))
    cache_na_k2 = cache_na_k.reshape(cache_na_k.shape[:3] + (na_heads * na_hd,))
    cache_na_v2 = cache_na_v.reshape(cache_na_v.shape[:3] + (na_heads * na_hd,))
    cache_gqa_k2 = cache_gqa_k.reshape(cache_gqa_k.shape[:3] + (gqa_kv * gqa_hd,))
    cache_gqa_v2 = cache_gqa_v.reshape(cache_gqa_v.shape[:3] + (gqa_kv * gqa_hd,))
    cos_t, sin_t = _rope_tables(dec_seq, gqa_hd)

    new_s5, new_na_k, new_na_v, new_gqa_k, new_gqa_v = [], [], [], [], []
    for i in range(depth):
        kind, slot = i % 3, i // 3
        g1 = norm_g[i, 0].reshape(1, d)
        g2 = norm_g[i, 1].reshape(1, d)
        if kind == 0:
            params = (s5_lam_re[slot], s5_lam_im[slot], s5_log_dt[slot], s5_b_re[slot], s5_b_im[slot],
                      s5_c_re[slot], s5_c_im[slot], s5_d[slot], s5_glu_w, slot, s5_glu_b[slot])
            x, st = _s5_mixer(x, g1, mod, i, tok_half, tok, params, state_s5[:, slot],
                              (batch, seq, dec_batch, dec_seq))
            new_s5.append(st)
        elif kind == 1:
            assert not isinstance(x, tuple)
            q, k_p, k_s, v_p, v_s = _na_qkv(x, g1, mod, i, na_w_qkv, slot, na_q_norm[slot], na_k_norm[slot], tok_half)
            o_p = _na_ctx_attn(q, k_p, v_p, batch, seq, na_hd)
            bias = _na_bias_table(na_rpb[slot])
            o_s = _na_lat_attn(q, k_s, v_s, cache_na_k2, cache_na_v2, slot, bias, n_prompt, dec_batch, dec_seq, na_hd)
            x = _proj_res(x, o_p, o_s, na_w_o, slot, mod, i, tok)
            new_na_k.append(k_p.reshape(batch, seq, na_heads, na_hd))
            new_na_v.append(v_p.reshape(batch, seq, na_heads, na_hd))
        else:
            assert not isinstance(x, tuple)
            q, k_p, k_s, v_p, v_s = _gqa_qkv(x, g1, mod, i, gqa_w_qkv, slot, gqa_q_norm[slot], gqa_k_norm[slot],
                                             cos_t, sin_t, gqa_kv, tok_half)
            o_p = _gqa_ctx_attn(q, k_p, v_p, batch, seq, gqa_kv)
            o_s = _gqa_lat_attn(q, k_s, v_s, cache_gqa_k2, cache_gqa_v2, slot, n_prompt, dec_batch, dec_seq, gqa_kv)
            x = _proj_res(x, o_p, o_s, gqa_w_o, slot, mod, i, tok)
            new_gqa_k.append(k_p.reshape(batch, seq, gqa_kv, gqa_hd))
            new_gqa_v.append(v_p.reshape(batch, seq, gqa_kv, gqa_hd))
        x = _mlp(x, g2, mod, i, mlp_w1, mlp_w2, tok, split_out=(i == depth - 1))
    y_p, y_s = x
    return (y_p.reshape(batch, seq, d), y_s.reshape(dec_batch, dec_seq, d),
            jnp.stack(new_s5, axis=1), jnp.stack(new_na_k, axis=1), jnp.stack(new_na_v, axis=1),
            jnp.stack(new_gqa_k, axis=1), jnp.stack(new_gqa_v, axis=1))
```

```python
import functools
import math

import jax
import jax.numpy as jnp
from jax import lax
from jax.experimental import pallas as pl
from jax.experimental.pallas import tpu as pltpu

F32 = jnp.float32
BF16 = jnp.bfloat16

EPS = 1e-6
GRID_W = 64
S5_GROUP = 16
NA_KH = 8
NA_KW = 16
ROPE_THETA = 10000.0
S5_CHUNK = 16
LOG2E = math.log2(math.e)
LANES = 128
SUBLANES = 8
VMEM_LIMIT = 56 * 1024 * 1024


def _cparams(*sem):
    return pltpu.CompilerParams(dimension_semantics=sem, vmem_limit_bytes=VMEM_LIMIT)


def _pow2_tile(pref, *ns):
    t = pref
    while any(n % t for n in ns):
        t //= 2
    return t


def _row_halves(n):
    if n % 2 or n // 2 < LANES:
        return [slice(0, n)]
    return [slice(0, n // 2), slice(n // 2, n)]


def _normmod(x, g, sc, sh):
    ms = jnp.mean(x * x, axis=-1, keepdims=True)
    y = x * lax.rsqrt(ms + EPS) * g
    return y * (1.0 + sc) + sh


class _Tok:
    def __init__(self, n_prompt, n_sample, dec_seq, pref):
        self.tm = _pow2_tile(pref, n_prompt, dec_seq)
        self.n = n_prompt + n_sample
        self.tiles = self.n // self.tm
        self.np_tiles = n_prompt // self.tm
        self.tps = dec_seq // self.tm

    def mod_row(self, i):
        return jnp.where(i < self.np_tiles, 0, 1 + (i - self.np_tiles) // self.tps)

    def prompt_spec(self, d, nargs):
        last = self.np_tiles - 1
        if nargs == 1:
            return pl.BlockSpec((self.tm, d), lambda i: (jnp.minimum(i, last), 0))
        return pl.BlockSpec((self.tm, d), lambda i, j: (jnp.minimum(i, last), 0))

    def sample_spec(self, d, nargs):
        first = self.np_tiles
        if nargs == 1:
            return pl.BlockSpec((self.tm, d), lambda i: (jnp.maximum(i - first, 0), 0))
        return pl.BlockSpec((self.tm, d), lambda i, j: (jnp.maximum(i - first, 0), 0))

    def mod_spec(self, layer, which, d, nargs):
        if nargs == 1:
            return pl.BlockSpec((None, None, None, 1, d), lambda i: (layer, self.mod_row(i), which, 0, 0))
        return pl.BlockSpec((None, None, None, 1, d), lambda i, j: (layer, self.mod_row(i), which, 0, 0))


def _mod_kernel(c_ref, w_ref, b_ref, o_ref):
    c = c_ref[...]
    s = c * jax.nn.sigmoid(c)
    o_ref[...] = jnp.dot(s.astype(BF16), w_ref[...].astype(BF16), preferred_element_type=F32) + b_ref[...]


def _modulation(cvec, ada_w, ada_b):
    depth, d, d6 = ada_w.shape
    rows = cvec.shape[0]
    tn = _pow2_tile(2048, d6) if d6 % 2048 == 0 else d
    return pl.pallas_call(
        _mod_kernel,
        grid=(depth, d6 // tn),
        in_specs=[pl.BlockSpec((rows, d), lambda l, j: (0, 0)),
                  pl.BlockSpec((None, d, tn), lambda l, j: (l, 0, j)),
                  pl.BlockSpec((None, 1, tn), lambda l, j: (l, 0, j))],
        out_specs=pl.BlockSpec((None, rows, tn), lambda l, j: (l, 0, j)),
        out_shape=jax.ShapeDtypeStruct((depth, rows, d6), F32),
        compiler_params=_cparams("arbitrary", "arbitrary"),
        name="adaln_modulation",
    )(cvec, ada_w, ada_b.reshape(depth, 1, d6))


def _read_x(x_refs, np_tiles, rows=slice(None), cols=slice(None)):
    if len(x_refs) == 1:
        return x_refs[0][rows, cols]
    return jnp.where(pl.program_id(0) < np_tiles, x_refs[0][rows, cols], x_refs[1][rows, cols])


def _x_specs(x, tok, nargs):
    if isinstance(x, tuple):
        d = x[0].shape[1]
        return [tok.prompt_spec(d, nargs), tok.sample_spec(d, nargs)], x
    d = x.shape[1]
    return [pl.BlockSpec((tok.tm, d), (lambda i: (i, 0)) if nargs == 1 else (lambda i, j: (i, 0)))], (x,)


def _s5_pre_kernel(*refs, n_x, np_tiles):
    x_refs, (g_ref, sc_ref, sh_ref, h_ref, h_scr) = refs[:n_x], refs[n_x:]
    x = _read_x(x_refs, np_tiles)
    rstd = lax.rsqrt(jnp.mean(x * x, axis=-1, keepdims=True) + EPS)
    nc = h_ref.shape[1]
    for a in range(h_scr.shape[0]):
        sl = slice(a * LANES, (a + 1) * LANES)
        h_scr[a] = (x[:, sl] * rstd * g_ref[:, sl]) * (1.0 + sc_ref[:, sl]) + sh_ref[:, sl]
        for j in range(S5_CHUNK):
            h_ref[j, :, sl] = h_scr[a, pl.ds(j, nc, stride=S5_CHUNK), :].astype(h_ref.dtype)


def _s5_pre(x, g, mod, layer, tok):
    x_specs, xs = _x_specs(x, tok, 1)
    d = xs[0].shape[1]
    nc = tok.tm // S5_CHUNK
    return pl.pallas_call(
        functools.partial(_s5_pre_kernel, n_x=len(xs), np_tiles=tok.np_tiles),
        grid=(tok.tiles,),
        in_specs=x_specs + [pl.BlockSpec((1, d), lambda i: (0, 0)),
                            tok.mod_spec(layer, 1, d, 1),
                            tok.mod_spec(layer, 0, d, 1)],
        out_specs=pl.BlockSpec((S5_CHUNK, nc, d), lambda i: (0, i, 0)),
        out_shape=jax.ShapeDtypeStruct((S5_CHUNK, tok.n // S5_CHUNK, d), BF16),
        scratch_shapes=[pltpu.VMEM((d // LANES, tok.tm, LANES), F32)],
        compiler_params=_cparams("arbitrary"),
        name="s5_pre",
    )(*xs, g, mod, mod)


def _s5_scan_kernel(h_ref, wi_ref, ws_ref, w2t_ref, at_ref, h0_ref, *rest, streams, unroll):
    ns = len(streams)
    perm_refs, (y_ref, fin_ref, xt_scr, yt_scr), chain_scr = rest[:ns], rest[ns:ns + 4], rest[ns + 4:]
    ngrp = wi_ref.shape[0]
    p2 = at_ref.shape[-1]
    fwd = lax.broadcasted_iota(jnp.int32, (1, p2), 1) < (p2 // 2)
    for j in range(S5_CHUNK):
        xt_scr[j] = h_ref[j].astype(F32).T.astype(BF16)

    def load_rmat(r, stream):
        row0, n_chunks, nb = stream[:3]
        cols = slice(row0, row0 + n_chunks * nb)
        c0 = pl.multiple_of(r * S5_GROUP, S5_GROUP)
        return jnp.concatenate([xt_scr[j, pl.ds(c0, S5_GROUP), cols] for j in range(S5_CHUNK)], axis=0), c0, cols

    def summaries(r, stream, perm_ref, s_scr):
        nb, use_h0 = stream[2], stream[3]
        rmat, _, _ = load_rmat(r, stream)
        rmat_cm = jnp.dot(rmat, perm_ref[...], preferred_element_type=F32).astype(BF16)
        zs = jnp.dot(ws_ref[r], rmat_cm, preferred_element_type=F32)
        s_scr[0] = zs[0:p2].T
        s_scr[1] = zs[p2:].T
        if use_h0:
            return h0_ref[r, 0], h0_ref[r, 1]
        return jnp.zeros((nb, p2), F32), jnp.zeros((nb, p2), F32)

    def scan_step(t, decay, stream, state, s_scr, e_scr):
        n_chunks, nb = stream[1], stream[2]
        hr, hi = state
        ar, ai = decay
        ft = slice(t * nb, (t + 1) * nb)
        bt = slice((n_chunks - 1 - t) * nb, (n_chunks - t) * nb)
        e_scr[0, ft, :] = hr
        e_scr[1, bt, :] = hr
        e_scr[2, ft, :] = hi
        e_scr[3, bt, :] = hi
        in_re = jnp.where(fwd, s_scr[0, ft, :], s_scr[0, bt, :])
        in_im = jnp.where(fwd, s_scr[1, ft, :], s_scr[1, bt, :])
        return ar * hr - ai * hi + in_re, ar * hi + ai * hr + in_im

    def outputs(r, stream, perm_ref, state, e_scr):
        rmat, c0, cols = load_rmat(r, stream)
        if stream[4]:
            fin_ref[r, 0] = state[0]
            fin_ref[r, 1] = state[1]
        e_cm = jnp.concatenate([jnp.where(fwd, e_scr[0], e_scr[1]), jnp.where(fwd, e_scr[2], e_scr[3])],
                               axis=1).astype(BF16)
        e = jnp.dot(perm_ref[...], e_cm, preferred_element_type=F32).astype(BF16)
        yt = (jnp.dot(wi_ref[r], rmat, preferred_element_type=F32)
              + lax.dot_general(w2t_ref[r], e, (((1,), (1,)), ((), ())), preferred_element_type=F32))
        for i in range(S5_CHUNK):
            yt_scr[i, pl.ds(c0, S5_GROUP), cols] = yt[i * S5_GROUP:(i + 1) * S5_GROUP]

    def body(rb, carry):
        chains = []
        for u in range(unroll):
            for si, stream in enumerate(streams):
                k = 2 * (u * ns + si)
                chains.append((rb * unroll + u, stream, perm_refs[si], chain_scr[k], chain_scr[k + 1]))
        states = [summaries(r, stream, pm, s_scr) for r, stream, pm, s_scr, _ in chains]
        decays = [(at_ref[r, 0:1, :], at_ref[r, 1:2, :]) for r, _, _, _, _ in chains]
        for t in range(max(stream[1] for stream in streams)):
            for ci, (_, stream, _, s_scr, e_scr) in enumerate(chains):
                if t < stream[1]:
                    states[ci] = scan_step(t, decays[ci], stream, states[ci], s_scr, e_scr)
        for (r, stream, pm, _, e_scr), state in zip(chains, states):
            outputs(r, stream, pm, state, e_scr)
        return carry

    lax.fori_loop(0, ngrp // unroll, body, 0)
    for i in range(S5_CHUNK):
        y_ref[i] = yt_scr[i].T


def _s5_scan(hperm, w_intra, w_sum, w2t, at, h0, nb_fin, streams):
    t, nrows, d = hperm.shape
    g, tk, p4 = w2t.shape
    p2 = at.shape[-1]
    gl = LANES // S5_GROUP
    nb0 = h0.shape[2]
    unroll = 8
    assert sum(s[1] * s[2] for s in streams) == nrows and gl % unroll == 0
    chain_scr = []
    for _ in range(unroll):
        for s in streams:
            chain_scr += [pltpu.VMEM((2, s[1] * s[2], p2), F32), pltpu.VMEM((4, s[1] * s[2], p2), F32)]
    perms = []
    for _, n_chunks, nb, _, _ in streams:
        m = jnp.arange(n_chunks * nb)
        perms.append((m[:, None] == ((m % nb) * n_chunks + m // nb)[None, :]).astype(BF16))
    return pl.pallas_call(
        functools.partial(_s5_scan_kernel, streams=streams, unroll=unroll),
        grid=(d // LANES,),
        in_specs=[pl.BlockSpec((t, nrows, LANES), lambda a: (0, 0, a)),
                  pl.BlockSpec((gl, tk, tk), lambda a: (a, 0, 0)),
                  pl.BlockSpec((gl, p4, tk), lambda a: (a, 0, 0)),
                  pl.BlockSpec((gl, tk, p4), lambda a: (a, 0, 0)),
                  pl.BlockSpec((gl, 2, p2), lambda a: (a, 0, 0)),
                  pl.BlockSpec((gl, 2, nb0, p2), lambda a: (a, 0, 0, 0))]
                 + [pl.BlockSpec(pm.shape, lambda a: (0, 0)) for pm in perms],
        out_specs=[pl.BlockSpec((t, nrows, LANES), lambda a: (0, 0, a)),
                   pl.BlockSpec((gl, 2, nb_fin, p2), lambda a: (a, 0, 0, 0))],
        out_shape=[jax.ShapeDtypeStruct((t, nrows, d), F32),
                   jax.ShapeDtypeStruct((g, 2, nb_fin, p2), F32)],
        scratch_shapes=[pltpu.VMEM((t, LANES, nrows), BF16), pltpu.VMEM((t, LANES, nrows), F32)] + chain_scr,
        compiler_params=_cparams("arbitrary"),
        name="s5_scan",
    )(hperm, w_intra, w_sum, w2t, at, h0, *perms)


def _s5_post_kernel(*refs, n_x, np_tiles):
    x_refs = refs[:n_x]
    (y_ref, g_ref, sc_ref, sh_ref, gate_ref, dsk_ref, wa_ref, wg_ref, ba_ref, bg_ref, o_ref,
     y_scr, wa_scr, wg_scr) = refs[n_x:]
    _cast_once(wa_ref, wa_scr)
    _cast_once(wg_ref, wg_scr)
    tm, d = o_ref.shape
    n_cb = max(d // (2 * LANES), 1)
    cw = d // n_cb

    def before(rows):
        c0, nc = rows.start // S5_CHUNK, (rows.stop - rows.start) // S5_CHUNK
        for a in range(y_scr.shape[0]):
            for j in range(S5_CHUNK):
                y_scr[a, pl.ds(rows.start + j, nc, stride=S5_CHUNK), :] = y_ref[j, c0:c0 + nc, a * LANES:(a + 1) * LANES]
        x = _read_x(x_refs, np_tiles, rows)
        h = _normmod(x, g_ref[...], sc_ref[...], sh_ref[...])
        y = h * dsk_ref[...] + jnp.concatenate([y_scr[a, rows, :] for a in range(y_scr.shape[0])], axis=1)
        return jax.nn.gelu(y).astype(BF16)

    def matmuls(a, cb):
        cols = slice(cb * cw, (cb + 1) * cw)
        return (jnp.dot(a, wa_scr[:, cols], preferred_element_type=F32),
                jnp.dot(a, wg_scr[:, cols], preferred_element_type=F32))

    def after(rows, cb, z):
        cols = slice(cb * cw, (cb + 1) * cw)
        za, zg = z[0] + ba_ref[:, cols], z[1] + bg_ref[:, cols]
        x = _read_x(x_refs, np_tiles, rows, cols)
        o_ref[rows, cols] = x + gate_ref[:, cols] * (za * jax.nn.sigmoid(zg))

    def row_chunks(rows):
        step = max((rows.stop - rows.start) // n_cb, S5_CHUNK)
        return [slice(r, min(r + step, rows.stop)) for r in range(rows.start, rows.stop, step)]

    halves = _row_halves(tm)
    a_cur = jnp.concatenate([before(rc) for rc in row_chunks(halves[0])], axis=0)
    z_prev = None
    for hi, rows in enumerate(halves):
        nxt = row_chunks(halves[hi + 1]) if hi + 1 < len(halves) else []
        a_next, z_cur = [], []
        for cb in range(n_cb):
            z_cur.append(matmuls(a_cur, cb))
            if cb < len(nxt):
                a_next.append(before(nxt[cb]))
            if z_prev is not None:
                after(halves[hi - 1], cb, z_prev[cb])
        a_next += [before(rc) for rc in nxt[n_cb:]]
        z_prev = z_cur
        if a_next:
            a_cur = jnp.concatenate(a_next, axis=0)
    for cb in range(n_cb):
        after(halves[-1], cb, z_prev[cb])


def _s5_post(x, y, g, mod, layer, dsk, glu_w, slot, glu_b, tok):
    x_specs, xs = _x_specs(x, tok, 1)
    n, d = tok.n, xs[0].shape[1]
    row = lambda i: (i, 0)
    fixed = lambda i: (0, 0)
    gb2 = glu_b.reshape(1, 2 * d)
    return pl.pallas_call(
        functools.partial(_s5_post_kernel, n_x=len(xs), np_tiles=tok.np_tiles),
        grid=(tok.tiles,),
        in_specs=x_specs + [
                  pl.BlockSpec((S5_CHUNK, tok.tm // S5_CHUNK, d), lambda i: (0, i, 0)),
                  pl.BlockSpec((1, d), fixed),
                  tok.mod_spec(layer, 1, d, 1),
                  tok.mod_spec(layer, 0, d, 1),
                  tok.mod_spec(layer, 2, d, 1),
                  pl.BlockSpec((1, d), fixed),
                  _resident((None, d, d), lambda i: (slot, 0, 0)),
                  _resident((None, d, d), lambda i: (slot, 0, 1)),
                  pl.BlockSpec((1, d), lambda i: (0, 0)),
                  pl.BlockSpec((1, d), lambda i: (0, 1))],
        out_specs=pl.BlockSpec((tok.tm, d), row),
        out_shape=jax.ShapeDtypeStruct((n, d), F32),
        scratch_shapes=[pltpu.VMEM((d // LANES, tok.tm, LANES), F32),
                        pltpu.VMEM((d, d), BF16), pltpu.VMEM((d, d), BF16)],
        compiler_params=_cparams("arbitrary"),
        name="s5_post",
    )(*xs, y, g, mod, mod, mod, dsk, glu_w, glu_w, gb2, gb2)


def _cast_once(w_ref, w_scr):
    @pl.when(pl.program_id(0) == 0)
    def _():
        w_scr[...] = w_ref[...].astype(w_scr.dtype)


def _resident(block_shape, index_map):
    return pl.BlockSpec(block_shape, index_map, pipeline_mode=pl.Buffered(1))


def _na_qkv_kernel(x_ref, g_ref, sc_ref, sh_ref, w_ref, qn_ref, kn_ref, q_ref, kp_ref, ks_ref, vp_ref, vs_ref, w_scr, *,
                   hd, np_tiles):
    i = pl.program_id(0)
    _cast_once(w_ref, w_scr)
    d = x_ref.shape[1]
    lo = lax.broadcasted_iota(jnp.int32, (1, LANES), 1) < hd

    def head_norm(z, gain):
        outs = []
        for s in range(d // LANES):
            seg = z[:, s * LANES:(s + 1) * LANES]
            sq = seg * seg
            s_lo = jnp.sum(jnp.where(lo, sq, 0.0), axis=-1, keepdims=True)
            s_hi = jnp.sum(jnp.where(lo, 0.0, sq), axis=-1, keepdims=True)
            ms = jnp.where(lo, s_lo, s_hi) / hd
            outs.append(seg * lax.rsqrt(ms + EPS) * gain)
        return jnp.concatenate(outs, axis=1)

    h = _normmod(x_ref[...], g_ref[...], sc_ref[...], sh_ref[...]).astype(BF16)
    proj = lambda part: jnp.dot(h, w_scr[:, part * d:(part + 1) * d], preferred_element_type=F32)
    q_ref[...] = (head_norm(proj(0), qn_ref[...]) * (hd ** -0.5 * LOG2E)).astype(q_ref.dtype)
    k = head_norm(proj(1), kn_ref[...])
    v = proj(2)

    @pl.when(i < np_tiles)
    def _():
        kp_ref[...] = k
        vp_ref[...] = v

    @pl.when(i >= np_tiles)
    def _():
        ks_ref[...] = k.astype(ks_ref.dtype)
        vs_ref[...] = v.astype(vs_ref.dtype)


def _na_qkv(x, g, mod, layer, w_qkv, slot, qn, kn, tok):
    n, d = x.shape
    hd = qn.shape[-1]
    assert 2 * hd == LANES
    row = lambda i: (i, 0)
    fixed = lambda i: (0, 0)
    qn2 = jnp.tile(qn, 2).reshape(1, LANES)
    kn2 = jnp.tile(kn, 2).reshape(1, LANES)
    n_p, n_s = tok.np_tiles * tok.tm, n - tok.np_tiles * tok.tm
    return pl.pallas_call(
        functools.partial(_na_qkv_kernel, hd=hd, np_tiles=tok.np_tiles),
        grid=(tok.tiles,),
        in_specs=[pl.BlockSpec((tok.tm, d), row),
                  pl.BlockSpec((1, d), fixed),
                  tok.mod_spec(layer, 1, d, 1),
                  tok.mod_spec(layer, 0, d, 1),
                  _resident((None, d, 3 * d), lambda i: (slot, 0, 0)),
                  pl.BlockSpec((1, LANES), fixed),
                  pl.BlockSpec((1, LANES), fixed)],
        out_specs=[pl.BlockSpec((tok.tm, d), row),
                   tok.prompt_spec(d, 1), tok.sample_spec(d, 1), tok.prompt_spec(d, 1), tok.sample_spec(d, 1)],
        out_shape=[jax.ShapeDtypeStruct((n, d), BF16),
                   jax.ShapeDtypeStruct((n_p, d), F32), jax.ShapeDtypeStruct((n_s, d), BF16),
                   jax.ShapeDtypeStruct((n_p, d), F32), jax.ShapeDtypeStruct((n_s, d), BF16)],
        scratch_shapes=[pltpu.VMEM((d, 3 * d), BF16)],
        compiler_params=_cparams("arbitrary"),
        name="na_qkv",
    )(x, g, mod, mod, w_qkv, qn2, kn2)


def _softmax2_pv(s_parts, v_parts):
    m = s_parts[0].max(axis=-1, keepdims=True)
    for s in s_parts[1:]:
        m = jnp.maximum(m, s.max(axis=-1, keepdims=True))
    den = 0.0
    acc = 0.0
    for s, v in zip(s_parts, v_parts):
        e = jnp.exp2(s - m)
        den = den + e.sum(axis=-1, keepdims=True)
        acc = acc + jnp.dot(e.astype(BF16), v, preferred_element_type=F32)
    return acc / den


def _qk(q, k):
    return lax.dot_general(q, k, (((1,), (1,)), ((), ())), preferred_element_type=F32)


def _na_ctx_attn_kernel(q_ref, k_ref, v_ref, o_ref, *, hd):
    lo = lax.broadcasted_iota(jnp.int32, (1, LANES), 1) < hd
    for s in range(q_ref.shape[1] // LANES):
        sl = slice(s * LANES, (s + 1) * LANES)
        q = q_ref[:, sl]
        k = k_ref[:, sl].astype(BF16)
        v = v_ref[:, sl].astype(BF16)
        zero = jnp.zeros_like(q)
        q2 = jnp.concatenate([jnp.where(lo, q, zero), jnp.where(lo, zero, q)], axis=0)
        out = _softmax2_pv([_qk(q2, k)], [v])
        n = q.shape[0]
        o_ref[:, sl] = jnp.where(lo, out[0:n], out[n:]).astype(o_ref.dtype)


def _na_ctx_attn(q, k, v, batch, seq, hd):
    d = q.shape[1]
    spec = pl.BlockSpec((seq, d), lambda b: (b, 0))
    return pl.pallas_call(
        functools.partial(_na_ctx_attn_kernel, hd=hd),
        grid=(batch,),
        in_specs=[spec, spec, spec],
        out_specs=spec,
        out_shape=jax.ShapeDtypeStruct((batch * seq, d), BF16),
        compiler_params=_cparams("arbitrary"),
        name="na_ctx_attn",
    )(q, k, v)


def _na_window(r0, q_rows, rows, kh):
    w_rows = min(rows, kh + q_rows + (kh + q_rows) % 2)
    rs = min(max(r0 - kh // 2, 0), rows - kh)
    return min(rs // 2 * 2, rows - w_rows), w_rows


def _na_lat_attn_kernel(q_ref, kl_ref, vl_ref, kc_ref, vc_ref, tab_ref, o_ref, kc_scr, vc_scr, bias_scr, *,
                        hd, rows, kh, tq):
    lane = lax.broadcasted_iota(jnp.int32, (1, LANES), 1)
    lo = lane < hd
    q_rows = tq // GRID_W
    n_qb = q_ref.shape[0] // tq
    w_cols = bias_scr.shape[2]
    kc_scr[...] = kc_ref[...].astype(BF16)
    vc_scr[...] = vc_ref[...].astype(BF16)

    @pl.when(pl.program_id(1) == 0)
    def _():
        neg = jnp.full((GRID_W, GRID_W), -jnp.inf, F32)
        for hh in range(2):
            for rq in range(rows):
                w0, w_rows = _na_window(rq // q_rows * q_rows, q_rows, rows, kh)
                rs = min(max(rq - kh // 2, 0), rows - kh)
                for kr in range(w_rows):
                    rk = w0 + kr
                    tile = tab_ref[hh, rk - rq + kh - 1] if rs <= rk < rs + kh else neg
                    bias_scr[hh, rq * GRID_W:(rq + 1) * GRID_W, kr * GRID_W:(kr + 1) * GRID_W] = tile

    per_iter = 4 if n_qb % 4 == 0 else 1

    def qblocks(it, carry):
        work = []
        for u in range(per_iter):
            qb = it * per_iter + u
            r0 = qb * q_rows
            w0 = jnp.minimum(jnp.clip(r0 - kh // 2, 0, rows - kh) // 2 * 2, rows - w_cols // GRID_W)
            k0 = pl.multiple_of(w0 * GRID_W, 2 * GRID_W)
            q0 = pl.multiple_of(qb * tq, tq)
            q = q_ref[pl.ds(q0, tq), :]
            kl = kl_ref[pl.ds(k0, w_cols), :]
            zero = jnp.zeros_like(q)
            q2 = jnp.concatenate([jnp.where(lo, q, zero), jnp.where(lo, zero, q)], axis=0)
            bias2 = jnp.concatenate([bias_scr[0, pl.ds(q0, tq), :], bias_scr[1, pl.ds(q0, tq), :]], axis=0)
            work.append((q0, k0, [_qk(q2, kl) + bias2, _qk(q2, kc_scr[...])]))
        for q0, k0, scores in work:
            vl = vl_ref[pl.ds(k0, w_cols), :]
            out = _softmax2_pv(scores, [vl, vc_scr[...]])
            o_ref[pl.ds(q0, tq), :] = jnp.where(lo, out[0:tq], out[tq:]).astype(o_ref.dtype)
        return carry

    lax.fori_loop(0, n_qb // per_iter, qblocks, 0)


def _na_lat_attn(q, k, v, cache_k, cache_v, slot, tab, n_prompt, dec_batch, dec_seq, hd):
    d = q.shape[1]
    past = cache_k.shape[2]
    rows = dec_seq // GRID_W
    kh = min(NA_KH, rows)
    tq = _pow2_tile(256, dec_seq)
    assert tq % GRID_W == 0 and rows % 2 == 0 and 2 * GRID_W == LANES and n_prompt % dec_seq == 0
    b0 = n_prompt // dec_seq
    w_rows = _na_window(0, tq // GRID_W, rows, kh)[1]
    return pl.pallas_call(
        functools.partial(_na_lat_attn_kernel, hd=hd, rows=rows, kh=kh, tq=tq),
        grid=(d // LANES, dec_batch),
        in_specs=[pl.BlockSpec((dec_seq, LANES), lambda hp, b: (b0 + b, hp)),
                  pl.BlockSpec((dec_seq, LANES), lambda hp, b: (b, hp)),
                  pl.BlockSpec((dec_seq, LANES), lambda hp, b: (b, hp)),
                  pl.BlockSpec((None, None, past, LANES), lambda hp, b: (b, slot, 0, hp)),
                  pl.BlockSpec((None, None, past, LANES), lambda hp, b: (b, slot, 0, hp)),
                  pl.BlockSpec((2,) + tab.shape[1:], lambda hp, b: (hp, 0, 0, 0))],
        out_specs=pl.BlockSpec((dec_seq, LANES), lambda hp, b: (b, hp)),
        out_shape=jax.ShapeDtypeStruct((dec_batch * dec_seq, d), BF16),
        scratch_shapes=[pltpu.VMEM((past, LANES), BF16), pltpu.VMEM((past, LANES), BF16),
                        pltpu.VMEM((2, dec_seq, w_rows * GRID_W), F32)],
        compiler_params=_cparams("arbitrary", "arbitrary"),
        name="na_lat_attn",
    )(q, k, v, cache_k, cache_v, tab)


def _na_bias_table(rpb):
    nh, nd, nc = rpb.shape
    c = jnp.arange(GRID_W)
    cs = jnp.clip(c - NA_KW // 2, 0, GRID_W - NA_KW)
    col_ok = (c[None, :] >= cs[:, None]) & (c[None, :] < cs[:, None] + NA_KW)
    dcol = c[None, :] - c[:, None] + NA_KW - 1
    onehot = (dcol[None] == jnp.arange(nc)[:, None, None]).astype(F32)
    tmp = jnp.einsum('hdc,cxy->hdxy', rpb.astype(F32), onehot, precision=lax.Precision.HIGHEST)
    return jnp.where(col_ok[None, None], tmp * LOG2E, -jnp.inf)


def _gqa_qkv_kernel(x_ref, g_ref, sc_ref, sh_ref, w_ref, qn_ref, kn_ref, cos_ref, sin_ref,
                    q_ref, kp_ref, ks_ref, vp_ref, vs_ref, w_scr, *, nq, nk, np_tiles):
    i = pl.program_id(0)
    _cast_once(w_ref, w_scr)
    is_sample = i >= np_tiles
    ks, vs = [], []
    for rows in _row_halves(x_ref.shape[0]):
        h = _normmod(x_ref[rows, :], g_ref[...], sc_ref[...], sh_ref[...]).astype(BF16)
        z = jnp.dot(h, w_scr[...], preferred_element_type=F32)
        cos = cos_ref[rows, :]
        sin = sin_ref[rows, :]

        def norm_rope(seg, gain):
            ms = jnp.mean(seg * seg, axis=-1, keepdims=True)
            y = seg * lax.rsqrt(ms + EPS) * gain
            roped = y * cos + pltpu.roll(y, LANES // 2, 1) * sin
            return jnp.where(is_sample, roped, y)

        for hh in range(nq):
            sl = slice(hh * LANES, (hh + 1) * LANES)
            q_ref[rows, sl] = (norm_rope(z[:, sl], qn_ref[...]) * (LANES ** -0.5 * LOG2E)).astype(q_ref.dtype)
        ks.append(jnp.concatenate(
            [norm_rope(z[:, (nq + hh) * LANES:(nq + hh + 1) * LANES], kn_ref[...]) for hh in range(nk)], axis=1))
        vs.append(z[:, (nq + nk) * LANES:])
    k = jnp.concatenate(ks, axis=0)
    v = jnp.concatenate(vs, axis=0)

    @pl.when(i < np_tiles)
    def _():
        kp_ref[...] = k
        vp_ref[...] = v

    @pl.when(is_sample)
    def _():
        ks_ref[...] = k.astype(ks_ref.dtype)
        vs_ref[...] = v.astype(vs_ref.dtype)


def _gqa_qkv(x, g, mod, layer, w_qkv, slot, qn, kn, cos_t, sin_t, nk, tok):
    n, d = x.shape
    hd = qn.shape[-1]
    assert hd == LANES
    nq = d // hd
    dk = nk * hd
    row = lambda i: (i, 0)
    fixed = lambda i: (0, 0)
    pos = lambda i: (jnp.maximum(i - tok.np_tiles, 0) % tok.tps, 0)
    n_p, n_s = tok.np_tiles * tok.tm, n - tok.np_tiles * tok.tm
    return pl.pallas_call(
        functools.partial(_gqa_qkv_kernel, nq=nq, nk=nk, np_tiles=tok.np_tiles),
        grid=(tok.tiles,),
        in_specs=[pl.BlockSpec((tok.tm, d), row),
                  pl.BlockSpec((1, d), fixed),
                  tok.mod_spec(layer, 1, d, 1),
                  tok.mod_spec(layer, 0, d, 1),
                  _resident((None, d, d + 2 * dk), lambda i: (slot, 0, 0)),
                  pl.BlockSpec((1, hd), fixed),
                  pl.BlockSpec((1, hd), fixed),
                  pl.BlockSpec((tok.tm, hd), pos),
                  pl.BlockSpec((tok.tm, hd), pos)],
        out_specs=[pl.BlockSpec((tok.tm, d), row),
                   tok.prompt_spec(dk, 1), tok.sample_spec(dk, 1), tok.prompt_spec(dk, 1), tok.sample_spec(dk, 1)],
        out_shape=[jax.ShapeDtypeStruct((n, d), BF16),
                   jax.ShapeDtypeStruct((n_p, dk), F32), jax.ShapeDtypeStruct((n_s, dk), BF16),
                   jax.ShapeDtypeStruct((n_p, dk), F32), jax.ShapeDtypeStruct((n_s, dk), BF16)],
        scratch_shapes=[pltpu.VMEM((d, d + 2 * dk), BF16)],
        compiler_params=_cparams("arbitrary"),
        name="gqa_qkv",
    )(x, g, mod, mod, w_qkv, qn.reshape(1, hd), kn.reshape(1, hd), cos_t, sin_t)


def _rope_tables(dec_seq, hd):
    t = jnp.arange(dec_seq)
    row = (t // GRID_W).astype(F32)
    col = (t % GRID_W).astype(F32)
    half = hd // 2
    inv = ROPE_THETA ** (-jnp.arange(0, half, 2, dtype=F32) / half)
    ang = jnp.concatenate([row[:, None] * inv, col[:, None] * inv], axis=-1)
    cos, sin = jnp.cos(ang), jnp.sin(ang)
    return jnp.concatenate([cos, cos], axis=-1), jnp.concatenate([-sin, sin], axis=-1)


def _gqa_attn_kernel(q_ref, *refs, rep, nk, n_kv):
    k_refs, v_refs, o_ref = refs[:n_kv], refs[n_kv:2 * n_kv], refs[2 * n_kv]
    tq = q_ref.shape[0]
    scores, values = [], []
    for kv in range(nk):
        sl = slice(kv * LANES, (kv + 1) * LANES)
        qs = jnp.concatenate([q_ref[:, (kv * rep + r) * LANES:(kv * rep + r + 1) * LANES] for r in range(rep)], axis=0)
        scores.append([_qk(qs, r[:, sl].astype(BF16)) for r in k_refs])
        values.append([r[:, sl].astype(BF16) for r in v_refs])
    for kv in range(nk):
        o = _softmax2_pv(scores[kv], values[kv])
        for r in range(rep):
            o_ref[:, (kv * rep + r) * LANES:(kv * rep + r + 1) * LANES] = o[r * tq:(r + 1) * tq].astype(o_ref.dtype)


def _gqa_ctx_attn(q, k, v, batch, seq, nk):
    d = q.shape[1]
    dk = k.shape[1]
    rep = d // dk
    qspec = pl.BlockSpec((seq, d), lambda b: (b, 0))
    kspec = pl.BlockSpec((seq, dk), lambda b: (b, 0))
    return pl.pallas_call(
        functools.partial(_gqa_attn_kernel, rep=rep, nk=nk, n_kv=1),
        grid=(batch,),
        in_specs=[qspec, kspec, kspec],
        out_specs=qspec,
        out_shape=jax.ShapeDtypeStruct((batch * seq, d), BF16),
        compiler_params=_cparams("arbitrary"),
        name="gqa_ctx_attn",
    )(q, k, v)


def _gqa_lat_attn(q, k, v, cache_k, cache_v, slot, n_prompt, dec_batch, dec_seq, nk):
    d = q.shape[1]
    dk = k.shape[1]
    rep = d // dk
    past = cache_k.shape[2]
    tq = _pow2_tile(256, dec_seq, n_prompt)
    nqb = dec_seq // tq
    q0 = n_prompt // tq
    lspec = pl.BlockSpec((dec_seq, dk), lambda b, qb: (b, 0))
    cspec = pl.BlockSpec((None, None, past, dk), lambda b, qb: (b, slot, 0, 0))
    return pl.pallas_call(
        functools.partial(_gqa_attn_kernel, rep=rep, nk=nk, n_kv=2),
        grid=(dec_batch, nqb),
        in_specs=[pl.BlockSpec((tq, d), lambda b, qb: (q0 + b * nqb + qb, 0)),
                  lspec, cspec, lspec, cspec],
        out_specs=pl.BlockSpec((tq, d), lambda b, qb: (b * nqb + qb, 0)),
        out_shape=jax.ShapeDtypeStruct((dec_batch * dec_seq, d), BF16),
        compiler_params=_cparams("arbitrary", "arbitrary"),
        name="gqa_lat_attn",
    )(q, k, cache_k, v, cache_v)


def _proj_res_kernel(x_ref, op_ref, os_ref, w_ref, gate_ref, out_ref, w_scr, *, np_tiles):
    _cast_once(w_ref, w_scr)
    o = jnp.where(pl.program_id(0) < np_tiles, op_ref[...], os_ref[...])
    y = jnp.dot(o, w_scr[...], preferred_element_type=F32)
    out_ref[...] = x_ref[...] + gate_ref[...] * y


def _proj_res(x, o_p, o_s, w_o, slot, mod, layer, tok):
    n, d = x.shape
    row = lambda i: (i, 0)
    return pl.pallas_call(
        functools.partial(_proj_res_kernel, np_tiles=tok.np_tiles),
        grid=(tok.tiles,),
        in_specs=[pl.BlockSpec((tok.tm, d), row),
                  tok.prompt_spec(d, 1),
                  tok.sample_spec(d, 1),
                  _resident((None, d, d), lambda i: (slot, 0, 0)),
                  tok.mod_spec(layer, 2, d, 1)],
        out_specs=pl.BlockSpec((tok.tm, d), row),
        out_shape=jax.ShapeDtypeStruct((n, d), F32),
        scratch_shapes=[pltpu.VMEM((d, d), BF16)],
        compiler_params=_cparams("arbitrary"),
        name="attn_proj_res",
    )(x, o_p, o_s, w_o, mod)


def _mlp_kernel(x_ref, xn_ref, g_ref, sc_ref, sh_ref, scn_ref, shn_ref, gate_ref, w1_ref, w2_ref, *rest,
                np_tiles, n_j):
    out_refs, (h_even, h_odd, acc_scr) = rest[:-3], rest[-3:]
    i = pl.program_id(0)
    j = pl.program_id(1)
    last = j == n_j - 1
    rs = x_ref.shape[0] // n_j
    r0 = pl.multiple_of(j * rs, rs)

    @pl.when(jnp.logical_and(i == 0, j == 0))
    def _():
        h_even[...] = _normmod(x_ref[...], g_ref[...], sc_ref[...], sh_ref[...]).astype(BF16)

    @pl.when(j == 0)
    def _():
        acc_scr[...] = jnp.zeros_like(acc_scr)

    def step(h_cur, h_next):
        h_next[pl.ds(r0, rs), :] = _normmod(xn_ref[pl.ds(r0, rs), :], g_ref[...], scn_ref[...], shn_ref[...]).astype(BF16)
        a = jnp.maximum(jnp.dot(h_cur[...], w1_ref[...].astype(BF16), preferred_element_type=F32), 0.0)
        acc_scr[...] += jnp.dot((a * a).astype(BF16), w2_ref[...].astype(BF16), preferred_element_type=F32)

    @pl.when(i % 2 == 0)
    def _():
        step(h_even, h_odd)

    @pl.when(i % 2 == 1)
    def _():
        step(h_odd, h_even)

    def result():
        return x_ref[...] + gate_ref[...] * acc_scr[...]

    if len(out_refs) == 1:
        @pl.when(last)
        def _():
            out_refs[0][...] = result()
    else:
        @pl.when(jnp.logical_and(last, i < np_tiles))
        def _():
            out_refs[0][...] = result()

        @pl.when(jnp.logical_and(last, i >= np_tiles))
        def _():
            out_refs[1][...] = result()


def _mlp(x, g, mod, layer, w1, w2, tok, split_out):
    n, d = x.shape
    f = w1.shape[2]
    tf = _pow2_tile(1024, f)
    row = lambda i, j: (i, 0)
    once = lambda index_map: pl.BlockSpec((tok.tm, d), index_map, pipeline_mode=pl.Buffered(1))
    if split_out:
        n_p = tok.np_tiles * tok.tm
        out_specs = [once(lambda i, j: (jnp.minimum(i, tok.np_tiles - 1), 0)),
                     once(lambda i, j: (jnp.maximum(i - tok.np_tiles, 0), 0))]
        out_shape = [jax.ShapeDtypeStruct((n_p, d), F32), jax.ShapeDtypeStruct((n - n_p, d), F32)]
    else:
        out_specs = once(row)
        out_shape = jax.ShapeDtypeStruct((n, d), F32)
    n_j = f // tf
    assert tok.tm % n_j == 0
    nxt = lambda i: jnp.minimum(i + 1, tok.tiles - 1)
    mod_next = lambda which: pl.BlockSpec((None, None, None, 1, d),
                                          lambda i, j: (layer, tok.mod_row(nxt(i)), which, 0, 0))
    return pl.pallas_call(
        functools.partial(_mlp_kernel, np_tiles=tok.np_tiles, n_j=n_j),
        grid=(tok.tiles, n_j),
        in_specs=[pl.BlockSpec((tok.tm, d), row),
                  pl.BlockSpec((tok.tm, d), lambda i, j: (nxt(i), 0)),
                  pl.BlockSpec((1, d), lambda i, j: (0, 0)),
                  tok.mod_spec(layer, 4, d, 2),
                  tok.mod_spec(layer, 3, d, 2),
                  mod_next(4),
                  mod_next(3),
                  tok.mod_spec(layer, 5, d, 2),
                  pl.BlockSpec((None, d, tf), lambda i, j: (layer, 0, j)),
                  pl.BlockSpec((None, tf, d), lambda i, j: (layer, j, 0))],
        out_specs=out_specs,
        out_shape=out_shape,
        scratch_shapes=[pltpu.VMEM((tok.tm, d), BF16), pltpu.VMEM((tok.tm, d), BF16), pltpu.VMEM((tok.tm, d), F32)],
        compiler_params=_cparams("arbitrary", "arbitrary"),
        name="mlp",
    )(x, x, g, mod, mod, mod, mod, mod, w1, w2)


def _s5_tables(lam_re, lam_im, log_dt, b_re, b_im, c_re, c_im):
    t = S5_CHUNK
    g, p, k = b_re.shape[1:]
    dt = jnp.exp(log_dt.astype(F32))[:, :, None]
    lr, li = lam_re.astype(F32), lam_im.astype(F32)
    ar, ai = lr * dt, li * dt
    mag = jnp.exp(ar)
    abr, abi = mag * jnp.cos(ai), mag * jnp.sin(ai)
    nr, ni = abr - 1.0, abi
    den = lr * lr + li * li
    f_re = (nr * lr + ni * li) / den
    f_im = (ni * lr - nr * li) / den
    bbr = f_re[..., None] * b_re - f_im[..., None] * b_im
    bbi = f_re[..., None] * b_im + f_im[..., None] * b_re
    n = jnp.arange(t + 1, dtype=F32)[None, None, :, None]
    pm = jnp.exp(n * ar[:, :, None, :])
    pr, pi = pm * jnp.cos(n * ai[:, :, None, :]), pm * jnp.sin(n * ai[:, :, None, :])
    cr, ci = c_re.astype(F32), c_im.astype(F32)

    assert 2 * p == LANES
    cat = lambda a, b: jnp.concatenate([a, b], axis=-1)
    bt_r, bt_i = bbr.transpose(0, 1, 3, 2), bbi.transpose(0, 1, 3, 2)
    bb2 = jnp.stack([cat(bt_r[0], bt_i[0]), cat(-bt_i[0], bt_r[0]),
                     cat(bt_r[1], bt_i[1]), cat(-bt_i[1], bt_r[1])], axis=1)
    pws = jnp.stack([cat(pr[0, :, :t][:, ::-1], pr[0, :, :t][:, ::-1]), cat(pi[0, :, :t][:, ::-1], pi[0, :, :t][:, ::-1]),
                     cat(pr[1, :, :t], pr[1, :, :t]), cat(pi[1, :, :t], pi[1, :, :t])], axis=1)
    pwr = jnp.stack([cat(pr[0, :, 1:], pr[1, :, 1:][:, ::-1]), cat(pi[0, :, 1:], pi[1, :, 1:][:, ::-1])], axis=1)
    c2 = jnp.stack([cat(cr[0], cr[1]), cat(ci[0], ci[1])], axis=1)
    cfb = jnp.stack([cat(cr[0], -ci[0]), cat(cr[1], -ci[1])], axis=1)
    w_intra, w_sum, w2t = _s5_toeplitz(cfb, bb2, pws, c2, pwr)
    at = jnp.stack([jnp.concatenate([pr[0, :, t], pr[1, :, t]], axis=-1),
                    jnp.concatenate([pi[0, :, t], pi[1, :, t]], axis=-1)], axis=1)
    return w_intra, w_sum, w2t, at


def _s5_toeplitz_kernel(cfb_ref, bb2_ref, pws_ref, c2_ref, pwr_ref, o_ref, ws_ref, w2t_ref):
    hi = lax.Precision.HIGHEST
    gl, _, k, p2 = cfb_ref.shape
    p = p2 // 2
    tk = o_ref.shape[1]
    t = tk // k
    for g in range(gl):
        def ab(d):
            blocks = [bb2_ref[g, 2 * d] * pws_ref[g, 2 * d, j:j + 1, :]
                      + bb2_ref[g, 2 * d + 1] * pws_ref[g, 2 * d + 1, j:j + 1, :] for j in range(t)]
            return jnp.concatenate(blocks, axis=0).T

        abf, abb = ab(0), ab(1)
        ws_ref[g] = jnp.concatenate([abf[0:p], abb[0:p], abf[p:], abb[p:]], axis=0).astype(ws_ref.dtype)
        cr2, ci2 = c2_ref[g, 0], c2_ref[g, 1]
        rows = []
        for i in range(t):
            wr, wi = pwr_ref[g, 0, i:i + 1, :], pwr_ref[g, 1, i:i + 1, :]
            rows.append(jnp.concatenate([cr2 * wr - ci2 * wi, -(cr2 * wi + ci2 * wr)], axis=1))
        w2t_ref[g] = jnp.concatenate(rows, axis=0).astype(w2t_ref.dtype)
        kf = jnp.dot(cfb_ref[g, 0], abf, precision=hi, preferred_element_type=F32)
        kb = jnp.dot(cfb_ref[g, 1], abb, precision=hi, preferred_element_type=F32)
        z = jnp.zeros_like(kf)
        krev = jnp.concatenate([kf, z], axis=1) + pltpu.roll(jnp.concatenate([z, kb], axis=1), 2 * tk - k, 1)
        for i in range(S5_CHUNK):
            sh = (S5_CHUNK - 1 - i) * k
            win = krev if sh == 0 else pltpu.roll(krev, 2 * tk - sh, 1)
            o_ref[g, i * k:(i + 1) * k, :] = win[:, 0:tk].astype(o_ref.dtype)


def _s5_toeplitz(cfb, bb2, pws, c2, pwr):
    g, _, k, p2 = cfb.shape
    t = pws.shape[2]
    tk = t * k
    gl = _pow2_tile(8, g)
    spec = lambda a: pl.BlockSpec((gl,) + a.shape[1:], lambda i: (i, 0, 0, 0))
    out = lambda rows, cols: (pl.BlockSpec((gl, rows, cols), lambda i: (i, 0, 0)),
                              jax.ShapeDtypeStruct((g, rows, cols), BF16))
    outs = [out(tk, tk), out(2 * p2, tk), out(tk, 2 * p2)]
    return pl.pallas_call(
        _s5_toeplitz_kernel,
        grid=(g // gl,),
        in_specs=[spec(a) for a in (cfb, bb2, pws, c2, pwr)],
        out_specs=[o[0] for o in outs],
        out_shape=[o[1] for o in outs],
        compiler_params=_cparams("arbitrary"),
        name="s5_toeplitz",
    )(cfb, bb2, pws, c2, pwr)


def _s5_mixer(x, g, mod, layer, tok, tok_post, params, h0, dims):
    (lam_re, lam_im, log_dt, b_re, b_im, c_re, c_im, d_skip, glu_w, slot, glu_b) = params
    batch, seq, dec_batch, dec_seq = dims
    n_prompt = batch * seq
    d = d_skip.shape[0]
    ngrp = d // S5_GROUP
    p = lam_re.shape[-1]
    w_intra, w_sum, w2t, at = _s5_tables(lam_re, lam_im, log_dt, b_re, b_im, c_re, c_im)
    hperm = _s5_pre(x, g, mod, layer, tok)
    h0g = h0.astype(F32).transpose(3, 2, 0, 1, 4).reshape(ngrp, 2, dec_batch, 2 * p)
    streams = ((0, seq // S5_CHUNK, batch, False, True),
               (n_prompt // S5_CHUNK, dec_seq // S5_CHUNK, dec_batch, True, False))
    yperm, fin = _s5_scan(hperm, w_intra, w_sum, w2t, at, h0g, batch, streams)
    x_new = _s5_post(x, yperm, g, mod, layer, d_skip.reshape(1, d), glu_w, slot, glu_b, tok_post)
    st = fin.reshape(ngrp, 2, batch, 2, p).transpose(2, 3, 1, 0, 4)
    return x_new, st


def kernel(x_prompt, x_sample, state_s5, cache_na_k, cache_na_v, cache_gqa_k, cache_gqa_v, c, c_ctx, norm_g, ada_w, ada_b, mlp_w1, mlp_w2, s5_lam_re, s5_lam_im, s5_log_dt, s5_b_re, s5_b_im, s5_c_re, s5_c_im, s5_d, s5_glu_w, s5_glu_b, na_w_qkv, na_q_norm, na_k_norm, na_rpb, na_w_o, gqa_w_qkv, gqa_q_norm, gqa_k_norm, gqa_w_o):
    batch, seq, d = x_prompt.shape
    dec_batch, dec_seq, _ = x_sample.shape
    depth = ada_w.shape[0]
    n_prompt = batch * seq
    n_sample = dec_batch * dec_seq
    na_heads, na_hd = cache_na_k.shape[3], cache_na_k.shape[4]
    gqa_kv, gqa_hd = cache_gqa_k.shape[3], cache_gqa_k.shape[4]
    assert n_prompt % dec_seq == 0

    tok = _Tok(n_prompt, n_sample, dec_seq, 1024)
    tok_half = _Tok(n_prompt, n_sample, dec_seq, 512)

    mod_rows = -(-(1 + dec_batch) // SUBLANES) * SUBLANES
    cvec = jnp.concatenate([c_ctx[None, :], c, jnp.zeros((mod_rows - 1 - dec_batch, d), F32)], axis=0)
    mod = _modulation(cvec, ada_w, ada_b).reshape(depth, mod_rows, 6, 1, d)

    x = (x_prompt.reshape(n_prompt, d), x_sample.reshape(n_sample, d))
    cache_na_k2 = cache_na_k.reshape(cache_na_k.shape[:3] + (na_heads * na_hd,))
    cache_na_v2 = cache_na_v.reshape(cache_na_v.shape[:3] + (na_heads * na_hd,))
    cache_gqa_k2 = cache_gqa_k.reshape(cache_gqa_k.shape[:3] + (gqa_kv * gqa_hd,))
    cache_gqa_v2 = cache_gqa_v.reshape(cache_gqa_v.shape[:3] + (gqa_kv * gqa_hd,))
    cos_t, sin_t = _rope_tables(dec_seq, gqa_hd)

    new_s5, new_na_k, new_na_v, new_gqa_k, new_gqa_v = [], [], [], [], []
    for i in range(depth):
        kind, slot = i % 3, i // 3
        g1 = norm_g[i, 0].reshape(1, d)
        g2 = norm_g[i, 1].reshape(1, d)
        if kind == 0:
            params = (s5_lam_re[slot], s5_lam_im[slot], s5_log_dt[slot], s5_b_re[slot], s5_b_im[slot],
                      s5_c_re[slot], s5_c_im[slot], s5_d[slot], s5_glu_w, slot, s5_glu_b[slot])
            x, st = _s5_mixer(x, g1, mod, i, tok_half, tok, params, state_s5[:, slot],
                              (batch, seq, dec_batch, dec_seq))
            new_s5.append(st)
        elif kind == 1:
            assert not isinstance(x, tuple)
            q, k_p, k_s, v_p, v_s = _na_qkv(x, g1, mod, i, na_w_qkv, slot, na_q_norm[slot], na_k_norm[slot], tok_half)
            o_p = _na_ctx_attn(q, k_p, v_p, batch, seq, na_hd)
            bias = _na_bias_table(na_rpb[slot])
            o_s = _na_lat_attn(q, k_s, v_s, cache_na_k2, cache_na_v2, slot, bias, n_prompt, dec_batch, dec_seq, na_hd)
            x = _proj_res(x, o_p, o_s, na_w_o, slot, mod, i, tok)
            new_na_k.append(k_p.reshape(batch, seq, na_heads, na_hd))
            new_na_v.append(v_p.reshape(batch, seq, na_heads, na_hd))
        else:
            assert not isinstance(x, tuple)
            q, k_p, k_s, v_p, v_s = _gqa_qkv(x, g1, mod, i, gqa_w_qkv, slot, gqa_q_norm[slot], gqa_k_norm[slot],
                                             cos_t, sin_t, gqa_kv, tok_half)
            o_p = _gqa_ctx_attn(q, k_p, v_p, batch, seq, gqa_kv)
            o_s = _gqa_lat_attn(q, k_s, v_s, cache_gqa_k2, cache_gqa_v2, slot, n_prompt, dec_batch, dec_seq, gqa_kv)
            x = _proj_res(x, o_p, o_s, gqa_w_o, slot, mod, i, tok)
            new_gqa_k.append(k_p.reshape(batch, seq, gqa_kv, gqa_hd))
            new_gqa_v.append(v_p.reshape(batch, seq, gqa_kv, gqa_hd))
        x = _mlp(x, g2, mod, i, mlp_w1, mlp_w2, tok, split_out=(i == depth - 1))
    y_p, y_s = x
    return (y_p.reshape(batch, seq, d), y_s.reshape(dec_batch, dec_seq, d),
            jnp.stack(new_s5, axis=1), jnp.stack(new_na_k, axis=1), jnp.stack(new_na_v, axis=1),
            jnp.stack(new_gqa_k, axis=1), jnp.stack(new_gqa_v, axis=1))
```
